```python
import functools
import jax, jax.numpy as jnp
from jax import lax
import numpy as np

D_MODEL = 1024
BATCH = 32
SEQ = 256
DEPTH = 2
DEC_BATCH = 2
DEC_SEQ = 1024
PAST_LEN = 256

GRID_W = 64
GLA_HEADS = 4
GLA_DK = 64
GLA_DV = 128
GLA_GATE_RANK = 16
GLA_GATE_TAU = 16.0
GLA_CHUNK = 64
NA_HEADS = 8
NA_DH = 64
NA_WIN_R = 8
NA_WIN_C = 16
NA_QBLK_C = 16
NA_KBLK_C = NA_QBLK_C + NA_WIN_C
Q_BLOCK = 128
CONV_CH = 512
CONV_K = 31
FF_DENSE = 2816
N_EXPERTS = 8
TOP_K = 2
FF_EXPERT = 3584
ROPE_BASE = 10000.0
LN_EPS = 1e-5
DN_ALPHA = (2 * DEPTH) ** 0.25
DN_BETA = (8 * DEPTH) ** -0.25
N_DENSE = (DEPTH + 1) // 2
N_MOE = DEPTH // 2
GLA_QK_W = GLA_HEADS * GLA_DK
GLA_V_W = GLA_HEADS * GLA_DV
NA_W = NA_HEADS * NA_DH
IN_SPLITS = (GLA_QK_W, GLA_QK_W, GLA_V_W, GLA_V_W, 2 * GLA_GATE_RANK, NA_W, NA_W, NA_W, 2 * CONV_CH, 3 * D_MODEL)
IN_COLS = sum(IN_SPLITS)

kernel_name = 'hybrid_gla_natten_conformer_dit_step'


def _ln(x, g, b):
    xf = x.astype(jnp.float32)
    mu = jnp.mean(xf, -1, keepdims=True)
    var = jnp.mean(jnp.square(xf - mu), -1, keepdims=True)
    return (xf - mu) * lax.rsqrt(var + LN_EPS) * g + b


def _rope_axis(x, pos):
    n = x.shape[-1] // 2
    inv = ROPE_BASE ** (-jnp.arange(n, dtype=jnp.float32) / n)
    ang = pos.astype(jnp.float32)[:, None] * inv[None, :]
    cos, sin = jnp.cos(ang)[:, None, :], jnp.sin(ang)[:, None, :]
    x1, x2 = x[..., :n], x[..., n:]
    return jnp.concatenate([x1 * cos - x2 * sin, x1 * sin + x2 * cos], -1)


def _rope2d(x):
    t = jnp.arange(x.shape[1])
    half = x.shape[-1] // 2
    return jnp.concatenate([_rope_axis(x[..., :half], t // GRID_W), _rope_axis(x[..., half:], t % GRID_W)], -1)


def _gla_scan(q, k, v, g, s0):
    b_, l_, h_, _ = q.shape
    dv = v.shape[-1]
    n = l_ // GLA_CHUNK

    def chunks(a):
        return a.astype(jnp.float32).reshape(b_, n, GLA_CHUNK, h_, a.shape[-1]).transpose(1, 0, 3, 2, 4)

    causal = jnp.tril(jnp.ones((GLA_CHUNK, GLA_CHUNK), dtype=bool))[:, :, None]

    def step(s, inp):
        qc, kc, vc, gc = inp
        cum = jnp.cumsum(gc, axis=2)
        rel = jnp.where(causal, cum[:, :, :, None, :] - cum[:, :, None, :, :], -jnp.inf)
        att = jnp.einsum('bhtd,bhsd,bhtsd->bhts', qc, kc, jnp.exp(rel))
        o = jnp.einsum('bhts,bhsv->bhtv', att, vc) + jnp.einsum('bhtd,bhdv->bhtv', qc * jnp.exp(cum), s)
        last = cum[:, :, -1:, :]
        s = jnp.exp(last[:, :, 0, :])[..., None] * s + jnp.einsum('bhsd,bhsv->bhdv', kc * jnp.exp(last - cum), vc)
        return s, o

    s_fin, o = lax.scan(step, s0.astype(jnp.float32), (chunks(q), chunks(k), chunks(v), chunks(g)))
    o = o.transpose(1, 0, 3, 2, 4).reshape(b_, l_, h_, dv)
    return o, s_fin


def _gla_bidir(q, k, v, g_f, g_b, s0_f, s0_b):
    o_f, s_f = _gla_scan(q, k, v, g_f, s0_f)
    fl = lambda a: jnp.flip(a, 1)
    o_b, s_b = _gla_scan(fl(q), fl(k), fl(v), fl(g_b), s0_b)
    return o_f + fl(o_b), s_f, s_b


def _ctx_attention(q, k, v):
    b_, l_, h_, dh = q.shape
    nb = l_ // Q_BLOCK
    qb = q.astype(jnp.float32).reshape(b_, nb, Q_BLOCK, h_, dh).transpose(1, 0, 2, 3, 4)
    kf, vf = k.astype(jnp.float32), v.astype(jnp.float32)

    def blk(qi):
        s = jnp.einsum('bqhd,bkhd->bhqk', qi, kf) * dh ** -0.5
        return jnp.einsum('bhqk,bkhd->bqhd', jax.nn.softmax(s, -1), vf)

    o = lax.map(blk, qb)
    return o.transpose(1, 0, 2, 3, 4).reshape(b_, l_, h_ * dh)


def _na_latent(q, k_lat, v_lat, k_ctx, v_ctx, rpb):
    b_, l_, h_, dh = q.shape
    rows = l_ // GRID_W
    kr = min(NA_WIN_R, rows)
    ncb = GRID_W // NA_QBLK_C
    r = jnp.arange(rows)
    j = jnp.arange(ncb)
    rs = jnp.clip(r - kr // 2, 0, rows - kr)
    ks = jnp.clip(j * NA_QBLK_C - NA_WIN_C // 2, 0, GRID_W - NA_KBLK_C)
    key_row = rs[:, None] + jnp.arange(kr)[None, :]
    key_col = ks[:, None] + jnp.arange(NA_KBLK_C)[None, :]
    nk = kr * NA_KBLK_C
    idx = (key_row[:, None, :, None] * GRID_W + key_col[None, :, None, :]).reshape(rows, ncb, nk)
    q_col = j[:, None] * NA_QBLK_C + jnp.arange(NA_QBLK_C)[None, :]
    cs = jnp.clip(q_col - NA_WIN_C // 2, 0, GRID_W - NA_WIN_C)
    kc3 = key_col[:, None, :]
    valid = (kc3 >= cs[..., None]) & (kc3 < cs[..., None] + NA_WIN_C)
    dc = jnp.clip(kc3 - q_col[..., None], -(NA_WIN_C - 1), NA_WIN_C - 1)
    dr = key_row - r[:, None]
    bias = rpb[:, (dr + NA_WIN_R - 1)[:, None, None, :, None], (dc + NA_WIN_C - 1)[None, :, :, None, :]]
    bias = bias.reshape(h_, rows, ncb, NA_QBLK_C, nk).transpose(1, 2, 0, 3, 4).astype(jnp.float32)
    mask = jnp.broadcast_to(valid[:, :, None, :], (ncb, NA_QBLK_C, kr, NA_KBLK_C)).reshape(ncb, NA_QBLK_C, nk)
    scale = dh ** -0.5
    qg = q.astype(jnp.float32).reshape(b_, rows, ncb, NA_QBLK_C, h_, dh)
    kg = k_lat.astype(jnp.float32)[:, idx]
    vg = v_lat.astype(jnp.float32)[:, idx]
    kcx, vcx = k_ctx.astype(jnp.float32), v_ctx.astype(jnp.float32)
    s_loc = jnp.einsum('brjqhd,brjkhd->brjhqk', qg, kg) * scale + bias[None]
    s_loc = jnp.where(mask[None, None, :, None], s_loc, -jnp.inf)
    s_ctx = jnp.einsum('brjqhd,bchd->brjhqc', qg, kcx) * scale
    p = jax.nn.softmax(jnp.concatenate([s_loc, s_ctx], -1), -1)
    o = jnp.einsum('brjhqk,brjkhd->brjqhd', p[..., :nk], vg) + jnp.einsum('brjhqc,bchd->brjqhd', p[..., nk:], vcx)
    return o.reshape(b_, l_, h_ * dh)


def _conv_module(u, lp):
    a, gt = jnp.split(u, 2, -1)
    z = (a * jax.nn.sigmoid(gt)).astype(jnp.float32)
    w = lp['cv_w'].astype(jnp.float32)[:, None, :]
    z = lax.conv_general_dilated(z, w, (1,), [(CONV_K // 2, CONV_K // 2)],
                                 dimension_numbers=('NWC', 'WIO', 'NWC'),
                                 feature_group_count=CONV_CH) + lp['cv_b']
    z = jax.nn.silu(_ln(z, lp['cv_ln_g'], lp['cv_ln_b']))
    return z @ lp['w_br_cv']


def _mixer(h, lp, latent, ctx_k, ctx_v, s0):
    b_, l_, _ = h.shape
    offs = [int(o) for o in np.cumsum(IN_SPLITS)[:-1]]
    z = jnp.einsum('bld,dn->bln', h, lp['w_in'])
    qa, ka, va, ra, gda, qb, kb, vb, cvi, mg = jnp.split(z, offs, axis=-1)
    qa = qa.astype(jnp.float32).reshape(b_, l_, GLA_HEADS, GLA_DK) * GLA_DK ** -0.5
    ka = ka.astype(jnp.float32).reshape(b_, l_, GLA_HEADS, GLA_DK)
    if latent:
        qa, ka = _rope2d(qa), _rope2d(ka)
    va = va.reshape(b_, l_, GLA_HEADS, GLA_DV)
    glog = jnp.einsum('blnr,nrk->blnk', gda.reshape(b_, l_, 2, GLA_GATE_RANK), lp['gla_w_gup']) + lp['gla_b_g']
    glog = (jax.nn.log_sigmoid(glog.astype(jnp.float32)) / GLA_GATE_TAU).reshape(b_, l_, 2, GLA_HEADS, GLA_DK)
    if s0 is None:
        s0 = jnp.zeros((b_, 2, GLA_HEADS, GLA_DK, GLA_DV), jnp.float32)
    o, s_f, s_b = _gla_bidir(qa, ka, va, glog[:, :, 0], glog[:, :, 1], s0[:, 0], s0[:, 1])
    o = o * lax.rsqrt(jnp.mean(jnp.square(o), -1, keepdims=True) + LN_EPS) * lp['gla_norm']
    y_a = (o.reshape(b_, l_, GLA_V_W) * jax.nn.silu(ra)) @ lp['w_br_gla']
    qb = qb.reshape(b_, l_, NA_HEADS, NA_DH)
    kb = kb.reshape(b_, l_, NA_HEADS, NA_DH)
    vb = vb.reshape(b_, l_, NA_HEADS, NA_DH)
    if latent:
        att = _na_latent(qb, kb, vb, ctx_k, ctx_v, lp['na_rpb'])
    else:
        att = _ctx_attention(qb, kb, vb)
    y_b = att @ lp['w_br_na']
    y_c = _conv_module(cvi, lp)
    g = jax.nn.sigmoid(mg.reshape(b_, l_, 3, D_MODEL))
    m = g[:, :, 0] * y_a + g[:, :, 1] * y_b + g[:, :, 2] * y_c
    return m @ lp['w_out'], kb, vb, jnp.stack([s_f, s_b], 1)


def _swiglu(h, wg, wu, wd):
    return (jax.nn.silu(h @ wg) * (h @ wu)) @ wd


def _moe(h, w_router, wg, wu, wd):
    b_, l_, d_ = h.shape
    t = h.reshape(b_ * l_, d_)
    logits = (t @ w_router).astype(jnp.float32)
    top_v, top_i = lax.top_k(logits, TOP_K)
    top_w = jax.nn.softmax(top_v, -1)
    gates = jnp.sum(jax.nn.one_hot(top_i, N_EXPERTS, dtype=jnp.float32) * top_w[..., None], axis=1)
    y = jnp.zeros((b_ * l_, d_), jnp.float32)
    for e in range(N_EXPERTS):
        y = y + gates[:, e:e + 1] * _swiglu(t, wg[e], wu[e], wd[e])
    return y.reshape(b_, l_, d_)


def _layer(x, cond, lp, ffn, latent, ctx_k, ctx_v, s0):
    mod = jax.nn.silu(cond) @ lp['w_mod'] + lp['b_mod']
    sh1, sc1, g1, sh2, sc2, g2 = jnp.split(mod[:, None, :], 6, -1)
    h = x * (1 + sc1) + sh1
    mix, kb, vb, s = _mixer(h, lp, latent, ctx_k, ctx_v, s0)
    x = _ln(DN_ALPHA * x + g1 * mix, lp['ln_g'][0], lp['ln_b'][0])
    h = x * (1 + sc2) + sh2
    x = _ln(DN_ALPHA * x + g2 * ffn(h), lp['ln_g'][1], lp['ln_b'][1])
    return x, kb, vb, s


def setup_inputs(seed: int = 0) -> dict:
    key = jax.random.key(seed)
    ks = iter(jax.random.split(key, 40))
    nrm = lambda shape, s: jax.random.normal(next(ks), shape, jnp.float32) * s
    D = D_MODEL
    return {
        'x_prompt': nrm((BATCH, SEQ, D), 1.0),
        'x_sample': nrm((DEC_BATCH, DEC_SEQ, D), 1.0),
        'cache_na_k': nrm((DEC_BATCH, DEPTH, PAST_LEN, NA_HEADS, NA_DH), 1.0),
        'cache_na_v': nrm((DEC_BATCH, DEPTH, PAST_LEN, NA_HEADS, NA_DH), 1.0),
        'state_gla': nrm((DEC_BATCH, DEPTH, 2, GLA_HEADS, GLA_DK, GLA_DV), 1.0),
        'c': nrm((DEC_BATCH, D), 1.0),
        'c_ctx': nrm((D,), 1.0),
        'w_mod': nrm((DEPTH, D, 6 * D), D ** -0.5),
        'b_mod': nrm((DEPTH, 6 * D), 0.02),
        'w_in': nrm((DEPTH, D, IN_COLS), D ** -0.5),
        'gla_w_gup': nrm((DEPTH, 2, GLA_GATE_RANK, GLA_QK_W), GLA_GATE_RANK ** -0.5),
        'gla_b_g': nrm((DEPTH, 2, GLA_QK_W), 0.1),
        'gla_norm': 1.0 + nrm((DEPTH, GLA_HEADS, GLA_DV), 0.01),
        'w_br_gla': nrm((DEPTH, GLA_V_W, D), GLA_V_W ** -0.5),
        'na_rpb': nrm((DEPTH, NA_HEADS, 2 * NA_WIN_R - 1, 2 * NA_WIN_C - 1), 0.1),
        'w_br_na': nrm((DEPTH, NA_W, D), NA_W ** -0.5),
        'cv_w': nrm((DEPTH, CONV_K, CONV_CH), CONV_K ** -0.5),
        'cv_b': nrm((DEPTH, CONV_CH), 0.02),
        'cv_ln_g': 1.0 + nrm((DEPTH, CONV_CH), 0.01),
        'cv_ln_b': nrm((DEPTH, CONV_CH), 0.01),
        'w_br_cv': nrm((DEPTH, CONV_CH, D), CONV_CH ** -0.5),
        'w_out': nrm((DEPTH, D, D), D ** -0.5 * DN_BETA),
        'ln_g': 1.0 + nrm((DEPTH, 2, D), 0.01),
        'ln_b': nrm((DEPTH, 2, D), 0.01),
        'ffd_w_gate': nrm((N_DENSE, D, FF_DENSE), D ** -0.5),
        'ffd_w_up': nrm((N_DENSE, D, FF_DENSE), D ** -0.5),
        'ffd_w_down': nrm((N_DENSE, FF_DENSE, D), FF_DENSE ** -0.5 * DN_BETA),
        'moe_w_router': nrm((N_MOE, D, N_EXPERTS), D ** -0.5),
        'moe_w_gate': nrm((N_MOE, N_EXPERTS, D, FF_EXPERT), D ** -0.5),
        'moe_w_up': nrm((N_MOE, N_EXPERTS, D, FF_EXPERT), D ** -0.5),
        'moe_w_down': nrm((N_MOE, N_EXPERTS, FF_EXPERT, D), FF_EXPERT ** -0.5 * DN_BETA),
    }


def reference(x_prompt, x_sample, cache_na_k, cache_na_v, state_gla, c, c_ctx,
              w_mod, b_mod, w_in, gla_w_gup, gla_b_g, gla_norm, w_br_gla, na_rpb, w_br_na,
              cv_w, cv_b, cv_ln_g, cv_ln_b, w_br_cv, w_out, ln_g, ln_b,
              ffd_w_gate, ffd_w_up, ffd_w_down,
              moe_w_router, moe_w_gate, moe_w_up, moe_w_down):
    y_prompt = x_prompt
    y_sample = x_sample
    ctx_cond = c_ctx[None, :]
    ks_l, vs_l, ss_l = [], [], []
    for l in range(DEPTH):
        lp = {'w_mod': w_mod[l], 'b_mod': b_mod[l], 'w_in': w_in[l],
              'gla_w_gup': gla_w_gup[l], 'gla_b_g': gla_b_g[l], 'gla_norm': gla_norm[l],
              'w_br_gla': w_br_gla[l], 'na_rpb': na_rpb[l], 'w_br_na': w_br_na[l],
              'cv_w': cv_w[l], 'cv_b': cv_b[l], 'cv_ln_g': cv_ln_g[l], 'cv_ln_b': cv_ln_b[l],
              'w_br_cv': w_br_cv[l], 'w_out': w_out[l], 'ln_g': ln_g[l], 'ln_b': ln_b[l]}
        i = l // 2
        if l % 2 == 0:
            ffn = functools.partial(_swiglu, wg=ffd_w_gate[i], wu=ffd_w_up[i], wd=ffd_w_down[i])
        else:
            ffn = functools.partial(_moe, w_router=moe_w_router[i], wg=moe_w_gate[i],
                                    wu=moe_w_up[i], wd=moe_w_down[i])
        y_prompt, kb, vb, s = _layer(y_prompt, ctx_cond, lp, ffn, False, None, None, None)
        ks_l.append(kb)
        vs_l.append(vb)
        ss_l.append(s)
        y_sample = _layer(y_sample, c, lp, ffn, True, cache_na_k[:, l], cache_na_v[:, l], state_gla[:, l])[0]
    new_cache_na_k = jnp.stack(ks_l, 1)
    new_cache_na_v = jnp.stack(vs_l, 1)
    new_state_gla = jnp.stack(ss_l, 1)
    return (y_prompt, y_sample, new_cache_na_k, new_cache_na_v, new_state_gla)
```

```python
import functools

import numpy as np
import jax
import jax.numpy as jnp
from jax import lax
from jax.experimental import pallas as pl
from jax.experimental.pallas import tpu as pltpu

F32 = jnp.float32
BF16 = jnp.bfloat16

GRID_W = 64
GLA_HEADS = 4
GLA_DK = 64
GLA_DV = 128
GLA_RANK = 16
GLA_TAU = 16.0
GLA_CHUNK = 64
NA_HEADS = 8
NA_DH = 64
NA_WIN_R = 8
NA_WIN_C = 16
CONV_CH = 512
CONV_K = 31
N_EXPERTS = 8
ROPE_BASE = 10000.0
LN_EPS = 1e-5

D_MODEL = 1024
LANES = 128
CHUNK_W = 512
N_F32_CHUNKS = 3
N_BF16_CHUNKS = 12
NEG_BIG = -1e30
VMEM_LIMIT = 56 * 1024 * 1024


def _cparams(sem, vmem=VMEM_LIMIT):
    return pltpu.CompilerParams(dimension_semantics=sem, vmem_limit_bytes=vmem)


def _dot(a, b):
    return jnp.dot(a, b, preferred_element_type=F32)


def _dot_hi(a, b):
    return jnp.dot(a, b, preferred_element_type=F32, precision=lax.Precision.HIGHEST)


def _dot_nt(a, b):
    return lax.dot_general(a, b, (((1,), (1,)), ((), ())), preferred_element_type=F32)


def _sigmoid(x):
    return 1.0 / (1.0 + jnp.exp(-x))


def _silu(x):
    return x * _sigmoid(x)


def _layer_norm(x, g, b):
    mu = jnp.mean(x, axis=-1, keepdims=True)
    xc = x - mu
    var = jnp.mean(xc * xc, axis=-1, keepdims=True)
    return xc * lax.rsqrt(var + LN_EPS) * g + b


def _mod_kernel(c_ref, w_ref, b_ref, o_ref):
    c = c_ref[...]
    s = _silu(c).astype(BF16)
    o_ref[...] = _dot(s, w_ref[...].astype(BF16)) + b_ref[...]


def _modulation(cond8, w_mod, b_mod):
    depth, d, n = w_mod.shape
    tn = 1024
    return pl.pallas_call(
        _mod_kernel,
        grid=(depth, n // tn),
        in_specs=[
            pl.BlockSpec((8, d), lambda l, j: (0, 0)),
            pl.BlockSpec((None, d, tn), lambda l, j: (l, 0, j)),
            pl.BlockSpec((None, 1, tn), lambda l, j: (l, 0, j)),
        ],
        out_specs=pl.BlockSpec((None, 8, tn), lambda l, j: (l, 0, j)),
        out_shape=jax.ShapeDtypeStruct((depth, 8, n), F32),
        compiler_params=_cparams(("arbitrary", "arbitrary")),
        name="modulation",
    )(cond8, w_mod, b_mod.reshape(depth, 1, n))


def _inproj_kernel(x_ref, sc_ref, sh_ref, w_ref, zf_ref, zb_ref, h_ref):
    h_ref[...] = (x_ref[...] * (1.0 + sc_ref[...]) + sh_ref[...]).astype(BF16)

    def f32_chunk(j, c):
        zf_ref[j] = _dot(h_ref[...], w_ref[j])
        return c

    lax.fori_loop(0, N_F32_CHUNKS, f32_chunk, 0)

    def bf16_chunk(j, c):
        zb_ref[j] = _dot(h_ref[...], w_ref[N_F32_CHUNKS + j]).astype(BF16)
        return c

    lax.fori_loop(0, N_BF16_CHUNKS, bf16_chunk, 0)


def _mod_spec(piece, row_of_tile):
    return pl.BlockSpec((None, None, 1, D_MODEL), lambda i: (row_of_tile(i), piece, 0, 0))


def _input_projection(x_all, mod_l, w_chunks, tm, row_of_tile):
    t, d = x_all.shape
    nch = N_F32_CHUNKS + N_BF16_CHUNKS
    return pl.pallas_call(
        _inproj_kernel,
        grid=(t // tm,),
        in_specs=[
            pl.BlockSpec((tm, d), lambda i: (i, 0)),
            _mod_spec(1, row_of_tile),
            _mod_spec(0, row_of_tile),
            pl.BlockSpec((nch, d, CHUNK_W), lambda i: (0, 0, 0), pipeline_mode=pl.Buffered(1)),
        ],
        out_specs=[
            pl.BlockSpec((N_F32_CHUNKS, tm, CHUNK_W), lambda i: (0, i, 0)),
            pl.BlockSpec((N_BF16_CHUNKS, tm, CHUNK_W), lambda i: (0, i, 0)),
        ],
        out_shape=[
            jax.ShapeDtypeStruct((N_F32_CHUNKS, t, CHUNK_W), F32),
            jax.ShapeDtypeStruct((N_BF16_CHUNKS, t, CHUNK_W), BF16),
        ],
        scratch_shapes=[pltpu.VMEM((tm, d), BF16)],
        compiler_params=_cparams(("arbitrary",)),
        name="input_projection",
    )(x_all, mod_l, mod_l, w_chunks)


def _gla_kernel(latent, seq, *refs):
    if latent:
        (qk_ref, v_ref, ra_ref, gd_ref, wg_ref, bg_ref, gn_ref, cos_ref, sin_ref, s0_ref, _alias,
         ya_ref, q_s, k_s, g_s, acc_s, st_s) = refs
    else:
        (qk_ref, v_ref, ra_ref, gd_ref, wg_ref, bg_ref, gn_ref,
         ya_ref, sfin_ref, q_s, k_s, g_s, acc_s, st_s) = refs
    c = GLA_CHUNK
    n_chunks = seq // c
    qkw = GLA_HEADS * GLA_DK

    q = qk_ref[:, 0:qkw].astype(F32) * (GLA_DK ** -0.5)
    k = qk_ref[:, qkw:2 * qkw].astype(F32)
    if latent:
        lane = lax.broadcasted_iota(jnp.int32, (1, qkw), 1)
        first_half = (lane % 32) < 16

        def rope(x):
            swapped = jnp.where(first_half, pltpu.roll(x, qkw - 16, 1), pltpu.roll(x, 16, 1))
            return x * cos_ref[...] + swapped * sin_ref[...]

        q, k = rope(q), rope(k)
    q_s[...] = q
    k_s[...] = k

    glog = _dot_hi(gd_ref[...], wg_ref[...]) + bg_ref[...]
    g_s[...] = (jnp.minimum(glog, 0.0) - jnp.log(1.0 + jnp.exp(-jnp.abs(glog)))) * (1.0 / GLA_TAU)
    acc_s[...] = jnp.zeros_like(acc_s)
    if latent:
        st_s[...] = s0_ref[...]
    else:
        st_s[...] = jnp.zeros_like(st_s)

    row = lax.broadcasted_iota(jnp.int32, (c, c), 0)
    col = lax.broadcasted_iota(jnp.int32, (c, c), 1)
    keep = (col <= row, col >= row)
    tri = (keep[0].astype(F32), keep[1].astype(F32))
    lane128 = lax.broadcasted_iota(jnp.int32, (1, LANES), 1)
    head_mask = (lane128 < GLA_DK, lane128 >= GLA_DK)
    srow = lax.broadcasted_iota(jnp.int32, (LANES, GLA_DV), 0)

    def chunk_step(i, carry):
        for d in range(2):
            ci = i if d == 0 else n_chunks - 1 - i
            rows = pl.ds(pl.multiple_of(ci * c, c), c)
            gc = g_s[rows, d * qkw:(d + 1) * qkw]
            cum = _dot_hi(tri[d], gc)
            last = cum[c - 1:c, :] if d == 0 else cum[0:1, :]
            ref = 0.5 * last
            qe = q_s[rows, :] * jnp.exp(cum - ref)
            ke = k_s[rows, :] * jnp.exp(ref - cum)
            qe2 = qe * jnp.exp(ref)
            ke2 = ke * jnp.exp(last - ref)
            for p in range(2):
                sl = slice(p * LANES, (p + 1) * LANES)
                ke_p = ke[:, sl].astype(BF16)
                ke2_t = ke2[:, sl].T.astype(BF16)
                s_pair = st_s[d, p]
                s_pair_b = s_pair.astype(BF16)
                upd = []
                for hh in range(2):
                    h = 2 * p + hh
                    vsl = slice(h * GLA_DV, (h + 1) * GLA_DV)
                    vh = v_ref[rows, vsl]
                    a = jnp.where(head_mask[hh], qe[:, sl], 0.0).astype(BF16)
                    a2 = jnp.where(head_mask[hh], qe2[:, sl], 0.0).astype(BF16)
                    att = jnp.where(keep[d], _dot_nt(a, ke_p), 0.0).astype(BF16)
                    acc_s[rows, vsl] += _dot(att, vh) + _dot(a2, s_pair_b)
                    upd.append(_dot(ke2_t, vh))
                decay = jnp.exp(jnp.broadcast_to(last[:, sl], (LANES, LANES))).T
                st_s[d, p] = decay * s_pair + jnp.where(srow < GLA_DK, upd[0], upd[1])
        return carry

    lax.fori_loop(0, n_chunks, chunk_step, 0)

    def epilogue(i, carry):
        rows = pl.ds(pl.multiple_of(i * c, c), c)
        for h in range(GLA_HEADS):
            vsl = slice(h * GLA_DV, (h + 1) * GLA_DV)
            o = acc_s[rows, vsl]
            o = o * lax.rsqrt(jnp.mean(o * o, axis=-1, keepdims=True) + LN_EPS) * gn_ref[:, vsl]
            ya_ref[rows, vsl] = (o * _silu(ra_ref[rows, vsl].astype(F32))).astype(BF16)
        return carry

    lax.fori_loop(0, n_chunks, epilogue, 0)
    if not latent:
        sfin_ref[...] = st_s[...]


def _gla_branch(zf, zb, wg_blk, bg, gn, *, latent, seq, nb, tok0, t_total, rope=None, s0=None, ya_prev=None):
    b0 = tok0 // seq
    qkw = GLA_HEADS * GLA_DK
    in_specs = [
        pl.BlockSpec((None, seq, CHUNK_W), lambda b: (0, b0 + b, 0)),
        pl.BlockSpec((None, seq, CHUNK_W), lambda b: (1, b0 + b, 0)),
        pl.BlockSpec((None, seq, CHUNK_W), lambda b: (2, b0 + b, 0)),
        pl.BlockSpec((None, seq, LANES), lambda b: (0, b0 + b, 0)),
        pl.BlockSpec((LANES, CHUNK_W), lambda b: (0, 0)),
        pl.BlockSpec((1, CHUNK_W), lambda b: (0, 0)),
        pl.BlockSpec((1, CHUNK_W), lambda b: (0, 0)),
    ]
    args = [zb, zb, zb, zf, wg_blk, bg, gn]
    scratch = [
        pltpu.VMEM((seq, qkw), F32), pltpu.VMEM((seq, qkw), F32), pltpu.VMEM((seq, CHUNK_W), F32),
        pltpu.VMEM((seq, CHUNK_W), F32), pltpu.VMEM((2, 2, LANES, GLA_DV), F32),
    ]
    ya_shape = jax.ShapeDtypeStruct((t_total, CHUNK_W), BF16)
    ya_spec = pl.BlockSpec((seq, CHUNK_W), lambda b: (b0 + b, 0))
    if latent:
        cos_t, sin_t = rope
        in_specs += [
            pl.BlockSpec((seq, qkw), lambda b: (0, 0)),
            pl.BlockSpec((seq, qkw), lambda b: (0, 0)),
            pl.BlockSpec((None, 2, 2, LANES, GLA_DV), lambda b: (b, 0, 0, 0, 0)),
            pl.BlockSpec(memory_space=pl.ANY),
        ]
        args += [cos_t, sin_t, s0, ya_prev]
        return pl.pallas_call(
            functools.partial(_gla_kernel, True, seq),
            grid=(nb,), in_specs=in_specs, out_specs=ya_spec, out_shape=ya_shape,
            scratch_shapes=scratch, input_output_aliases={len(args) - 1: 0},
            compiler_params=_cparams(("arbitrary",)), name="gla_latent",
        )(*args)
    return pl.pallas_call(
        functools.partial(_gla_kernel, False, seq),
        grid=(nb,), in_specs=in_specs,
        out_specs=[ya_spec, pl.BlockSpec((None, 2, 2, LANES, GLA_DV), lambda b: (b, 0, 0, 0, 0))],
        out_shape=[ya_shape, jax.ShapeDtypeStruct((nb, 2, 2, LANES, GLA_DV), F32)],
        scratch_shapes=scratch,
        compiler_params=_cparams(("arbitrary",)), name="gla_context",
    )(*args)


def _attn_ctx_kernel(q_ref, k_ref, v_ref, o_ref):
    lane = lax.broadcasted_iota(jnp.int32, (1, LANES), 1)
    masks = (lane < NA_DH, lane >= NA_DH)
    scale = NA_DH ** -0.5
    for p in range(NA_HEADS // 2):
        sl = slice(p * LANES, (p + 1) * LANES)
        qp = q_ref[:, sl]
        kp = k_ref[:, sl].astype(BF16)
        vp = v_ref[:, sl].astype(BF16)
        outs = []
        for hh in range(2):
            qm = jnp.where(masks[hh], qp, jnp.zeros_like(qp))
            s = _dot_nt(qm, kp) * scale
            e = jnp.exp(s - jnp.max(s, axis=-1, keepdims=True))
            outs.append(_dot(e.astype(BF16), vp) / jnp.sum(e, axis=-1, keepdims=True))
        o_ref[:, sl] = jnp.where(masks[0], outs[0], outs[1]).astype(BF16)


def _attn_context(zf, zb, nb, seq, t_total):
    return pl.pallas_call(
        _attn_ctx_kernel,
        grid=(nb,),
        in_specs=[
            pl.BlockSpec((None, seq, CHUNK_W), lambda b: (3, b, 0)),
            pl.BlockSpec((None, seq, CHUNK_W), lambda b: (1, b, 0)),
            pl.BlockSpec((None, seq, CHUNK_W), lambda b: (2, b, 0)),
        ],
        out_specs=pl.BlockSpec((seq, CHUNK_W), lambda b: (b, 0)),
        out_shape=jax.ShapeDtypeStruct((t_total, CHUNK_W), BF16),
        compiler_params=_cparams(("arbitrary",)), name="attention_context",
    )(zb, zf, zf)


def _na_kernel(rows_total, kr, q_ref, kl_ref, vl_ref, kc_ref, vc_ref, bias_ref, _alias, o_ref):
    r = pl.program_id(1)
    rs = jnp.clip(r - kr // 2, 0, rows_total - kr)
    krows = pl.ds(pl.multiple_of(rs * GRID_W, GRID_W), kr * GRID_W)
    lane = lax.broadcasted_iota(jnp.int32, (1, LANES), 1)
    masks = (lane < NA_DH, lane >= NA_DH)
    scale = NA_DH ** -0.5
    for p in range(NA_HEADS // 2):
        sl = slice(p * LANES, (p + 1) * LANES)
        qp = q_ref[:, sl]
        klp = kl_ref[krows, sl].astype(BF16)
        vlp = vl_ref[krows, sl].astype(BF16)
        kcp = kc_ref[:, sl].astype(BF16)
        vcp = vc_ref[:, sl].astype(BF16)
        outs = []
        for hh in range(2):
            qm = jnp.where(masks[hh], qp, jnp.zeros_like(qp))
            s_loc = _dot_nt(qm, klp) * scale + bias_ref[2 * p + hh]
            s_ctx = _dot_nt(qm, kcp) * scale
            m = jnp.maximum(jnp.max(s_loc, axis=-1, keepdims=True), jnp.max(s_ctx, axis=-1, keepdims=True))
            e_loc = jnp.exp(s_loc - m)
            e_ctx = jnp.exp(s_ctx - m)
            den = jnp.sum(e_loc, axis=-1, keepdims=True) + jnp.sum(e_ctx, axis=-1, keepdims=True)
            outs.append((_dot(e_loc.astype(BF16), vlp) + _dot(e_ctx.astype(BF16), vcp)) / den)
        o_ref[:, sl] = jnp.where(masks[0], outs[0], outs[1]).astype(BF16)


def _na_bias_table(rpb, rows_total, kr):
    r = np.arange(rows_total)
    rs = np.clip(r - kr // 2, 0, rows_total - kr)
    dr = rs[:, None] + np.arange(kr)[None, :] - r[:, None]
    qc = np.arange(GRID_W)
    kc = np.arange(GRID_W)
    cs = np.clip(qc - NA_WIN_C // 2, 0, GRID_W - NA_WIN_C)
    valid = (kc[None, :] >= cs[:, None]) & (kc[None, :] < cs[:, None] + NA_WIN_C)
    dc = np.clip(kc[None, :] - qc[:, None], -(NA_WIN_C - 1), NA_WIN_C - 1)
    ri = (dr + NA_WIN_R - 1)[:, None, :, None]
    ci = (dc + NA_WIN_C - 1)[None, :, None, :]
    b = rpb[:, ri, ci]
    b = jnp.where(valid[None, None, :, None, :], b.astype(F32), NEG_BIG)
    return b.transpose(1, 0, 2, 3, 4).reshape(rows_total, rpb.shape[0], GRID_W, kr * GRID_W)


def _attn_latent(zf, zb, cache_k, cache_v, layer, bias, att_prev, nb, seq, tok0):
    rows_total = seq // GRID_W
    kr = min(NA_WIN_R, rows_total)
    q0 = tok0 // GRID_W
    b0 = tok0 // seq
    past = cache_k.shape[2]
    ck = cache_k.reshape(cache_k.shape[0], cache_k.shape[1], past, NA_HEADS * NA_DH)
    cv = cache_v.reshape(ck.shape)
    return pl.pallas_call(
        functools.partial(_na_kernel, rows_total, kr),
        grid=(nb, rows_total),
        in_specs=[
            pl.BlockSpec((None, GRID_W, CHUNK_W), lambda b, r: (3, q0 + b * rows_total + r, 0)),
            pl.BlockSpec((None, seq, CHUNK_W), lambda b, r: (1, b0 + b, 0)),
            pl.BlockSpec((None, seq, CHUNK_W), lambda b, r: (2, b0 + b, 0)),
            pl.BlockSpec((None, None, past, CHUNK_W), lambda b, r: (b, layer, 0, 0)),
            pl.BlockSpec((None, None, past, CHUNK_W), lambda b, r: (b, layer, 0, 0)),
            pl.BlockSpec((None, NA_HEADS, GRID_W, kr * GRID_W), lambda b, r: (r, 0, 0, 0)),
            pl.BlockSpec(memory_space=pl.ANY),
        ],
        out_specs=pl.BlockSpec((GRID_W, CHUNK_W), lambda b, r: (q0 + b * rows_total + r, 0)),
        out_shape=jax.ShapeDtypeStruct(att_prev.shape, BF16),
        input_output_aliases={6: 0},
        compiler_params=_cparams(("arbitrary", "arbitrary")), name="attention_latent",
    )(zb, zf, zf, ck, cv, bias, att_prev)


CONV_PAD = 16
CONV_ROWS = 64


def _conv_kernel(seq, aliased, *refs):
    if aliased:
        a_ref, gt_ref, w_ref, cb_ref, lg_ref, lb_ref, _alias, o_ref, z_s = refs
    else:
        a_ref, gt_ref, w_ref, cb_ref, lg_ref, lb_ref, o_ref, z_s = refs
    z_s[0:CONV_PAD, :] = jnp.zeros((CONV_PAD, CONV_CH), F32)
    z_s[CONV_PAD + seq:2 * CONV_PAD + seq, :] = jnp.zeros((CONV_PAD, CONV_CH), F32)
    z_s[CONV_PAD:CONV_PAD + seq, :] = a_ref[...].astype(F32) * _sigmoid(gt_ref[...].astype(F32))
    off = CONV_PAD - CONV_K // 2
    for t0 in range(0, seq, CONV_ROWS):
        acc = jnp.zeros((CONV_ROWS, CONV_CH), F32)
        for k in range(CONV_K):
            acc = acc + z_s[t0 + off + k:t0 + off + k + CONV_ROWS, :] * w_ref[k:k + 1, :]
        y = _layer_norm(acc + cb_ref[...], lg_ref[...], lb_ref[...])
        o_ref[t0:t0 + CONV_ROWS, :] = _silu(y).astype(BF16)


def _conv_branch(zb, cv_w, cv_b, ln_g, ln_b, *, seq, nb, tok0, t_total, prev=None):
    b0 = tok0 // seq
    in_specs = [
        pl.BlockSpec((None, seq, CHUNK_W), lambda b: (4, b0 + b, 0)),
        pl.BlockSpec((None, seq, CHUNK_W), lambda b: (5, b0 + b, 0)),
        pl.BlockSpec((CONV_K, CONV_CH), lambda b: (0, 0)),
        pl.BlockSpec((1, CONV_CH), lambda b: (0, 0)),
        pl.BlockSpec((1, CONV_CH), lambda b: (0, 0)),
        pl.BlockSpec((1, CONV_CH), lambda b: (0, 0)),
    ]
    args = [zb, zb, cv_w, cv_b, ln_g, ln_b]
    aliases = {}
    if prev is not None:
        in_specs.append(pl.BlockSpec(memory_space=pl.ANY))
        args.append(prev)
        aliases = {6: 0}
    return pl.pallas_call(
        functools.partial(_conv_kernel, seq, prev is not None),
        grid=(nb,), in_specs=in_specs,
        out_specs=pl.BlockSpec((seq, CONV_CH), lambda b: (b0 + b, 0)),
        out_shape=jax.ShapeDtypeStruct((t_total, CONV_CH), BF16),
        scratch_shapes=[pltpu.VMEM((seq + 2 * CONV_PAD, CONV_CH), F32)],
        input_output_aliases=aliases,
        compiler_params=_cparams(("arbitrary",)), name="conv_module_%d" % seq,
    )(*args)


def _merge_kernel(alpha, with_router, *refs):
    if with_router:
        (ya_ref, yb_ref, yc_ref, mg_ref, x_ref, g1_ref, sc2_ref, sh2_ref, wa_ref, wb_ref, wc_ref, wo_ref,
         lg_ref, lb_ref, wr_ref, x1_ref, h2_ref, gates_ref) = refs
    else:
        (ya_ref, yb_ref, yc_ref, mg_ref, x_ref, g1_ref, sc2_ref, sh2_ref, wa_ref, wb_ref, wc_ref, wo_ref,
         lg_ref, lb_ref, x1_ref, h2_ref) = refs
    halves = []
    for n in range(2):
        m = None
        for j, (y_ref, w_ref) in enumerate(((ya_ref, wa_ref), (yb_ref, wb_ref), (yc_ref, wc_ref))):
            proj = _dot(y_ref[...], w_ref[:, n * CHUNK_W:(n + 1) * CHUNK_W])
            term = _sigmoid(mg_ref[2 * j + n].astype(F32)) * proj
            m = term if m is None else m + term
        halves.append(m.astype(BF16))
    mix = _dot(halves[0], wo_ref[0:CHUNK_W, :]) + _dot(halves[1], wo_ref[CHUNK_W:2 * CHUNK_W, :])
    x1 = _layer_norm(alpha * x_ref[...] + g1_ref[...] * mix, lg_ref[...], lb_ref[...])
    x1_ref[...] = x1
    h2 = x1 * (1.0 + sc2_ref[...]) + sh2_ref[...]
    h2_ref[...] = h2.astype(BF16)
    if with_router:
        logits = _dot_hi(h2, wr_ref[...])
        lane = lax.broadcasted_iota(jnp.int32, logits.shape, 1)
        lg = jnp.where(lane < N_EXPERTS, logits, -jnp.inf)
        m1 = jnp.max(lg, axis=-1, keepdims=True)
        i1 = jnp.min(jnp.where(lg == m1, lane, LANES), axis=-1, keepdims=True)
        lg2 = jnp.where(lane == i1, -jnp.inf, lg)
        m2 = jnp.max(lg2, axis=-1, keepdims=True)
        i2 = jnp.min(jnp.where(lg2 == m2, lane, LANES), axis=-1, keepdims=True)
        e2 = jnp.exp(m2 - m1)
        w1 = 1.0 / (1.0 + e2)
        gates_ref[...] = jnp.where(lane == i1, w1, 0.0) + jnp.where(lane == i2, e2 * w1, 0.0)


def _merge(ya, yb, yc, zb, x_all, mod_l, wa, wb, wc, wo, ln_g, ln_b, alpha, tm, row_of_tile, w_router=None):
    t, d = x_all.shape
    full = lambda shape: pl.BlockSpec(shape, lambda i: tuple(0 for _ in shape))
    in_specs = [
        pl.BlockSpec((tm, CHUNK_W), lambda i: (i, 0)),
        pl.BlockSpec((tm, CHUNK_W), lambda i: (i, 0)),
        pl.BlockSpec((tm, CHUNK_W), lambda i: (i, 0)),
        pl.BlockSpec((6, tm, CHUNK_W), lambda i: (1, i, 0)),
        pl.BlockSpec((tm, d), lambda i: (i, 0)),
        _mod_spec(2, row_of_tile), _mod_spec(4, row_of_tile), _mod_spec(3, row_of_tile),
        full(wa.shape), full(wb.shape), full(wc.shape), full(wo.shape), full((1, d)), full((1, d)),
    ]
    args = [ya, yb, yc, zb, x_all, mod_l, mod_l, mod_l, wa, wb, wc, wo, ln_g, ln_b]
    out_specs = [pl.BlockSpec((tm, d), lambda i: (i, 0)), pl.BlockSpec((tm, d), lambda i: (i, 0))]
    out_shape = [jax.ShapeDtypeStruct((t, d), F32), jax.ShapeDtypeStruct((t, d), BF16)]
    if w_router is not None:
        in_specs.append(full(w_router.shape))
        args.append(w_router)
        out_specs.append(pl.BlockSpec((tm, LANES), lambda i: (i, 0)))
        out_shape.append(jax.ShapeDtypeStruct((t, LANES), F32))
    return pl.pallas_call(
        functools.partial(_merge_kernel, alpha, w_router is not None),
        grid=(t // tm,), in_specs=in_specs, out_specs=out_specs, out_shape=out_shape,
        compiler_params=_cparams(("arbitrary",)), name="merge",
    )(*args)


def _ffn_kernel(alpha, gated, *refs):
    if gated:
        h_ref, x1_ref, g2_ref, gate_ref, wg_ref, wu_ref, wd_ref, lg_ref, lb_ref, o_ref, acc_s = refs
    else:
        h_ref, x1_ref, g2_ref, wg_ref, wu_ref, wd_ref, lg_ref, lb_ref, o_ref, acc_s = refs
    e, f = pl.program_id(1), pl.program_id(2)

    @pl.when((e == 0) & (f == 0))
    def _():
        acc_s[...] = jnp.zeros_like(acc_s)

    h = h_ref[...]
    act = _silu(_dot(h, wg_ref[...].astype(BF16))) * _dot(h, wu_ref[...].astype(BF16))
    y = _dot(act.astype(BF16), wd_ref[...].astype(BF16))
    if gated:
        y = y * gate_ref[...]
    acc_s[...] += y

    @pl.when((e == pl.num_programs(1) - 1) & (f == pl.num_programs(2) - 1))
    def _():
        o_ref[...] = _layer_norm(alpha * x1_ref[...] + g2_ref[...] * acc_s[...], lg_ref[...], lb_ref[...])


def _channel_mixer(h2, x1, mod_l, wg, wu, wd, ln_g, ln_b, alpha, tm, tf, row_of_tile, gates=None):
    t, d = x1.shape
    n_e, _, ff = wg.shape
    mod_spec = pl.BlockSpec((None, None, 1, d), lambda i, e, f: (row_of_tile(i), 5, 0, 0))
    in_specs = [
        pl.BlockSpec((tm, d), lambda i, e, f: (i, 0)),
        pl.BlockSpec((tm, d), lambda i, e, f: (i, 0)),
        mod_spec,
    ]
    args = [h2, x1, mod_l]
    if gates is not None:
        in_specs.append(pl.BlockSpec((None, tm, 1), lambda i, e, f: (e, i, 0)))
        args.append(gates)
    in_specs += [
        pl.BlockSpec((None, d, tf), lambda i, e, f: (e, 0, f)),
        pl.BlockSpec((None, d, tf), lambda i, e, f: (e, 0, f)),
        pl.BlockSpec((None, tf, d), lambda i, e, f: (e, f, 0)),
        pl.BlockSpec((1, d), lambda i, e, f: (0, 0)),
        pl.BlockSpec((1, d), lambda i, e, f: (0, 0)),
    ]
    args += [wg, wu, wd, ln_g, ln_b]
    return pl.pallas_call(
        functools.partial(_ffn_kernel, alpha, gates is not None),
        grid=(t // tm, n_e, ff // tf), in_specs=in_specs,
        out_specs=pl.BlockSpec((tm, d), lambda i, e, f: (i, 0)),
        out_shape=jax.ShapeDtypeStruct((t, d), F32),
        scratch_shapes=[pltpu.VMEM((tm, d), F32)],
        compiler_params=_cparams(("arbitrary", "arbitrary", "arbitrary")), name="channel_mixer",
    )(*args)


def _rope_tables(seq):
    half = GLA_DK // 2
    n = half // 2
    t = np.arange(seq)
    inv = ROPE_BASE ** (-np.arange(n, dtype=np.float64) / n)
    cos = np.zeros((seq, GLA_DK), np.float64)
    sin = np.zeros((seq, GLA_DK), np.float64)
    for a, pos in enumerate((t // GRID_W, t % GRID_W)):
        ang = pos[:, None].astype(np.float64) * inv[None, :]
        base = a * half
        cos[:, base:base + n] = np.cos(ang)
        cos[:, base + n:base + half] = np.cos(ang)
        sin[:, base:base + n] = -np.sin(ang)
        sin[:, base + n:base + half] = np.sin(ang)
    tile = lambda m: jnp.asarray(np.tile(m, (1, GLA_HEADS)), F32)
    return tile(cos), tile(sin)


def _chunk_w_in(w):
    gda = jnp.pad(w[:, 1536:1568], ((0, 0), (0, CHUNK_W - 2 * GLA_RANK)))
    starts = (2080, 2592, 0, 512, 1024, 1568, 3104, 3616, 4128, 4640, 5152, 5664, 6176, 6688)
    return jnp.stack([gda] + [w[:, s:s + CHUNK_W] for s in starts]).astype(BF16)


def kernel(x_prompt, x_sample, cache_na_k, cache_na_v, state_gla, c, c_ctx, w_mod, b_mod, w_in, gla_w_gup, gla_b_g, gla_norm, w_br_gla, na_rpb, w_br_na, cv_w, cv_b, cv_ln_g, cv_ln_b, w_br_cv, w_out, ln_g, ln_b, ffd_w_gate, ffd_w_up, ffd_w_down, moe_w_router, moe_w_gate, moe_w_up, moe_w_down):
    nb_c, l_c, d = x_prompt.shape
    nb_l, l_l, _ = x_sample.shape
    depth = w_mod.shape[0]
    t_ctx, t_lat = nb_c * l_c, nb_l * l_l
    t_all = t_ctx + t_lat
    alpha = (2 * depth) ** 0.25
    tm = 512
    row_of_tile = _row_of_tile_fn(t_ctx, l_l, tm)

    x_all = jnp.concatenate([x_prompt.reshape(t_ctx, d), x_sample.reshape(t_lat, d)], axis=0)
    cond8 = jnp.concatenate([c_ctx[None, :], c, jnp.zeros((8 - 1 - nb_l, d), F32)], axis=0)
    mod = _modulation(cond8, w_mod, b_mod).reshape(depth, 8, 6, 1, d)
    rope = _rope_tables(l_l)
    rows_lat = l_l // GRID_W
    kr = min(NA_WIN_R, rows_lat)
    s0_all = state_gla.reshape(nb_l, depth, 2, 2, LANES, GLA_DV)

    ks, vs, ss = [], [], []
    for l in range(depth):
        mod_l = mod[l]
        zf, zb = _input_projection(x_all, mod_l, _chunk_w_in(w_in[l]), tm, row_of_tile)

        wg_blk = jnp.zeros((LANES, CHUNK_W), F32)
        wg_blk = wg_blk.at[0:GLA_RANK, 0:256].set(gla_w_gup[l, 0]).at[GLA_RANK:2 * GLA_RANK, 256:512].set(gla_w_gup[l, 1])
        bg = gla_b_g[l].reshape(1, CHUNK_W)
        gn = gla_norm[l].reshape(1, CHUNK_W)
        ya, s_fin = _gla_branch(zf, zb, wg_blk, bg, gn, latent=False, seq=l_c, nb=nb_c, tok0=0, t_total=t_all)
        ya = _gla_branch(zf, zb, wg_blk, bg, gn, latent=True, seq=l_l, nb=nb_l, tok0=t_ctx, t_total=t_all,
                         rope=rope, s0=s0_all[:, l], ya_prev=ya)

        yb = _attn_context(zf, zb, nb_c, l_c, t_all)
        bias = _na_bias_table(na_rpb[l], rows_lat, kr)
        yb = _attn_latent(zf, zb, cache_na_k, cache_na_v, l, bias, yb, nb_l, l_l, t_ctx)

        conv_args = (zb, cv_w[l], cv_b[l][None, :], cv_ln_g[l][None, :], cv_ln_b[l][None, :])
        yc = _conv_branch(*conv_args, seq=l_c, nb=nb_c, tok0=0, t_total=t_all)
        yc = _conv_branch(*conv_args, seq=l_l, nb=nb_l, tok0=t_ctx, t_total=t_all, prev=yc)

        moe = l % 2 == 1
        w_router = None
        if moe:
            w_router = jnp.pad(moe_w_router[l // 2], ((0, 0), (0, LANES - N_EXPERTS)))
        merged = _merge(ya, yb, yc, zb, x_all, mod_l,
                        w_br_gla[l].astype(BF16), w_br_na[l].astype(BF16), w_br_cv[l].astype(BF16),
                        w_out[l].astype(BF16), ln_g[l, 0][None, :], ln_b[l, 0][None, :], alpha, tm, row_of_tile,
                        w_router)
        ln2 = (ln_g[l, 1][None, :], ln_b[l, 1][None, :])
        if moe:
            x1, h2, gates = merged
            gates_t = gates[:, :N_EXPERTS].T.reshape(N_EXPERTS, t_all, 1)
            x_all = _channel_mixer(h2, x1, mod_l, moe_w_gate[l // 2], moe_w_up[l // 2], moe_w_down[l // 2],
                                   *ln2, alpha, 1024, 512, _row_of_tile_fn(t_ctx, l_l, 1024), gates_t)
        else:
            x1, h2 = merged
            i = l // 2
            x_all = _channel_mixer(h2, x1, mod_l, ffd_w_gate[i][None], ffd_w_up[i][None], ffd_w_down[i][None],
                                   *ln2, alpha, 1024, 256, _row_of_tile_fn(t_ctx, l_l, 1024))

        ks.append(zf[1, :t_ctx].reshape(nb_c, l_c, NA_HEADS, NA_DH))
        vs.append(zf[2, :t_ctx].reshape(nb_c, l_c, NA_HEADS, NA_DH))
        ss.append(s_fin.reshape(nb_c, 2, GLA_HEADS, GLA_DK, GLA_DV))

    y_prompt = x_all[:t_ctx].reshape(nb_c, l_c, d)
    y_sample = x_all[t_ctx:].reshape(nb_l, l_l, d)
    return (y_prompt, y_sample, jnp.stack(ks, 1), jnp.stack(vs, 1), jnp.stack(ss, 1))


def _row_of_tile_fn(t_ctx, l_lat, tm):
    n_ctx = t_ctx // tm

    def row_of_tile(i):
        return jnp.where(i < n_ctx, 0, 1 + (i - n_ctx) // (l_lat // tm))

    return row_of_tile
```

```python
import functools

import numpy as np
import jax
import jax.numpy as jnp
from jax import lax
from jax.experimental import pallas as pl
from jax.experimental.pallas import tpu as pltpu

F32 = jnp.float32
BF16 = jnp.bfloat16

GRID_W = 64
GLA_HEADS = 4
GLA_DK = 64
GLA_DV = 128
GLA_RANK = 16
GLA_TAU = 16.0
GLA_CHUNK = 64
NA_HEADS = 8
NA_DH = 64
NA_WIN_R = 8
NA_WIN_C = 16
CONV_CH = 512
CONV_K = 31
N_EXPERTS = 8
ROPE_BASE = 10000.0
LN_EPS = 1e-5

D_MODEL = 1024
LANES = 128
CHUNK_W = 512
N_F32_CHUNKS = 3
N_BF16_CHUNKS = 12
NEG_BIG = -1e30
VMEM_LIMIT = 56 * 1024 * 1024


def _cparams(sem, vmem=VMEM_LIMIT):
    return pltpu.CompilerParams(dimension_semantics=sem, vmem_limit_bytes=vmem)


def _dot(a, b):
    return jnp.dot(a, b, preferred_element_type=F32)


def _dot_hi(a, b):
    return jnp.dot(a, b, preferred_element_type=F32, precision=lax.Precision.HIGHEST)


def _dot_nt(a, b):
    return lax.dot_general(a, b, (((1,), (1,)), ((), ())), preferred_element_type=F32)


def _sigmoid(x):
    return 1.0 / (1.0 + jnp.exp(-x))


def _silu(x):
    return x * _sigmoid(x)


def _layer_norm(x, g, b):
    mu = jnp.mean(x, axis=-1, keepdims=True)
    xc = x - mu
    var = jnp.mean(xc * xc, axis=-1, keepdims=True)
    return xc * lax.rsqrt(var + LN_EPS) * g + b


def _mod_kernel(c_ref, w_ref, b_ref, o_ref):
    c = c_ref[...]
    s = _silu(c).astype(BF16)
    o_ref[...] = _dot(s, w_ref[...].astype(BF16)) + b_ref[...]


def _modulation(cond8, w_mod, b_mod):
    depth, d, n = w_mod.shape
    tn = 1024
    return pl.pallas_call(
        _mod_kernel,
        grid=(depth, n // tn),
        in_specs=[
            pl.BlockSpec((8, d), lambda l, j: (0, 0)),
            pl.BlockSpec((None, d, tn), lambda l, j: (l, 0, j)),
            pl.BlockSpec((None, 1, tn), lambda l, j: (l, 0, j)),
        ],
        out_specs=pl.BlockSpec((None, 8, tn), lambda l, j: (l, 0, j)),
        out_shape=jax.ShapeDtypeStruct((depth, 8, n), F32),
        compiler_params=_cparams(("arbitrary", "arbitrary")),
        name="modulation",
    )(cond8, w_mod, b_mod.reshape(depth, 1, n))


def _inproj_kernel(x_ref, sc_ref, sh_ref, w_ref, zf_ref, zb_ref, h_ref):
    h_ref[...] = (x_ref[...] * (1.0 + sc_ref[...]) + sh_ref[...]).astype(BF16)

    def f32_chunk(j, c):
        zf_ref[j] = _dot(h_ref[...], w_ref[j])
        return c

    lax.fori_loop(0, N_F32_CHUNKS, f32_chunk, 0)

    def bf16_chunk(j, c):
        zb_ref[j] = _dot(h_ref[...], w_ref[N_F32_CHUNKS + j]).astype(BF16)
        return c

    lax.fori_loop(0, N_BF16_CHUNKS, bf16_chunk, 0)


def _mod_spec(piece, row_of_tile):
    return pl.BlockSpec((None, None, 1, D_MODEL), lambda i: (row_of_tile(i), piece, 0, 0))


def _input_projection(x_all, mod_l, w_chunks, tm, row_of_tile):
    t, d = x_all.shape
    nch = N_F32_CHUNKS + N_BF16_CHUNKS
    return pl.pallas_call(
        _inproj_kernel,
        grid=(t // tm,),
        in_specs=[
            pl.BlockSpec((tm, d), lambda i: (i, 0)),
            _mod_spec(1, row_of_tile),
            _mod_spec(0, row_of_tile),
            pl.BlockSpec((nch, d, CHUNK_W), lambda i: (0, 0, 0), pipeline_mode=pl.Buffered(1)),
        ],
        out_specs=[
            pl.BlockSpec((N_F32_CHUNKS, tm, CHUNK_W), lambda i: (0, i, 0)),
            pl.BlockSpec((N_BF16_CHUNKS, tm, CHUNK_W), lambda i: (0, i, 0)),
        ],
        out_shape=[
            jax.ShapeDtypeStruct((N_F32_CHUNKS, t, CHUNK_W), F32),
            jax.ShapeDtypeStruct((N_BF16_CHUNKS, t, CHUNK_W), BF16),
        ],
        scratch_shapes=[pltpu.VMEM((tm, d), BF16)],
        compiler_params=_cparams(("arbitrary",)),
        name="input_projection",
    )(x_all, mod_l, mod_l, w_chunks)


def _gla_kernel(latent, seq, *refs):
    if latent:
        (qk_ref, v_ref, ra_ref, gd_ref, wg_ref, bg_ref, gn_ref, cos_ref, sin_ref, s0_ref, _alias,
         ya_ref, q_s, k_s, g_s, acc_s, st_s) = refs
    else:
        (qk_ref, v_ref, ra_ref, gd_ref, wg_ref, bg_ref, gn_ref,
         ya_ref, sfin_ref, q_s, k_s, g_s, acc_s, st_s) = refs
    c = GLA_CHUNK
    n_chunks = seq // c
    qkw = GLA_HEADS * GLA_DK

    q = qk_ref[:, 0:qkw].astype(F32) * (GLA_DK ** -0.5)
    k = qk_ref[:, qkw:2 * qkw].astype(F32)
    if latent:
        lane = lax.broadcasted_iota(jnp.int32, (1, qkw), 1)
        first_half = (lane % 32) < 16

        def rope(x):
            swapped = jnp.where(first_half, pltpu.roll(x, qkw - 16, 1), pltpu.roll(x, 16, 1))
            return x * cos_ref[...] + swapped * sin_ref[...]

        q, k = rope(q), rope(k)
    q_s[...] = q
    k_s[...] = k

    glog = _dot_hi(gd_ref[...], wg_ref[...]) + bg_ref[...]
    g_s[...] = (jnp.minimum(glog, 0.0) - jnp.log(1.0 + jnp.exp(-jnp.abs(glog)))) * (1.0 / GLA_TAU)
    acc_s[...] = jnp.zeros_like(acc_s)
    if latent:
        st_s[...] = s0_ref[...]
    else:
        st_s[...] = jnp.zeros_like(st_s)

    row = lax.broadcasted_iota(jnp.int32, (c, c), 0)
    col = lax.broadcasted_iota(jnp.int32, (c, c), 1)
    keep = (col <= row, col >= row)
    tri = (keep[0].astype(F32), keep[1].astype(F32))
    lane128 = lax.broadcasted_iota(jnp.int32, (1, LANES), 1)
    head_mask = (lane128 < GLA_DK, lane128 >= GLA_DK)
    srow = lax.broadcasted_iota(jnp.int32, (LANES, GLA_DV), 0)

    def chunk_step(i, carry):
        for d in range(2):
            ci = i if d == 0 else n_chunks - 1 - i
            rows = pl.ds(pl.multiple_of(ci * c, c), c)
            gc = g_s[rows, d * qkw:(d + 1) * qkw]
            cum = _dot_hi(tri[d], gc)
            last = cum[c - 1:c, :] if d == 0 else cum[0:1, :]
            ref = 0.5 * last
            qe = q_s[rows, :] * jnp.exp(cum - ref)
            ke = k_s[rows, :] * jnp.exp(ref - cum)
            qe2 = qe * jnp.exp(ref)
            ke2 = ke * jnp.exp(last - ref)
            for p in range(2):
                sl = slice(p * LANES, (p + 1) * LANES)
                ke_p = ke[:, sl].astype(BF16)
                ke2_t = ke2[:, sl].T.astype(BF16)
                s_pair = st_s[d, p]
                s_pair_b = s_pair.astype(BF16)
                upd = []
                for hh in range(2):
                    h = 2 * p + hh
                    vsl = slice(h * GLA_DV, (h + 1) * GLA_DV)
                    vh = v_ref[rows, vsl]
                    a = jnp.where(head_mask[hh], qe[:, sl], 0.0).astype(BF16)
                    a2 = jnp.where(head_mask[hh], qe2[:, sl], 0.0).astype(BF16)
                    att = jnp.where(keep[d], _dot_nt(a, ke_p), 0.0).astype(BF16)
                    acc_s[rows, vsl] += _dot(att, vh) + _dot(a2, s_pair_b)
                    upd.append(_dot(ke2_t, vh))
                decay = jnp.exp(jnp.broadcast_to(last[:, sl], (LANES, LANES))).T
                st_s[d, p] = decay * s_pair + jnp.where(srow < GLA_DK, upd[0], upd[1])
        return carry

    lax.fori_loop(0, n_chunks, chunk_step, 0)

    def epilogue(i, carry):
        rows = pl.ds(pl.multiple_of(i * c, c), c)
        for h in range(GLA_HEADS):
            vsl = slice(h * GLA_DV, (h + 1) * GLA_DV)
            o = acc_s[rows, vsl]
            o = o * lax.rsqrt(jnp.mean(o * o, axis=-1, keepdims=True) + LN_EPS) * gn_ref[:, vsl]
            ya_ref[rows, vsl] = (o * _silu(ra_ref[rows, vsl].astype(F32))).astype(BF16)
        return carry

    lax.fori_loop(0, n_chunks, epilogue, 0)
    if not latent:
        sfin_ref[...] = st_s[...]


def _gla_branch(zf, zb, wg_blk, bg, gn, *, latent, seq, nb, tok0, t_total, rope=None, s0=None, ya_prev=None):
    b0 = tok0 // seq
    qkw = GLA_HEADS * GLA_DK
    in_specs = [
        pl.BlockSpec((None, seq, CHUNK_W), lambda b: (0, b0 + b, 0)),
        pl.BlockSpec((None, seq, CHUNK_W), lambda b: (1, b0 + b, 0)),
        pl.BlockSpec((None, seq, CHUNK_W), lambda b: (2, b0 + b, 0)),
        pl.BlockSpec((None, seq, LANES), lambda b: (0, b0 + b, 0)),
        pl.BlockSpec((LANES, CHUNK_W), lambda b: (0, 0)),
        pl.BlockSpec((1, CHUNK_W), lambda b: (0, 0)),
        pl.BlockSpec((1, CHUNK_W), lambda b: (0, 0)),
    ]
    args = [zb, zb, zb, zf, wg_blk, bg, gn]
    scratch = [
        pltpu.VMEM((seq, qkw), F32), pltpu.VMEM((seq, qkw), F32), pltpu.VMEM((seq, CHUNK_W), F32),
        pltpu.VMEM((seq, CHUNK_W), F32), pltpu.VMEM((2, 2, LANES, GLA_DV), F32),
    ]
    ya_shape = jax.ShapeDtypeStruct((t_total, CHUNK_W), BF16)
    ya_spec = pl.BlockSpec((seq, CHUNK_W), lambda b: (b0 + b, 0))
    if latent:
        cos_t, sin_t = rope
        in_specs += [
            pl.BlockSpec((seq, qkw), lambda b: (0, 0)),
            pl.BlockSpec((seq, qkw), lambda b: (0, 0)),
            pl.BlockSpec((None, 2, 2, LANES, GLA_DV), lambda b: (b, 0, 0, 0, 0)),
            pl.BlockSpec(memory_space=pl.ANY),
        ]
        args += [cos_t, sin_t, s0, ya_prev]
        return pl.pallas_call(
            functools.partial(_gla_kernel, True, seq),
            grid=(nb,), in_specs=in_specs, out_specs=ya_spec, out_shape=ya_shape,
            scratch_shapes=scratch, input_output_aliases={len(args) - 1: 0},
            compiler_params=_cparams(("arbitrary",)), name="gla_latent",
        )(*args)
    return pl.pallas_call(
        functools.partial(_gla_kernel, False, seq),
        grid=(nb,), in_specs=in_specs,
        out_specs=[ya_spec, pl.BlockSpec((None, 2, 2, LANES, GLA_DV), lambda b: (b, 0, 0, 0, 0))],
        out_shape=[ya_shape, jax.ShapeDtypeStruct((nb, 2, 2, LANES, GLA_DV), F32)],
        scratch_shapes=scratch,
        compiler_params=_cparams(("arbitrary",)), name="gla_context",
    )(*args)


def _attn_ctx_kernel(q_ref, k_ref, v_ref, o_ref):
    lane = lax.broadcasted_iota(jnp.int32, (1, LANES), 1)
    masks = (lane < NA_DH, lane >= NA_DH)
    scale = NA_DH ** -0.5
    for p in range(NA_HEADS // 2):
        sl = slice(p * LANES, (p + 1) * LANES)
        qp = q_ref[:, sl]
        kp = k_ref[:, sl].astype(BF16)
        vp = v_ref[:, sl].astype(BF16)
        outs = []
        for hh in range(2):
            qm = jnp.where(masks[hh], qp, jnp.zeros_like(qp))
            s = _dot_nt(qm, kp) * scale
            e = jnp.exp(s - jnp.max(s, axis=-1, keepdims=True))
            outs.append(_dot(e.astype(BF16), vp) / jnp.sum(e, axis=-1, keepdims=True))
        o_ref[:, sl] = jnp.where(masks[0], outs[0], outs[1]).astype(BF16)


def _attn_context(zf, zb, nb, seq, t_total):
    return pl.pallas_call(
        _attn_ctx_kernel,
        grid=(nb,),
        in_specs=[
            pl.BlockSpec((None, seq, CHUNK_W), lambda b: (3, b, 0)),
            pl.BlockSpec((None, seq, CHUNK_W), lambda b: (1, b, 0)),
            pl.BlockSpec((None, seq, CHUNK_W), lambda b: (2, b, 0)),
        ],
        out_specs=pl.BlockSpec((seq, CHUNK_W), lambda b: (b, 0)),
        out_shape=jax.ShapeDtypeStruct((t_total, CHUNK_W), BF16),
        compiler_params=_cparams(("arbitrary",)), name="attention_context",
    )(zb, zf, zf)


def _na_kernel(rows_total, kr, q_ref, kl_ref, vl_ref, kc_ref, vc_ref, bias_ref, _alias, o_ref):
    r = pl.program_id(1)
    rs = jnp.clip(r - kr // 2, 0, rows_total - kr)
    krows = pl.ds(pl.multiple_of(rs * GRID_W, GRID_W), kr * GRID_W)
    lane = lax.broadcasted_iota(jnp.int32, (1, LANES), 1)
    masks = (lane < NA_DH, lane >= NA_DH)
    scale = NA_DH ** -0.5
    for p in range(NA_HEADS // 2):
        sl = slice(p * LANES, (p + 1) * LANES)
        qp = q_ref[:, sl]
        klp = kl_ref[krows, sl].astype(BF16)
        vlp = vl_ref[krows, sl].astype(BF16)
        kcp = kc_ref[:, sl].astype(BF16)
        vcp = vc_ref[:, sl].astype(BF16)
        outs = []
        for hh in range(2):
            qm = jnp.where(masks[hh], qp, jnp.zeros_like(qp))
            s_loc = _dot_nt(qm, klp) * scale + bias_ref[2 * p + hh]
            s_ctx = _dot_nt(qm, kcp) * scale
            m = jnp.maximum(jnp.max(s_loc, axis=-1, keepdims=True), jnp.max(s_ctx, axis=-1, keepdims=True))
            e_loc = jnp.exp(s_loc - m)
            e_ctx = jnp.exp(s_ctx - m)
            den = jnp.sum(e_loc, axis=-1, keepdims=True) + jnp.sum(e_ctx, axis=-1, keepdims=True)
            outs.append((_dot(e_loc.astype(BF16), vlp) + _dot(e_ctx.astype(BF16), vcp)) / den)
        o_ref[:, sl] = jnp.where(masks[0], outs[0], outs[1]).astype(BF16)


def _na_bias_table(rpb, kr):
    heads, nr, nc = rpb.shape
    qc = np.arange(GRID_W)[:, None]
    kc = np.arange(GRID_W)[None, :]
    cs = np.clip(qc - NA_WIN_C // 2, 0, GRID_W - NA_WIN_C)
    valid = (kc >= cs) & (kc < cs + NA_WIN_C)
    period = 2 * GRID_W
    lead = GRID_W - 1 - (NA_WIN_C - 1)
    v = jnp.pad(rpb.astype(F32), ((0, 0), (0, 0), (lead, period - lead - nc)))
    skew = jnp.tile(v, (1, 1, GRID_W))[:, :, :GRID_W * (period - 1)].reshape(heads, nr, GRID_W, period - 1)
    toep = jnp.where(valid, skew[..., GRID_W - 1:], NEG_BIG)
    pats = []
    for p in range(kr):
        pats.append(jnp.concatenate([toep[:, i - p + NA_WIN_R - 1] for i in range(kr)], axis=-1))
    return jnp.stack(pats)


def _attn_latent(zf, zb, cache_k, cache_v, layer, bias, att_prev, nb, seq, tok0):
    rows_total = seq // GRID_W
    kr = min(NA_WIN_R, rows_total)
    q0 = tok0 // GRID_W
    b0 = tok0 // seq
    past = cache_k.shape[2]
    ck = cache_k.reshape(cache_k.shape[0], cache_k.shape[1], past, NA_HEADS * NA_DH)
    cv = cache_v.reshape(ck.shape)
    return pl.pallas_call(
        functools.partial(_na_kernel, rows_total, kr),
        grid=(nb, rows_total),
        in_specs=[
            pl.BlockSpec((None, GRID_W, CHUNK_W), lambda b, r: (3, q0 + b * rows_total + r, 0)),
            pl.BlockSpec((None, seq, CHUNK_W), lambda b, r: (1, b0 + b, 0)),
            pl.BlockSpec((None, seq, CHUNK_W), lambda b, r: (2, b0 + b, 0)),
            pl.BlockSpec((None, None, past, CHUNK_W), lambda b, r: (b, layer, 0, 0)),
            pl.BlockSpec((None, None, past, CHUNK_W), lambda b, r: (b, layer, 0, 0)),
            pl.BlockSpec((None, NA_HEADS, GRID_W, kr * GRID_W),
                         lambda b, r: (r - jnp.clip(r - kr // 2, 0, rows_total - kr), 0, 0, 0)),
            pl.BlockSpec(memory_space=pl.ANY),
        ],
        out_specs=pl.BlockSpec((GRID_W, CHUNK_W), lambda b, r: (q0 + b * rows_total + r, 0)),
        out_shape=jax.ShapeDtypeStruct(att_prev.shape, BF16),
        input_output_aliases={6: 0},
        compiler_params=_cparams(("arbitrary", "arbitrary")), name="attention_latent",
    )(zb, zf, zf, ck, cv, bias, att_prev)


CONV_PAD = 16
CONV_ROWS = 64


def _conv_kernel(seq, aliased, *refs):
    if aliased:
        a_ref, gt_ref, w_ref, cb_ref, lg_ref, lb_ref, _alias, o_ref, z_s = refs
    else:
        a_ref, gt_ref, w_ref, cb_ref, lg_ref, lb_ref, o_ref, z_s = refs
    z_s[0:CONV_PAD, :] = jnp.zeros((CONV_PAD, CONV_CH), F32)
    z_s[CONV_PAD + seq:2 * CONV_PAD + seq, :] = jnp.zeros((CONV_PAD, CONV_CH), F32)
    z_s[CONV_PAD:CONV_PAD + seq, :] = a_ref[...].astype(F32) * _sigmoid(gt_ref[...].astype(F32))
    off = CONV_PAD - CONV_K // 2
    for t0 in range(0, seq, CONV_ROWS):
        acc = jnp.zeros((CONV_ROWS, CONV_CH), F32)
        for k in range(CONV_K):
            acc = acc + z_s[t0 + off + k:t0 + off + k + CONV_ROWS, :] * w_ref[k:k + 1, :]
        y = _layer_norm(acc + cb_ref[...], lg_ref[...], lb_ref[...])
        o_ref[t0:t0 + CONV_ROWS, :] = _silu(y).astype(BF16)


def _conv_branch(zb, cv_w, cv_b, ln_g, ln_b, *, seq, nb, tok0, t_total, prev=None):
    b0 = tok0 // seq
    in_specs = [
        pl.BlockSpec((None, seq, CHUNK_W), lambda b: (4, b0 + b, 0)),
        pl.BlockSpec((None, seq, CHUNK_W), lambda b: (5, b0 + b, 0)),
        pl.BlockSpec((CONV_K, CONV_CH), lambda b: (0, 0)),
        pl.BlockSpec((1, CONV_CH), lambda b: (0, 0)),
        pl.BlockSpec((1, CONV_CH), lambda b: (0, 0)),
        pl.BlockSpec((1, CONV_CH), lambda b: (0, 0)),
    ]
    args = [zb, zb, cv_w, cv_b, ln_g, ln_b]
    aliases = {}
    if prev is not None:
        in_specs.append(pl.BlockSpec(memory_space=pl.ANY))
        args.append(prev)
        aliases = {6: 0}
    return pl.pallas_call(
        functools.partial(_conv_kernel, seq, prev is not None),
        grid=(nb,), in_specs=in_specs,
        out_specs=pl.BlockSpec((seq, CONV_CH), lambda b: (b0 + b, 0)),
        out_shape=jax.ShapeDtypeStruct((t_total, CONV_CH), BF16),
        scratch_shapes=[pltpu.VMEM((seq + 2 * CONV_PAD, CONV_CH), F32)],
        input_output_aliases=aliases,
        compiler_params=_cparams(("arbitrary",)), name="conv_module_%d" % seq,
    )(*args)


def _merge_kernel(alpha, with_router, *refs):
    if with_router:
        (ya_ref, yb_ref, yc_ref, mg_ref, x_ref, g1_ref, sc2_ref, sh2_ref, wa_ref, wb_ref, wc_ref, wo_ref,
         lg_ref, lb_ref, wr_ref, x1_ref, h2_ref, gates_ref) = refs
    else:
        (ya_ref, yb_ref, yc_ref, mg_ref, x_ref, g1_ref, sc2_ref, sh2_ref, wa_ref, wb_ref, wc_ref, wo_ref,
         lg_ref, lb_ref, x1_ref, h2_ref) = refs
    halves = []
    for n in range(2):
        m = None
        for j, (y_ref, w_ref) in enumerate(((ya_ref, wa_ref), (yb_ref, wb_ref), (yc_ref, wc_ref))):
            proj = _dot(y_ref[...], w_ref[:, n * CHUNK_W:(n + 1) * CHUNK_W])
            term = _sigmoid(mg_ref[2 * j + n].astype(F32)) * proj
            m = term if m is None else m + term
        halves.append(m.astype(BF16))
    mix = _dot(halves[0], wo_ref[0:CHUNK_W, :]) + _dot(halves[1], wo_ref[CHUNK_W:2 * CHUNK_W, :])
    x1 = _layer_norm(alpha * x_ref[...] + g1_ref[...] * mix, lg_ref[...], lb_ref[...])
    x1_ref[...] = x1
    h2 = x1 * (1.0 + sc2_ref[...]) + sh2_ref[...]
    h2_ref[...] = h2.astype(BF16)
    if with_router:
        logits = _dot_hi(h2, wr_ref[...])
        lane = lax.broadcasted_iota(jnp.int32, logits.shape, 1)
        lg = jnp.where(lane < N_EXPERTS, logits, -jnp.inf)
        m1 = jnp.max(lg, axis=-1, keepdims=True)
        i1 = jnp.min(jnp.where(lg == m1, lane, LANES), axis=-1, keepdims=True)
        lg2 = jnp.where(lane == i1, -jnp.inf, lg)
        m2 = jnp.max(lg2, axis=-1, keepdims=True)
        i2 = jnp.min(jnp.where(lg2 == m2, lane, LANES), axis=-1, keepdims=True)
        e2 = jnp.exp(m2 - m1)
        w1 = 1.0 / (1.0 + e2)
        gates_ref[...] = jnp.where(lane == i1, w1, 0.0) + jnp.where(lane == i2, e2 * w1, 0.0)


def _merge(ya, yb, yc, zb, x_all, mod_l, wa, wb, wc, wo, ln_g, ln_b, alpha, tm, row_of_tile, w_router=None):
    t, d = x_all.shape
    full = lambda shape: pl.BlockSpec(shape, lambda i: tuple(0 for _ in shape))
    in_specs = [
        pl.BlockSpec((tm, CHUNK_W), lambda i: (i, 0)),
        pl.BlockSpec((tm, CHUNK_W), lambda i: (i, 0)),
        pl.BlockSpec((tm, CHUNK_W), lambda i: (i, 0)),
        pl.BlockSpec((6, tm, CHUNK_W), lambda i: (1, i, 0)),
        pl.BlockSpec((tm, d), lambda i: (i, 0)),
        _mod_spec(2, row_of_tile), _mod_spec(4, row_of_tile), _mod_spec(3, row_of_tile),
        full(wa.shape), full(wb.shape), full(wc.shape), full(wo.shape), full((1, d)), full((1, d)),
    ]
    args = [ya, yb, yc, zb, x_all, mod_l, mod_l, mod_l, wa, wb, wc, wo, ln_g, ln_b]
    out_specs = [pl.BlockSpec((tm, d), lambda i: (i, 0)), pl.BlockSpec((tm, d), lambda i: (i, 0))]
    out_shape = [jax.ShapeDtypeStruct((t, d), F32), jax.ShapeDtypeStruct((t, d), BF16)]
    if w_router is not None:
        in_specs.append(full(w_router.shape))
        args.append(w_router)
        out_specs.append(pl.BlockSpec((tm, LANES), lambda i: (i, 0)))
        out_shape.append(jax.ShapeDtypeStruct((t, LANES), F32))
    return pl.pallas_call(
        functools.partial(_merge_kernel, alpha, w_router is not None),
        grid=(t // tm,), in_specs=in_specs, out_specs=out_specs, out_shape=out_shape,
        compiler_params=_cparams(("arbitrary",)), name="merge",
    )(*args)


def _ffn_kernel(alpha, h_ref, x1_ref, g2_ref, wg_ref, wu_ref, wd_ref, lg_ref, lb_ref, o_ref, acc_s):
    f = pl.program_id(1)

    @pl.when(f == 0)
    def _():
        acc_s[...] = jnp.zeros_like(acc_s)

    h = h_ref[...]
    act = _silu(_dot(h, wg_ref[...].astype(BF16))) * _dot(h, wu_ref[...].astype(BF16))
    acc_s[...] += _dot(act.astype(BF16), wd_ref[...].astype(BF16))

    @pl.when(f == pl.num_programs(1) - 1)
    def _():
        o_ref[...] = _layer_norm(alpha * x1_ref[...] + g2_ref[...] * acc_s[...], lg_ref[...], lb_ref[...])


def _dense_mixer(h2, x1, mod_l, wg, wu, wd, ln_g, ln_b, alpha, tm, tf, row_of_tile):
    t, d = x1.shape
    ff = wg.shape[1]
    return pl.pallas_call(
        functools.partial(_ffn_kernel, alpha),
        grid=(t // tm, ff // tf),
        in_specs=[
            pl.BlockSpec((tm, d), lambda i, f: (i, 0)),
            pl.BlockSpec((tm, d), lambda i, f: (i, 0)),
            pl.BlockSpec((None, None, 1, d), lambda i, f: (row_of_tile(i), 5, 0, 0)),
            pl.BlockSpec((d, tf), lambda i, f: (0, f)),
            pl.BlockSpec((d, tf), lambda i, f: (0, f)),
            pl.BlockSpec((tf, d), lambda i, f: (f, 0)),
            pl.BlockSpec((1, d), lambda i, f: (0, 0)),
            pl.BlockSpec((1, d), lambda i, f: (0, 0)),
        ],
        out_specs=pl.BlockSpec((tm, d), lambda i, f: (i, 0)),
        out_shape=jax.ShapeDtypeStruct((t, d), F32),
        scratch_shapes=[pltpu.VMEM((tm, d), F32)],
        compiler_params=_cparams(("arbitrary", "arbitrary")), name="dense_mixer",
    )(h2, x1, mod_l, wg, wu, wd, ln_g, ln_b)


MOE_TILE = 256
MOE_TF = 1792


def _route_kernel(g_ref, rank_ref, gate_ref, before_ref, run_s):
    w = pl.program_id(0)

    @pl.when(w == 0)
    def _():
        run_s[...] = jnp.zeros_like(run_s)

    g = g_ref[...]
    sel = g > 0.0
    row = lax.broadcasted_iota(jnp.int32, (MOE_TILE, MOE_TILE), 0)
    col = lax.broadcasted_iota(jnp.int32, (MOE_TILE, MOE_TILE), 1)
    earlier = (col < row).astype(BF16)
    ones = jnp.where(sel, 1.0, 0.0)
    rank = _dot(earlier, ones.astype(BF16)) + run_s[...]
    rank = jnp.where(sel, rank, -1.0)
    before_ref[...] = run_s[...]
    run_s[...] += jnp.sum(ones, axis=0, keepdims=True)
    rank_ref[...] = rank.T[0:N_EXPERTS, :]
    gate_ref[...] = g.T[0:N_EXPERTS, :]


def _route(gates):
    t = gates.shape[0]
    nw = t // MOE_TILE
    return pl.pallas_call(
        _route_kernel,
        grid=(nw,),
        in_specs=[pl.BlockSpec((MOE_TILE, LANES), lambda w: (w, 0))],
        out_specs=[
            pl.BlockSpec((None, N_EXPERTS, MOE_TILE), lambda w: (w, 0, 0)),
            pl.BlockSpec((None, N_EXPERTS, MOE_TILE), lambda w: (w, 0, 0)),
            pl.BlockSpec((None, 1, LANES), lambda w: (w, 0, 0)),
        ],
        out_shape=[
            jax.ShapeDtypeStruct((nw, N_EXPERTS, MOE_TILE), F32),
            jax.ShapeDtypeStruct((nw, N_EXPERTS, MOE_TILE), F32),
            jax.ShapeDtypeStruct((nw, 1, LANES), F32),
        ],
        scratch_shapes=[pltpu.VMEM((1, LANES), F32)],
        compiler_params=_cparams(("arbitrary",)), name="moe_route",
    )(gates)


def _tile_onehot(rank_ref, w, e, base):
    rank_row = rank_ref[w, pl.ds(e, 1), :]
    rows = lax.broadcasted_iota(jnp.int32, (MOE_TILE, MOE_TILE), 0).astype(F32) + base.astype(F32)
    return rank_row == rows


def _gather_kernel(te_ref, base_ref, wlo_ref, whi_ref, rank_ref, x_ref, o_ref, acc_s):
    j = pl.program_id(0)
    e, base = te_ref[j], base_ref[j]
    acc_s[...] = jnp.zeros_like(acc_s)

    def window(w, carry):
        p = jnp.where(_tile_onehot(rank_ref, w, e, base), 1.0, 0.0).astype(BF16)
        acc_s[...] += _dot(p, x_ref[pl.ds(pl.multiple_of(w * MOE_TILE, MOE_TILE), MOE_TILE), :])
        return carry

    lax.fori_loop(wlo_ref[j], whi_ref[j] + 1, window, 0)
    o_ref[...] = acc_s[...].astype(BF16)


def _moe_gather(meta, rank_t, h2, n_tiles):
    t, d = h2.shape
    nw = t // MOE_TILE
    return pl.pallas_call(
        _gather_kernel,
        grid_spec=pltpu.PrefetchScalarGridSpec(
            num_scalar_prefetch=4, grid=(n_tiles,),
            in_specs=[
                pl.BlockSpec((nw, N_EXPERTS, MOE_TILE), lambda j, *_: (0, 0, 0)),
                pl.BlockSpec((t, d), lambda j, *_: (0, 0), pipeline_mode=pl.Buffered(1)),
            ],
            out_specs=pl.BlockSpec((MOE_TILE, d), lambda j, *_: (j, 0)),
            scratch_shapes=[pltpu.VMEM((MOE_TILE, d), F32)],
        ),
        out_shape=jax.ShapeDtypeStruct((n_tiles * MOE_TILE, d), BF16),
        compiler_params=_cparams(("arbitrary",)), name="moe_gather",
    )(*meta, rank_t, h2)


def _moe_up_kernel(te_ref, chg_ref, nused_ref, x_ref, wg_ref, wu_ref, o_ref, wg_s, wu_s):
    j = pl.program_id(1)

    @pl.when(chg_ref[j] == 1)
    def _():
        wg_s[...] = wg_ref[...].astype(BF16)
        wu_s[...] = wu_ref[...].astype(BF16)

    @pl.when(j < nused_ref[0])
    def _():
        x = x_ref[...]
        o_ref[...] = (_silu(_dot(x, wg_s[...])) * _dot(x, wu_s[...])).astype(BF16)

    @pl.when(j >= nused_ref[0])
    def _():
        o_ref[...] = jnp.zeros_like(o_ref)


def _moe_up(te, chg, nused, xs, wg, wu):
    rows, d = xs.shape
    n_tiles = rows // MOE_TILE
    ff = wg.shape[2]
    return pl.pallas_call(
        _moe_up_kernel,
        grid_spec=pltpu.PrefetchScalarGridSpec(
            num_scalar_prefetch=3, grid=(ff // MOE_TF, n_tiles),
            in_specs=[
                pl.BlockSpec((MOE_TILE, d), lambda f, j, *_: (j, 0)),
                pl.BlockSpec((None, d, MOE_TF), lambda f, j, te, *_: (te[j], 0, f)),
                pl.BlockSpec((None, d, MOE_TF), lambda f, j, te, *_: (te[j], 0, f)),
            ],
            out_specs=pl.BlockSpec((MOE_TILE, MOE_TF), lambda f, j, *_: (j, f)),
            scratch_shapes=[pltpu.VMEM((d, MOE_TF), BF16), pltpu.VMEM((d, MOE_TF), BF16)],
        ),
        out_shape=jax.ShapeDtypeStruct((rows, ff), BF16),
        compiler_params=_cparams(("arbitrary", "arbitrary")), name="moe_up",
    )(te, chg, nused, xs, wg, wu)


def _moe_down_kernel(te_ref, chg_ref, nused_ref, a_ref, wd_ref, o_ref, wd_s):
    j = pl.program_id(0)

    @pl.when(chg_ref[j] == 1)
    def _():
        wd_s[...] = wd_ref[...].astype(BF16)

    @pl.when(j < nused_ref[0])
    def _():
        o_ref[...] = _dot(a_ref[...], wd_s[...]).astype(BF16)

    @pl.when(j >= nused_ref[0])
    def _():
        o_ref[...] = jnp.zeros_like(o_ref)


def _moe_down(te, chg, nused, act, wd):
    rows, ff = act.shape
    d = wd.shape[2]
    return pl.pallas_call(
        _moe_down_kernel,
        grid_spec=pltpu.PrefetchScalarGridSpec(
            num_scalar_prefetch=3, grid=(rows // MOE_TILE,),
            in_specs=[
                pl.BlockSpec((MOE_TILE, ff), lambda j, *_: (j, 0)),
                pl.BlockSpec((None, ff, d), lambda j, te, *_: (te[j], 0, 0)),
            ],
            out_specs=pl.BlockSpec((MOE_TILE, d), lambda j, *_: (j, 0)),
            scratch_shapes=[pltpu.VMEM((ff, d), BF16)],
        ),
        out_shape=jax.ShapeDtypeStruct((rows, d), BF16),
        compiler_params=_cparams(("arbitrary",)), name="moe_down",
    )(te, chg, nused, act, wd)


def _combine_kernel(alpha, n_tiles, te_ref, base_ref, wlo_ref, whi_ref, rank_ref, gate_ref, y_ref, x1_ref,
                    g2_ref, lg_ref, lb_ref, o_ref, acc_s):
    s = pl.program_id(0)

    @pl.when(s == 0)
    def _():
        acc_s[...] = jnp.zeros_like(acc_s)

    @pl.when(s < n_tiles)
    def _():
        e, base = te_ref[s], base_ref[s]

        def window(w, carry):
            hit = _tile_onehot(rank_ref, w, e, base)
            q = jnp.where(hit, gate_ref[w, pl.ds(e, 1), :], 0.0).T.astype(BF16)
            rows = pl.ds(pl.multiple_of(w * MOE_TILE, MOE_TILE), MOE_TILE)
            acc_s[rows, :] += _dot(q, y_ref[...])
            return carry

        lax.fori_loop(wlo_ref[s], whi_ref[s] + 1, window, 0)

    @pl.when(s >= n_tiles)
    def _():
        w = s - n_tiles
        rows = pl.ds(pl.multiple_of(w * MOE_TILE, MOE_TILE), MOE_TILE)
        o_ref[...] = _layer_norm(alpha * x1_ref[...] + g2_ref[...] * acc_s[rows, :], lg_ref[...], lb_ref[...])


def _moe_combine(meta, rank_t, gate_t, y, x1, mod_l, ln_g, ln_b, alpha, row_of_tile):
    t, d = x1.shape
    nw = t // MOE_TILE
    n_tiles = y.shape[0] // MOE_TILE
    win = lambda s: jnp.maximum(s - n_tiles, 0)
    return pl.pallas_call(
        functools.partial(_combine_kernel, alpha, n_tiles),
        grid_spec=pltpu.PrefetchScalarGridSpec(
            num_scalar_prefetch=4, grid=(n_tiles + nw,),
            in_specs=[
                pl.BlockSpec((nw, N_EXPERTS, MOE_TILE), lambda s, *_: (0, 0, 0)),
                pl.BlockSpec((nw, N_EXPERTS, MOE_TILE), lambda s, *_: (0, 0, 0)),
                pl.BlockSpec((MOE_TILE, d), lambda s, *_: (jnp.minimum(s, n_tiles - 1), 0)),
                pl.BlockSpec((MOE_TILE, d), lambda s, *_: (win(s), 0)),
                pl.BlockSpec((None, None, 1, d), lambda s, *_: (row_of_tile(win(s)), 5, 0, 0)),
                pl.BlockSpec((1, d), lambda s, *_: (0, 0)),
                pl.BlockSpec((1, d), lambda s, *_: (0, 0)),
            ],
            out_specs=pl.BlockSpec((MOE_TILE, d), lambda s, *_: (win(s), 0)),
            scratch_shapes=[pltpu.VMEM((t, d), F32)],
        ),
        out_shape=jax.ShapeDtypeStruct((t, d), F32),
        compiler_params=_cparams(("arbitrary",)), name="moe_combine",
    )(*meta, rank_t, gate_t, y, x1, mod_l, ln_g, ln_b)


def _moe_layer(gates, h2, x1, mod_l, wg, wu, wd, ln_g, ln_b, alpha, row_of_tile):
    t = h2.shape[0]
    n_tiles = -(-(2 * t + N_EXPERTS * (MOE_TILE - 1)) // MOE_TILE)
    rank_t, gate_t, before = _route(gates)
    before = before[:, 0, :N_EXPERTS].astype(jnp.int32)
    counts = jnp.sum(gates[:, :N_EXPERTS] > 0.0, axis=0).astype(jnp.int32)
    tiles_e = (counts + MOE_TILE - 1) // MOE_TILE
    tile_end = jnp.cumsum(tiles_e)
    n_used = tile_end[-1]
    j = jnp.arange(n_tiles, dtype=jnp.int32)
    te = jnp.minimum(jnp.sum(j[:, None] >= tile_end[None, :], axis=1), N_EXPERTS - 1).astype(jnp.int32)
    base = (j - jnp.take(tile_end - tiles_e, te)) * MOE_TILE
    last = jnp.minimum(base + MOE_TILE - 1, jnp.take(counts, te) - 1)
    before_t = jnp.take(before, te, axis=1)
    used = j < n_used
    wlo = jnp.where(used, jnp.sum(before_t <= base[None, :], axis=0) - 1, 1).astype(jnp.int32)
    whi = jnp.where(used, jnp.sum(before_t <= last[None, :], axis=0) - 1, 0).astype(jnp.int32)
    chg = jnp.concatenate([jnp.ones((1,), jnp.int32), (te[1:] != te[:-1]).astype(jnp.int32)])
    nused = n_used.reshape(1).astype(jnp.int32)
    meta = (te, base.astype(jnp.int32), wlo, whi)
    xs = _moe_gather(meta, rank_t, h2, n_tiles)
    act = _moe_up(te, chg, nused, xs, wg, wu)
    y = _moe_down(te, chg, nused, act, wd)
    return _moe_combine(meta, rank_t, gate_t, y, x1, mod_l, ln_g, ln_b, alpha, row_of_tile)


def _rope_tables(seq):
    half = GLA_DK // 2
    n = half // 2
    t = np.arange(seq)
    inv = ROPE_BASE ** (-np.arange(n, dtype=np.float64) / n)
    cos = np.zeros((seq, GLA_DK), np.float64)
    sin = np.zeros((seq, GLA_DK), np.float64)
    for a, pos in enumerate((t // GRID_W, t % GRID_W)):
        ang = pos[:, None].astype(np.float64) * inv[None, :]
        base = a * half
        cos[:, base:base + n] = np.cos(ang)
        cos[:, base + n:base + half] = np.cos(ang)
        sin[:, base:base + n] = -np.sin(ang)
        sin[:, base + n:base + half] = np.sin(ang)
    tile = lambda m: jnp.asarray(np.tile(m, (1, GLA_HEADS)), F32)
    return tile(cos), tile(sin)


def _chunk_w_in(w):
    gda = jnp.pad(w[:, 1536:1568], ((0, 0), (0, CHUNK_W - 2 * GLA_RANK)))
    starts = (2080, 2592, 0, 512, 1024, 1568, 3104, 3616, 4128, 4640, 5152, 5664, 6176, 6688)
    return jnp.stack([gda] + [w[:, s:s + CHUNK_W] for s in starts]).astype(BF16)


def kernel(x_prompt, x_sample, cache_na_k, cache_na_v, state_gla, c, c_ctx, w_mod, b_mod, w_in, gla_w_gup, gla_b_g, gla_norm, w_br_gla, na_rpb, w_br_na, cv_w, cv_b, cv_ln_g, cv_ln_b, w_br_cv, w_out, ln_g, ln_b, ffd_w_gate, ffd_w_up, ffd_w_down, moe_w_router, moe_w_gate, moe_w_up, moe_w_down):
    nb_c, l_c, d = x_prompt.shape
    nb_l, l_l, _ = x_sample.shape
    depth = w_mod.shape[0]
    t_ctx, t_lat = nb_c * l_c, nb_l * l_l
    t_all = t_ctx + t_lat
    alpha = (2 * depth) ** 0.25
    tm = 512
    row_of_tile = _row_of_tile_fn(t_ctx, l_l, tm)

    x_all = jnp.concatenate([x_prompt.reshape(t_ctx, d), x_sample.reshape(t_lat, d)], axis=0)
    cond8 = jnp.concatenate([c_ctx[None, :], c, jnp.zeros((8 - 1 - nb_l, d), F32)], axis=0)
    mod = _modulation(cond8, w_mod, b_mod).reshape(depth, 8, 6, 1, d)
    rope = _rope_tables(l_l)
    rows_lat = l_l // GRID_W
    kr = min(NA_WIN_R, rows_lat)
    s0_all = state_gla.reshape(nb_l, depth, 2, 2, LANES, GLA_DV)

    ks, vs, ss = [], [], []
    for l in range(depth):
        mod_l = mod[l]
        zf, zb = _input_projection(x_all, mod_l, _chunk_w_in(w_in[l]), tm, row_of_tile)

        wg_blk = jnp.zeros((LANES, CHUNK_W), F32)
        wg_blk = wg_blk.at[0:GLA_RANK, 0:256].set(gla_w_gup[l, 0]).at[GLA_RANK:2 * GLA_RANK, 256:512].set(gla_w_gup[l, 1])
        bg = gla_b_g[l].reshape(1, CHUNK_W)
        gn = gla_norm[l].reshape(1, CHUNK_W)
        ya, s_fin = _gla_branch(zf, zb, wg_blk, bg, gn, latent=False, seq=l_c, nb=nb_c, tok0=0, t_total=t_all)
        ya = _gla_branch(zf, zb, wg_blk, bg, gn, latent=True, seq=l_l, nb=nb_l, tok0=t_ctx, t_total=t_all,
                         rope=rope, s0=s0_all[:, l], ya_prev=ya)

        yb = _attn_context(zf, zb, nb_c, l_c, t_all)
        bias = _na_bias_table(na_rpb[l], kr)
        yb = _attn_latent(zf, zb, cache_na_k, cache_na_v, l, bias, yb, nb_l, l_l, t_ctx)

        conv_args = (zb, cv_w[l], cv_b[l][None, :], cv_ln_g[l][None, :], cv_ln_b[l][None, :])
        yc = _conv_branch(*conv_args, seq=l_c, nb=nb_c, tok0=0, t_total=t_all)
        yc = _conv_branch(*conv_args, seq=l_l, nb=nb_l, tok0=t_ctx, t_total=t_all, prev=yc)

        moe = l % 2 == 1
        w_router = None
        if moe:
            w_router = jnp.pad(moe_w_router[l // 2], ((0, 0), (0, LANES - N_EXPERTS)))
        merged = _merge(ya, yb, yc, zb, x_all, mod_l,
                        w_br_gla[l].astype(BF16), w_br_na[l].astype(BF16), w_br_cv[l].astype(BF16),
                        w_out[l].astype(BF16), ln_g[l, 0][None, :], ln_b[l, 0][None, :], alpha, tm, row_of_tile,
                        w_router)
        ln2 = (ln_g[l, 1][None, :], ln_b[l, 1][None, :])
        if moe:
            x1, h2, gates = merged
            x_all = _moe_layer(gates, h2, x1, mod_l, moe_w_gate[l // 2], moe_w_up[l // 2], moe_w_down[l // 2],
                               *ln2, alpha, _row_of_tile_fn(t_ctx, l_l, MOE_TILE))
        else:
            x1, h2 = merged
            i = l // 2
            x_all = _dense_mixer(h2, x1, mod_l, ffd_w_gate[i], ffd_w_up[i], ffd_w_down[i],
                                 *ln2, alpha, 1024, 256, _row_of_tile_fn(t_ctx, l_l, 1024))

        ks.append(zf[1, :t_ctx].reshape(nb_c, l_c, NA_HEADS, NA_DH))
        vs.append(zf[2, :t_ctx].reshape(nb_c, l_c, NA_HEADS, NA_DH))
        ss.append(s_fin.reshape(nb_c, 2, GLA_HEADS, GLA_DK, GLA_DV))

    y_prompt = x_all[:t_ctx].reshape(nb_c, l_c, d)
    y_sample = x_all[t_ctx:].reshape(nb_l, l_l, d)
    return (y_prompt, y_sample, jnp.stack(ks, 1), jnp.stack(vs, 1), jnp.stack(ss, 1))


def _row_of_tile_fn(t_ctx, l_lat, tm):
    n_ctx = t_ctx // tm

    def row_of_tile(i):
        return jnp.where(i < n_ctx, 0, 1 + (i - n_ctx) // (l_lat // tm))

    return row_of_tile
```

```python
import functools

import numpy as np
import jax
import jax.numpy as jnp
from jax import lax
from jax.experimental import pallas as pl
from jax.experimental.pallas import tpu as pltpu

F32 = jnp.float32
BF16 = jnp.bfloat16

GRID_W = 64
GLA_HEADS = 4
GLA_DK = 64
GLA_DV = 128
GLA_RANK = 16
GLA_TAU = 16.0
GLA_CHUNK = 64
GLA_GROUP = 256
NA_HEADS = 8
NA_DH = 64
NA_WIN_R = 8
NA_WIN_C = 16
CONV_CH = 512
CONV_K = 31
N_EXPERTS = 8
ROPE_BASE = 10000.0
LN_EPS = 1e-5

D_MODEL = 1024
LANES = 128
SUBLANES = 8
CHUNK_W = 512
N_F32_CHUNKS = 3
N_BF16_CHUNKS = 12
NEG_BIG = -1e30
VMEM_LIMIT = 56 * 1024 * 1024


def _cparams(sem, vmem=VMEM_LIMIT):
    return pltpu.CompilerParams(dimension_semantics=sem, vmem_limit_bytes=vmem)


def _dot(a, b):
    return jnp.dot(a, b, preferred_element_type=F32)


def _split(x):
    hi = x.astype(BF16)
    return hi, (x - hi.astype(F32)).astype(BF16)


def _dot_split(a, b):
    a_hi, a_lo = _split(a)
    b_hi, b_lo = _split(b)
    return _dot(a_hi, b_hi) + _dot(a_lo, b_hi) + _dot(a_hi, b_lo)


def _dot_split_rhs(a, b):
    b_hi, b_lo = _split(b)
    return _dot(a, b_hi) + _dot(a, b_lo)


def _dot_nt(a, b):
    return lax.dot_general(a, b, (((1,), (1,)), ((), ())), preferred_element_type=F32)


def _sigmoid(x):
    return 1.0 / (1.0 + jnp.exp(-x))


def _silu(x):
    return x * _sigmoid(x)


def _layer_norm(x, g, b):
    mu = jnp.mean(x, axis=-1, keepdims=True)
    xc = x - mu
    var = jnp.mean(xc * xc, axis=-1, keepdims=True)
    return xc * lax.rsqrt(var + LN_EPS) * g + b


def _mod_kernel(c_ref, w_ref, b_ref, o_ref):
    c = c_ref[...]
    s = _silu(c).astype(BF16)
    o_ref[...] = _dot(s, w_ref[...].astype(BF16)) + b_ref[...]


def _modulation(cond8, w_mod, b_mod):
    depth, d, n = w_mod.shape
    tn = 1024
    return pl.pallas_call(
        _mod_kernel,
        grid=(depth, n // tn),
        in_specs=[
            pl.BlockSpec((8, d), lambda l, j: (0, 0)),
            pl.BlockSpec((None, d, tn), lambda l, j: (l, 0, j)),
            pl.BlockSpec((None, 1, tn), lambda l, j: (l, 0, j)),
        ],
        out_specs=pl.BlockSpec((None, 8, tn), lambda l, j: (l, 0, j)),
        out_shape=jax.ShapeDtypeStruct((depth, 8, n), F32),
        compiler_params=_cparams(("arbitrary", "arbitrary")),
        name="modulation",
    )(cond8, w_mod, b_mod.reshape(depth, 1, n))


def _inproj_kernel(x_ref, sc_ref, sh_ref, w_ref, zf_ref, zb_ref, h_ref):
    h_ref[...] = (x_ref[...] * (1.0 + sc_ref[...]) + sh_ref[...]).astype(BF16)

    def f32_chunk(j, c):
        zf_ref[j] = _dot(h_ref[...], w_ref[j])
        return c

    lax.fori_loop(0, N_F32_CHUNKS, f32_chunk, 0)

    def bf16_chunk(j, c):
        zb_ref[j] = _dot(h_ref[...], w_ref[N_F32_CHUNKS + j]).astype(BF16)
        return c

    lax.fori_loop(0, N_BF16_CHUNKS, bf16_chunk, 0)


def _mod_spec(piece, row_of_tile):
    return pl.BlockSpec((None, None, 1, D_MODEL), lambda i: (row_of_tile(i), piece, 0, 0))


def _input_projection(x_all, mod_l, w_chunks, tm, row_of_tile):
    t, d = x_all.shape
    nch = N_F32_CHUNKS + N_BF16_CHUNKS
    return pl.pallas_call(
        _inproj_kernel,
        grid=(t // tm,),
        in_specs=[
            pl.BlockSpec((tm, d), lambda i: (i, 0)),
            _mod_spec(1, row_of_tile),
            _mod_spec(0, row_of_tile),
            pl.BlockSpec((nch, d, CHUNK_W), lambda i: (0, 0, 0), pipeline_mode=pl.Buffered(1)),
        ],
        out_specs=[
            pl.BlockSpec((N_F32_CHUNKS, tm, CHUNK_W), lambda i: (0, i, 0)),
            pl.BlockSpec((N_BF16_CHUNKS, tm, CHUNK_W), lambda i: (0, i, 0)),
        ],
        out_shape=[
            jax.ShapeDtypeStruct((N_F32_CHUNKS, t, CHUNK_W), F32),
            jax.ShapeDtypeStruct((N_BF16_CHUNKS, t, CHUNK_W), BF16),
        ],
        scratch_shapes=[pltpu.VMEM((tm, d), BF16)],
        compiler_params=_cparams(("arbitrary",)),
        name="input_projection",
    )(x_all, mod_l, mod_l, w_chunks)


def _gla_kernel(latent, seq, *refs):
    if latent:
        (qk_ref, v_ref, ra_ref, gd_ref, wg_ref, bg_ref, gn_ref, cos_ref, sin_ref, s0_ref, _alias,
         ya_ref, q_s, k_s, g_s, acc_s, st_s) = refs
    else:
        (qk_ref, v_ref, ra_ref, gd_ref, wg_ref, bg_ref, gn_ref,
         ya_ref, sfin_ref, q_s, k_s, g_s, acc_s, st_s) = refs
    c = GLA_CHUNK
    n_chunks = seq // c
    qkw = GLA_HEADS * GLA_DK

    q = qk_ref[:, 0:qkw].astype(F32) * (GLA_DK ** -0.5)
    k = qk_ref[:, qkw:2 * qkw].astype(F32)
    if latent:
        lane = lax.broadcasted_iota(jnp.int32, (1, qkw), 1)
        first_half = (lane % 32) < 16

        def rope(x):
            swapped = jnp.where(first_half, pltpu.roll(x, qkw - 16, 1), pltpu.roll(x, 16, 1))
            return x * cos_ref[...] + swapped * sin_ref[...]

        q, k = rope(q), rope(k)
    q_s[...] = q
    k_s[...] = k

    glog = _dot_split(gd_ref[...], wg_ref[...]) + bg_ref[...]
    g_s[...] = (jnp.minimum(glog, 0.0) - jnp.log(1.0 + jnp.exp(-jnp.abs(glog)))) * (1.0 / GLA_TAU)
    acc_s[...] = jnp.zeros_like(acc_s)
    if latent:
        st_s[...] = s0_ref[...]
    else:
        st_s[...] = jnp.zeros_like(st_s)

    grp = GLA_GROUP
    cpg = grp // c
    n_groups = seq // grp
    grow = lax.broadcasted_iota(jnp.int32, (grp, grp), 0)
    gcol = lax.broadcasted_iota(jnp.int32, (grp, grp), 1)
    same_chunk = (grow // c) == (gcol // c)
    keep = (same_chunk & (gcol <= grow), same_chunk & (gcol >= grow))
    tri = (keep[0].astype(BF16), keep[1].astype(BF16))
    lane128 = lax.broadcasted_iota(jnp.int32, (1, LANES), 1)
    head_mask = (lane128 < GLA_DK, lane128 >= GLA_DK)
    urow = lax.broadcasted_iota(jnp.int32, (cpg * LANES, GLA_DV), 0)
    chunk_of_col = lax.broadcasted_iota(jnp.int32, (LANES, grp), 1) // c
    chunk_of_row = lax.broadcasted_iota(jnp.int32, (grp, LANES), 0) // c
    pad_rows = jnp.zeros((SUBLANES - cpg, qkw), F32)

    def group_step(i, carry):
        for d in range(2):
            gi = i if d == 0 else n_groups - 1 - i
            rows = pl.ds(pl.multiple_of(gi * grp, grp), grp)
            cum = _dot_split_rhs(tri[d], g_s[rows, d * qkw:(d + 1) * qkw])
            edge = c - 1 if d == 0 else 0
            lasts = [cum[ci * c + edge:ci * c + edge + 1, :] for ci in range(cpg)]
            last = jnp.concatenate([jnp.broadcast_to(r, (c, qkw)) for r in lasts], axis=0)
            ref = 0.5 * last
            q = q_s[rows, :]
            k = k_s[rows, :]
            qe = q * jnp.exp(cum - ref)
            ke = k * jnp.exp(ref - cum)
            qe2 = q * jnp.exp(cum)
            ke2 = k * jnp.exp(last - cum)
            last8 = jnp.concatenate(lasts + [pad_rows], axis=0)
            for p in range(2):
                sl = slice(p * LANES, (p + 1) * LANES)
                ke_p = ke[:, sl].astype(BF16)
                ke2_t = ke2[:, sl].T
                ke2_blk = jnp.concatenate(
                    [jnp.where(chunk_of_col == ci, ke2_t, 0.0) for ci in range(cpg)], axis=0).astype(BF16)
                dec = jnp.exp(last8[:, sl]).T
                vs = [v_ref[rows, (2 * p + hh) * GLA_DV:(2 * p + hh + 1) * GLA_DV] for hh in range(2)]
                u = [_dot(ke2_blk, vs[hh]) for hh in range(2)]
                upd = jnp.where(urow % LANES < GLA_DK, u[0], u[1])
                s = st_s[d, p]
                s_in = [None] * cpg
                for ci in (range(cpg) if d == 0 else range(cpg - 1, -1, -1)):
                    s_in[ci] = s.astype(BF16)
                    s = dec[:, ci:ci + 1] * s + upd[ci * LANES:(ci + 1) * LANES]
                st_s[d, p] = s
                s_stack = jnp.concatenate(s_in, axis=0)
                for hh in range(2):
                    vsl = slice((2 * p + hh) * GLA_DV, (2 * p + hh + 1) * GLA_DV)
                    a = jnp.where(head_mask[hh], qe[:, sl], 0.0).astype(BF16)
                    a2 = jnp.where(head_mask[hh], qe2[:, sl], 0.0)
                    a2_blk = jnp.concatenate(
                        [jnp.where(chunk_of_row == ci, a2, 0.0) for ci in range(cpg)], axis=1).astype(BF16)
                    att = jnp.where(keep[d], _dot_nt(a, ke_p), 0.0).astype(BF16)
                    acc_s[rows, vsl] += _dot(att, vs[hh]) + _dot(a2_blk, s_stack)
        return carry

    lax.fori_loop(0, n_groups, group_step, 0)

    def epilogue(i, carry):
        rows = pl.ds(pl.multiple_of(i * c, c), c)
        for h in range(GLA_HEADS):
            vsl = slice(h * GLA_DV, (h + 1) * GLA_DV)
            o = acc_s[rows, vsl]
            o = o * lax.rsqrt(jnp.mean(o * o, axis=-1, keepdims=True) + LN_EPS) * gn_ref[:, vsl]
            ya_ref[rows, vsl] = (o * _silu(ra_ref[rows, vsl].astype(F32))).astype(BF16)
        return carry

    lax.fori_loop(0, n_chunks, epilogue, 0)
    if not latent:
        sfin_ref[...] = st_s[...]


def _gla_branch(zf, zb, wg_blk, bg, gn, *, latent, seq, nb, tok0, t_total, rope=None, s0=None, ya_prev=None):
    b0 = tok0 // seq
    qkw = GLA_HEADS * GLA_DK
    in_specs = [
        pl.BlockSpec((None, seq, CHUNK_W), lambda b: (0, b0 + b, 0)),
        pl.BlockSpec((None, seq, CHUNK_W), lambda b: (1, b0 + b, 0)),
        pl.BlockSpec((None, seq, CHUNK_W), lambda b: (2, b0 + b, 0)),
        pl.BlockSpec((None, seq, LANES), lambda b: (0, b0 + b, 0)),
        pl.BlockSpec((LANES, CHUNK_W), lambda b: (0, 0)),
        pl.BlockSpec((1, CHUNK_W), lambda b: (0, 0)),
        pl.BlockSpec((1, CHUNK_W), lambda b: (0, 0)),
    ]
    args = [zb, zb, zb, zf, wg_blk, bg, gn]
    scratch = [
        pltpu.VMEM((seq, qkw), F32), pltpu.VMEM((seq, qkw), F32), pltpu.VMEM((seq, CHUNK_W), F32),
        pltpu.VMEM((seq, CHUNK_W), F32), pltpu.VMEM((2, 2, LANES, GLA_DV), F32),
    ]
    ya_shape = jax.ShapeDtypeStruct((t_total, CHUNK_W), BF16)
    ya_spec = pl.BlockSpec((seq, CHUNK_W), lambda b: (b0 + b, 0))
    if latent:
        cos_t, sin_t = rope
        in_specs += [
            pl.BlockSpec((seq, qkw), lambda b: (0, 0)),
            pl.BlockSpec((seq, qkw), lambda b: (0, 0)),
            pl.BlockSpec((None, 2, 2, LANES, GLA_DV), lambda b: (b, 0, 0, 0, 0)),
            pl.BlockSpec(memory_space=pl.ANY),
        ]
        args += [cos_t, sin_t, s0, ya_prev]
        return pl.pallas_call(
            functools.partial(_gla_kernel, True, seq),
            grid=(nb,), in_specs=in_specs, out_specs=ya_spec, out_shape=ya_shape,
            scratch_shapes=scratch, input_output_aliases={len(args) - 1: 0},
            compiler_params=_cparams(("arbitrary",)), name="gla_latent",
        )(*args)
    return pl.pallas_call(
        functools.partial(_gla_kernel, False, seq),
        grid=(nb,), in_specs=in_specs,
        out_specs=[ya_spec, pl.BlockSpec((None, 2, 2, LANES, GLA_DV), lambda b: (b, 0, 0, 0, 0))],
        out_shape=[ya_shape, jax.ShapeDtypeStruct((nb, 2, 2, LANES, GLA_DV), F32)],
        scratch_shapes=scratch,
        compiler_params=_cparams(("arbitrary",)), name="gla_context",
    )(*args)


def _attn_ctx_kernel(q_ref, k_ref, v_ref, o_ref):
    lane = lax.broadcasted_iota(jnp.int32, (1, LANES), 1)
    masks = (lane < NA_DH, lane >= NA_DH)
    scale = NA_DH ** -0.5
    for p in range(NA_HEADS // 2):
        sl = slice(p * LANES, (p + 1) * LANES)
        qp = q_ref[:, sl]
        kp = k_ref[:, sl].astype(BF16)
        vp = v_ref[:, sl].astype(BF16)
        outs = []
        for hh in range(2):
            qm = jnp.where(masks[hh], qp, jnp.zeros_like(qp))
            s = _dot_nt(qm, kp) * scale
            e = jnp.exp(s - jnp.max(s, axis=-1, keepdims=True))
            outs.append(_dot(e.astype(BF16), vp) / jnp.sum(e, axis=-1, keepdims=True))
        o_ref[:, sl] = jnp.where(masks[0], outs[0], outs[1]).astype(BF16)


def _attn_context(zf, zb, nb, seq, t_total):
    return pl.pallas_call(
        _attn_ctx_kernel,
        grid=(nb,),
        in_specs=[
            pl.BlockSpec((None, seq, CHUNK_W), lambda b: (3, b, 0)),
            pl.BlockSpec((None, seq, CHUNK_W), lambda b: (1, b, 0)),
            pl.BlockSpec((None, seq, CHUNK_W), lambda b: (2, b, 0)),
        ],
        out_specs=pl.BlockSpec((seq, CHUNK_W), lambda b: (b, 0)),
        out_shape=jax.ShapeDtypeStruct((t_total, CHUNK_W), BF16),
        compiler_params=_cparams(("arbitrary",)), name="attention_context",
    )(zb, zf, zf)


def _na_kernel(rows_total, kr, q_ref, kl_ref, vl_ref, kc_ref, vc_ref, bias_ref, _alias, o_ref):
    r = pl.program_id(1)
    rs = jnp.clip(r - kr // 2, 0, rows_total - kr)
    krows = pl.ds(pl.multiple_of(rs * GRID_W, GRID_W), kr * GRID_W)
    lane = lax.broadcasted_iota(jnp.int32, (1, LANES), 1)
    masks = (lane < NA_DH, lane >= NA_DH)
    scale = NA_DH ** -0.5
    for p in range(NA_HEADS // 2):
        sl = slice(p * LANES, (p + 1) * LANES)
        qp = q_ref[:, sl]
        klp = kl_ref[krows, sl].astype(BF16)
        vlp = vl_ref[krows, sl].astype(BF16)
        kcp = kc_ref[:, sl].astype(BF16)
        vcp = vc_ref[:, sl].astype(BF16)
        outs = []
        for hh in range(2):
            qm = jnp.where(masks[hh], qp, jnp.zeros_like(qp))
            s_loc = _dot_nt(qm, klp) * scale + bias_ref[2 * p + hh]
            s_ctx = _dot_nt(qm, kcp) * scale
            m = jnp.maximum(jnp.max(s_loc, axis=-1, keepdims=True), jnp.max(s_ctx, axis=-1, keepdims=True))
            e_loc = jnp.exp(s_loc - m)
            e_ctx = jnp.exp(s_ctx - m)
            den = jnp.sum(e_loc, axis=-1, keepdims=True) + jnp.sum(e_ctx, axis=-1, keepdims=True)
            outs.append((_dot(e_loc.astype(BF16), vlp) + _dot(e_ctx.astype(BF16), vcp)) / den)
        o_ref[:, sl] = jnp.where(masks[0], outs[0], outs[1]).astype(BF16)


def _na_bias_table(rpb, kr):
    heads, nr, nc = rpb.shape
    qc = np.arange(GRID_W)[:, None]
    kc = np.arange(GRID_W)[None, :]
    cs = np.clip(qc - NA_WIN_C // 2, 0, GRID_W - NA_WIN_C)
    valid = (kc >= cs) & (kc < cs + NA_WIN_C)
    period = 2 * GRID_W
    lead = GRID_W - 1 - (NA_WIN_C - 1)
    v = jnp.pad(rpb.astype(F32), ((0, 0), (0, 0), (lead, period - lead - nc)))
    skew = jnp.tile(v, (1, 1, GRID_W))[:, :, :GRID_W * (period - 1)].reshape(heads, nr, GRID_W, period - 1)
    toep = jnp.where(valid, skew[..., GRID_W - 1:], NEG_BIG)
    pats = []
    for p in range(kr):
        pats.append(jnp.concatenate([toep[:, i - p + NA_WIN_R - 1] for i in range(kr)], axis=-1))
    return jnp.stack(pats)


def _attn_latent(zf, zb, cache_k, cache_v, layer, bias, att_prev, nb, seq, tok0):
    rows_total = seq // GRID_W
    kr = min(NA_WIN_R, rows_total)
    q0 = tok0 // GRID_W
    b0 = tok0 // seq
    past = cache_k.shape[2]
    ck = cache_k.reshape(cache_k.shape[0], cache_k.shape[1], past, NA_HEADS * NA_DH)
    cv = cache_v.reshape(ck.shape)
    return pl.pallas_call(
        functools.partial(_na_kernel, rows_total, kr),
        grid=(nb, rows_total),
        in_specs=[
            pl.BlockSpec((None, GRID_W, CHUNK_W), lambda b, r: (3, q0 + b * rows_total + r, 0)),
            pl.BlockSpec((None, seq, CHUNK_W), lambda b, r: (1, b0 + b, 0)),
            pl.BlockSpec((None, seq, CHUNK_W), lambda b, r: (2, b0 + b, 0)),
            pl.BlockSpec((None, None, past, CHUNK_W), lambda b, r: (b, layer, 0, 0)),
            pl.BlockSpec((None, None, past, CHUNK_W), lambda b, r: (b, layer, 0, 0)),
            pl.BlockSpec((None, NA_HEADS, GRID_W, kr * GRID_W),
                         lambda b, r: (r - jnp.clip(r - kr // 2, 0, rows_total - kr), 0, 0, 0)),
            pl.BlockSpec(memory_space=pl.ANY),
        ],
        out_specs=pl.BlockSpec((GRID_W, CHUNK_W), lambda b, r: (q0 + b * rows_total + r, 0)),
        out_shape=jax.ShapeDtypeStruct(att_prev.shape, BF16),
        input_output_aliases={6: 0},
        compiler_params=_cparams(("arbitrary", "arbitrary")), name="attention_latent",
    )(zb, zf, zf, ck, cv, bias, att_prev)


CONV_PAD = 16
CONV_ROWS = 64


def _conv_kernel(seq, aliased, *refs):
    if aliased:
        a_ref, gt_ref, w_ref, cb_ref, lg_ref, lb_ref, _alias, o_ref, z_s, sh_s = refs
    else:
        a_ref, gt_ref, w_ref, cb_ref, lg_ref, lb_ref, o_ref, z_s, sh_s = refs
    z_s[0:CONV_PAD, :] = jnp.zeros((CONV_PAD, CONV_CH), F32)
    z_s[CONV_PAD + seq:2 * CONV_PAD + seq, :] = jnp.zeros((CONV_PAD, CONV_CH), F32)
    z_s[CONV_PAD:CONV_PAD + seq, :] = a_ref[...].astype(F32) * _sigmoid(gt_ref[...].astype(F32))
    ext = seq + 2 * CONV_PAD - SUBLANES
    for b in range(1, SUBLANES):
        for r0 in range(0, ext, CONV_ROWS):
            n = min(CONV_ROWS, ext - r0)
            sh_s[b - 1, r0:r0 + n, :] = z_s[r0 + b:r0 + b + n, :]
    off = CONV_PAD - CONV_K // 2
    for t0 in range(0, seq, CONV_ROWS):
        acc = jnp.zeros((CONV_ROWS, CONV_CH), F32)
        for k in range(CONV_K):
            a, b = divmod(off + k, SUBLANES)
            r0 = t0 + a * SUBLANES
            tap = z_s[r0:r0 + CONV_ROWS, :] if b == 0 else sh_s[b - 1, r0:r0 + CONV_ROWS, :]
            acc = acc + tap * w_ref[k:k + 1, :]
        y = _layer_norm(acc + cb_ref[...], lg_ref[...], lb_ref[...])
        o_ref[t0:t0 + CONV_ROWS, :] = _silu(y).astype(BF16)


def _conv_branch(zb, cv_w, cv_b, ln_g, ln_b, *, seq, nb, tok0, t_total, prev=None):
    b0 = tok0 // seq
    in_specs = [
        pl.BlockSpec((None, seq, CHUNK_W), lambda b: (4, b0 + b, 0)),
        pl.BlockSpec((None, seq, CHUNK_W), lambda b: (5, b0 + b, 0)),
        pl.BlockSpec((CONV_K, CONV_CH), lambda b: (0, 0)),
        pl.BlockSpec((1, CONV_CH), lambda b: (0, 0)),
        pl.BlockSpec((1, CONV_CH), lambda b: (0, 0)),
        pl.BlockSpec((1, CONV_CH), lambda b: (0, 0)),
    ]
    args = [zb, zb, cv_w, cv_b, ln_g, ln_b]
    aliases = {}
    if prev is not None:
        in_specs.append(pl.BlockSpec(memory_space=pl.ANY))
        args.append(prev)
        aliases = {6: 0}
    return pl.pallas_call(
        functools.partial(_conv_kernel, seq, prev is not None),
        grid=(nb,), in_specs=in_specs,
        out_specs=pl.BlockSpec((seq, CONV_CH), lambda b: (b0 + b, 0)),
        out_shape=jax.ShapeDtypeStruct((t_total, CONV_CH), BF16),
        scratch_shapes=[pltpu.VMEM((seq + 2 * CONV_PAD, CONV_CH), F32),
                        pltpu.VMEM((SUBLANES - 1, seq + 2 * CONV_PAD - SUBLANES, CONV_CH), F32)],
        input_output_aliases=aliases,
        compiler_params=_cparams(("arbitrary",)), name="conv_module_%d" % seq,
    )(*args)


def _merge_kernel(alpha, with_router, *refs):
    if with_router:
        (ya_ref, yb_ref, yc_ref, mg_ref, x_ref, g1_ref, sc2_ref, sh2_ref, wa_ref, wb_ref, wc_ref, wo_ref,
         lg_ref, lb_ref, wr_ref, x1_ref, h2_ref, gates_ref) = refs
    else:
        (ya_ref, yb_ref, yc_ref, mg_ref, x_ref, g1_ref, sc2_ref, sh2_ref, wa_ref, wb_ref, wc_ref, wo_ref,
         lg_ref, lb_ref, x1_ref, h2_ref) = refs
    halves = []
    for n in range(2):
        m = None
        for j, (y_ref, w_ref) in enumerate(((ya_ref, wa_ref), (yb_ref, wb_ref), (yc_ref, wc_ref))):
            proj = _dot(y_ref[...], w_ref[:, n * CHUNK_W:(n + 1) * CHUNK_W])
            term = _sigmoid(mg_ref[2 * j + n].astype(F32)) * proj
            m = term if m is None else m + term
        halves.append(m.astype(BF16))
    mix = _dot(halves[0], wo_ref[0:CHUNK_W, :]) + _dot(halves[1], wo_ref[CHUNK_W:2 * CHUNK_W, :])
    x1 = _layer_norm(alpha * x_ref[...] + g1_ref[...] * mix, lg_ref[...], lb_ref[...])
    x1_ref[...] = x1
    h2 = x1 * (1.0 + sc2_ref[...]) + sh2_ref[...]
    h2_ref[...] = h2.astype(BF16)
    if with_router:
        logits = _dot_split(h2, wr_ref[...])
        lane = lax.broadcasted_iota(jnp.int32, logits.shape, 1)
        lg = jnp.where(lane < N_EXPERTS, logits, -jnp.inf)
        m1 = jnp.max(lg, axis=-1, keepdims=True)
        i1 = jnp.min(jnp.where(lg == m1, lane, LANES), axis=-1, keepdims=True)
        lg2 = jnp.where(lane == i1, -jnp.inf, lg)
        m2 = jnp.max(lg2, axis=-1, keepdims=True)
        i2 = jnp.min(jnp.where(lg2 == m2, lane, LANES), axis=-1, keepdims=True)
        e2 = jnp.exp(m2 - m1)
        w1 = 1.0 / (1.0 + e2)
        gates_ref[...] = jnp.where(lane == i1, w1, 0.0) + jnp.where(lane == i2, e2 * w1, 0.0)


def _merge(ya, yb, yc, zb, x_all, mod_l, wa, wb, wc, wo, ln_g, ln_b, alpha, tm, row_of_tile, w_router=None):
    t, d = x_all.shape
    full = lambda shape: pl.BlockSpec(shape, lambda i: tuple(0 for _ in shape))
    in_specs = [
        pl.BlockSpec((tm, CHUNK_W), lambda i: (i, 0)),
        pl.BlockSpec((tm, CHUNK_W), lambda i: (i, 0)),
        pl.BlockSpec((tm, CHUNK_W), lambda i: (i, 0)),
        pl.BlockSpec((6, tm, CHUNK_W), lambda i: (1, i, 0)),
        pl.BlockSpec((tm, d), lambda i: (i, 0)),
        _mod_spec(2, row_of_tile), _mod_spec(4, row_of_tile), _mod_spec(3, row_of_tile),
        full(wa.shape), full(wb.shape), full(wc.shape), full(wo.shape), full((1, d)), full((1, d)),
    ]
    args = [ya, yb, yc, zb, x_all, mod_l, mod_l, mod_l, wa, wb, wc, wo, ln_g, ln_b]
    out_specs = [pl.BlockSpec((tm, d), lambda i: (i, 0)), pl.BlockSpec((tm, d), lambda i: (i, 0))]
    out_shape = [jax.ShapeDtypeStruct((t, d), F32), jax.ShapeDtypeStruct((t, d), BF16)]
    if w_router is not None:
        in_specs.append(full(w_router.shape))
        args.append(w_router)
        out_specs.append(pl.BlockSpec((tm, LANES), lambda i: (i, 0)))
        out_shape.append(jax.ShapeDtypeStruct((t, LANES), F32))
    return pl.pallas_call(
        functools.partial(_merge_kernel, alpha, w_router is not None),
        grid=(t // tm,), in_specs=in_specs, out_specs=out_specs, out_shape=out_shape,
        compiler_params=_cparams(("arbitrary",)), name="merge",
    )(*args)


def _ffn_kernel(alpha, h_ref, x1_ref, g2_ref, wg_ref, wu_ref, wd_ref, lg_ref, lb_ref, o_ref, acc_s):
    f = pl.program_id(1)

    @pl.when(f == 0)
    def _():
        acc_s[...] = jnp.zeros_like(acc_s)

    h = h_ref[...]
    act = _silu(_dot(h, wg_ref[...].astype(BF16))) * _dot(h, wu_ref[...].astype(BF16))
    acc_s[...] += _dot(act.astype(BF16), wd_ref[...].astype(BF16))

    @pl.when(f == pl.num_programs(1) - 1)
    def _():
        o_ref[...] = _layer_norm(alpha * x1_ref[...] + g2_ref[...] * acc_s[...], lg_ref[...], lb_ref[...])


def _dense_mixer(h2, x1, mod_l, wg, wu, wd, ln_g, ln_b, alpha, tm, tf, row_of_tile):
    t, d = x1.shape
    ff = wg.shape[1]
    return pl.pallas_call(
        functools.partial(_ffn_kernel, alpha),
        grid=(t // tm, ff // tf),
        in_specs=[
            pl.BlockSpec((tm, d), lambda i, f: (i, 0)),
            pl.BlockSpec((tm, d), lambda i, f: (i, 0)),
            pl.BlockSpec((None, None, 1, d), lambda i, f: (row_of_tile(i), 5, 0, 0)),
            pl.BlockSpec((d, tf), lambda i, f: (0, f)),
            pl.BlockSpec((d, tf), lambda i, f: (0, f)),
            pl.BlockSpec((tf, d), lambda i, f: (f, 0)),
            pl.BlockSpec((1, d), lambda i, f: (0, 0)),
            pl.BlockSpec((1, d), lambda i, f: (0, 0)),
        ],
        out_specs=pl.BlockSpec((tm, d), lambda i, f: (i, 0)),
        out_shape=jax.ShapeDtypeStruct((t, d), F32),
        scratch_shapes=[pltpu.VMEM((tm, d), F32)],
        compiler_params=_cparams(("arbitrary", "arbitrary")), name="dense_mixer",
    )(h2, x1, mod_l, wg, wu, wd, ln_g, ln_b)


MOE_TILE = 256
MOE_TF = 1792


def _route_kernel(g_ref, rank_ref, gate_ref, before_ref, run_s):
    w = pl.program_id(0)

    @pl.when(w == 0)
    def _():
        run_s[...] = jnp.zeros_like(run_s)

    g = g_ref[...]
    sel = g > 0.0
    row = lax.broadcasted_iota(jnp.int32, (MOE_TILE, MOE_TILE), 0)
    col = lax.broadcasted_iota(jnp.int32, (MOE_TILE, MOE_TILE), 1)
    earlier = (col < row).astype(BF16)
    ones = jnp.where(sel, 1.0, 0.0)
    rank = _dot(earlier, ones.astype(BF16)) + run_s[...]
    rank = jnp.where(sel, rank, -1.0)
    before_ref[...] = run_s[...]
    run_s[...] += jnp.sum(ones, axis=0, keepdims=True)
    rank_ref[...] = rank.T[0:N_EXPERTS, :]
    gate_ref[...] = g.T[0:N_EXPERTS, :]


def _route(gates):
    t = gates.shape[0]
    nw = t // MOE_TILE
    return pl.pallas_call(
        _route_kernel,
        grid=(nw,),
        in_specs=[pl.BlockSpec((MOE_TILE, LANES), lambda w: (w, 0))],
        out_specs=[
            pl.BlockSpec((None, N_EXPERTS, MOE_TILE), lambda w: (w, 0, 0)),
            pl.BlockSpec((None, N_EXPERTS, MOE_TILE), lambda w: (w, 0, 0)),
            pl.BlockSpec((None, 1, LANES), lambda w: (w, 0, 0)),
        ],
        out_shape=[
            jax.ShapeDtypeStruct((nw, N_EXPERTS, MOE_TILE), F32),
            jax.ShapeDtypeStruct((nw, N_EXPERTS, MOE_TILE), F32),
            jax.ShapeDtypeStruct((nw, 1, LANES), F32),
        ],
        scratch_shapes=[pltpu.VMEM((1, LANES), F32)],
        compiler_params=_cparams(("arbitrary",)), name="moe_route",
    )(gates)


def _tile_onehot(rank_ref, w, e, base):
    rank_row = rank_ref[w, pl.ds(e, 1), :]
    rows = lax.broadcasted_iota(jnp.int32, (MOE_TILE, MOE_TILE), 0).astype(F32) + base.astype(F32)
    return rank_row == rows


def _gather_kernel(te_ref, base_ref, wlo_ref, whi_ref, rank_ref, x_ref, o_ref, acc_s):
    j = pl.program_id(0)
    e, base = te_ref[j], base_ref[j]
    acc_s[...] = jnp.zeros_like(acc_s)

    def window(w, carry):
        p = jnp.where(_tile_onehot(rank_ref, w, e, base), 1.0, 0.0).astype(BF16)
        acc_s[...] += _dot(p, x_ref[pl.ds(pl.multiple_of(w * MOE_TILE, MOE_TILE), MOE_TILE), :])
        return carry

    lax.fori_loop(wlo_ref[j], whi_ref[j] + 1, window, 0)
    o_ref[...] = acc_s[...].astype(BF16)


def _moe_gather(meta, rank_t, h2, n_tiles):
    t, d = h2.shape
    nw = t // MOE_TILE
    return pl.pallas_call(
        _gather_kernel,
        grid_spec=pltpu.PrefetchScalarGridSpec(
            num_scalar_prefetch=4, grid=(n_tiles,),
            in_specs=[
                pl.BlockSpec((nw, N_EXPERTS, MOE_TILE), lambda j, *_: (0, 0, 0)),
                pl.BlockSpec((t, d), lambda j, *_: (0, 0), pipeline_mode=pl.Buffered(1)),
            ],
            out_specs=pl.BlockSpec((MOE_TILE, d), lambda j, *_: (j, 0)),
            scratch_shapes=[pltpu.VMEM((MOE_TILE, d), F32)],
        ),
        out_shape=jax.ShapeDtypeStruct((n_tiles * MOE_TILE, d), BF16),
        compiler_params=_cparams(("arbitrary",)), name="moe_gather",
    )(*meta, rank_t, h2)


def _moe_up_kernel(te_ref, chg_ref, nused_ref, x_ref, wg_ref, wu_ref, o_ref, wg_s, wu_s):
    j = pl.program_id(1)

    @pl.when(chg_ref[j] == 1)
    def _():
        wg_s[...] = wg_ref[...].astype(BF16)
        wu_s[...] = wu_ref[...].astype(BF16)

    @pl.when(j < nused_ref[0])
    def _():
        x = x_ref[...]
        o_ref[...] = (_silu(_dot(x, wg_s[...])) * _dot(x, wu_s[...])).astype(BF16)

    @pl.when(j >= nused_ref[0])
    def _():
        o_ref[...] = jnp.zeros_like(o_ref)


def _moe_up(te, chg, nused, xs, wg, wu):
    rows, d = xs.shape
    n_tiles = rows // MOE_TILE
    ff = wg.shape[2]
    return pl.pallas_call(
        _moe_up_kernel,
        grid_spec=pltpu.PrefetchScalarGridSpec(
            num_scalar_prefetch=3, grid=(ff // MOE_TF, n_tiles),
            in_specs=[
                pl.BlockSpec((MOE_TILE, d), lambda f, j, *_: (j, 0)),
                pl.BlockSpec((None, d, MOE_TF), lambda f, j, te, *_: (te[j], 0, f)),
                pl.BlockSpec((None, d, MOE_TF), lambda f, j, te, *_: (te[j], 0, f)),
            ],
            out_specs=pl.BlockSpec((MOE_TILE, MOE_TF), lambda f, j, *_: (j, f)),
            scratch_shapes=[pltpu.VMEM((d, MOE_TF), BF16), pltpu.VMEM((d, MOE_TF), BF16)],
        ),
        out_shape=jax.ShapeDtypeStruct((rows, ff), BF16),
        compiler_params=_cparams(("arbitrary", "arbitrary")), name="moe_up",
    )(te, chg, nused, xs, wg, wu)


def _moe_down_kernel(te_ref, chg_ref, nused_ref, a_ref, wd_ref, o_ref, wd_s):
    j = pl.program_id(0)

    @pl.when(chg_ref[j] == 1)
    def _():
        wd_s[...] = wd_ref[...].astype(BF16)

    @pl.when(j < nused_ref[0])
    def _():
        o_ref[...] = _dot(a_ref[...], wd_s[...]).astype(BF16)

    @pl.when(j >= nused_ref[0])
    def _():
        o_ref[...] = jnp.zeros_like(o_ref)


def _moe_down(te, chg, nused, act, wd):
    rows, ff = act.shape
    d = wd.shape[2]
    return pl.pallas_call(
        _moe_down_kernel,
        grid_spec=pltpu.PrefetchScalarGridSpec(
            num_scalar_prefetch=3, grid=(rows // MOE_TILE,),
            in_specs=[
                pl.BlockSpec((MOE_TILE, ff), lambda j, *_: (j, 0)),
                pl.BlockSpec((None, ff, d), lambda j, te, *_: (te[j], 0, 0)),
            ],
            out_specs=pl.BlockSpec((MOE_TILE, d), lambda j, *_: (j, 0)),
            scratch_shapes=[pltpu.VMEM((ff, d), BF16)],
        ),
        out_shape=jax.ShapeDtypeStruct((rows, d), BF16),
        compiler_params=_cparams(("arbitrary",)), name="moe_down",
    )(te, chg, nused, act, wd)


def _combine_kernel(alpha, n_tiles, nw_ctx, te_ref, base_ref, wlo_ref, whi_ref, rank_ref, gate_ref, y_ref, x1_ref,
                    g2_ref, lg_ref, lb_ref, oc_ref, ol_ref, acc_s):
    s = pl.program_id(0)

    @pl.when(s == 0)
    def _():
        acc_s[...] = jnp.zeros_like(acc_s)

    @pl.when(s < n_tiles)
    def _():
        e, base = te_ref[s], base_ref[s]

        def window(w, carry):
            hit = _tile_onehot(rank_ref, w, e, base)
            q = jnp.where(hit, gate_ref[w, pl.ds(e, 1), :], 0.0).T.astype(BF16)
            rows = pl.ds(pl.multiple_of(w * MOE_TILE, MOE_TILE), MOE_TILE)
            acc_s[rows, :] += _dot(q, y_ref[...])
            return carry

        lax.fori_loop(wlo_ref[s], whi_ref[s] + 1, window, 0)

    @pl.when(s >= n_tiles)
    def _():
        w = s - n_tiles
        rows = pl.ds(pl.multiple_of(w * MOE_TILE, MOE_TILE), MOE_TILE)
        out = _layer_norm(alpha * x1_ref[...] + g2_ref[...] * acc_s[rows, :], lg_ref[...], lb_ref[...])

        @pl.when(w < nw_ctx)
        def _():
            oc_ref[...] = out

        @pl.when(w >= nw_ctx)
        def _():
            ol_ref[...] = out


def _moe_combine(meta, rank_t, gate_t, y, x1, mod_l, ln_g, ln_b, alpha, row_of_tile, t_ctx):
    t, d = x1.shape
    nw = t // MOE_TILE
    nw_ctx = t_ctx // MOE_TILE
    n_tiles = y.shape[0] // MOE_TILE
    win = lambda s: jnp.maximum(s - n_tiles, 0)
    return pl.pallas_call(
        functools.partial(_combine_kernel, alpha, n_tiles, nw_ctx),
        grid_spec=pltpu.PrefetchScalarGridSpec(
            num_scalar_prefetch=4, grid=(n_tiles + nw,),
            in_specs=[
                pl.BlockSpec((nw, N_EXPERTS, MOE_TILE), lambda s, *_: (0, 0, 0)),
                pl.BlockSpec((nw, N_EXPERTS, MOE_TILE), lambda s, *_: (0, 0, 0)),
                pl.BlockSpec((MOE_TILE, d), lambda s, *_: (jnp.minimum(s, n_tiles - 1), 0)),
                pl.BlockSpec((MOE_TILE, d), lambda s, *_: (win(s), 0)),
                pl.BlockSpec((None, None, 1, d), lambda s, *_: (row_of_tile(win(s)), 5, 0, 0)),
                pl.BlockSpec((1, d), lambda s, *_: (0, 0)),
                pl.BlockSpec((1, d), lambda s, *_: (0, 0)),
            ],
            out_specs=[
                pl.BlockSpec((MOE_TILE, d), lambda s, *_: (jnp.minimum(win(s), nw_ctx - 1), 0)),
                pl.BlockSpec((MOE_TILE, d), lambda s, *_: (jnp.maximum(win(s) - nw_ctx, 0), 0)),
            ],
            scratch_shapes=[pltpu.VMEM((t, d), F32)],
        ),
        out_shape=[jax.ShapeDtypeStruct((t_ctx, d), F32), jax.ShapeDtypeStruct((t - t_ctx, d), F32)],
        compiler_params=_cparams(("arbitrary",)), name="moe_combine",
    )(*meta, rank_t, gate_t, y, x1, mod_l, ln_g, ln_b)


def _moe_layer(gates, h2, x1, mod_l, wg, wu, wd, ln_g, ln_b, alpha, row_of_tile, t_ctx):
    t = h2.shape[0]
    n_tiles = -(-(2 * t + N_EXPERTS * (MOE_TILE - 1)) // MOE_TILE)
    rank_t, gate_t, before = _route(gates)
    before = before[:, 0, :N_EXPERTS].astype(jnp.int32)
    counts = jnp.sum(gates[:, :N_EXPERTS] > 0.0, axis=0).astype(jnp.int32)
    tiles_e = (counts + MOE_TILE - 1) // MOE_TILE
    tile_end = jnp.cumsum(tiles_e)
    n_used = tile_end[-1]
    j = jnp.arange(n_tiles, dtype=jnp.int32)
    te = jnp.minimum(jnp.sum(j[:, None] >= tile_end[None, :], axis=1), N_EXPERTS - 1).astype(jnp.int32)
    base = (j - jnp.take(tile_end - tiles_e, te)) * MOE_TILE
    last = jnp.minimum(base + MOE_TILE - 1, jnp.take(counts, te) - 1)
    before_t = jnp.take(before, te, axis=1)
    used = j < n_used
    wlo = jnp.where(used, jnp.sum(before_t <= base[None, :], axis=0) - 1, 1).astype(jnp.int32)
    whi = jnp.where(used, jnp.sum(before_t <= last[None, :], axis=0) - 1, 0).astype(jnp.int32)
    chg = jnp.concatenate([jnp.ones((1,), jnp.int32), (te[1:] != te[:-1]).astype(jnp.int32)])
    nused = n_used.reshape(1).astype(jnp.int32)
    meta = (te, base.astype(jnp.int32), wlo, whi)
    xs = _moe_gather(meta, rank_t, h2, n_tiles)
    act = _moe_up(te, chg, nused, xs, wg, wu)
    y = _moe_down(te, chg, nused, act, wd)
    return _moe_combine(meta, rank_t, gate_t, y, x1, mod_l, ln_g, ln_b, alpha, row_of_tile, t_ctx)


def _rope_tables(seq):
    half = GLA_DK // 2
    n = half // 2
    t = np.arange(seq)
    inv = ROPE_BASE ** (-np.arange(n, dtype=np.float64) / n)
    cos = np.zeros((seq, GLA_DK), np.float64)
    sin = np.zeros((seq, GLA_DK), np.float64)
    for a, pos in enumerate((t // GRID_W, t % GRID_W)):
        ang = pos[:, None].astype(np.float64) * inv[None, :]
        base = a * half
        cos[:, base:base + n] = np.cos(ang)
        cos[:, base + n:base + half] = np.cos(ang)
        sin[:, base:base + n] = -np.sin(ang)
        sin[:, base + n:base + half] = np.sin(ang)
    tile = lambda m: jnp.asarray(np.tile(m, (1, GLA_HEADS)), F32)
    return tile(cos), tile(sin)


def _chunk_w_in(w):
    gda = jnp.pad(w[:, 1536:1568], ((0, 0), (0, CHUNK_W - 2 * GLA_RANK)))
    starts = (2080, 2592, 0, 512, 1024, 1568, 3104, 3616, 4128, 4640, 5152, 5664, 6176, 6688)
    return jnp.stack([gda] + [w[:, s:s + CHUNK_W] for s in starts]).astype(BF16)


def kernel(x_prompt, x_sample, cache_na_k, cache_na_v, state_gla, c, c_ctx, w_mod, b_mod, w_in, gla_w_gup, gla_b_g, gla_norm, w_br_gla, na_rpb, w_br_na, cv_w, cv_b, cv_ln_g, cv_ln_b, w_br_cv, w_out, ln_g, ln_b, ffd_w_gate, ffd_w_up, ffd_w_down, moe_w_router, moe_w_gate, moe_w_up, moe_w_down):
    nb_c, l_c, d = x_prompt.shape
    nb_l, l_l, _ = x_sample.shape
    depth = w_mod.shape[0]
    t_ctx, t_lat = nb_c * l_c, nb_l * l_l
    t_all = t_ctx + t_lat
    alpha = (2 * depth) ** 0.25
    tm = 512
    row_of_tile = _row_of_tile_fn(t_ctx, l_l, tm)

    x_all = jnp.concatenate([x_prompt.reshape(t_ctx, d), x_sample.reshape(t_lat, d)], axis=0)
    cond8 = jnp.concatenate([c_ctx[None, :], c, jnp.zeros((8 - 1 - nb_l, d), F32)], axis=0)
    mod = _modulation(cond8, w_mod, b_mod).reshape(depth, 8, 6, 1, d)
    rope = _rope_tables(l_l)
    rows_lat = l_l // GRID_W
    kr = min(NA_WIN_R, rows_lat)
    s0_all = state_gla.reshape(nb_l, depth, 2, 2, LANES, GLA_DV)

    ks, vs, ss = [], [], []
    for l in range(depth):
        mod_l = mod[l]
        zf, zb = _input_projection(x_all, mod_l, _chunk_w_in(w_in[l]), tm, row_of_tile)

        wg_blk = jnp.zeros((LANES, CHUNK_W), F32)
        wg_blk = wg_blk.at[0:GLA_RANK, 0:256].set(gla_w_gup[l, 0]).at[GLA_RANK:2 * GLA_RANK, 256:512].set(gla_w_gup[l, 1])
        bg = gla_b_g[l].reshape(1, CHUNK_W)
        gn = gla_norm[l].reshape(1, CHUNK_W)
        ya, s_fin = _gla_branch(zf, zb, wg_blk, bg, gn, latent=False, seq=l_c, nb=nb_c, tok0=0, t_total=t_all)
        ya = _gla_branch(zf, zb, wg_blk, bg, gn, latent=True, seq=l_l, nb=nb_l, tok0=t_ctx, t_total=t_all,
                         rope=rope, s0=s0_all[:, l], ya_prev=ya)

        yb = _attn_context(zf, zb, nb_c, l_c, t_all)
        bias = _na_bias_table(na_rpb[l], kr)
        yb = _attn_latent(zf, zb, cache_na_k, cache_na_v, l, bias, yb, nb_l, l_l, t_ctx)

        conv_args = (zb, cv_w[l], cv_b[l][None, :], cv_ln_g[l][None, :], cv_ln_b[l][None, :])
        yc = _conv_branch(*conv_args, seq=l_c, nb=nb_c, tok0=0, t_total=t_all)
        yc = _conv_branch(*conv_args, seq=l_l, nb=nb_l, tok0=t_ctx, t_total=t_all, prev=yc)

        moe = l % 2 == 1
        w_router = None
        if moe:
            w_router = jnp.pad(moe_w_router[l // 2], ((0, 0), (0, LANES - N_EXPERTS)))
        merged = _merge(ya, yb, yc, zb, x_all, mod_l,
                        w_br_gla[l].astype(BF16), w_br_na[l].astype(BF16), w_br_cv[l].astype(BF16),
                        w_out[l].astype(BF16), ln_g[l, 0][None, :], ln_b[l, 0][None, :], alpha, tm, row_of_tile,
                        w_router)
        ln2 = (ln_g[l, 1][None, :], ln_b[l, 1][None, :])
        if moe:
            x1, h2, gates = merged
            x_split = _moe_layer(gates, h2, x1, mod_l, moe_w_gate[l // 2], moe_w_up[l // 2], moe_w_down[l // 2],
                                 *ln2, alpha, _row_of_tile_fn(t_ctx, l_l, MOE_TILE), t_ctx)
            x_all = None if l == depth - 1 else jnp.concatenate(x_split, axis=0)
        else:
            x1, h2 = merged
            i = l // 2
            x_all = _dense_mixer(h2, x1, mod_l, ffd_w_gate[i], ffd_w_up[i], ffd_w_down[i],
                                 *ln2, alpha, 1024, 256, _row_of_tile_fn(t_ctx, l_l, 1024))

        ks.append(zf[1, :t_ctx].reshape(nb_c, l_c, NA_HEADS, NA_DH))
        vs.append(zf[2, :t_ctx].reshape(nb_c, l_c, NA_HEADS, NA_DH))
        ss.append(s_fin.reshape(nb_c, 2, GLA_HEADS, GLA_DK, GLA_DV))

    if x_all is not None:
        x_split = (x_all[:t_ctx], x_all[t_ctx:])
    y_prompt = x_split[0].reshape(nb_c, l_c, d)
    y_sample = x_split[1].reshape(nb_l, l_l, d)
    return (y_prompt, y_sample, jnp.stack(ks, 1), jnp.stack(vs, 1), jnp.stack(ss, 1))


def _row_of_tile_fn(t_ctx, l_lat, tm):
    n_ctx = t_ctx // tm

    def row_of_tile(i):
        return jnp.where(i < n_ctx, 0, 1 + (i - n_ctx) // (l_lat // tm))

    return row_of_tile
```

```python
import functools

import numpy as np
import jax
import jax.numpy as jnp
from jax import lax
from jax.experimental import pallas as pl
from jax.experimental.pallas import tpu as pltpu

F32 = jnp.float32
BF16 = jnp.bfloat16

GRID_W = 64
GLA_HEADS = 4
GLA_DK = 64
GLA_DV = 128
GLA_RANK = 16
GLA_TAU = 16.0
GLA_CHUNK = 64
GLA_GROUP = 256
NA_HEADS = 8
NA_DH = 64
NA_WIN_R = 8
NA_WIN_C = 16
CONV_CH = 512
CONV_K = 31
N_EXPERTS = 8
ROPE_BASE = 10000.0
LN_EPS = 1e-5

D_MODEL = 1024
LANES = 128
SUBLANES = 8
CHUNK_W = 512
N_BF16_CHUNKS = 12
NEG_BIG = -1e30
VMEM_LIMIT = 56 * 1024 * 1024


def _cparams(sem, vmem=VMEM_LIMIT):
    return pltpu.CompilerParams(dimension_semantics=sem, vmem_limit_bytes=vmem)


def _dot(a, b):
    return jnp.dot(a, b, preferred_element_type=F32)


def _split(x):
    hi = x.astype(BF16)
    return hi, (x - hi.astype(F32)).astype(BF16)


def _dot_split(a, b):
    a_hi, a_lo = _split(a)
    b_hi, b_lo = _split(b)
    return _dot(a_hi, b_hi) + _dot(a_lo, b_hi) + _dot(a_hi, b_lo)


def _dot_split_rhs(a, b):
    b_hi, b_lo = _split(b)
    return _dot(a, b_hi) + _dot(a, b_lo)


def _dot_nt(a, b):
    return lax.dot_general(a, b, (((1,), (1,)), ((), ())), preferred_element_type=F32)


def _sigmoid(x):
    return 0.5 * jnp.tanh(0.5 * x) + 0.5


def _silu(x):
    return x * _sigmoid(x)


def _layer_norm(x, g, b):
    mu = jnp.mean(x, axis=-1, keepdims=True)
    xc = x - mu
    var = jnp.mean(xc * xc, axis=-1, keepdims=True)
    return xc * lax.rsqrt(var + LN_EPS) * g + b


def _mod_kernel(c_ref, w_ref, b_ref, o_ref):
    c = c_ref[...]
    s = _silu(c).astype(BF16)
    o_ref[...] = _dot(s, w_ref[...].astype(BF16)) + b_ref[...]


def _modulation(cond8, w_mod, b_mod):
    depth, d, n = w_mod.shape
    tn = 1024
    return pl.pallas_call(
        _mod_kernel,
        grid=(depth, n // tn),
        in_specs=[
            pl.BlockSpec((8, d), lambda l, j: (0, 0)),
            pl.BlockSpec((None, d, tn), lambda l, j: (l, 0, j)),
            pl.BlockSpec((None, 1, tn), lambda l, j: (l, 0, j)),
        ],
        out_specs=pl.BlockSpec((None, 8, tn), lambda l, j: (l, 0, j)),
        out_shape=jax.ShapeDtypeStruct((depth, 8, n), F32),
        compiler_params=_cparams(("arbitrary", "arbitrary")),
        name="modulation",
    )(cond8, w_mod, b_mod.reshape(depth, 1, n))


def _inproj_kernel(n_ctx_tiles, aliased, *refs):
    if aliased:
        (xc_ref, xl_ref, sc_ref, sh_ref, wgd_ref, wkv_ref, w_ref, _kc_in, _vc_in,
         gd_ref, kc_ref, vc_ref, kl_ref, vl_ref, zb_ref, h_ref) = refs
    else:
        (xc_ref, xl_ref, sc_ref, sh_ref, wgd_ref, wkv_ref, w_ref,
         gd_ref, kc_ref, vc_ref, kl_ref, vl_ref, zb_ref, h_ref) = refs
    is_ctx = pl.program_id(0) < n_ctx_tiles
    x = jnp.where(is_ctx, xc_ref[...], xl_ref[...])
    h_ref[...] = (x * (1.0 + sc_ref[...]) + sh_ref[...]).astype(BF16)
    gd_ref[...] = _dot(h_ref[...], wgd_ref[...])
    k = _dot(h_ref[...], wkv_ref[0])
    v = _dot(h_ref[...], wkv_ref[1])
    seqs, l_c, _ = kc_ref.shape

    @pl.when(is_ctx)
    def _():
        for s in range(seqs):
            kc_ref[s] = k[s * l_c:(s + 1) * l_c]
            vc_ref[s] = v[s * l_c:(s + 1) * l_c]

    @pl.when(jnp.logical_not(is_ctx))
    def _():
        kl_ref[...] = k
        vl_ref[...] = v

    for j in range(N_BF16_CHUNKS):
        zb_ref[j] = _dot(h_ref[...], w_ref[j]).astype(BF16)


def _mod_spec(piece, row_of_tile):
    return pl.BlockSpec((None, None, 1, D_MODEL), lambda i: (row_of_tile(i), piece, 0, 0))


def _pair_specs(tm, d, n_ctx_tiles):
    return [pl.BlockSpec((tm, d), lambda i, *_: (jnp.minimum(i, n_ctx_tiles - 1), 0)),
            pl.BlockSpec((tm, d), lambda i, *_: (jnp.maximum(i - n_ctx_tiles, 0), 0))]


def _input_projection(x_pair, mod_l, w_parts, layer, depth, l_c, tm, row_of_tile, caches=None):
    xc, xl = x_pair
    w_gd, w_kv, w_chunks = w_parts
    d = xc.shape[1]
    t_ctx, t_lat = xc.shape[0], xl.shape[0]
    t = t_ctx + t_lat
    n_ctx_tiles = t_ctx // tm
    seqs = tm // l_c
    cache_spec = pl.BlockSpec((seqs, None, l_c, CHUNK_W),
                              lambda i: (jnp.minimum(i, n_ctx_tiles - 1), layer, 0, 0))
    lat_spec = pl.BlockSpec((tm, CHUNK_W), lambda i: (jnp.maximum(i - n_ctx_tiles, 0), 0))
    const = lambda shape: pl.BlockSpec(shape, lambda i: tuple(0 for _ in shape), pipeline_mode=pl.Buffered(1))
    in_specs = _pair_specs(tm, d, n_ctx_tiles) + [
        _mod_spec(1, row_of_tile), _mod_spec(0, row_of_tile),
        const(w_gd.shape), const(w_kv.shape), const(w_chunks.shape),
    ]
    args = [xc, xl, mod_l, mod_l, w_gd, w_kv, w_chunks]
    aliases = {}
    if caches is not None:
        in_specs += [pl.BlockSpec(memory_space=pl.ANY), pl.BlockSpec(memory_space=pl.ANY)]
        args += list(caches)
        aliases = {7: 1, 8: 2}
    cache_shape = jax.ShapeDtypeStruct((t_ctx // l_c, depth, l_c, CHUNK_W), F32)
    return pl.pallas_call(
        functools.partial(_inproj_kernel, n_ctx_tiles, caches is not None),
        grid=(t // tm,),
        in_specs=in_specs,
        out_specs=[
            pl.BlockSpec((tm, LANES), lambda i: (i, 0)),
            cache_spec, cache_spec, lat_spec, lat_spec,
            pl.BlockSpec((N_BF16_CHUNKS, tm, CHUNK_W), lambda i: (0, i, 0)),
        ],
        out_shape=[
            jax.ShapeDtypeStruct((t, LANES), F32),
            cache_shape, cache_shape,
            jax.ShapeDtypeStruct((t_lat, CHUNK_W), F32), jax.ShapeDtypeStruct((t_lat, CHUNK_W), F32),
            jax.ShapeDtypeStruct((N_BF16_CHUNKS, t, CHUNK_W), BF16),
        ],
        scratch_shapes=[pltpu.VMEM((tm, d), BF16)],
        input_output_aliases=aliases,
        compiler_params=_cparams(("arbitrary",)),
        name="input_projection",
    )(*args)


def _gla_kernel(latent, seq, *refs):
    if latent:
        (qk_ref, v_ref, ra_ref, gd_ref, wg_ref, bg_ref, gn_ref, cos_ref, sin_ref, s0_ref, _alias,
         ya_ref, q_s, k_s, g_s, acc_s, st_s) = refs
    else:
        (qk_ref, v_ref, ra_ref, gd_ref, wg_ref, bg_ref, gn_ref,
         ya_ref, sfin_ref, q_s, k_s, g_s, acc_s, st_s) = refs
    c = GLA_CHUNK
    n_chunks = seq // c
    qkw = GLA_HEADS * GLA_DK

    q = qk_ref[:, 0:qkw].astype(F32) * (GLA_DK ** -0.5)
    k = qk_ref[:, qkw:2 * qkw].astype(F32)
    if latent:
        lane = lax.broadcasted_iota(jnp.int32, (1, qkw), 1)
        first_half = (lane % 32) < 16

        def rope(x):
            swapped = jnp.where(first_half, pltpu.roll(x, qkw - 16, 1), pltpu.roll(x, 16, 1))
            return x * cos_ref[...] + swapped * sin_ref[...]

        q, k = rope(q), rope(k)
    q_s[...] = q
    k_s[...] = k

    glog = _dot_split(gd_ref[...], wg_ref[...]) + bg_ref[...]
    g_s[...] = (jnp.minimum(glog, 0.0) - jnp.log(1.0 + jnp.exp(-jnp.abs(glog)))) * (1.0 / GLA_TAU)
    acc_s[...] = jnp.zeros_like(acc_s)
    if latent:
        st_s[...] = s0_ref[...]
    else:
        st_s[...] = jnp.zeros_like(st_s)

    grp = GLA_GROUP
    cpg = grp // c
    n_groups = seq // grp
    grow = lax.broadcasted_iota(jnp.int32, (grp, grp), 0)
    gcol = lax.broadcasted_iota(jnp.int32, (grp, grp), 1)
    same_chunk = (grow // c) == (gcol // c)
    keep = (same_chunk & (gcol <= grow), same_chunk & (gcol >= grow))
    tri = (keep[0].astype(BF16), keep[1].astype(BF16))
    lane128 = lax.broadcasted_iota(jnp.int32, (1, LANES), 1)
    head_mask = (lane128 < GLA_DK, lane128 >= GLA_DK)
    urow = lax.broadcasted_iota(jnp.int32, (cpg * LANES, GLA_DV), 0)
    chunk_of_col = lax.broadcasted_iota(jnp.int32, (LANES, grp), 1) // c
    chunk_of_row = lax.broadcasted_iota(jnp.int32, (grp, LANES), 0) // c
    pad_rows = jnp.zeros((SUBLANES - cpg, qkw), F32)

    def group_step(i, carry):
        for d in range(2):
            gi = i if d == 0 else n_groups - 1 - i
            rows = pl.ds(pl.multiple_of(gi * grp, grp), grp)
            cum = _dot_split_rhs(tri[d], g_s[rows, d * qkw:(d + 1) * qkw])
            edge = c - 1 if d == 0 else 0
            lasts = [cum[ci * c + edge:ci * c + edge + 1, :] for ci in range(cpg)]
            last = jnp.concatenate([jnp.broadcast_to(r, (c, qkw)) for r in lasts], axis=0)
            ref = 0.5 * last
            q = q_s[rows, :]
            k = k_s[rows, :]
            qe = q * jnp.exp(cum - ref)
            ke = k * jnp.exp(ref - cum)
            qe2 = q * jnp.exp(cum)
            ke2 = k * jnp.exp(last - cum)
            last8 = jnp.concatenate(lasts + [pad_rows], axis=0)
            for p in range(2):
                sl = slice(p * LANES, (p + 1) * LANES)
                ke_p = ke[:, sl].astype(BF16)
                ke2_t = ke2[:, sl].T
                ke2_blk = jnp.concatenate(
                    [jnp.where(chunk_of_col == ci, ke2_t, 0.0) for ci in range(cpg)], axis=0).astype(BF16)
                dec = jnp.exp(last8[:, sl]).T
                vs = [v_ref[rows, (2 * p + hh) * GLA_DV:(2 * p + hh + 1) * GLA_DV] for hh in range(2)]
                u = [_dot(ke2_blk, vs[hh]) for hh in range(2)]
                upd = jnp.where(urow % LANES < GLA_DK, u[0], u[1])
                s = st_s[d, p]
                s_in = [None] * cpg
                for ci in (range(cpg) if d == 0 else range(cpg - 1, -1, -1)):
                    s_in[ci] = s.astype(BF16)
                    s = dec[:, ci:ci + 1] * s + upd[ci * LANES:(ci + 1) * LANES]
                st_s[d, p] = s
                s_stack = jnp.concatenate(s_in, axis=0)
                for hh in range(2):
                    vsl = slice((2 * p + hh) * GLA_DV, (2 * p + hh + 1) * GLA_DV)
                    a = jnp.where(head_mask[hh], qe[:, sl], 0.0).astype(BF16)
                    a2 = jnp.where(head_mask[hh], qe2[:, sl], 0.0)
                    a2_blk = jnp.concatenate(
                        [jnp.where(chunk_of_row == ci, a2, 0.0) for ci in range(cpg)], axis=1).astype(BF16)
                    att = jnp.where(keep[d], _dot_nt(a, ke_p), 0.0).astype(BF16)
                    acc_s[rows, vsl] += _dot(att, vs[hh]) + _dot(a2_blk, s_stack)
        return carry

    lax.fori_loop(0, n_groups, group_step, 0)

    def epilogue(i, carry):
        rows = pl.ds(pl.multiple_of(i * c, c), c)
        for h in range(GLA_HEADS):
            vsl = slice(h * GLA_DV, (h + 1) * GLA_DV)
            o = acc_s[rows, vsl]
            o = o * lax.rsqrt(jnp.mean(o * o, axis=-1, keepdims=True) + LN_EPS) * gn_ref[:, vsl]
            ya_ref[rows, vsl] = (o * _silu(ra_ref[rows, vsl].astype(F32))).astype(BF16)
        return carry

    lax.fori_loop(0, n_chunks, epilogue, 0)
    if not latent:
        sfin_ref[...] = st_s[...]


def _gla_branch(gd, zb, wg_blk, bg, gn, *, latent, seq, nb, tok0, t_total, rope=None, s0=None, ya_prev=None):
    b0 = tok0 // seq
    qkw = GLA_HEADS * GLA_DK
    in_specs = [
        pl.BlockSpec((None, seq, CHUNK_W), lambda b: (0, b0 + b, 0)),
        pl.BlockSpec((None, seq, CHUNK_W), lambda b: (1, b0 + b, 0)),
        pl.BlockSpec((None, seq, CHUNK_W), lambda b: (2, b0 + b, 0)),
        pl.BlockSpec((seq, LANES), lambda b: (b0 + b, 0)),
        pl.BlockSpec((LANES, CHUNK_W), lambda b: (0, 0)),
        pl.BlockSpec((1, CHUNK_W), lambda b: (0, 0)),
        pl.BlockSpec((1, CHUNK_W), lambda b: (0, 0)),
    ]
    args = [zb, zb, zb, gd, wg_blk, bg, gn]
    scratch = [
        pltpu.VMEM((seq, qkw), F32), pltpu.VMEM((seq, qkw), F32), pltpu.VMEM((seq, CHUNK_W), F32),
        pltpu.VMEM((seq, CHUNK_W), F32), pltpu.VMEM((2, 2, LANES, GLA_DV), F32),
    ]
    ya_shape = jax.ShapeDtypeStruct((t_total, CHUNK_W), BF16)
    ya_spec = pl.BlockSpec((seq, CHUNK_W), lambda b: (b0 + b, 0))
    if latent:
        cos_t, sin_t = rope
        in_specs += [
            pl.BlockSpec((seq, qkw), lambda b: (0, 0)),
            pl.BlockSpec((seq, qkw), lambda b: (0, 0)),
            pl.BlockSpec((None, 2, 2, LANES, GLA_DV), lambda b: (b, 0, 0, 0, 0)),
            pl.BlockSpec(memory_space=pl.ANY),
        ]
        args += [cos_t, sin_t, s0, ya_prev]
        return pl.pallas_call(
            functools.partial(_gla_kernel, True, seq),
            grid=(nb,), in_specs=in_specs, out_specs=ya_spec, out_shape=ya_shape,
            scratch_shapes=scratch, input_output_aliases={len(args) - 1: 0},
            compiler_params=_cparams(("arbitrary",)), name="gla_latent",
        )(*args)
    return pl.pallas_call(
        functools.partial(_gla_kernel, False, seq),
        grid=(nb,), in_specs=in_specs,
        out_specs=[ya_spec, pl.BlockSpec((None, 2, 2, LANES, GLA_DV), lambda b: (b, 0, 0, 0, 0))],
        out_shape=[ya_shape, jax.ShapeDtypeStruct((nb, 2, 2, LANES, GLA_DV), F32)],
        scratch_shapes=scratch,
        compiler_params=_cparams(("arbitrary",)), name="gla_context",
    )(*args)


def _attn_ctx_kernel(q_ref, k_ref, v_ref, o_ref):
    lane = lax.broadcasted_iota(jnp.int32, (1, LANES), 1)
    masks = (lane < NA_DH, lane >= NA_DH)
    scale = NA_DH ** -0.5
    for p in range(NA_HEADS // 2):
        sl = slice(p * LANES, (p + 1) * LANES)
        qp = q_ref[:, sl]
        kp = k_ref[:, sl].astype(BF16)
        vp = v_ref[:, sl].astype(BF16)
        outs = []
        for hh in range(2):
            qm = jnp.where(masks[hh], qp, jnp.zeros_like(qp))
            s = _dot_nt(qm, kp) * scale
            e = jnp.exp(s - jnp.max(s, axis=-1, keepdims=True))
            outs.append(_dot(e.astype(BF16), vp) / jnp.sum(e, axis=-1, keepdims=True))
        o_ref[:, sl] = jnp.where(masks[0], outs[0], outs[1]).astype(BF16)


def _attn_context(zb, kc, vc, layer, nb, seq, t_total):
    return pl.pallas_call(
        _attn_ctx_kernel,
        grid=(nb,),
        in_specs=[
            pl.BlockSpec((None, seq, CHUNK_W), lambda b: (3, b, 0)),
            pl.BlockSpec((None, None, seq, CHUNK_W), lambda b: (b, layer, 0, 0)),
            pl.BlockSpec((None, None, seq, CHUNK_W), lambda b: (b, layer, 0, 0)),
        ],
        out_specs=pl.BlockSpec((seq, CHUNK_W), lambda b: (b, 0)),
        out_shape=jax.ShapeDtypeStruct((t_total, CHUNK_W), BF16),
        compiler_params=_cparams(("arbitrary",)), name="attention_context",
    )(zb, kc, vc)


def _na_kernel(rows_total, kr, q_ref, kl_ref, vl_ref, kc_ref, vc_ref, bias_ref, _alias, o_ref):
    r = pl.program_id(1)
    rs = jnp.clip(r - kr // 2, 0, rows_total - kr)
    krows = pl.ds(pl.multiple_of(rs * GRID_W, GRID_W), kr * GRID_W)
    lane = lax.broadcasted_iota(jnp.int32, (1, LANES), 1)
    masks = (lane < NA_DH, lane >= NA_DH)
    scale = NA_DH ** -0.5
    for p in range(NA_HEADS // 2):
        sl = slice(p * LANES, (p + 1) * LANES)
        qp = q_ref[:, sl]
        klp = kl_ref[krows, sl].astype(BF16)
        vlp = vl_ref[krows, sl].astype(BF16)
        kcp = kc_ref[:, sl].astype(BF16)
        vcp = vc_ref[:, sl].astype(BF16)
        outs = []
        for hh in range(2):
            qm = jnp.where(masks[hh], qp, jnp.zeros_like(qp))
            s_loc = _dot_nt(qm, klp) * scale + bias_ref[2 * p + hh]
            s_ctx = _dot_nt(qm, kcp) * scale
            m = jnp.maximum(jnp.max(s_loc, axis=-1, keepdims=True), jnp.max(s_ctx, axis=-1, keepdims=True))
            e_loc = jnp.exp(s_loc - m)
            e_ctx = jnp.exp(s_ctx - m)
            den = jnp.sum(e_loc, axis=-1, keepdims=True) + jnp.sum(e_ctx, axis=-1, keepdims=True)
            outs.append((_dot(e_loc.astype(BF16), vlp) + _dot(e_ctx.astype(BF16), vcp)) / den)
        o_ref[:, sl] = jnp.where(masks[0], outs[0], outs[1]).astype(BF16)


def _na_bias_table(rpb, kr):
    heads, nr, nc = rpb.shape
    qc = np.arange(GRID_W)[:, None]
    kc = np.arange(GRID_W)[None, :]
    cs = np.clip(qc - NA_WIN_C // 2, 0, GRID_W - NA_WIN_C)
    valid = (kc >= cs) & (kc < cs + NA_WIN_C)
    period = 2 * GRID_W
    lead = GRID_W - 1 - (NA_WIN_C - 1)
    v = jnp.pad(rpb.astype(F32), ((0, 0), (0, 0), (lead, period - lead - nc)))
    skew = jnp.tile(v, (1, 1, GRID_W))[:, :, :GRID_W * (period - 1)].reshape(heads, nr, GRID_W, period - 1)
    toep = jnp.where(valid, skew[..., GRID_W - 1:], NEG_BIG)
    pats = []
    for p in range(kr):
        pats.append(jnp.concatenate([toep[:, i - p + NA_WIN_R - 1] for i in range(kr)], axis=-1))
    return jnp.stack(pats)


def _attn_latent(zb, kl, vl, cache_k, cache_v, layer, bias, att_prev, nb, seq, tok0):
    rows_total = seq // GRID_W
    kr = min(NA_WIN_R, rows_total)
    q0 = tok0 // GRID_W
    past = cache_k.shape[2]
    ck = cache_k.reshape(cache_k.shape[0], cache_k.shape[1], past, NA_HEADS * NA_DH)
    cv = cache_v.reshape(ck.shape)
    return pl.pallas_call(
        functools.partial(_na_kernel, rows_total, kr),
        grid=(nb, rows_total),
        in_specs=[
            pl.BlockSpec((None, GRID_W, CHUNK_W), lambda b, r: (3, q0 + b * rows_total + r, 0)),
            pl.BlockSpec((seq, CHUNK_W), lambda b, r: (b, 0)),
            pl.BlockSpec((seq, CHUNK_W), lambda b, r: (b, 0)),
            pl.BlockSpec((None, None, past, CHUNK_W), lambda b, r: (b, layer, 0, 0)),
            pl.BlockSpec((None, None, past, CHUNK_W), lambda b, r: (b, layer, 0, 0)),
            pl.BlockSpec((None, NA_HEADS, GRID_W, kr * GRID_W),
                         lambda b, r: (r - jnp.clip(r - kr // 2, 0, rows_total - kr), 0, 0, 0)),
            pl.BlockSpec(memory_space=pl.ANY),
        ],
        out_specs=pl.BlockSpec((GRID_W, CHUNK_W), lambda b, r: (q0 + b * rows_total + r, 0)),
        out_shape=jax.ShapeDtypeStruct(att_prev.shape, BF16),
        input_output_aliases={6: 0},
        compiler_params=_cparams(("arbitrary", "arbitrary")), name="attention_latent",
    )(zb, kl, vl, ck, cv, bias, att_prev)


CONV_PAD = 16
CONV_ROWS = 64


def _conv_kernel(seq, aliased, *refs):
    if aliased:
        a_ref, gt_ref, w_ref, cb_ref, lg_ref, lb_ref, _alias, o_ref, z_s, sh_s = refs
    else:
        a_ref, gt_ref, w_ref, cb_ref, lg_ref, lb_ref, o_ref, z_s, sh_s = refs
    z_s[0:CONV_PAD, :] = jnp.zeros((CONV_PAD, CONV_CH), F32)
    z_s[CONV_PAD + seq:2 * CONV_PAD + seq, :] = jnp.zeros((CONV_PAD, CONV_CH), F32)
    z_s[CONV_PAD:CONV_PAD + seq, :] = a_ref[...].astype(F32) * _sigmoid(gt_ref[...].astype(F32))
    ext = seq + 2 * CONV_PAD - SUBLANES
    for b in range(1, SUBLANES):
        for r0 in range(0, ext, CONV_ROWS):
            n = min(CONV_ROWS, ext - r0)
            sh_s[b - 1, r0:r0 + n, :] = z_s[r0 + b:r0 + b + n, :]
    off = CONV_PAD - CONV_K // 2
    for t0 in range(0, seq, CONV_ROWS):
        acc = jnp.zeros((CONV_ROWS, CONV_CH), F32)
        for k in range(CONV_K):
            a, b = divmod(off + k, SUBLANES)
            r0 = t0 + a * SUBLANES
            tap = z_s[r0:r0 + CONV_ROWS, :] if b == 0 else sh_s[b - 1, r0:r0 + CONV_ROWS, :]
            acc = acc + tap * w_ref[k:k + 1, :]
        y = _layer_norm(acc + cb_ref[...], lg_ref[...], lb_ref[...])
        o_ref[t0:t0 + CONV_ROWS, :] = _silu(y).astype(BF16)


def _conv_branch(zb, cv_w, cv_b, ln_g, ln_b, *, seq, nb, tok0, t_total, prev=None):
    b0 = tok0 // seq
    in_specs = [
        pl.BlockSpec((None, seq, CHUNK_W), lambda b: (4, b0 + b, 0)),
        pl.BlockSpec((None, seq, CHUNK_W), lambda b: (5, b0 + b, 0)),
        pl.BlockSpec((CONV_K, CONV_CH), lambda b: (0, 0)),
        pl.BlockSpec((1, CONV_CH), lambda b: (0, 0)),
        pl.BlockSpec((1, CONV_CH), lambda b: (0, 0)),
        pl.BlockSpec((1, CONV_CH), lambda b: (0, 0)),
    ]
    args = [zb, zb, cv_w, cv_b, ln_g, ln_b]
    aliases = {}
    if prev is not None:
        in_specs.append(pl.BlockSpec(memory_space=pl.ANY))
        args.append(prev)
        aliases = {6: 0}
    return pl.pallas_call(
        functools.partial(_conv_kernel, seq, prev is not None),
        grid=(nb,), in_specs=in_specs,
        out_specs=pl.BlockSpec((seq, CONV_CH), lambda b: (b0 + b, 0)),
        out_shape=jax.ShapeDtypeStruct((t_total, CONV_CH), BF16),
        scratch_shapes=[pltpu.VMEM((seq + 2 * CONV_PAD, CONV_CH), F32),
                        pltpu.VMEM((SUBLANES - 1, seq + 2 * CONV_PAD - SUBLANES, CONV_CH), F32)],
        input_output_aliases=aliases,
        compiler_params=_cparams(("arbitrary",)), name="conv_module_%d" % seq,
    )(*args)


def _merge_kernel(alpha, n_ctx_tiles, with_router, *refs):
    if with_router:
        (ya_ref, yb_ref, yc_ref, mg_ref, xc_ref, xl_ref, g1_ref, sc2_ref, sh2_ref, wa_ref, wb_ref, wc_ref, wo_ref,
         lg_ref, lb_ref, wr_ref, x1_ref, h2_ref, gates_ref) = refs
    else:
        (ya_ref, yb_ref, yc_ref, mg_ref, xc_ref, xl_ref, g1_ref, sc2_ref, sh2_ref, wa_ref, wb_ref, wc_ref, wo_ref,
         lg_ref, lb_ref, x1_ref, h2_ref) = refs
    x = jnp.where(pl.program_id(0) < n_ctx_tiles, xc_ref[...], xl_ref[...])
    halves = []
    for n in range(2):
        m = None
        for j, (y_ref, w_ref) in enumerate(((ya_ref, wa_ref), (yb_ref, wb_ref), (yc_ref, wc_ref))):
            proj = _dot(y_ref[...], w_ref[:, n * CHUNK_W:(n + 1) * CHUNK_W])
            term = _sigmoid(mg_ref[2 * j + n].astype(F32)) * proj
            m = term if m is None else m + term
        halves.append(m.astype(BF16))
    mix = _dot(halves[0], wo_ref[0:CHUNK_W, :]) + _dot(halves[1], wo_ref[CHUNK_W:2 * CHUNK_W, :])
    x1 = _layer_norm(alpha * x + g1_ref[...] * mix, lg_ref[...], lb_ref[...])
    x1_ref[...] = x1
    h2 = x1 * (1.0 + sc2_ref[...]) + sh2_ref[...]
    h2_ref[...] = h2.astype(BF16)
    if with_router:
        logits = _dot_split(h2, wr_ref[...])
        lane = lax.broadcasted_iota(jnp.int32, logits.shape, 1)
        lg = jnp.where(lane < N_EXPERTS, logits, -jnp.inf)
        m1 = jnp.max(lg, axis=-1, keepdims=True)
        i1 = jnp.min(jnp.where(lg == m1, lane, LANES), axis=-1, keepdims=True)
        lg2 = jnp.where(lane == i1, -jnp.inf, lg)
        m2 = jnp.max(lg2, axis=-1, keepdims=True)
        i2 = jnp.min(jnp.where(lg2 == m2, lane, LANES), axis=-1, keepdims=True)
        e2 = jnp.exp(m2 - m1)
        w1 = 1.0 / (1.0 + e2)
        gates_ref[...] = jnp.where(lane == i1, w1, 0.0) + jnp.where(lane == i2, e2 * w1, 0.0)


def _merge(ya, yb, yc, zb, x_pair, mod_l, wa, wb, wc, wo, ln_g, ln_b, alpha, tm, row_of_tile, w_router=None):
    xc, xl = x_pair
    d = xc.shape[1]
    t = xc.shape[0] + xl.shape[0]
    n_ctx_tiles = xc.shape[0] // tm
    full = lambda shape: pl.BlockSpec(shape, lambda i: tuple(0 for _ in shape), pipeline_mode=pl.Buffered(1))
    in_specs = [
        pl.BlockSpec((tm, CHUNK_W), lambda i: (i, 0)),
        pl.BlockSpec((tm, CHUNK_W), lambda i: (i, 0)),
        pl.BlockSpec((tm, CHUNK_W), lambda i: (i, 0)),
        pl.BlockSpec((6, tm, CHUNK_W), lambda i: (1, i, 0)),
    ] + _pair_specs(tm, d, n_ctx_tiles) + [
        _mod_spec(2, row_of_tile), _mod_spec(4, row_of_tile), _mod_spec(3, row_of_tile),
        full(wa.shape), full(wb.shape), full(wc.shape), full(wo.shape), full((1, d)), full((1, d)),
    ]
    args = [ya, yb, yc, zb, xc, xl, mod_l, mod_l, mod_l, wa, wb, wc, wo, ln_g, ln_b]
    out_specs = [pl.BlockSpec((tm, d), lambda i: (i, 0)), pl.BlockSpec((tm, d), lambda i: (i, 0))]
    out_shape = [jax.ShapeDtypeStruct((t, d), F32), jax.ShapeDtypeStruct((t, d), BF16)]
    if w_router is not None:
        in_specs.append(full(w_router.shape))
        args.append(w_router)
        out_specs.append(pl.BlockSpec((tm, LANES), lambda i: (i, 0)))
        out_shape.append(jax.ShapeDtypeStruct((t, LANES), F32))
    return pl.pallas_call(
        functools.partial(_merge_kernel, alpha, n_ctx_tiles, w_router is not None),
        grid=(t // tm,), in_specs=in_specs, out_specs=out_specs, out_shape=out_shape,
        compiler_params=_cparams(("arbitrary",)), name="merge",
    )(*args)


def _ffn_kernel(alpha, n_ctx_tiles, n_split, h_ref, x1_ref, g2_ref, wg_ref, wu_ref, wd_ref, lg_ref, lb_ref,
                oc_ref, ol_ref):
    h = h_ref[...]
    tf = wg_ref.shape[1] // n_split
    y = None
    for f in range(n_split):
        cols = slice(f * tf, (f + 1) * tf)
        act = _silu(_dot(h, wg_ref[:, cols])) * _dot(h, wu_ref[:, cols])
        part = _dot(act.astype(BF16), wd_ref[cols, :])
        y = part if y is None else y + part
    out = _layer_norm(alpha * x1_ref[...] + g2_ref[...] * y, lg_ref[...], lb_ref[...])
    is_ctx = pl.program_id(0) < n_ctx_tiles

    @pl.when(is_ctx)
    def _():
        oc_ref[...] = out

    @pl.when(jnp.logical_not(is_ctx))
    def _():
        ol_ref[...] = out


def _dense_mixer(h2, x1, mod_l, wg, wu, wd, ln_g, ln_b, alpha, tm, n_split, row_of_tile, t_ctx):
    t, d = x1.shape
    n_ctx_tiles = t_ctx // tm
    const = lambda shape: pl.BlockSpec(shape, lambda i: tuple(0 for _ in shape), pipeline_mode=pl.Buffered(1))
    return pl.pallas_call(
        functools.partial(_ffn_kernel, alpha, n_ctx_tiles, n_split),
        grid=(t // tm,),
        in_specs=[
            pl.BlockSpec((tm, d), lambda i: (i, 0)),
            pl.BlockSpec((tm, d), lambda i: (i, 0)),
            _mod_spec(5, row_of_tile),
            const(wg.shape), const(wu.shape), const(wd.shape), const((1, d)), const((1, d)),
        ],
        out_specs=_pair_specs(tm, d, n_ctx_tiles),
        out_shape=[jax.ShapeDtypeStruct((t_ctx, d), F32), jax.ShapeDtypeStruct((t - t_ctx, d), F32)],
        compiler_params=_cparams(("arbitrary",)), name="dense_mixer",
    )(h2, x1, mod_l, wg, wu, wd, ln_g, ln_b)


MOE_TILE = 256
MOE_TF = 1792


def _route_kernel(g_ref, rank_ref, gate_ref, before_ref, run_s):
    w = pl.program_id(0)

    @pl.when(w == 0)
    def _():
        run_s[...] = jnp.zeros_like(run_s)

    g = g_ref[...]
    sel = g > 0.0
    row = lax.broadcasted_iota(jnp.int32, (MOE_TILE, MOE_TILE), 0)
    col = lax.broadcasted_iota(jnp.int32, (MOE_TILE, MOE_TILE), 1)
    earlier = (col < row).astype(BF16)
    ones = jnp.where(sel, 1.0, 0.0)
    rank = _dot(earlier, ones.astype(BF16)) + run_s[...]
    rank = jnp.where(sel, rank, -1.0)
    before_ref[...] = run_s[...]
    run_s[...] += jnp.sum(ones, axis=0, keepdims=True)
    rank_ref[...] = rank.T[0:N_EXPERTS, :]
    gate_ref[...] = g.T[0:N_EXPERTS, :]


def _route(gates):
    t = gates.shape[0]
    nw = t // MOE_TILE
    return pl.pallas_call(
        _route_kernel,
        grid=(nw,),
        in_specs=[pl.BlockSpec((MOE_TILE, LANES), lambda w: (w, 0))],
        out_specs=[
            pl.BlockSpec((None, N_EXPERTS, MOE_TILE), lambda w: (w, 0, 0)),
            pl.BlockSpec((None, N_EXPERTS, MOE_TILE), lambda w: (w, 0, 0)),
            pl.BlockSpec((None, 1, LANES), lambda w: (w, 0, 0)),
        ],
        out_shape=[
            jax.ShapeDtypeStruct((nw, N_EXPERTS, MOE_TILE), F32),
            jax.ShapeDtypeStruct((nw, N_EXPERTS, MOE_TILE), F32),
            jax.ShapeDtypeStruct((nw, 1, LANES), F32),
        ],
        scratch_shapes=[pltpu.VMEM((1, LANES), F32)],
        compiler_params=_cparams(("arbitrary",)), name="moe_route",
    )(gates)


def _tile_onehot(rank_ref, w, e, base):
    rank_row = rank_ref[w, pl.ds(e, 1), :]
    rows = lax.broadcasted_iota(jnp.int32, (MOE_TILE, MOE_TILE), 0).astype(F32) + base.astype(F32)
    return rank_row == rows


def _gather_kernel(te_ref, base_ref, wlo_ref, whi_ref, rank_ref, x_ref, o_ref, acc_s):
    j = pl.program_id(0)
    e, base = te_ref[j], base_ref[j]
    acc_s[...] = jnp.zeros_like(acc_s)

    def window(w, carry):
        p = jnp.where(_tile_onehot(rank_ref, w, e, base), 1.0, 0.0).astype(BF16)
        acc_s[...] += _dot(p, x_ref[pl.ds(pl.multiple_of(w * MOE_TILE, MOE_TILE), MOE_TILE), :])
        return carry

    lax.fori_loop(wlo_ref[j], whi_ref[j] + 1, window, 0)
    o_ref[...] = acc_s[...].astype(BF16)


def _moe_gather(meta, rank_t, h2, n_tiles):
    t, d = h2.shape
    nw = t // MOE_TILE
    return pl.pallas_call(
        _gather_kernel,
        grid_spec=pltpu.PrefetchScalarGridSpec(
            num_scalar_prefetch=4, grid=(n_tiles,),
            in_specs=[
                pl.BlockSpec((nw, N_EXPERTS, MOE_TILE), lambda j, *_: (0, 0, 0)),
                pl.BlockSpec((t, d), lambda j, *_: (0, 0), pipeline_mode=pl.Buffered(1)),
            ],
            out_specs=pl.BlockSpec((MOE_TILE, d), lambda j, *_: (j, 0)),
            scratch_shapes=[pltpu.VMEM((MOE_TILE, d), F32)],
        ),
        out_shape=jax.ShapeDtypeStruct((n_tiles * MOE_TILE, d), BF16),
        compiler_params=_cparams(("arbitrary",)), name="moe_gather",
    )(*meta, rank_t, h2)


def _moe_up_kernel(te_ref, chg_ref, nused_ref, x_ref, wg_ref, wu_ref, o_ref, wg_s, wu_s):
    j = pl.program_id(1)

    @pl.when(chg_ref[j] == 1)
    def _():
        wg_s[...] = wg_ref[...].astype(BF16)
        wu_s[...] = wu_ref[...].astype(BF16)

    @pl.when(j < nused_ref[0])
    def _():
        x = x_ref[...]
        o_ref[...] = (_silu(_dot(x, wg_s[...])) * _dot(x, wu_s[...])).astype(BF16)

    @pl.when(j >= nused_ref[0])
    def _():
        o_ref[...] = jnp.zeros_like(o_ref)


def _moe_up(te, chg, nused, xs, wg, wu):
    rows, d = xs.shape
    n_tiles = rows // MOE_TILE
    ff = wg.shape[2]
    return pl.pallas_call(
        _moe_up_kernel,
        grid_spec=pltpu.PrefetchScalarGridSpec(
            num_scalar_prefetch=3, grid=(ff // MOE_TF, n_tiles),
            in_specs=[
                pl.BlockSpec((MOE_TILE, d), lambda f, j, *_: (j, 0)),
                pl.BlockSpec((None, d, MOE_TF), lambda f, j, te, *_: (te[j], 0, f)),
                pl.BlockSpec((None, d, MOE_TF), lambda f, j, te, *_: (te[j], 0, f)),
            ],
            out_specs=pl.BlockSpec((MOE_TILE, MOE_TF), lambda f, j, *_: (j, f)),
            scratch_shapes=[pltpu.VMEM((d, MOE_TF), BF16), pltpu.VMEM((d, MOE_TF), BF16)],
        ),
        out_shape=jax.ShapeDtypeStruct((rows, ff), BF16),
        compiler_params=_cparams(("arbitrary", "arbitrary")), name="moe_up",
    )(te, chg, nused, xs, wg, wu)


def _moe_down_kernel(te_ref, chg_ref, nused_ref, a_ref, wd_ref, o_ref, wd_s):
    j = pl.program_id(0)

    @pl.when(chg_ref[j] == 1)
    def _():
        wd_s[...] = wd_ref[...].astype(BF16)

    @pl.when(j < nused_ref[0])
    def _():
        o_ref[...] = _dot(a_ref[...], wd_s[...]).astype(BF16)

    @pl.when(j >= nused_ref[0])
    def _():
        o_ref[...] = jnp.zeros_like(o_ref)


def _moe_down(te, chg, nused, act, wd):
    rows, ff = act.shape
    d = wd.shape[2]
    return pl.pallas_call(
        _moe_down_kernel,
        grid_spec=pltpu.PrefetchScalarGridSpec(
            num_scalar_prefetch=3, grid=(rows // MOE_TILE,),
            in_specs=[
                pl.BlockSpec((MOE_TILE, ff), lambda j, *_: (j, 0)),
                pl.BlockSpec((None, ff, d), lambda j, te, *_: (te[j], 0, 0)),
            ],
            out_specs=pl.BlockSpec((MOE_TILE, d), lambda j, *_: (j, 0)),
            scratch_shapes=[pltpu.VMEM((ff, d), BF16)],
        ),
        out_shape=jax.ShapeDtypeStruct((rows, d), BF16),
        compiler_params=_cparams(("arbitrary",)), name="moe_down",
    )(te, chg, nused, act, wd)


def _combine_kernel(alpha, n_tiles, nw_ctx, te_ref, base_ref, wlo_ref, whi_ref, rank_ref, gate_ref, y_ref, x1_ref,
                    g2_ref, lg_ref, lb_ref, oc_ref, ol_ref, acc_s):
    s = pl.program_id(0)

    @pl.when(s == 0)
    def _():
        acc_s[...] = jnp.zeros_like(acc_s)

    @pl.when(s < n_tiles)
    def _():
        e, base = te_ref[s], base_ref[s]

        def window(w, carry):
            hit = _tile_onehot(rank_ref, w, e, base)
            q = jnp.where(hit, gate_ref[w, pl.ds(e, 1), :], 0.0).T.astype(BF16)
            rows = pl.ds(pl.multiple_of(w * MOE_TILE, MOE_TILE), MOE_TILE)
            acc_s[rows, :] += _dot(q, y_ref[...])
            return carry

        lax.fori_loop(wlo_ref[s], whi_ref[s] + 1, window, 0)

    @pl.when(s >= n_tiles)
    def _():
        w = s - n_tiles
        rows = pl.ds(pl.multiple_of(w * MOE_TILE, MOE_TILE), MOE_TILE)
        out = _layer_norm(alpha * x1_ref[...] + g2_ref[...] * acc_s[rows, :], lg_ref[...], lb_ref[...])

        @pl.when(w < nw_ctx)
        def _():
            oc_ref[...] = out

        @pl.when(w >= nw_ctx)
        def _():
            ol_ref[...] = out


def _moe_combine(meta, rank_t, gate_t, y, x1, mod_l, ln_g, ln_b, alpha, row_of_tile, t_ctx):
    t, d = x1.shape
    nw = t // MOE_TILE
    nw_ctx = t_ctx // MOE_TILE
    n_tiles = y.shape[0] // MOE_TILE
    win = lambda s: jnp.maximum(s - n_tiles, 0)
    return pl.pallas_call(
        functools.partial(_combine_kernel, alpha, n_tiles, nw_ctx),
        grid_spec=pltpu.PrefetchScalarGridSpec(
            num_scalar_prefetch=4, grid=(n_tiles + nw,),
            in_specs=[
                pl.BlockSpec((nw, N_EXPERTS, MOE_TILE), lambda s, *_: (0, 0, 0)),
                pl.BlockSpec((nw, N_EXPERTS, MOE_TILE), lambda s, *_: (0, 0, 0)),
                pl.BlockSpec((MOE_TILE, d), lambda s, *_: (jnp.minimum(s, n_tiles - 1), 0)),
                pl.BlockSpec((MOE_TILE, d), lambda s, *_: (win(s), 0)),
                pl.BlockSpec((None, None, 1, d), lambda s, *_: (row_of_tile(win(s)), 5, 0, 0)),
                pl.BlockSpec((1, d), lambda s, *_: (0, 0)),
                pl.BlockSpec((1, d), lambda s, *_: (0, 0)),
            ],
            out_specs=[
                pl.BlockSpec((MOE_TILE, d), lambda s, *_: (jnp.minimum(win(s), nw_ctx - 1), 0)),
                pl.BlockSpec((MOE_TILE, d), lambda s, *_: (jnp.maximum(win(s) - nw_ctx, 0), 0)),
            ],
            scratch_shapes=[pltpu.VMEM((t, d), F32)],
        ),
        out_shape=[jax.ShapeDtypeStruct((t_ctx, d), F32), jax.ShapeDtypeStruct((t - t_ctx, d), F32)],
        compiler_params=_cparams(("arbitrary",)), name="moe_combine",
    )(*meta, rank_t, gate_t, y, x1, mod_l, ln_g, ln_b)


def _moe_layer(gates, h2, x1, mod_l, wg, wu, wd, ln_g, ln_b, alpha, row_of_tile, t_ctx):
    t = h2.shape[0]
    n_tiles = -(-(2 * t + N_EXPERTS * (MOE_TILE - 1)) // MOE_TILE)
    rank_t, gate_t, before = _route(gates)
    before = before[:, 0, :N_EXPERTS].astype(jnp.int32)
    counts = jnp.sum(gates[:, :N_EXPERTS] > 0.0, axis=0).astype(jnp.int32)
    tiles_e = (counts + MOE_TILE - 1) // MOE_TILE
    tile_end = jnp.cumsum(tiles_e)
    n_used = tile_end[-1]
    j = jnp.arange(n_tiles, dtype=jnp.int32)
    te = jnp.minimum(jnp.sum(j[:, None] >= tile_end[None, :], axis=1), N_EXPERTS - 1).astype(jnp.int32)
    base = (j - jnp.take(tile_end - tiles_e, te)) * MOE_TILE
    last = jnp.minimum(base + MOE_TILE - 1, jnp.take(counts, te) - 1)
    before_t = jnp.take(before, te, axis=1)
    used = j < n_used
    wlo = jnp.where(used, jnp.sum(before_t <= base[None, :], axis=0) - 1, 1).astype(jnp.int32)
    whi = jnp.where(used, jnp.sum(before_t <= last[None, :], axis=0) - 1, 0).astype(jnp.int32)
    chg = jnp.concatenate([jnp.ones((1,), jnp.int32), (te[1:] != te[:-1]).astype(jnp.int32)])
    nused = n_used.reshape(1).astype(jnp.int32)
    meta = (te, base.astype(jnp.int32), wlo, whi)
    xs = _moe_gather(meta, rank_t, h2, n_tiles)
    act = _moe_up(te, chg, nused, xs, wg, wu)
    y = _moe_down(te, chg, nused, act, wd)
    return _moe_combine(meta, rank_t, gate_t, y, x1, mod_l, ln_g, ln_b, alpha, row_of_tile, t_ctx)


def _rope_tables(seq):
    half = GLA_DK // 2
    n = half // 2
    t = np.arange(seq)
    inv = ROPE_BASE ** (-np.arange(n, dtype=np.float64) / n)
    cos = np.zeros((seq, GLA_DK), np.float64)
    sin = np.zeros((seq, GLA_DK), np.float64)
    for a, pos in enumerate((t // GRID_W, t % GRID_W)):
        ang = pos[:, None].astype(np.float64) * inv[None, :]
        base = a * half
        cos[:, base:base + n] = np.cos(ang)
        cos[:, base + n:base + half] = np.cos(ang)
        sin[:, base:base + n] = -np.sin(ang)
        sin[:, base + n:base + half] = np.sin(ang)
    tile = lambda m: jnp.asarray(np.tile(m, (1, GLA_HEADS)), F32)
    return tile(cos), tile(sin)


def _chunk_w_in(w):
    bf = lambda a: a.astype(BF16)
    gd = jnp.pad(w[:, 1536:1568], ((0, 0), (0, LANES - 2 * GLA_RANK)))
    kv = jnp.stack([w[:, 2080:2592], w[:, 2592:3104]])
    starts = (0, 512, 1024, 1568, 3104, 3616, 4128, 4640, 5152, 5664, 6176, 6688)
    return bf(gd), bf(kv), bf(jnp.stack([w[:, s:s + CHUNK_W] for s in starts]))


def kernel(x_prompt, x_sample, cache_na_k, cache_na_v, state_gla, c, c_ctx, w_mod, b_mod, w_in, gla_w_gup, gla_b_g, gla_norm, w_br_gla, na_rpb, w_br_na, cv_w, cv_b, cv_ln_g, cv_ln_b, w_br_cv, w_out, ln_g, ln_b, ffd_w_gate, ffd_w_up, ffd_w_down, moe_w_router, moe_w_gate, moe_w_up, moe_w_down):
    nb_c, l_c, d = x_prompt.shape
    nb_l, l_l, _ = x_sample.shape
    depth = w_mod.shape[0]
    t_ctx, t_lat = nb_c * l_c, nb_l * l_l
    t_all = t_ctx + t_lat
    alpha = (2 * depth) ** 0.25
    tm = 512
    row_of_tile = _row_of_tile_fn(t_ctx, l_l, tm)

    x_pair = (x_prompt.reshape(t_ctx, d), x_sample.reshape(t_lat, d))
    cond8 = jnp.concatenate([c_ctx[None, :], c, jnp.zeros((8 - 1 - nb_l, d), F32)], axis=0)
    mod = _modulation(cond8, w_mod, b_mod).reshape(depth, 8, 6, 1, d)
    rope = _rope_tables(l_l)
    rows_lat = l_l // GRID_W
    kr = min(NA_WIN_R, rows_lat)
    s0_all = state_gla.reshape(nb_l, depth, 2, 2, LANES, GLA_DV)

    caches = None
    ss = []
    for l in range(depth):
        mod_l = mod[l]
        gd, kc, vc, kl, vl, zb = _input_projection(x_pair, mod_l, _chunk_w_in(w_in[l]), l, depth, l_c, tm,
                                                   row_of_tile, caches)
        caches = (kc, vc)

        wg_blk = jnp.zeros((LANES, CHUNK_W), F32)
        wg_blk = wg_blk.at[0:GLA_RANK, 0:256].set(gla_w_gup[l, 0]).at[GLA_RANK:2 * GLA_RANK, 256:512].set(gla_w_gup[l, 1])
        bg = gla_b_g[l].reshape(1, CHUNK_W)
        gn = gla_norm[l].reshape(1, CHUNK_W)
        ya, s_fin = _gla_branch(gd, zb, wg_blk, bg, gn, latent=False, seq=l_c, nb=nb_c, tok0=0, t_total=t_all)
        ya = _gla_branch(gd, zb, wg_blk, bg, gn, latent=True, seq=l_l, nb=nb_l, tok0=t_ctx, t_total=t_all,
                         rope=rope, s0=s0_all[:, l], ya_prev=ya)

        yb = _attn_context(zb, kc, vc, l, nb_c, l_c, t_all)
        bias = _na_bias_table(na_rpb[l], kr)
        yb = _attn_latent(zb, kl, vl, cache_na_k, cache_na_v, l, bias, yb, nb_l, l_l, t_ctx)

        conv_args = (zb, cv_w[l], cv_b[l][None, :], cv_ln_g[l][None, :], cv_ln_b[l][None, :])
        yc = _conv_branch(*conv_args, seq=l_c, nb=nb_c, tok0=0, t_total=t_all)
        yc = _conv_branch(*conv_args, seq=l_l, nb=nb_l, tok0=t_ctx, t_total=t_all, prev=yc)

        moe = l % 2 == 1
        w_router = None
        if moe:
            w_router = jnp.pad(moe_w_router[l // 2], ((0, 0), (0, LANES - N_EXPERTS)))
        merged = _merge(ya, yb, yc, zb, x_pair, mod_l,
                        w_br_gla[l].astype(BF16), w_br_na[l].astype(BF16), w_br_cv[l].astype(BF16),
                        w_out[l].astype(BF16), ln_g[l, 0][None, :], ln_b[l, 0][None, :], alpha, tm, row_of_tile,
                        w_router)
        ln2 = (ln_g[l, 1][None, :], ln_b[l, 1][None, :])
        if moe:
            x1, h2, gates = merged
            x_pair = _moe_layer(gates, h2, x1, mod_l, moe_w_gate[l // 2], moe_w_up[l // 2], moe_w_down[l // 2],
                                *ln2, alpha, _row_of_tile_fn(t_ctx, l_l, MOE_TILE), t_ctx)
        else:
            x1, h2 = merged
            i = l // 2
            x_pair = _dense_mixer(h2, x1, mod_l, ffd_w_gate[i].astype(BF16), ffd_w_up[i].astype(BF16),
                                  ffd_w_down[i].astype(BF16), *ln2, alpha, tm, 2, row_of_tile, t_ctx)
        ss.append(s_fin.reshape(nb_c, 2, GLA_HEADS, GLA_DK, GLA_DV))

    y_prompt = x_pair[0].reshape(nb_c, l_c, d)
    y_sample = x_pair[1].reshape(nb_l, l_l, d)
    cache_shape = (nb_c, depth, l_c, NA_HEADS, NA_DH)
    return (y_prompt, y_sample, caches[0].reshape(cache_shape), caches[1].reshape(cache_shape), jnp.stack(ss, 1))


def _row_of_tile_fn(t_ctx, l_lat, tm):
    n_ctx = t_ctx // tm

    def row_of_tile(i):
        return jnp.where(i < n_ctx, 0, 1 + (i - n_ctx) // (l_lat // tm))

    return row_of_tile
```

```python
import functools

import numpy as np
import jax
import jax.numpy as jnp
from jax import lax
from jax.experimental import pallas as pl
from jax.experimental.pallas import tpu as pltpu

F32 = jnp.float32
BF16 = jnp.bfloat16

GRID_W = 64
GLA_HEADS = 4
GLA_DK = 64
GLA_DV = 128
GLA_RANK = 16
GLA_TAU = 16.0
GLA_CHUNK = 64
GLA_GROUP = 256
NA_HEADS = 8
NA_DH = 64
NA_WIN_R = 8
NA_WIN_C = 16
CONV_CH = 512
CONV_K = 31
N_EXPERTS = 8
ROPE_BASE = 10000.0
LN_EPS = 1e-5

D_MODEL = 1024
LANES = 128
SUBLANES = 8
CHUNK_W = 512
N_BF16_CHUNKS = 12
NEG_BIG = -1e30
VMEM_LIMIT = 56 * 1024 * 1024


def _cparams(sem, vmem=VMEM_LIMIT):
    return pltpu.CompilerParams(dimension_semantics=sem, vmem_limit_bytes=vmem)


def _dot(a, b):
    return jnp.dot(a, b, preferred_element_type=F32)


def _split(x):
    hi = x.astype(BF16)
    return hi, (x - hi.astype(F32)).astype(BF16)


def _dot_split(a, b):
    a_hi, a_lo = _split(a)
    b_hi, b_lo = _split(b)
    return _dot(a_hi, b_hi) + _dot(a_lo, b_hi) + _dot(a_hi, b_lo)


def _dot_split_rhs(a, b):
    b_hi, b_lo = _split(b)
    return _dot(a, b_hi) + _dot(a, b_lo)


def _dot_nt(a, b):
    return lax.dot_general(a, b, (((1,), (1,)), ((), ())), preferred_element_type=F32)


def _sigmoid(x):
    return 0.5 * jnp.tanh(0.5 * x) + 0.5


def _silu(x):
    return x * _sigmoid(x)


def _layer_norm(x, g, b):
    mu = jnp.mean(x, axis=-1, keepdims=True)
    xc = x - mu
    var = jnp.mean(xc * xc, axis=-1, keepdims=True)
    return xc * lax.rsqrt(var + LN_EPS) * g + b


def _mod_kernel(c_ref, w_ref, b_ref, o_ref):
    c = c_ref[...]
    s = _silu(c).astype(BF16)
    o_ref[...] = _dot(s, w_ref[...].astype(BF16)) + b_ref[...]


def _modulation(cond8, w_mod, b_mod):
    depth, d, n = w_mod.shape
    tn = 1024
    return pl.pallas_call(
        _mod_kernel,
        grid=(depth, n // tn),
        in_specs=[
            pl.BlockSpec((8, d), lambda l, j: (0, 0)),
            pl.BlockSpec((None, d, tn), lambda l, j: (l, 0, j)),
            pl.BlockSpec((None, 1, tn), lambda l, j: (l, 0, j)),
        ],
        out_specs=pl.BlockSpec((None, 8, tn), lambda l, j: (l, 0, j)),
        out_shape=jax.ShapeDtypeStruct((depth, 8, n), F32),
        compiler_params=_cparams(("arbitrary", "arbitrary")),
        name="modulation",
    )(cond8, w_mod, b_mod.reshape(depth, 1, n))


def _inproj_kernel(n_ctx_tiles, aliased, *refs):
    if aliased:
        (xc_ref, xl_ref, sc_ref, sh_ref, wgd_ref, wkv_ref, w_ref, _kc_in, _vc_in,
         gd_ref, kc_ref, vc_ref, kl_ref, vl_ref, zb_ref, h_ref) = refs
    else:
        (xc_ref, xl_ref, sc_ref, sh_ref, wgd_ref, wkv_ref, w_ref,
         gd_ref, kc_ref, vc_ref, kl_ref, vl_ref, zb_ref, h_ref) = refs
    is_ctx = pl.program_id(0) < n_ctx_tiles
    x = jnp.where(is_ctx, xc_ref[...], xl_ref[...])
    h_ref[...] = (x * (1.0 + sc_ref[...]) + sh_ref[...]).astype(BF16)
    gd_ref[...] = _dot(h_ref[...], wgd_ref[...])
    k = _dot(h_ref[...], wkv_ref[0])
    v = _dot(h_ref[...], wkv_ref[1])
    seqs, l_c, _ = kc_ref.shape

    @pl.when(is_ctx)
    def _():
        for s in range(seqs):
            kc_ref[s] = k[s * l_c:(s + 1) * l_c]
            vc_ref[s] = v[s * l_c:(s + 1) * l_c]

    @pl.when(jnp.logical_not(is_ctx))
    def _():
        kl_ref[...] = k
        vl_ref[...] = v

    for j in range(N_BF16_CHUNKS):
        zb_ref[j] = _dot(h_ref[...], w_ref[j]).astype(BF16)


def _mod_spec(piece, row_of_tile):
    return pl.BlockSpec((None, None, 1, D_MODEL), lambda i: (row_of_tile(i), piece, 0, 0))


def _pair_specs(tm, d, n_ctx_tiles):
    return [pl.BlockSpec((tm, d), lambda i, *_: (jnp.minimum(i, n_ctx_tiles - 1), 0)),
            pl.BlockSpec((tm, d), lambda i, *_: (jnp.maximum(i - n_ctx_tiles, 0), 0))]


def _input_projection(x_pair, mod_l, w_parts, layer, depth, l_c, tm, row_of_tile, caches=None):
    xc, xl = x_pair
    w_gd, w_kv, w_chunks = w_parts
    d = xc.shape[1]
    t_ctx, t_lat = xc.shape[0], xl.shape[0]
    t = t_ctx + t_lat
    n_ctx_tiles = t_ctx // tm
    seqs = tm // l_c
    cache_spec = pl.BlockSpec((seqs, None, l_c, CHUNK_W),
                              lambda i: (jnp.minimum(i, n_ctx_tiles - 1), layer, 0, 0))
    lat_spec = pl.BlockSpec((tm, CHUNK_W), lambda i: (jnp.maximum(i - n_ctx_tiles, 0), 0))
    const = lambda shape: pl.BlockSpec(shape, lambda i: tuple(0 for _ in shape), pipeline_mode=pl.Buffered(1))
    in_specs = _pair_specs(tm, d, n_ctx_tiles) + [
        _mod_spec(1, row_of_tile), _mod_spec(0, row_of_tile),
        const(w_gd.shape), const(w_kv.shape), const(w_chunks.shape),
    ]
    args = [xc, xl, mod_l, mod_l, w_gd, w_kv, w_chunks]
    aliases = {}
    if caches is not None:
        in_specs += [pl.BlockSpec(memory_space=pl.ANY), pl.BlockSpec(memory_space=pl.ANY)]
        args += list(caches)
        aliases = {7: 1, 8: 2}
    cache_shape = jax.ShapeDtypeStruct((t_ctx // l_c, depth, l_c, CHUNK_W), F32)
    return pl.pallas_call(
        functools.partial(_inproj_kernel, n_ctx_tiles, caches is not None),
        grid=(t // tm,),
        in_specs=in_specs,
        out_specs=[
            pl.BlockSpec((tm, LANES), lambda i: (i, 0)),
            cache_spec, cache_spec, lat_spec, lat_spec,
            pl.BlockSpec((N_BF16_CHUNKS, tm, CHUNK_W), lambda i: (0, i, 0)),
        ],
        out_shape=[
            jax.ShapeDtypeStruct((t, LANES), F32),
            cache_shape, cache_shape,
            jax.ShapeDtypeStruct((t_lat, CHUNK_W), F32), jax.ShapeDtypeStruct((t_lat, CHUNK_W), F32),
            jax.ShapeDtypeStruct((N_BF16_CHUNKS, t, CHUNK_W), BF16),
        ],
        scratch_shapes=[pltpu.VMEM((tm, d), BF16)],
        input_output_aliases=aliases,
        compiler_params=_cparams(("arbitrary",)),
        name="input_projection",
    )(*args)


def _gla_kernel(latent, seq, *refs):
    if latent:
        (qk_ref, v_ref, ra_ref, gd_ref, wg_ref, bg_ref, gn_ref, cos_ref, sin_ref, s0_ref, _alias,
         ya_ref, q_s, k_s, g_s, acc_s, st_s) = refs
    else:
        (qk_ref, v_ref, ra_ref, gd_ref, wg_ref, bg_ref, gn_ref,
         ya_ref, sfin_ref, q_s, k_s, g_s, acc_s, st_s) = refs
    c = GLA_CHUNK
    n_chunks = seq // c
    qkw = GLA_HEADS * GLA_DK

    q = qk_ref[:, 0:qkw].astype(F32) * (GLA_DK ** -0.5)
    k = qk_ref[:, qkw:2 * qkw].astype(F32)
    if latent:
        lane = lax.broadcasted_iota(jnp.int32, (1, qkw), 1)
        first_half = (lane % 32) < 16

        def rope(x):
            swapped = jnp.where(first_half, pltpu.roll(x, qkw - 16, 1), pltpu.roll(x, 16, 1))
            return x * cos_ref[...] + swapped * sin_ref[...]

        q, k = rope(q), rope(k)
    q_s[...] = q
    k_s[...] = k

    glog = _dot_split(gd_ref[...], wg_ref[...]) + bg_ref[...]
    g_s[...] = (jnp.minimum(glog, 0.0) - jnp.log(1.0 + jnp.exp(-jnp.abs(glog)))) * (1.0 / GLA_TAU)
    acc_s[...] = jnp.zeros_like(acc_s)
    if latent:
        st_s[...] = s0_ref[...]
    else:
        st_s[...] = jnp.zeros_like(st_s)

    grp = GLA_GROUP
    cpg = grp // c
    n_groups = seq // grp
    grow = lax.broadcasted_iota(jnp.int32, (grp, grp), 0)
    gcol = lax.broadcasted_iota(jnp.int32, (grp, grp), 1)
    same_chunk = (grow // c) == (gcol // c)
    keep = (same_chunk & (gcol <= grow), same_chunk & (gcol >= grow))
    tri = (keep[0].astype(BF16), keep[1].astype(BF16))
    lane128 = lax.broadcasted_iota(jnp.int32, (1, LANES), 1)
    head_mask = (lane128 < GLA_DK, lane128 >= GLA_DK)
    urow = lax.broadcasted_iota(jnp.int32, (cpg * LANES, GLA_DV), 0)
    srow = lax.broadcasted_iota(jnp.int32, (LANES, GLA_DV), 0)
    chunk_of_col = lax.broadcasted_iota(jnp.int32, (LANES, grp), 1) // c
    chunk_of_row = lax.broadcasted_iota(jnp.int32, (grp, LANES), 0) // c
    pad_rows = jnp.zeros((SUBLANES - cpg, qkw), F32)

    def group_step(i, carry):
        for d in range(2):
            gi = i if d == 0 else n_groups - 1 - i
            rows = pl.ds(pl.multiple_of(gi * grp, grp), grp)
            cum = _dot_split_rhs(tri[d], g_s[rows, d * qkw:(d + 1) * qkw])
            edge = c - 1 if d == 0 else 0
            lasts = [cum[ci * c + edge:ci * c + edge + 1, :] for ci in range(cpg)]
            last = jnp.concatenate([jnp.broadcast_to(r, (c, qkw)) for r in lasts], axis=0)
            ref = 0.5 * last
            q = q_s[rows, :]
            k = k_s[rows, :]
            qe = q * jnp.exp(cum - ref)
            ke = k * jnp.exp(ref - cum)
            qe2 = q * jnp.exp(cum)
            ke2 = k * jnp.exp(last - cum)
            last8 = jnp.concatenate(lasts + [pad_rows], axis=0)
            for p in range(2):
                sl = slice(p * LANES, (p + 1) * LANES)
                ke_p = ke[:, sl].astype(BF16)
                ke2_t = ke2[:, sl].T
                ke2_blk = jnp.concatenate(
                    [jnp.where(chunk_of_col == ci, ke2_t, 0.0) for ci in range(cpg)], axis=0).astype(BF16)
                dec = jnp.exp(last8[:, sl]).T
                psl = slice(2 * p * GLA_DV, (2 * p + 2) * GLA_DV)
                v_pair = v_ref[rows, psl]
                u = _dot(ke2_blk, v_pair)
                upd = jnp.where(urow % LANES < GLA_DK, u[:, :GLA_DV], u[:, GLA_DV:])
                s = st_s[d, p]
                s_in = [None] * cpg
                for ci in (range(cpg) if d == 0 else range(cpg - 1, -1, -1)):
                    s_in[ci] = jnp.concatenate(
                        [jnp.where(srow < GLA_DK, s, 0.0), jnp.where(srow >= GLA_DK, s, 0.0)], axis=1).astype(BF16)
                    s = dec[:, ci:ci + 1] * s + upd[ci * LANES:(ci + 1) * LANES]
                st_s[d, p] = s
                s_stack = jnp.concatenate(s_in, axis=0)
                a2_blk = jnp.concatenate(
                    [jnp.where(chunk_of_row == ci, qe2[:, sl], 0.0) for ci in range(cpg)], axis=1).astype(BF16)
                intra = []
                for hh in range(2):
                    a = jnp.where(head_mask[hh], qe[:, sl], 0.0).astype(BF16)
                    att = jnp.where(keep[d], _dot_nt(a, ke_p), 0.0).astype(BF16)
                    intra.append(_dot(att, v_pair[:, hh * GLA_DV:(hh + 1) * GLA_DV]))
                acc_s[rows, psl] += jnp.concatenate(intra, axis=1) + _dot(a2_blk, s_stack)
        return carry

    lax.fori_loop(0, n_groups, group_step, 0)

    def epilogue(i, carry):
        rows = pl.ds(pl.multiple_of(i * c, c), c)
        for h in range(GLA_HEADS):
            vsl = slice(h * GLA_DV, (h + 1) * GLA_DV)
            o = acc_s[rows, vsl]
            o = o * lax.rsqrt(jnp.mean(o * o, axis=-1, keepdims=True) + LN_EPS) * gn_ref[:, vsl]
            ya_ref[rows, vsl] = (o * _silu(ra_ref[rows, vsl].astype(F32))).astype(BF16)
        return carry

    lax.fori_loop(0, n_chunks, epilogue, 0)
    if not latent:
        sfin_ref[...] = st_s[...]


def _gla_branch(gd, zb, wg_blk, bg, gn, *, latent, seq, nb, tok0, t_total, rope=None, s0=None, ya_prev=None):
    b0 = tok0 // seq
    qkw = GLA_HEADS * GLA_DK
    in_specs = [
        pl.BlockSpec((None, seq, CHUNK_W), lambda b: (0, b0 + b, 0)),
        pl.BlockSpec((None, seq, CHUNK_W), lambda b: (1, b0 + b, 0)),
        pl.BlockSpec((None, seq, CHUNK_W), lambda b: (2, b0 + b, 0)),
        pl.BlockSpec((seq, LANES), lambda b: (b0 + b, 0)),
        pl.BlockSpec((LANES, CHUNK_W), lambda b: (0, 0)),
        pl.BlockSpec((1, CHUNK_W), lambda b: (0, 0)),
        pl.BlockSpec((1, CHUNK_W), lambda b: (0, 0)),
    ]
    args = [zb, zb, zb, gd, wg_blk, bg, gn]
    scratch = [
        pltpu.VMEM((seq, qkw), F32), pltpu.VMEM((seq, qkw), F32), pltpu.VMEM((seq, CHUNK_W), F32),
        pltpu.VMEM((seq, CHUNK_W), F32), pltpu.VMEM((2, 2, LANES, GLA_DV), F32),
    ]
    ya_shape = jax.ShapeDtypeStruct((t_total, CHUNK_W), BF16)
    ya_spec = pl.BlockSpec((seq, CHUNK_W), lambda b: (b0 + b, 0))
    if latent:
        cos_t, sin_t = rope
        in_specs += [
            pl.BlockSpec((seq, qkw), lambda b: (0, 0)),
            pl.BlockSpec((seq, qkw), lambda b: (0, 0)),
            pl.BlockSpec((None, 2, 2, LANES, GLA_DV), lambda b: (b, 0, 0, 0, 0)),
            pl.BlockSpec(memory_space=pl.ANY),
        ]
        args += [cos_t, sin_t, s0, ya_prev]
        return pl.pallas_call(
            functools.partial(_gla_kernel, True, seq),
            grid=(nb,), in_specs=in_specs, out_specs=ya_spec, out_shape=ya_shape,
            scratch_shapes=scratch, input_output_aliases={len(args) - 1: 0},
            compiler_params=_cparams(("arbitrary",)), name="gla_latent",
        )(*args)
    return pl.pallas_call(
        functools.partial(_gla_kernel, False, seq),
        grid=(nb,), in_specs=in_specs,
        out_specs=[ya_spec, pl.BlockSpec((None, 2, 2, LANES, GLA_DV), lambda b: (b, 0, 0, 0, 0))],
        out_shape=[ya_shape, jax.ShapeDtypeStruct((nb, 2, 2, LANES, GLA_DV), F32)],
        scratch_shapes=scratch,
        compiler_params=_cparams(("arbitrary",)), name="gla_context",
    )(*args)


def _attn_ctx_kernel(q_ref, k_ref, v_ref, o_ref):
    lane = lax.broadcasted_iota(jnp.int32, (1, LANES), 1)
    masks = (lane < NA_DH, lane >= NA_DH)
    scale = NA_DH ** -0.5
    seq = q_ref.shape[0]
    for p in range(NA_HEADS // 2):
        sl = slice(p * LANES, (p + 1) * LANES)
        q2 = _stack_heads(q_ref[:, sl], masks)
        kp = k_ref[:, sl].astype(BF16)
        vp = v_ref[:, sl].astype(BF16)
        s = _dot_nt(q2, kp) * scale
        e = jnp.exp(s - jnp.max(s, axis=-1, keepdims=True))
        o2 = _dot(e.astype(BF16), vp) / jnp.sum(e, axis=-1, keepdims=True)
        o_ref[:, sl] = jnp.where(masks[0], o2[0:seq], o2[seq:2 * seq]).astype(BF16)


def _stack_heads(qp, masks):
    zero = jnp.zeros_like(qp)
    return jnp.concatenate([jnp.where(masks[0], qp, zero), jnp.where(masks[1], qp, zero)], axis=0)


def _attn_context(zb, kc, vc, layer, nb, seq, t_total):
    return pl.pallas_call(
        _attn_ctx_kernel,
        grid=(nb,),
        in_specs=[
            pl.BlockSpec((None, seq, CHUNK_W), lambda b: (3, b, 0)),
            pl.BlockSpec((None, None, seq, CHUNK_W), lambda b: (b, layer, 0, 0)),
            pl.BlockSpec((None, None, seq, CHUNK_W), lambda b: (b, layer, 0, 0)),
        ],
        out_specs=pl.BlockSpec((seq, CHUNK_W), lambda b: (b, 0)),
        out_shape=jax.ShapeDtypeStruct((t_total, CHUNK_W), BF16),
        compiler_params=_cparams(("arbitrary",)), name="attention_context",
    )(zb, kc, vc)


def _na_kernel(rows_total, kr, q_ref, kl_ref, vl_ref, kc_ref, vc_ref, toep_ref, _alias, o_ref):
    r = pl.program_id(1)
    rs = jnp.clip(r - kr // 2, 0, rows_total - kr)
    krows = pl.ds(pl.multiple_of(rs * GRID_W, GRID_W), kr * GRID_W)
    dr0 = rs - r + NA_WIN_R - 1
    lane = lax.broadcasted_iota(jnp.int32, (1, LANES), 1)
    masks = (lane < NA_DH, lane >= NA_DH)
    scale = NA_DH ** -0.5
    nq = q_ref.shape[0]
    for p in range(NA_HEADS // 2):
        sl = slice(p * LANES, (p + 1) * LANES)
        q2 = _stack_heads(q_ref[:, sl], masks)
        klp = kl_ref[krows, sl].astype(BF16)
        vlp = vl_ref[krows, sl].astype(BF16)
        kcp = kc_ref[:, sl].astype(BF16)
        vcp = vc_ref[:, sl].astype(BF16)
        bias = jnp.concatenate(
            [jnp.concatenate([toep_ref[2 * p + hh, dr0 + 2 * m] for m in range(kr // 2)], axis=1)
             for hh in range(2)], axis=0)
        s_loc = _dot_nt(q2, klp) * scale + bias
        s_ctx = _dot_nt(q2, kcp) * scale
        m = jnp.maximum(jnp.max(s_loc, axis=-1, keepdims=True), jnp.max(s_ctx, axis=-1, keepdims=True))
        e_loc = jnp.exp(s_loc - m)
        e_ctx = jnp.exp(s_ctx - m)
        den = jnp.sum(e_loc, axis=-1, keepdims=True) + jnp.sum(e_ctx, axis=-1, keepdims=True)
        o2 = (_dot(e_loc.astype(BF16), vlp) + _dot(e_ctx.astype(BF16), vcp)) / den
        o_ref[:, sl] = jnp.where(masks[0], o2[0:nq], o2[nq:2 * nq]).astype(BF16)


def _na_bias_tables(rpb):
    depth, heads, nr, nc = rpb.shape
    qc = np.arange(GRID_W)[:, None]
    kc = np.arange(GRID_W)[None, :]
    cs = np.clip(qc - NA_WIN_C // 2, 0, GRID_W - NA_WIN_C)
    valid = (kc >= cs) & (kc < cs + NA_WIN_C)
    period = 2 * GRID_W
    lead = GRID_W - 1 - (NA_WIN_C - 1)
    v = jnp.pad(rpb.astype(F32), ((0, 0), (0, 0), (0, 0), (lead, period - lead - nc)))
    skew = jnp.tile(v, (1, 1, 1, GRID_W))[..., :GRID_W * (period - 1)].reshape(depth, heads, nr, GRID_W, period - 1)
    toep = jnp.where(valid, skew[..., GRID_W - 1:], NEG_BIG)
    return jnp.concatenate([toep[:, :, :-1], toep[:, :, 1:]], axis=-1)


def _attn_latent(zb, kl, vl, cache_k, cache_v, layer, toep, att_prev, nb, seq, tok0):
    rows_total = seq // GRID_W
    kr = min(NA_WIN_R, rows_total)
    q0 = tok0 // GRID_W
    past = cache_k.shape[2]
    ck = cache_k.reshape(cache_k.shape[0], cache_k.shape[1], past, NA_HEADS * NA_DH)
    cv = cache_v.reshape(ck.shape)
    return pl.pallas_call(
        functools.partial(_na_kernel, rows_total, kr),
        grid=(nb, rows_total),
        in_specs=[
            pl.BlockSpec((None, GRID_W, CHUNK_W), lambda b, r: (3, q0 + b * rows_total + r, 0)),
            pl.BlockSpec((seq, CHUNK_W), lambda b, r: (b, 0)),
            pl.BlockSpec((seq, CHUNK_W), lambda b, r: (b, 0)),
            pl.BlockSpec((None, None, past, CHUNK_W), lambda b, r: (b, layer, 0, 0)),
            pl.BlockSpec((None, None, past, CHUNK_W), lambda b, r: (b, layer, 0, 0)),
            pl.BlockSpec((None,) + toep.shape[1:], lambda b, r: (layer, 0, 0, 0, 0), pipeline_mode=pl.Buffered(1)),
            pl.BlockSpec(memory_space=pl.ANY),
        ],
        out_specs=pl.BlockSpec((GRID_W, CHUNK_W), lambda b, r: (q0 + b * rows_total + r, 0)),
        out_shape=jax.ShapeDtypeStruct(att_prev.shape, BF16),
        input_output_aliases={6: 0},
        compiler_params=_cparams(("arbitrary", "arbitrary")), name="attention_latent",
    )(zb, kl, vl, ck, cv, toep, att_prev)


CONV_PAD = 16
CONV_ROWS = 64


def _conv_kernel(seq, aliased, *refs):
    if aliased:
        a_ref, gt_ref, w_ref, cb_ref, lg_ref, lb_ref, _alias, o_ref, z_s, sh_s = refs
    else:
        a_ref, gt_ref, w_ref, cb_ref, lg_ref, lb_ref, o_ref, z_s, sh_s = refs
    z_s[0:CONV_PAD, :] = jnp.zeros((CONV_PAD, CONV_CH), F32)
    z_s[CONV_PAD + seq:2 * CONV_PAD + seq, :] = jnp.zeros((CONV_PAD, CONV_CH), F32)
    z_s[CONV_PAD:CONV_PAD + seq, :] = a_ref[...].astype(F32) * _sigmoid(gt_ref[...].astype(F32))
    ext = seq + 2 * CONV_PAD - SUBLANES
    for b in range(1, SUBLANES):
        for r0 in range(0, ext, CONV_ROWS):
            n = min(CONV_ROWS, ext - r0)
            sh_s[b - 1, r0:r0 + n, :] = z_s[r0 + b:r0 + b + n, :]
    off = CONV_PAD - CONV_K // 2
    for t0 in range(0, seq, CONV_ROWS):
        acc = jnp.zeros((CONV_ROWS, CONV_CH), F32)
        for k in range(CONV_K):
            a, b = divmod(off + k, SUBLANES)
            r0 = t0 + a * SUBLANES
            tap = z_s[r0:r0 + CONV_ROWS, :] if b == 0 else sh_s[b - 1, r0:r0 + CONV_ROWS, :]
            acc = acc + tap * w_ref[k:k + 1, :]
        y = _layer_norm(acc + cb_ref[...], lg_ref[...], lb_ref[...])
        o_ref[t0:t0 + CONV_ROWS, :] = _silu(y).astype(BF16)


def _conv_branch(zb, cv_w, cv_b, ln_g, ln_b, *, seq, nb, tok0, t_total, prev=None):
    b0 = tok0 // seq
    in_specs = [
        pl.BlockSpec((None, seq, CHUNK_W), lambda b: (4, b0 + b, 0)),
        pl.BlockSpec((None, seq, CHUNK_W), lambda b: (5, b0 + b, 0)),
        pl.BlockSpec((CONV_K, CONV_CH), lambda b: (0, 0)),
        pl.BlockSpec((1, CONV_CH), lambda b: (0, 0)),
        pl.BlockSpec((1, CONV_CH), lambda b: (0, 0)),
        pl.BlockSpec((1, CONV_CH), lambda b: (0, 0)),
    ]
    args = [zb, zb, cv_w, cv_b, ln_g, ln_b]
    aliases = {}
    if prev is not None:
        in_specs.append(pl.BlockSpec(memory_space=pl.ANY))
        args.append(prev)
        aliases = {6: 0}
    return pl.pallas_call(
        functools.partial(_conv_kernel, seq, prev is not None),
        grid=(nb,), in_specs=in_specs,
        out_specs=pl.BlockSpec((seq, CONV_CH), lambda b: (b0 + b, 0)),
        out_shape=jax.ShapeDtypeStruct((t_total, CONV_CH), BF16),
        scratch_shapes=[pltpu.VMEM((seq + 2 * CONV_PAD, CONV_CH), F32),
                        pltpu.VMEM((SUBLANES - 1, seq + 2 * CONV_PAD - SUBLANES, CONV_CH), F32)],
        input_output_aliases=aliases,
        compiler_params=_cparams(("arbitrary",)), name="conv_module_%d" % seq,
    )(*args)


def _merge_kernel(alpha, n_ctx_tiles, with_router, *refs):
    if with_router:
        (ya_ref, yb_ref, yc_ref, mg_ref, xc_ref, xl_ref, g1_ref, sc2_ref, sh2_ref, wa_ref, wb_ref, wc_ref, wo_ref,
         lg_ref, lb_ref, wr_ref, x1_ref, h2_ref, gates_ref) = refs
    else:
        (ya_ref, yb_ref, yc_ref, mg_ref, xc_ref, xl_ref, g1_ref, sc2_ref, sh2_ref, wa_ref, wb_ref, wc_ref, wo_ref,
         lg_ref, lb_ref, x1_ref, h2_ref) = refs
    x = jnp.where(pl.program_id(0) < n_ctx_tiles, xc_ref[...], xl_ref[...])
    halves = []
    for n in range(2):
        m = None
        for j, (y_ref, w_ref) in enumerate(((ya_ref, wa_ref), (yb_ref, wb_ref), (yc_ref, wc_ref))):
            proj = _dot(y_ref[...], w_ref[:, n * CHUNK_W:(n + 1) * CHUNK_W])
            term = _sigmoid(mg_ref[2 * j + n].astype(F32)) * proj
            m = term if m is None else m + term
        halves.append(m.astype(BF16))
    mix = _dot(halves[0], wo_ref[0:CHUNK_W, :]) + _dot(halves[1], wo_ref[CHUNK_W:2 * CHUNK_W, :])
    x1 = _layer_norm(alpha * x + g1_ref[...] * mix, lg_ref[...], lb_ref[...])
    x1_ref[...] = x1
    h2 = x1 * (1.0 + sc2_ref[...]) + sh2_ref[...]
    h2_ref[...] = h2.astype(BF16)
    if with_router:
        logits = _dot_split(h2, wr_ref[...])
        lane = lax.broadcasted_iota(jnp.int32, logits.shape, 1)
        lg = jnp.where(lane < N_EXPERTS, logits, -jnp.inf)
        m1 = jnp.max(lg, axis=-1, keepdims=True)
        i1 = jnp.min(jnp.where(lg == m1, lane, LANES), axis=-1, keepdims=True)
        lg2 = jnp.where(lane == i1, -jnp.inf, lg)
        m2 = jnp.max(lg2, axis=-1, keepdims=True)
        i2 = jnp.min(jnp.where(lg2 == m2, lane, LANES), axis=-1, keepdims=True)
        e2 = jnp.exp(m2 - m1)
        w1 = 1.0 / (1.0 + e2)
        gates_ref[...] = jnp.where(lane == i1, w1, 0.0) + jnp.where(lane == i2, e2 * w1, 0.0)


def _merge(ya, yb, yc, zb, x_pair, mod_l, wa, wb, wc, wo, ln_g, ln_b, alpha, tm, row_of_tile, w_router=None):
    xc, xl = x_pair
    d = xc.shape[1]
    t = xc.shape[0] + xl.shape[0]
    n_ctx_tiles = xc.shape[0] // tm
    full = lambda shape: pl.BlockSpec(shape, lambda i: tuple(0 for _ in shape), pipeline_mode=pl.Buffered(1))
    in_specs = [
        pl.BlockSpec((tm, CHUNK_W), lambda i: (i, 0)),
        pl.BlockSpec((tm, CHUNK_W), lambda i: (i, 0)),
        pl.BlockSpec((tm, CHUNK_W), lambda i: (i, 0)),
        pl.BlockSpec((6, tm, CHUNK_W), lambda i: (1, i, 0)),
    ] + _pair_specs(tm, d, n_ctx_tiles) + [
        _mod_spec(2, row_of_tile), _mod_spec(4, row_of_tile), _mod_spec(3, row_of_tile),
        full(wa.shape), full(wb.shape), full(wc.shape), full(wo.shape), full((1, d)), full((1, d)),
    ]
    args = [ya, yb, yc, zb, xc, xl, mod_l, mod_l, mod_l, wa, wb, wc, wo, ln_g, ln_b]
    out_specs = [pl.BlockSpec((tm, d), lambda i: (i, 0)), pl.BlockSpec((tm, d), lambda i: (i, 0))]
    out_shape = [jax.ShapeDtypeStruct((t, d), F32), jax.ShapeDtypeStruct((t, d), BF16)]
    if w_router is not None:
        in_specs.append(full(w_router.shape))
        args.append(w_router)
        out_specs.append(pl.BlockSpec((tm, LANES), lambda i: (i, 0)))
        out_shape.append(jax.ShapeDtypeStruct((t, LANES), F32))
    return pl.pallas_call(
        functools.partial(_merge_kernel, alpha, n_ctx_tiles, w_router is not None),
        grid=(t // tm,), in_specs=in_specs, out_specs=out_specs, out_shape=out_shape,
        compiler_params=_cparams(("arbitrary",)), name="merge",
    )(*args)


def _ffn_kernel(alpha, n_ctx_tiles, n_split, h_ref, x1_ref, g2_ref, wg_ref, wu_ref, wd_ref, lg_ref, lb_ref,
                oc_ref, ol_ref):
    h = h_ref[...]
    tf = wg_ref.shape[1] // n_split
    y = None
    for f in range(n_split):
        cols = slice(f * tf, (f + 1) * tf)
        act = _silu(_dot(h, wg_ref[:, cols])) * _dot(h, wu_ref[:, cols])
        part = _dot(act.astype(BF16), wd_ref[cols, :])
        y = part if y is None else y + part
    out = _layer_norm(alpha * x1_ref[...] + g2_ref[...] * y, lg_ref[...], lb_ref[...])
    is_ctx = pl.program_id(0) < n_ctx_tiles

    @pl.when(is_ctx)
    def _():
        oc_ref[...] = out

    @pl.when(jnp.logical_not(is_ctx))
    def _():
        ol_ref[...] = out


def _dense_mixer(h2, x1, mod_l, wg, wu, wd, ln_g, ln_b, alpha, tm, n_split, row_of_tile, t_ctx):
    t, d = x1.shape
    n_ctx_tiles = t_ctx // tm
    const = lambda shape: pl.BlockSpec(shape, lambda i: tuple(0 for _ in shape), pipeline_mode=pl.Buffered(1))
    return pl.pallas_call(
        functools.partial(_ffn_kernel, alpha, n_ctx_tiles, n_split),
        grid=(t // tm,),
        in_specs=[
            pl.BlockSpec((tm, d), lambda i: (i, 0)),
            pl.BlockSpec((tm, d), lambda i: (i, 0)),
            _mod_spec(5, row_of_tile),
            const(wg.shape), const(wu.shape), const(wd.shape), const((1, d)), const((1, d)),
        ],
        out_specs=_pair_specs(tm, d, n_ctx_tiles),
        out_shape=[jax.ShapeDtypeStruct((t_ctx, d), F32), jax.ShapeDtypeStruct((t - t_ctx, d), F32)],
        compiler_params=_cparams(("arbitrary",)), name="dense_mixer",
    )(h2, x1, mod_l, wg, wu, wd, ln_g, ln_b)


MOE_TILE = 256
MOE_TF = 1792


def _route_kernel(g_ref, rank_ref, gate_ref, before_ref, run_s):
    w = pl.program_id(0)

    @pl.when(w == 0)
    def _():
        run_s[...] = jnp.zeros_like(run_s)

    g = g_ref[...]
    sel = g > 0.0
    row = lax.broadcasted_iota(jnp.int32, (MOE_TILE, MOE_TILE), 0)
    col = lax.broadcasted_iota(jnp.int32, (MOE_TILE, MOE_TILE), 1)
    earlier = (col < row).astype(BF16)
    ones = jnp.where(sel, 1.0, 0.0)
    rank = _dot(earlier, ones.astype(BF16)) + run_s[...]
    rank = jnp.where(sel, rank, -1.0)
    before_ref[...] = run_s[...]
    run_s[...] += jnp.sum(ones, axis=0, keepdims=True)
    rank_ref[...] = rank.T[0:N_EXPERTS, :]
    gate_ref[...] = g.T[0:N_EXPERTS, :]


def _route(gates):
    t = gates.shape[0]
    nw = t // MOE_TILE
    return pl.pallas_call(
        _route_kernel,
        grid=(nw,),
        in_specs=[pl.BlockSpec((MOE_TILE, LANES), lambda w: (w, 0))],
        out_specs=[
            pl.BlockSpec((None, N_EXPERTS, MOE_TILE), lambda w: (w, 0, 0)),
            pl.BlockSpec((None, N_EXPERTS, MOE_TILE), lambda w: (w, 0, 0)),
            pl.BlockSpec((None, 1, LANES), lambda w: (w, 0, 0)),
        ],
        out_shape=[
            jax.ShapeDtypeStruct((nw, N_EXPERTS, MOE_TILE), F32),
            jax.ShapeDtypeStruct((nw, N_EXPERTS, MOE_TILE), F32),
            jax.ShapeDtypeStruct((nw, 1, LANES), F32),
        ],
        scratch_shapes=[pltpu.VMEM((1, LANES), F32)],
        compiler_params=_cparams(("arbitrary",)), name="moe_route",
    )(gates)


def _tile_onehot(rank_ref, w, e, base):
    rank_row = rank_ref[w, pl.ds(e, 1), :]
    rows = lax.broadcasted_iota(jnp.int32, (MOE_TILE, MOE_TILE), 0).astype(F32) + base.astype(F32)
    return rank_row == rows


def _gather_kernel(te_ref, base_ref, wlo_ref, whi_ref, rank_ref, x_ref, o_ref, acc_s):
    j = pl.program_id(0)
    e, base = te_ref[j], base_ref[j]
    acc_s[...] = jnp.zeros_like(acc_s)

    def window(w, carry):
        p = jnp.where(_tile_onehot(rank_ref, w, e, base), 1.0, 0.0).astype(BF16)
        acc_s[...] += _dot(p, x_ref[pl.ds(pl.multiple_of(w * MOE_TILE, MOE_TILE), MOE_TILE), :])
        return carry

    lax.fori_loop(wlo_ref[j], whi_ref[j] + 1, window, 0)
    o_ref[...] = acc_s[...].astype(BF16)


def _moe_gather(meta, rank_t, h2, n_tiles):
    t, d = h2.shape
    nw = t // MOE_TILE
    return pl.pallas_call(
        _gather_kernel,
        grid_spec=pltpu.PrefetchScalarGridSpec(
            num_scalar_prefetch=4, grid=(n_tiles,),
            in_specs=[
                pl.BlockSpec((nw, N_EXPERTS, MOE_TILE), lambda j, *_: (0, 0, 0)),
                pl.BlockSpec((t, d), lambda j, *_: (0, 0), pipeline_mode=pl.Buffered(1)),
            ],
            out_specs=pl.BlockSpec((MOE_TILE, d), lambda j, *_: (j, 0)),
            scratch_shapes=[pltpu.VMEM((MOE_TILE, d), F32)],
        ),
        out_shape=jax.ShapeDtypeStruct((n_tiles * MOE_TILE, d), BF16),
        compiler_params=_cparams(("arbitrary",)), name="moe_gather",
    )(*meta, rank_t, h2)


def _moe_up_kernel(te_ref, chg_ref, nused_ref, x_ref, wg_ref, wu_ref, o_ref, wg_s, wu_s):
    j = pl.program_id(1)

    @pl.when(chg_ref[j] == 1)
    def _():
        wg_s[...] = wg_ref[...].astype(BF16)
        wu_s[...] = wu_ref[...].astype(BF16)

    @pl.when(j < nused_ref[0])
    def _():
        x = x_ref[...]
        o_ref[...] = (_silu(_dot(x, wg_s[...])) * _dot(x, wu_s[...])).astype(BF16)

    @pl.when(j >= nused_ref[0])
    def _():
        o_ref[...] = jnp.zeros_like(o_ref)


def _moe_up(te, chg, nused, xs, wg, wu):
    rows, d = xs.shape
    n_tiles = rows // MOE_TILE
    ff = wg.shape[2]
    return pl.pallas_call(
        _moe_up_kernel,
        grid_spec=pltpu.PrefetchScalarGridSpec(
            num_scalar_prefetch=3, grid=(ff // MOE_TF, n_tiles),
            in_specs=[
                pl.BlockSpec((MOE_TILE, d), lambda f, j, *_: (j, 0)),
                pl.BlockSpec((None, d, MOE_TF), lambda f, j, te, *_: (te[j], 0, f)),
                pl.BlockSpec((None, d, MOE_TF), lambda f, j, te, *_: (te[j], 0, f)),
            ],
            out_specs=pl.BlockSpec((MOE_TILE, MOE_TF), lambda f, j, *_: (j, f)),
            scratch_shapes=[pltpu.VMEM((d, MOE_TF), BF16), pltpu.VMEM((d, MOE_TF), BF16)],
        ),
        out_shape=jax.ShapeDtypeStruct((rows, ff), BF16),
        compiler_params=_cparams(("arbitrary", "arbitrary")), name="moe_up",
    )(te, chg, nused, xs, wg, wu)


def _moe_down_kernel(te_ref, chg_ref, nused_ref, a_ref, wd_ref, o_ref, wd_s):
    j = pl.program_id(0)

    @pl.when(chg_ref[j] == 1)
    def _():
        wd_s[...] = wd_ref[...].astype(BF16)

    @pl.when(j < nused_ref[0])
    def _():
        o_ref[...] = _dot(a_ref[...], wd_s[...]).astype(BF16)

    @pl.when(j >= nused_ref[0])
    def _():
        o_ref[...] = jnp.zeros_like(o_ref)


def _moe_down(te, chg, nused, act, wd):
    rows, ff = act.shape
    d = wd.shape[2]
    return pl.pallas_call(
        _moe_down_kernel,
        grid_spec=pltpu.PrefetchScalarGridSpec(
            num_scalar_prefetch=3, grid=(rows // MOE_TILE,),
            in_specs=[
                pl.BlockSpec((MOE_TILE, ff), lambda j, *_: (j, 0)),
                pl.BlockSpec((None, ff, d), lambda j, te, *_: (te[j], 0, 0)),
            ],
            out_specs=pl.BlockSpec((MOE_TILE, d), lambda j, *_: (j, 0)),
            scratch_shapes=[pltpu.VMEM((ff, d), BF16)],
        ),
        out_shape=jax.ShapeDtypeStruct((rows, d), BF16),
        compiler_params=_cparams(("arbitrary",)), name="moe_down",
    )(te, chg, nused, act, wd)


def _combine_kernel(alpha, n_tiles, nw_ctx, te_ref, base_ref, wlo_ref, whi_ref, rank_ref, gate_ref, y_ref, x1_ref,
                    g2_ref, lg_ref, lb_ref, oc_ref, ol_ref, acc_s):
    s = pl.program_id(0)

    @pl.when(s == 0)
    def _():
        acc_s[...] = jnp.zeros_like(acc_s)

    @pl.when(s < n_tiles)
    def _():
        e, base = te_ref[s], base_ref[s]

        def window(w, carry):
            hit = _tile_onehot(rank_ref, w, e, base)
            q = jnp.where(hit, gate_ref[w, pl.ds(e, 1), :], 0.0).T.astype(BF16)
            rows = pl.ds(pl.multiple_of(w * MOE_TILE, MOE_TILE), MOE_TILE)
            acc_s[rows, :] += _dot(q, y_ref[...])
            return carry

        lax.fori_loop(wlo_ref[s], whi_ref[s] + 1, window, 0)

    @pl.when(s >= n_tiles)
    def _():
        w = s - n_tiles
        rows = pl.ds(pl.multiple_of(w * MOE_TILE, MOE_TILE), MOE_TILE)
        out = _layer_norm(alpha * x1_ref[...] + g2_ref[...] * acc_s[rows, :], lg_ref[...], lb_ref[...])

        @pl.when(w < nw_ctx)
        def _():
            oc_ref[...] = out

        @pl.when(w >= nw_ctx)
        def _():
            ol_ref[...] = out


def _moe_combine(meta, rank_t, gate_t, y, x1, mod_l, ln_g, ln_b, alpha, row_of_tile, t_ctx):
    t, d = x1.shape
    nw = t // MOE_TILE
    nw_ctx = t_ctx // MOE_TILE
    n_tiles = y.shape[0] // MOE_TILE
    win = lambda s: jnp.maximum(s - n_tiles, 0)
    return pl.pallas_call(
        functools.partial(_combine_kernel, alpha, n_tiles, nw_ctx),
        grid_spec=pltpu.PrefetchScalarGridSpec(
            num_scalar_prefetch=4, grid=(n_tiles + nw,),
            in_specs=[
                pl.BlockSpec((nw, N_EXPERTS, MOE_TILE), lambda s, *_: (0, 0, 0)),
                pl.BlockSpec((nw, N_EXPERTS, MOE_TILE), lambda s, *_: (0, 0, 0)),
                pl.BlockSpec((MOE_TILE, d), lambda s, *_: (jnp.minimum(s, n_tiles - 1), 0)),
                pl.BlockSpec((MOE_TILE, d), lambda s, *_: (win(s), 0)),
                pl.BlockSpec((None, None, 1, d), lambda s, *_: (row_of_tile(win(s)), 5, 0, 0)),
                pl.BlockSpec((1, d), lambda s, *_: (0, 0)),
                pl.BlockSpec((1, d), lambda s, *_: (0, 0)),
            ],
            out_specs=[
                pl.BlockSpec((MOE_TILE, d), lambda s, *_: (jnp.minimum(win(s), nw_ctx - 1), 0)),
                pl.BlockSpec((MOE_TILE, d), lambda s, *_: (jnp.maximum(win(s) - nw_ctx, 0), 0)),
            ],
            scratch_shapes=[pltpu.VMEM((t, d), F32)],
        ),
        out_shape=[jax.ShapeDtypeStruct((t_ctx, d), F32), jax.ShapeDtypeStruct((t - t_ctx, d), F32)],
        compiler_params=_cparams(("arbitrary",)), name="moe_combine",
    )(*meta, rank_t, gate_t, y, x1, mod_l, ln_g, ln_b)


def _moe_layer(gates, h2, x1, mod_l, wg, wu, wd, ln_g, ln_b, alpha, row_of_tile, t_ctx):
    t = h2.shape[0]
    n_tiles = -(-(2 * t + N_EXPERTS * (MOE_TILE - 1)) // MOE_TILE)
    rank_t, gate_t, before = _route(gates)
    before = before[:, 0, :N_EXPERTS].astype(jnp.int32)
    counts = jnp.sum(gates[:, :N_EXPERTS] > 0.0, axis=0).astype(jnp.int32)
    tiles_e = (counts + MOE_TILE - 1) // MOE_TILE
    tile_end = jnp.cumsum(tiles_e)
    n_used = tile_end[-1]
    j = jnp.arange(n_tiles, dtype=jnp.int32)
    te = jnp.minimum(jnp.sum(j[:, None] >= tile_end[None, :], axis=1), N_EXPERTS - 1).astype(jnp.int32)
    base = (j - jnp.take(tile_end - tiles_e, te)) * MOE_TILE
    last = jnp.minimum(base + MOE_TILE - 1, jnp.take(counts, te) - 1)
    before_t = jnp.take(before, te, axis=1)
    used = j < n_used
    wlo = jnp.where(used, jnp.sum(before_t <= base[None, :], axis=0) - 1, 1).astype(jnp.int32)
    whi = jnp.where(used, jnp.sum(before_t <= last[None, :], axis=0) - 1, 0).astype(jnp.int32)
    chg = jnp.concatenate([jnp.ones((1,), jnp.int32), (te[1:] != te[:-1]).astype(jnp.int32)])
    nused = n_used.reshape(1).astype(jnp.int32)
    meta = (te, base.astype(jnp.int32), wlo, whi)
    xs = _moe_gather(meta, rank_t, h2, n_tiles)
    act = _moe_up(te, chg, nused, xs, wg, wu)
    y = _moe_down(te, chg, nused, act, wd)
    return _moe_combine(meta, rank_t, gate_t, y, x1, mod_l, ln_g, ln_b, alpha, row_of_tile, t_ctx)


def _rope_tables(seq):
    half = GLA_DK // 2
    n = half // 2
    t = np.arange(seq)
    inv = ROPE_BASE ** (-np.arange(n, dtype=np.float64) / n)
    cos = np.zeros((seq, GLA_DK), np.float64)
    sin = np.zeros((seq, GLA_DK), np.float64)
    for a, pos in enumerate((t // GRID_W, t % GRID_W)):
        ang = pos[:, None].astype(np.float64) * inv[None, :]
        base = a * half
        cos[:, base:base + n] = np.cos(ang)
        cos[:, base + n:base + half] = np.cos(ang)
        sin[:, base:base + n] = -np.sin(ang)
        sin[:, base + n:base + half] = np.sin(ang)
    tile = lambda m: jnp.asarray(np.tile(m, (1, GLA_HEADS)), F32)
    return tile(cos), tile(sin)


def _chunk_w_in(w):
    bf = lambda a: a.astype(BF16)
    gd = jnp.pad(w[:, :, 1536:1568], ((0, 0), (0, 0), (0, LANES - 2 * GLA_RANK)))
    kv = jnp.stack([w[:, :, 2080:2592], w[:, :, 2592:3104]], axis=1)
    starts = (0, 512, 1024, 1568, 3104, 3616, 4128, 4640, 5152, 5664, 6176, 6688)
    return bf(gd), bf(kv), bf(jnp.stack([w[:, :, s:s + CHUNK_W] for s in starts], axis=1))


def _gate_up_blocks(w_gup):
    qkw = w_gup.shape[-1]
    fwd = jnp.pad(w_gup[:, 0], ((0, 0), (0, 0), (0, qkw)))
    bwd = jnp.pad(w_gup[:, 1], ((0, 0), (0, 0), (qkw, 0)))
    return jnp.pad(jnp.concatenate([fwd, bwd], axis=1), ((0, 0), (0, LANES - 2 * GLA_RANK), (0, 0)))


def kernel(x_prompt, x_sample, cache_na_k, cache_na_v, state_gla, c, c_ctx, w_mod, b_mod, w_in, gla_w_gup, gla_b_g, gla_norm, w_br_gla, na_rpb, w_br_na, cv_w, cv_b, cv_ln_g, cv_ln_b, w_br_cv, w_out, ln_g, ln_b, ffd_w_gate, ffd_w_up, ffd_w_down, moe_w_router, moe_w_gate, moe_w_up, moe_w_down):
    nb_c, l_c, d = x_prompt.shape
    nb_l, l_l, _ = x_sample.shape
    depth = w_mod.shape[0]
    t_ctx, t_lat = nb_c * l_c, nb_l * l_l
    t_all = t_ctx + t_lat
    alpha = (2 * depth) ** 0.25
    tm = 512
    row_of_tile = _row_of_tile_fn(t_ctx, l_l, tm)

    x_pair = (x_prompt.reshape(t_ctx, d), x_sample.reshape(t_lat, d))
    cond8 = jnp.concatenate([c_ctx[None, :], c, jnp.zeros((8 - 1 - nb_l, d), F32)], axis=0)
    mod = _modulation(cond8, w_mod, b_mod).reshape(depth, 8, 6, 1, d)
    rope = _rope_tables(l_l)
    rows_lat = l_l // GRID_W
    kr = min(NA_WIN_R, rows_lat)
    s0_all = state_gla.reshape(nb_l, depth, 2, 2, LANES, GLA_DV)
    w_gd, w_kv, w_chunks = _chunk_w_in(w_in)
    wg_blks = _gate_up_blocks(gla_w_gup)
    toep = _na_bias_tables(na_rpb)
    w_br = [w.astype(BF16) for w in (w_br_gla, w_br_na, w_br_cv, w_out)]

    caches = None
    ss = []
    for l in range(depth):
        mod_l = mod[l]
        gd, kc, vc, kl, vl, zb = _input_projection(x_pair, mod_l, (w_gd[l], w_kv[l], w_chunks[l]), l, depth, l_c,
                                                   tm, row_of_tile, caches)
        caches = (kc, vc)

        wg_blk = wg_blks[l]
        bg = gla_b_g[l].reshape(1, CHUNK_W)
        gn = gla_norm[l].reshape(1, CHUNK_W)
        ya, s_fin = _gla_branch(gd, zb, wg_blk, bg, gn, latent=False, seq=l_c, nb=nb_c, tok0=0, t_total=t_all)
        ya = _gla_branch(gd, zb, wg_blk, bg, gn, latent=True, seq=l_l, nb=nb_l, tok0=t_ctx, t_total=t_all,
                         rope=rope, s0=s0_all[:, l], ya_prev=ya)

        yb = _attn_context(zb, kc, vc, l, nb_c, l_c, t_all)
        yb = _attn_latent(zb, kl, vl, cache_na_k, cache_na_v, l, toep, yb, nb_l, l_l, t_ctx)

        conv_args = (zb, cv_w[l], cv_b[l][None, :], cv_ln_g[l][None, :], cv_ln_b[l][None, :])
        yc = _conv_branch(*conv_args, seq=l_c, nb=nb_c, tok0=0, t_total=t_all)
        yc = _conv_branch(*conv_args, seq=l_l, nb=nb_l, tok0=t_ctx, t_total=t_all, prev=yc)

        moe = l % 2 == 1
        w_router = None
        if moe:
            w_router = jnp.pad(moe_w_router[l // 2], ((0, 0), (0, LANES - N_EXPERTS)))
        merged = _merge(ya, yb, yc, zb, x_pair, mod_l, w_br[0][l], w_br[1][l], w_br[2][l], w_br[3][l],
                        ln_g[l, 0][None, :], ln_b[l, 0][None, :], alpha, tm, row_of_tile, w_router)
        ln2 = (ln_g[l, 1][None, :], ln_b[l, 1][None, :])
        if moe:
            x1, h2, gates = merged
            x_pair = _moe_layer(gates, h2, x1, mod_l, moe_w_gate[l // 2], moe_w_up[l // 2], moe_w_down[l // 2],
                                *ln2, alpha, _row_of_tile_fn(t_ctx, l_l, MOE_TILE), t_ctx)
        else:
            x1, h2 = merged
            i = l // 2
            x_pair = _dense_mixer(h2, x1, mod_l, ffd_w_gate[i].astype(BF16), ffd_w_up[i].astype(BF16),
                                  ffd_w_down[i].astype(BF16), *ln2, alpha, tm, 2, row_of_tile, t_ctx)
        ss.append(s_fin.reshape(nb_c, 2, GLA_HEADS, GLA_DK, GLA_DV))

    y_prompt = x_pair[0].reshape(nb_c, l_c, d)
    y_sample = x_pair[1].reshape(nb_l, l_l, d)
    cache_shape = (nb_c, depth, l_c, NA_HEADS, NA_DH)
    return (y_prompt, y_sample, caches[0].reshape(cache_shape), caches[1].reshape(cache_shape), jnp.stack(ss, 1))


def _row_of_tile_fn(t_ctx, l_lat, tm):
    n_ctx = t_ctx // tm

    def row_of_tile(i):
        return jnp.where(i < n_ctx, 0, 1 + (i - n_ctx) // (l_lat // tm))

    return row_of_tile
```

```python
import functools

import numpy as np
import jax
import jax.numpy as jnp
from jax import lax
from jax.experimental import pallas as pl
from jax.experimental.pallas import tpu as pltpu

F32 = jnp.float32
BF16 = jnp.bfloat16

GRID_W = 64
GLA_HEADS = 4
GLA_DK = 64
GLA_DV = 128
GLA_RANK = 16
GLA_TAU = 16.0
GLA_CHUNK = 64
GLA_GROUP = 256
NA_HEADS = 8
NA_DH = 64
NA_WIN_R = 8
NA_WIN_C = 16
CONV_CH = 512
CONV_K = 31
N_EXPERTS = 8
ROPE_BASE = 10000.0
LN_EPS = 1e-5

D_MODEL = 1024
LANES = 128
SUBLANES = 8
CHUNK_W = 512
N_BF16_CHUNKS = 12
NEG_BIG = -1e30
VMEM_LIMIT = 56 * 1024 * 1024


def _cparams(sem, vmem=VMEM_LIMIT):
    return pltpu.CompilerParams(dimension_semantics=sem, vmem_limit_bytes=vmem)


def _dot(a, b):
    return jnp.dot(a, b, preferred_element_type=F32)


def _split(x):
    hi = x.astype(BF16)
    return hi, (x - hi.astype(F32)).astype(BF16)


def _dot_split(a, b):
    a_hi, a_lo = _split(a)
    b_hi, b_lo = _split(b)
    return _dot(a_hi, b_hi) + _dot(a_lo, b_hi) + _dot(a_hi, b_lo)


def _dot_split_rhs(a, b):
    b_hi, b_lo = _split(b)
    return _dot(a, b_hi) + _dot(a, b_lo)


def _dot_nt(a, b):
    return lax.dot_general(a, b, (((1,), (1,)), ((), ())), preferred_element_type=F32)


def _sigmoid(x):
    return 0.5 * jnp.tanh(0.5 * x) + 0.5


def _silu(x):
    return x * _sigmoid(x)


def _layer_norm(x, g, b):
    mu = jnp.mean(x, axis=-1, keepdims=True)
    xc = x - mu
    var = jnp.mean(xc * xc, axis=-1, keepdims=True)
    return xc * lax.rsqrt(var + LN_EPS) * g + b


def _mod_kernel(c_ref, w_ref, b_ref, o_ref):
    c = c_ref[...]
    s = _silu(c).astype(BF16)
    o_ref[...] = _dot(s, w_ref[...].astype(BF16)) + b_ref[...]


def _modulation(cond8, w_mod, b_mod):
    depth, d, n = w_mod.shape
    tn = 1024
    return pl.pallas_call(
        _mod_kernel,
        grid=(depth, n // tn),
        in_specs=[
            pl.BlockSpec((8, d), lambda l, j: (0, 0)),
            pl.BlockSpec((None, d, tn), lambda l, j: (l, 0, j)),
            pl.BlockSpec((None, 1, tn), lambda l, j: (l, 0, j)),
        ],
        out_specs=pl.BlockSpec((None, 8, tn), lambda l, j: (l, 0, j)),
        out_shape=jax.ShapeDtypeStruct((depth, 8, n), F32),
        compiler_params=_cparams(("arbitrary", "arbitrary")),
        name="modulation",
    )(cond8, w_mod, b_mod.reshape(depth, 1, n))


def _inproj_kernel(n_ctx_tiles, aliased, *refs):
    if aliased:
        (xc_ref, xl_ref, sc_ref, sh_ref, wgd_ref, wkv_ref, w_ref, _kc_in, _vc_in,
         gd_ref, kc_ref, vc_ref, kl_ref, vl_ref, zb_ref, h_ref) = refs
    else:
        (xc_ref, xl_ref, sc_ref, sh_ref, wgd_ref, wkv_ref, w_ref,
         gd_ref, kc_ref, vc_ref, kl_ref, vl_ref, zb_ref, h_ref) = refs
    is_ctx = pl.program_id(0) < n_ctx_tiles
    x = jnp.where(is_ctx, xc_ref[...], xl_ref[...])
    h_ref[...] = (x * (1.0 + sc_ref[...]) + sh_ref[...]).astype(BF16)
    gd_ref[...] = _dot(h_ref[...], wgd_ref[...])
    k = _dot(h_ref[...], wkv_ref[0])
    v = _dot(h_ref[...], wkv_ref[1])
    seqs, l_c, _ = kc_ref.shape

    @pl.when(is_ctx)
    def _():
        for s in range(seqs):
            kc_ref[s] = k[s * l_c:(s + 1) * l_c]
            vc_ref[s] = v[s * l_c:(s + 1) * l_c]

    @pl.when(jnp.logical_not(is_ctx))
    def _():
        kl_ref[...] = k
        vl_ref[...] = v

    for j in range(N_BF16_CHUNKS):
        zb_ref[j] = _dot(h_ref[...], w_ref[j]).astype(BF16)


def _mod_spec(piece, row_of_tile):
    return pl.BlockSpec((None, None, 1, D_MODEL), lambda i: (row_of_tile(i), piece, 0, 0))


def _pair_specs(tm, d, n_ctx_tiles):
    return [pl.BlockSpec((tm, d), lambda i, *_: (jnp.minimum(i, n_ctx_tiles - 1), 0)),
            pl.BlockSpec((tm, d), lambda i, *_: (jnp.maximum(i - n_ctx_tiles, 0), 0))]


def _input_projection(x_pair, mod_l, w_parts, layer, depth, l_c, tm, row_of_tile, caches=None):
    xc, xl = x_pair
    w_gd, w_kv, w_chunks = w_parts
    d = xc.shape[1]
    t_ctx, t_lat = xc.shape[0], xl.shape[0]
    t = t_ctx + t_lat
    n_ctx_tiles = t_ctx // tm
    seqs = tm // l_c
    cache_spec = pl.BlockSpec((seqs, None, l_c, CHUNK_W),
                              lambda i: (jnp.minimum(i, n_ctx_tiles - 1), layer, 0, 0))
    lat_spec = pl.BlockSpec((tm, CHUNK_W), lambda i: (jnp.maximum(i - n_ctx_tiles, 0), 0))
    const = lambda shape: pl.BlockSpec(shape, lambda i: tuple(0 for _ in shape), pipeline_mode=pl.Buffered(1))
    in_specs = _pair_specs(tm, d, n_ctx_tiles) + [
        _mod_spec(1, row_of_tile), _mod_spec(0, row_of_tile),
        const(w_gd.shape), const(w_kv.shape), const(w_chunks.shape),
    ]
    args = [xc, xl, mod_l, mod_l, w_gd, w_kv, w_chunks]
    aliases = {}
    if caches is not None:
        in_specs += [pl.BlockSpec(memory_space=pl.ANY), pl.BlockSpec(memory_space=pl.ANY)]
        args += list(caches)
        aliases = {7: 1, 8: 2}
    cache_shape = jax.ShapeDtypeStruct((t_ctx // l_c, depth, l_c, CHUNK_W), F32)
    return pl.pallas_call(
        functools.partial(_inproj_kernel, n_ctx_tiles, caches is not None),
        grid=(t // tm,),
        in_specs=in_specs,
        out_specs=[
            pl.BlockSpec((tm, LANES), lambda i: (i, 0)),
            cache_spec, cache_spec, lat_spec, lat_spec,
            pl.BlockSpec((N_BF16_CHUNKS, tm, CHUNK_W), lambda i: (0, i, 0)),
        ],
        out_shape=[
            jax.ShapeDtypeStruct((t, LANES), F32),
            cache_shape, cache_shape,
            jax.ShapeDtypeStruct((t_lat, CHUNK_W), F32), jax.ShapeDtypeStruct((t_lat, CHUNK_W), F32),
            jax.ShapeDtypeStruct((N_BF16_CHUNKS, t, CHUNK_W), BF16),
        ],
        scratch_shapes=[pltpu.VMEM((tm, d), BF16)],
        input_output_aliases=aliases,
        compiler_params=_cparams(("arbitrary",)),
        name="input_projection",
    )(*args)


def _gla_kernel(latent, seq, *refs):
    if latent:
        (qk_ref, v_ref, ra_ref, gd_ref, wg_ref, bg_ref, gn_ref, cos_ref, sin_ref, s0_ref, _alias,
         ya_ref, q_s, k_s, g_s, acc_s, st_s) = refs
    else:
        (qk_ref, v_ref, ra_ref, gd_ref, wg_ref, bg_ref, gn_ref,
         ya_ref, sfin_ref, q_s, k_s, g_s, acc_s, st_s) = refs
    c = GLA_CHUNK
    n_chunks = seq // c
    qkw = GLA_HEADS * GLA_DK

    q = qk_ref[:, 0:qkw].astype(F32) * (GLA_DK ** -0.5)
    k = qk_ref[:, qkw:2 * qkw].astype(F32)
    if latent:
        lane = lax.broadcasted_iota(jnp.int32, (1, qkw), 1)
        first_half = (lane % 32) < 16

        def rope(x):
            swapped = jnp.where(first_half, pltpu.roll(x, qkw - 16, 1), pltpu.roll(x, 16, 1))
            return x * cos_ref[...] + swapped * sin_ref[...]

        q, k = rope(q), rope(k)
    q_s[...] = q
    k_s[...] = k

    glog = _dot_split(gd_ref[...], wg_ref[...]) + bg_ref[...]
    g_s[...] = (jnp.minimum(glog, 0.0) - jnp.log(1.0 + jnp.exp(-jnp.abs(glog)))) * (1.0 / GLA_TAU)
    acc_s[...] = jnp.zeros_like(acc_s)
    if latent:
        st_s[...] = s0_ref[...]
    else:
        st_s[...] = jnp.zeros_like(st_s)

    grp = GLA_GROUP
    cpg = grp // c
    n_groups = seq // grp
    grow = lax.broadcasted_iota(jnp.int32, (grp, grp), 0)
    gcol = lax.broadcasted_iota(jnp.int32, (grp, grp), 1)
    same_chunk = (grow // c) == (gcol // c)
    keep = (same_chunk & (gcol <= grow), same_chunk & (gcol >= grow))
    tri = (keep[0].astype(BF16), keep[1].astype(BF16))
    lane128 = lax.broadcasted_iota(jnp.int32, (1, LANES), 1)
    head_mask = (lane128 < GLA_DK, lane128 >= GLA_DK)
    urow = lax.broadcasted_iota(jnp.int32, (cpg * LANES, GLA_DV), 0)
    srow = lax.broadcasted_iota(jnp.int32, (LANES, GLA_DV), 0)
    chunk_of_col = lax.broadcasted_iota(jnp.int32, (LANES, grp), 1) // c
    chunk_of_row = lax.broadcasted_iota(jnp.int32, (grp, LANES), 0) // c
    pad_rows = jnp.zeros((SUBLANES - cpg, qkw), F32)

    def group_step(i, carry):
        for d in range(2):
            gi = i if d == 0 else n_groups - 1 - i
            rows = pl.ds(pl.multiple_of(gi * grp, grp), grp)
            cum = _dot_split_rhs(tri[d], g_s[rows, d * qkw:(d + 1) * qkw])
            edge = c - 1 if d == 0 else 0
            lasts = [cum[ci * c + edge:ci * c + edge + 1, :] for ci in range(cpg)]
            last = jnp.concatenate([jnp.broadcast_to(r, (c, qkw)) for r in lasts], axis=0)
            ref = 0.5 * last
            q = q_s[rows, :]
            k = k_s[rows, :]
            qe = q * jnp.exp(cum - ref)
            ke = k * jnp.exp(ref - cum)
            qe2 = q * jnp.exp(cum)
            ke2 = k * jnp.exp(last - cum)
            last8 = jnp.concatenate(lasts + [pad_rows], axis=0)
            for p in range(2):
                sl = slice(p * LANES, (p + 1) * LANES)
                ke_p = ke[:, sl].astype(BF16)
                ke2_t = ke2[:, sl].T
                ke2_blk = jnp.concatenate(
                    [jnp.where(chunk_of_col == ci, ke2_t, 0.0) for ci in range(cpg)], axis=0).astype(BF16)
                dec = jnp.exp(last8[:, sl]).T
                psl = slice(2 * p * GLA_DV, (2 * p + 2) * GLA_DV)
                v_pair = v_ref[rows, psl]
                u = _dot(ke2_blk, v_pair)
                upd = jnp.where(urow % LANES < GLA_DK, u[:, :GLA_DV], u[:, GLA_DV:])
                s = st_s[d, p]
                s_in = [None] * cpg
                for ci in (range(cpg) if d == 0 else range(cpg - 1, -1, -1)):
                    s_in[ci] = jnp.concatenate(
                        [jnp.where(srow < GLA_DK, s, 0.0), jnp.where(srow >= GLA_DK, s, 0.0)], axis=1).astype(BF16)
                    s = dec[:, ci:ci + 1] * s + upd[ci * LANES:(ci + 1) * LANES]
                st_s[d, p] = s
                s_stack = jnp.concatenate(s_in, axis=0)
                a2_blk = jnp.concatenate(
                    [jnp.where(chunk_of_row == ci, qe2[:, sl], 0.0) for ci in range(cpg)], axis=1).astype(BF16)
                intra = []
                for hh in range(2):
                    a = jnp.where(head_mask[hh], qe[:, sl], 0.0).astype(BF16)
                    att = jnp.where(keep[d], _dot_nt(a, ke_p), 0.0).astype(BF16)
                    intra.append(_dot(att, v_pair[:, hh * GLA_DV:(hh + 1) * GLA_DV]))
                acc_s[rows, psl] += jnp.concatenate(intra, axis=1) + _dot(a2_blk, s_stack)
        return carry

    lax.fori_loop(0, n_groups, group_step, 0)

    def epilogue(i, carry):
        rows = pl.ds(pl.multiple_of(i * c, c), c)
        for h in range(GLA_HEADS):
            vsl = slice(h * GLA_DV, (h + 1) * GLA_DV)
            o = acc_s[rows, vsl]
            o = o * lax.rsqrt(jnp.mean(o * o, axis=-1, keepdims=True) + LN_EPS) * gn_ref[:, vsl]
            ya_ref[rows, vsl] = (o * _silu(ra_ref[rows, vsl].astype(F32))).astype(BF16)
        return carry

    lax.fori_loop(0, n_chunks, epilogue, 0)
    if not latent:
        sfin_ref[...] = st_s[...]


def _gla_branch(gd, zb, wg_blk, bg, gn, *, latent, seq, nb, tok0, t_total, rope=None, s0=None, ya_prev=None):
    b0 = tok0 // seq
    qkw = GLA_HEADS * GLA_DK
    in_specs = [
        pl.BlockSpec((None, seq, CHUNK_W), lambda b: (0, b0 + b, 0)),
        pl.BlockSpec((None, seq, CHUNK_W), lambda b: (1, b0 + b, 0)),
        pl.BlockSpec((None, seq, CHUNK_W), lambda b: (2, b0 + b, 0)),
        pl.BlockSpec((seq, LANES), lambda b: (b0 + b, 0)),
        pl.BlockSpec((LANES, CHUNK_W), lambda b: (0, 0)),
        pl.BlockSpec((1, CHUNK_W), lambda b: (0, 0)),
        pl.BlockSpec((1, CHUNK_W), lambda b: (0, 0)),
    ]
    args = [zb, zb, zb, gd, wg_blk, bg, gn]
    scratch = [
        pltpu.VMEM((seq, qkw), F32), pltpu.VMEM((seq, qkw), F32), pltpu.VMEM((seq, CHUNK_W), F32),
        pltpu.VMEM((seq, CHUNK_W), F32), pltpu.VMEM((2, 2, LANES, GLA_DV), F32),
    ]
    ya_shape = jax.ShapeDtypeStruct((t_total, CHUNK_W), BF16)
    ya_spec = pl.BlockSpec((seq, CHUNK_W), lambda b: (b0 + b, 0))
    if latent:
        cos_t, sin_t = rope
        in_specs += [
            pl.BlockSpec((seq, qkw), lambda b: (0, 0)),
            pl.BlockSpec((seq, qkw), lambda b: (0, 0)),
            pl.BlockSpec((None, 2, 2, LANES, GLA_DV), lambda b: (b, 0, 0, 0, 0)),
            pl.BlockSpec(memory_space=pl.ANY),
        ]
        args += [cos_t, sin_t, s0, ya_prev]
        return pl.pallas_call(
            functools.partial(_gla_kernel, True, seq),
            grid=(nb,), in_specs=in_specs, out_specs=ya_spec, out_shape=ya_shape,
            scratch_shapes=scratch, input_output_aliases={len(args) - 1: 0},
            compiler_params=_cparams(("arbitrary",)), name="gla_latent",
        )(*args)
    return pl.pallas_call(
        functools.partial(_gla_kernel, False, seq),
        grid=(nb,), in_specs=in_specs,
        out_specs=[ya_spec, pl.BlockSpec((None, 2, 2, LANES, GLA_DV), lambda b: (b, 0, 0, 0, 0))],
        out_shape=[ya_shape, jax.ShapeDtypeStruct((nb, 2, 2, LANES, GLA_DV), F32)],
        scratch_shapes=scratch,
        compiler_params=_cparams(("arbitrary",)), name="gla_context",
    )(*args)


def _attn_ctx_kernel(q_ref, k_ref, v_ref, o_ref):
    lane = lax.broadcasted_iota(jnp.int32, (1, LANES), 1)
    masks = (lane < NA_DH, lane >= NA_DH)
    scale = NA_DH ** -0.5
    seq = q_ref.shape[0]
    for p in range(NA_HEADS // 2):
        sl = slice(p * LANES, (p + 1) * LANES)
        q2 = _stack_heads(q_ref[:, sl], masks)
        kp = k_ref[:, sl].astype(BF16)
        vp = v_ref[:, sl].astype(BF16)
        s = _dot_nt(q2, kp) * scale
        e = jnp.exp(s - jnp.max(s, axis=-1, keepdims=True))
        o2 = _dot(e.astype(BF16), vp) / jnp.sum(e, axis=-1, keepdims=True)
        o_ref[:, sl] = jnp.where(masks[0], o2[0:seq], o2[seq:2 * seq]).astype(BF16)


def _stack_heads(qp, masks):
    zero = jnp.zeros_like(qp)
    return jnp.concatenate([jnp.where(masks[0], qp, zero), jnp.where(masks[1], qp, zero)], axis=0)


def _attn_context(zb, kc, vc, layer, nb, seq, t_total):
    return pl.pallas_call(
        _attn_ctx_kernel,
        grid=(nb,),
        in_specs=[
            pl.BlockSpec((None, seq, CHUNK_W), lambda b: (3, b, 0)),
            pl.BlockSpec((None, None, seq, CHUNK_W), lambda b: (b, layer, 0, 0)),
            pl.BlockSpec((None, None, seq, CHUNK_W), lambda b: (b, layer, 0, 0)),
        ],
        out_specs=pl.BlockSpec((seq, CHUNK_W), lambda b: (b, 0)),
        out_shape=jax.ShapeDtypeStruct((t_total, CHUNK_W), BF16),
        compiler_params=_cparams(("arbitrary",)), name="attention_context",
    )(zb, kc, vc)


def _na_kernel(rows_total, kr, q_ref, kl_ref, vl_ref, kc_ref, vc_ref, toep_ref, _alias, o_ref):
    r = pl.program_id(1)
    rs = jnp.clip(r - kr // 2, 0, rows_total - kr)
    krows = pl.ds(pl.multiple_of(rs * GRID_W, GRID_W), kr * GRID_W)
    dr0 = rs - r + NA_WIN_R - 1
    lane = lax.broadcasted_iota(jnp.int32, (1, LANES), 1)
    masks = (lane < NA_DH, lane >= NA_DH)
    scale = NA_DH ** -0.5
    nq = q_ref.shape[0]
    for p in range(NA_HEADS // 2):
        sl = slice(p * LANES, (p + 1) * LANES)
        q2 = _stack_heads(q_ref[:, sl], masks)
        klp = kl_ref[krows, sl].astype(BF16)
        vlp = vl_ref[krows, sl].astype(BF16)
        kcp = kc_ref[:, sl].astype(BF16)
        vcp = vc_ref[:, sl].astype(BF16)
        bias = jnp.concatenate(
            [jnp.concatenate([toep_ref[2 * p + hh, dr0 + 2 * m] for m in range(kr // 2)], axis=1)
             for hh in range(2)], axis=0)
        s_loc = _dot_nt(q2, klp) * scale + bias
        s_ctx = _dot_nt(q2, kcp) * scale
        m = jnp.maximum(jnp.max(s_loc, axis=-1, keepdims=True), jnp.max(s_ctx, axis=-1, keepdims=True))
        e_loc = jnp.exp(s_loc - m)
        e_ctx = jnp.exp(s_ctx - m)
        den = jnp.sum(e_loc, axis=-1, keepdims=True) + jnp.sum(e_ctx, axis=-1, keepdims=True)
        o2 = (_dot(e_loc.astype(BF16), vlp) + _dot(e_ctx.astype(BF16), vcp)) / den
        o_ref[:, sl] = jnp.where(masks[0], o2[0:nq], o2[nq:2 * nq]).astype(BF16)


def _na_bias_tables(rpb):
    depth, heads, nr, nc = rpb.shape
    qc = np.arange(GRID_W)[:, None]
    kc = np.arange(GRID_W)[None, :]
    cs = np.clip(qc - NA_WIN_C // 2, 0, GRID_W - NA_WIN_C)
    valid = (kc >= cs) & (kc < cs + NA_WIN_C)
    period = 2 * GRID_W
    lead = GRID_W - 1 - (NA_WIN_C - 1)
    v = jnp.pad(rpb.astype(F32), ((0, 0), (0, 0), (0, 0), (lead, period - lead - nc)))
    skew = jnp.tile(v, (1, 1, 1, GRID_W))[..., :GRID_W * (period - 1)].reshape(depth, heads, nr, GRID_W, period - 1)
    toep = jnp.where(valid, skew[..., GRID_W - 1:], NEG_BIG)
    return jnp.concatenate([toep[:, :, :-1], toep[:, :, 1:]], axis=-1)


def _attn_latent(zb, kl, vl, cache_k, cache_v, layer, toep, att_prev, nb, seq, tok0):
    rows_total = seq // GRID_W
    kr = min(NA_WIN_R, rows_total)
    q0 = tok0 // GRID_W
    past = cache_k.shape[2]
    ck = cache_k.reshape(cache_k.shape[0], cache_k.shape[1], past, NA_HEADS * NA_DH)
    cv = cache_v.reshape(ck.shape)
    return pl.pallas_call(
        functools.partial(_na_kernel, rows_total, kr),
        grid=(nb, rows_total),
        in_specs=[
            pl.BlockSpec((None, GRID_W, CHUNK_W), lambda b, r: (3, q0 + b * rows_total + r, 0)),
            pl.BlockSpec((seq, CHUNK_W), lambda b, r: (b, 0)),
            pl.BlockSpec((seq, CHUNK_W), lambda b, r: (b, 0)),
            pl.BlockSpec((None, None, past, CHUNK_W), lambda b, r: (b, layer, 0, 0)),
            pl.BlockSpec((None, None, past, CHUNK_W), lambda b, r: (b, layer, 0, 0)),
            pl.BlockSpec((None,) + toep.shape[1:], lambda b, r: (layer, 0, 0, 0, 0), pipeline_mode=pl.Buffered(1)),
            pl.BlockSpec(memory_space=pl.ANY),
        ],
        out_specs=pl.BlockSpec((GRID_W, CHUNK_W), lambda b, r: (q0 + b * rows_total + r, 0)),
        out_shape=jax.ShapeDtypeStruct(att_prev.shape, BF16),
        input_output_aliases={6: 0},
        compiler_params=_cparams(("arbitrary", "arbitrary")), name="attention_latent",
    )(zb, kl, vl, ck, cv, toep, att_prev)


CONV_PAD = 16
CONV_ROWS = 64


def _conv_kernel(seq, aliased, *refs):
    if aliased:
        a_ref, gt_ref, w_ref, cb_ref, lg_ref, lb_ref, _alias, o_ref, z_s, sh_s = refs
    else:
        a_ref, gt_ref, w_ref, cb_ref, lg_ref, lb_ref, o_ref, z_s, sh_s = refs
    z_s[0:CONV_PAD, :] = jnp.zeros((CONV_PAD, CONV_CH), F32)
    z_s[CONV_PAD + seq:2 * CONV_PAD + seq, :] = jnp.zeros((CONV_PAD, CONV_CH), F32)
    z_s[CONV_PAD:CONV_PAD + seq, :] = a_ref[...].astype(F32) * _sigmoid(gt_ref[...].astype(F32))
    ext = seq + 2 * CONV_PAD - SUBLANES
    for b in range(1, SUBLANES):
        for r0 in range(0, ext, CONV_ROWS):
            n = min(CONV_ROWS, ext - r0)
            sh_s[b - 1, r0:r0 + n, :] = z_s[r0 + b:r0 + b + n, :]
    off = CONV_PAD - CONV_K // 2
    for t0 in range(0, seq, CONV_ROWS):
        acc = jnp.zeros((CONV_ROWS, CONV_CH), F32)
        for k in range(CONV_K):
            a, b = divmod(off + k, SUBLANES)
            r0 = t0 + a * SUBLANES
            tap = z_s[r0:r0 + CONV_ROWS, :] if b == 0 else sh_s[b - 1, r0:r0 + CONV_ROWS, :]
            acc = acc + tap * w_ref[k:k + 1, :]
        y = _layer_norm(acc + cb_ref[...], lg_ref[...], lb_ref[...])
        o_ref[t0:t0 + CONV_ROWS, :] = _silu(y).astype(BF16)


def _conv_branch(zb, cv_w, cv_b, ln_g, ln_b, *, seq, nb, tok0, t_total, prev=None):
    b0 = tok0 // seq
    in_specs = [
        pl.BlockSpec((None, seq, CHUNK_W), lambda b: (4, b0 + b, 0)),
        pl.BlockSpec((None, seq, CHUNK_W), lambda b: (5, b0 + b, 0)),
        pl.BlockSpec((CONV_K, CONV_CH), lambda b: (0, 0)),
        pl.BlockSpec((1, CONV_CH), lambda b: (0, 0)),
        pl.BlockSpec((1, CONV_CH), lambda b: (0, 0)),
        pl.BlockSpec((1, CONV_CH), lambda b: (0, 0)),
    ]
    args = [zb, zb, cv_w, cv_b, ln_g, ln_b]
    aliases = {}
    if prev is not None:
        in_specs.append(pl.BlockSpec(memory_space=pl.ANY))
        args.append(prev)
        aliases = {6: 0}
    return pl.pallas_call(
        functools.partial(_conv_kernel, seq, prev is not None),
        grid=(nb,), in_specs=in_specs,
        out_specs=pl.BlockSpec((seq, CONV_CH), lambda b: (b0 + b, 0)),
        out_shape=jax.ShapeDtypeStruct((t_total, CONV_CH), BF16),
        scratch_shapes=[pltpu.VMEM((seq + 2 * CONV_PAD, CONV_CH), F32),
                        pltpu.VMEM((SUBLANES - 1, seq + 2 * CONV_PAD - SUBLANES, CONV_CH), F32)],
        input_output_aliases=aliases,
        compiler_params=_cparams(("arbitrary",)), name="conv_module_%d" % seq,
    )(*args)


def _merge_kernel(alpha, n_ctx_tiles, with_router, *refs):
    if with_router:
        (ya_ref, yb_ref, yc_ref, mg_ref, xc_ref, xl_ref, g1_ref, sc2_ref, sh2_ref, wa_ref, wb_ref, wc_ref, wo_ref,
         lg_ref, lb_ref, wr_ref, x1_ref, h2_ref, gates_ref) = refs
    else:
        (ya_ref, yb_ref, yc_ref, mg_ref, xc_ref, xl_ref, g1_ref, sc2_ref, sh2_ref, wa_ref, wb_ref, wc_ref, wo_ref,
         lg_ref, lb_ref, x1_ref, h2_ref) = refs
    x = jnp.where(pl.program_id(0) < n_ctx_tiles, xc_ref[...], xl_ref[...])
    halves = []
    for n in range(2):
        m = None
        for j, (y_ref, w_ref) in enumerate(((ya_ref, wa_ref), (yb_ref, wb_ref), (yc_ref, wc_ref))):
            proj = _dot(y_ref[...], w_ref[:, n * CHUNK_W:(n + 1) * CHUNK_W])
            term = _sigmoid(mg_ref[2 * j + n].astype(F32)) * proj
            m = term if m is None else m + term
        halves.append(m.astype(BF16))
    mix = _dot(halves[0], wo_ref[0:CHUNK_W, :]) + _dot(halves[1], wo_ref[CHUNK_W:2 * CHUNK_W, :])
    x1 = _layer_norm(alpha * x + g1_ref[...] * mix, lg_ref[...], lb_ref[...])
    x1_ref[...] = x1
    h2 = x1 * (1.0 + sc2_ref[...]) + sh2_ref[...]
    h2_ref[...] = h2.astype(BF16)
    if with_router:
        w_hi, w_lo = _split(wr_ref[...])
        h_hi, h_lo = _split(h2)
        lg = _dot_nt(w_hi, h_hi) + _dot_nt(w_hi, h_lo) + _dot_nt(w_lo, h_hi)
        eid = lax.broadcasted_iota(jnp.int32, lg.shape, 0)
        m1 = jnp.max(lg, axis=0, keepdims=True)
        i1 = jnp.min(jnp.where(lg == m1, eid, N_EXPERTS), axis=0, keepdims=True)
        lg2 = jnp.where(eid == i1, -jnp.inf, lg)
        m2 = jnp.max(lg2, axis=0, keepdims=True)
        i2 = jnp.min(jnp.where(lg2 == m2, eid, N_EXPERTS), axis=0, keepdims=True)
        e2 = jnp.exp(m2 - m1)
        w1 = 1.0 / (1.0 + e2)
        gates_ref[...] = jnp.where(eid == i1, w1, 0.0) + jnp.where(eid == i2, e2 * w1, 0.0)


def _merge(ya, yb, yc, zb, x_pair, mod_l, wa, wb, wc, wo, ln_g, ln_b, alpha, tm, row_of_tile, w_router=None):
    xc, xl = x_pair
    d = xc.shape[1]
    t = xc.shape[0] + xl.shape[0]
    n_ctx_tiles = xc.shape[0] // tm
    full = lambda shape: pl.BlockSpec(shape, lambda i: tuple(0 for _ in shape), pipeline_mode=pl.Buffered(1))
    in_specs = [
        pl.BlockSpec((tm, CHUNK_W), lambda i: (i, 0)),
        pl.BlockSpec((tm, CHUNK_W), lambda i: (i, 0)),
        pl.BlockSpec((tm, CHUNK_W), lambda i: (i, 0)),
        pl.BlockSpec((6, tm, CHUNK_W), lambda i: (1, i, 0)),
    ] + _pair_specs(tm, d, n_ctx_tiles) + [
        _mod_spec(2, row_of_tile), _mod_spec(4, row_of_tile), _mod_spec(3, row_of_tile),
        full(wa.shape), full(wb.shape), full(wc.shape), full(wo.shape), full((1, d)), full((1, d)),
    ]
    args = [ya, yb, yc, zb, xc, xl, mod_l, mod_l, mod_l, wa, wb, wc, wo, ln_g, ln_b]
    out_specs = [pl.BlockSpec((tm, d), lambda i: (i, 0)), pl.BlockSpec((tm, d), lambda i: (i, 0))]
    out_shape = [jax.ShapeDtypeStruct((t, d), F32), jax.ShapeDtypeStruct((t, d), BF16)]
    if w_router is not None:
        in_specs.append(full(w_router.shape))
        args.append(w_router)
        out_specs.append(pl.BlockSpec((N_EXPERTS, tm), lambda i: (0, i)))
        out_shape.append(jax.ShapeDtypeStruct((N_EXPERTS, t), F32))
    return pl.pallas_call(
        functools.partial(_merge_kernel, alpha, n_ctx_tiles, w_router is not None),
        grid=(t // tm,), in_specs=in_specs, out_specs=out_specs, out_shape=out_shape,
        compiler_params=_cparams(("arbitrary",)), name="merge",
    )(*args)


def _ffn_kernel(alpha, n_ctx_tiles, n_split, h_ref, x1_ref, g2_ref, wg_ref, wu_ref, wd_ref, lg_ref, lb_ref,
                oc_ref, ol_ref):
    h = h_ref[...]
    tf = wg_ref.shape[1] // n_split
    y = None
    for f in range(n_split):
        cols = slice(f * tf, (f + 1) * tf)
        act = _silu(_dot(h, wg_ref[:, cols])) * _dot(h, wu_ref[:, cols])
        part = _dot(act.astype(BF16), wd_ref[cols, :])
        y = part if y is None else y + part
    out = _layer_norm(alpha * x1_ref[...] + g2_ref[...] * y, lg_ref[...], lb_ref[...])
    is_ctx = pl.program_id(0) < n_ctx_tiles

    @pl.when(is_ctx)
    def _():
        oc_ref[...] = out

    @pl.when(jnp.logical_not(is_ctx))
    def _():
        ol_ref[...] = out


def _dense_mixer(h2, x1, mod_l, wg, wu, wd, ln_g, ln_b, alpha, tm, n_split, row_of_tile, t_ctx):
    t, d = x1.shape
    n_ctx_tiles = t_ctx // tm
    const = lambda shape: pl.BlockSpec(shape, lambda i: tuple(0 for _ in shape), pipeline_mode=pl.Buffered(1))
    return pl.pallas_call(
        functools.partial(_ffn_kernel, alpha, n_ctx_tiles, n_split),
        grid=(t // tm,),
        in_specs=[
            pl.BlockSpec((tm, d), lambda i: (i, 0)),
            pl.BlockSpec((tm, d), lambda i: (i, 0)),
            _mod_spec(5, row_of_tile),
            const(wg.shape), const(wu.shape), const(wd.shape), const((1, d)), const((1, d)),
        ],
        out_specs=_pair_specs(tm, d, n_ctx_tiles),
        out_shape=[jax.ShapeDtypeStruct((t_ctx, d), F32), jax.ShapeDtypeStruct((t - t_ctx, d), F32)],
        compiler_params=_cparams(("arbitrary",)), name="dense_mixer",
    )(h2, x1, mod_l, wg, wu, wd, ln_g, ln_b)


MOE_TILE = 256
MOE_TF = 1792


def _route_kernel(g_ref, rank_ref, gate_ref, before_ref, run_s):
    w = pl.program_id(0)

    @pl.when(w == 0)
    def _():
        run_s[...] = jnp.zeros_like(run_s)

    g = g_ref[...]
    sel = g > 0.0
    row = lax.broadcasted_iota(jnp.int32, (MOE_TILE, MOE_TILE), 0)
    col = lax.broadcasted_iota(jnp.int32, (MOE_TILE, MOE_TILE), 1)
    earlier = (row < col).astype(BF16)
    ones = jnp.where(sel, 1.0, 0.0)
    rank = _dot(ones.astype(BF16), earlier) + run_s[:, 0:1]
    rank_ref[...] = jnp.where(sel, rank, -1.0)
    gate_ref[...] = g
    before_ref[...] = run_s[...]
    run_s[...] += jnp.sum(ones, axis=1, keepdims=True)


def _route(gates_t):
    t = gates_t.shape[1]
    nw = t // MOE_TILE
    return pl.pallas_call(
        _route_kernel,
        grid=(nw,),
        in_specs=[pl.BlockSpec((N_EXPERTS, MOE_TILE), lambda w: (0, w))],
        out_specs=[
            pl.BlockSpec((None, N_EXPERTS, MOE_TILE), lambda w: (w, 0, 0)),
            pl.BlockSpec((None, N_EXPERTS, MOE_TILE), lambda w: (w, 0, 0)),
            pl.BlockSpec((None, N_EXPERTS, LANES), lambda w: (w, 0, 0)),
        ],
        out_shape=[
            jax.ShapeDtypeStruct((nw, N_EXPERTS, MOE_TILE), F32),
            jax.ShapeDtypeStruct((nw, N_EXPERTS, MOE_TILE), F32),
            jax.ShapeDtypeStruct((nw, N_EXPERTS, LANES), F32),
        ],
        scratch_shapes=[pltpu.VMEM((N_EXPERTS, LANES), F32)],
        compiler_params=_cparams(("arbitrary",)), name="moe_route",
    )(gates_t)


def _tile_onehot(rank_ref, w, e, base):
    rank_row = rank_ref[w, pl.ds(e, 1), :]
    rows = lax.broadcasted_iota(jnp.int32, (MOE_TILE, MOE_TILE), 0).astype(F32) + base.astype(F32)
    return rank_row == rows


def _gather_kernel(te_ref, base_ref, wlo_ref, whi_ref, rank_ref, x_ref, o_ref):
    j = pl.program_id(0)
    e, base = te_ref[j], base_ref[j]
    o_ref[...] = jnp.zeros_like(o_ref)

    def window(w, carry):
        p = jnp.where(_tile_onehot(rank_ref, w, e, base), 1.0, 0.0).astype(BF16)
        x_w = x_ref[pl.ds(pl.multiple_of(w * MOE_TILE, MOE_TILE), MOE_TILE), :]
        o_ref[...] += _dot(p, x_w).astype(BF16)
        return carry

    lax.fori_loop(wlo_ref[j], whi_ref[j] + 1, window, 0)


def _moe_gather(meta, rank_t, h2, n_tiles):
    t, d = h2.shape
    nw = t // MOE_TILE
    return pl.pallas_call(
        _gather_kernel,
        grid_spec=pltpu.PrefetchScalarGridSpec(
            num_scalar_prefetch=4, grid=(n_tiles,),
            in_specs=[
                pl.BlockSpec((nw, N_EXPERTS, MOE_TILE), lambda j, *_: (0, 0, 0)),
                pl.BlockSpec((t, d), lambda j, *_: (0, 0), pipeline_mode=pl.Buffered(1)),
            ],
            out_specs=pl.BlockSpec((MOE_TILE, d), lambda j, *_: (j, 0)),
        ),
        out_shape=jax.ShapeDtypeStruct((n_tiles * MOE_TILE, d), BF16),
        compiler_params=_cparams(("arbitrary",)), name="moe_gather",
    )(*meta, rank_t, h2)


def _moe_up_kernel(te_ref, chg_ref, nused_ref, x_ref, wg_ref, wu_ref, o_ref, wg_s, wu_s):
    j = pl.program_id(1)

    @pl.when(chg_ref[j] == 1)
    def _():
        wg_s[...] = wg_ref[...].astype(BF16)
        wu_s[...] = wu_ref[...].astype(BF16)

    @pl.when(j < nused_ref[0])
    def _():
        x = x_ref[...]
        o_ref[...] = (_silu(_dot(x, wg_s[...])) * _dot(x, wu_s[...])).astype(BF16)

    @pl.when(j >= nused_ref[0])
    def _():
        o_ref[...] = jnp.zeros_like(o_ref)


def _moe_up(te, chg, nused, xs, wg, wu):
    rows, d = xs.shape
    n_tiles = rows // MOE_TILE
    ff = wg.shape[2]
    return pl.pallas_call(
        _moe_up_kernel,
        grid_spec=pltpu.PrefetchScalarGridSpec(
            num_scalar_prefetch=3, grid=(ff // MOE_TF, n_tiles),
            in_specs=[
                pl.BlockSpec((MOE_TILE, d), lambda f, j, *_: (j, 0)),
                pl.BlockSpec((None, d, MOE_TF), lambda f, j, te, *_: (te[j], 0, f)),
                pl.BlockSpec((None, d, MOE_TF), lambda f, j, te, *_: (te[j], 0, f)),
            ],
            out_specs=pl.BlockSpec((MOE_TILE, MOE_TF), lambda f, j, *_: (j, f)),
            scratch_shapes=[pltpu.VMEM((d, MOE_TF), BF16), pltpu.VMEM((d, MOE_TF), BF16)],
        ),
        out_shape=jax.ShapeDtypeStruct((rows, ff), BF16),
        compiler_params=_cparams(("arbitrary", "arbitrary")), name="moe_up",
    )(te, chg, nused, xs, wg, wu)


def _moe_down_kernel(te_ref, chg_ref, nused_ref, a_ref, wd_ref, o_ref, wd_s):
    j = pl.program_id(0)

    @pl.when(chg_ref[j] == 1)
    def _():
        wd_s[...] = wd_ref[...].astype(BF16)

    @pl.when(j < nused_ref[0])
    def _():
        o_ref[...] = _dot(a_ref[...], wd_s[...]).astype(BF16)

    @pl.when(j >= nused_ref[0])
    def _():
        o_ref[...] = jnp.zeros_like(o_ref)


def _moe_down(te, chg, nused, act, wd):
    rows, ff = act.shape
    d = wd.shape[2]
    return pl.pallas_call(
        _moe_down_kernel,
        grid_spec=pltpu.PrefetchScalarGridSpec(
            num_scalar_prefetch=3, grid=(rows // MOE_TILE,),
            in_specs=[
                pl.BlockSpec((MOE_TILE, ff), lambda j, *_: (j, 0)),
                pl.BlockSpec((None, ff, d), lambda j, te, *_: (te[j], 0, 0)),
            ],
            out_specs=pl.BlockSpec((MOE_TILE, d), lambda j, *_: (j, 0)),
            scratch_shapes=[pltpu.VMEM((ff, d), BF16)],
        ),
        out_shape=jax.ShapeDtypeStruct((rows, d), BF16),
        compiler_params=_cparams(("arbitrary",)), name="moe_down",
    )(te, chg, nused, act, wd)


def _combine_kernel(alpha, n_tiles, nw_ctx, te_ref, base_ref, wlo_ref, whi_ref, rank_ref, gate_ref, y_ref, x1_ref,
                    g2_ref, lg_ref, lb_ref, oc_ref, ol_ref, acc_s):
    s = pl.program_id(0)

    @pl.when(s == 0)
    def _():
        acc_s[...] = jnp.zeros_like(acc_s)

    @pl.when(s < n_tiles)
    def _():
        e, base = te_ref[s], base_ref[s]

        lane_e = lax.broadcasted_iota(jnp.int32, (MOE_TILE, N_EXPERTS), 1) == e
        row_ids = lax.broadcasted_iota(jnp.int32, (MOE_TILE, MOE_TILE), 1).astype(F32) + base.astype(F32)

        def window(w, carry):
            rank_col = jnp.sum(jnp.where(lane_e, rank_ref[w].T, 0.0), axis=1, keepdims=True)
            gate_col = jnp.sum(jnp.where(lane_e, gate_ref[w].T, 0.0), axis=1, keepdims=True)
            q = jnp.where(rank_col == row_ids, gate_col, 0.0).astype(BF16)
            rows = pl.ds(pl.multiple_of(w * MOE_TILE, MOE_TILE), MOE_TILE)
            acc_s[rows, :] += _dot(q, y_ref[...])
            return carry

        lax.fori_loop(wlo_ref[s], whi_ref[s] + 1, window, 0)

    @pl.when(s >= n_tiles)
    def _():
        w = s - n_tiles
        rows = pl.ds(pl.multiple_of(w * MOE_TILE, MOE_TILE), MOE_TILE)
        out = _layer_norm(alpha * x1_ref[...] + g2_ref[...] * acc_s[rows, :], lg_ref[...], lb_ref[...])

        @pl.when(w < nw_ctx)
        def _():
            oc_ref[...] = out

        @pl.when(w >= nw_ctx)
        def _():
            ol_ref[...] = out


def _moe_combine(meta, rank_t, gate_t, y, x1, mod_l, ln_g, ln_b, alpha, row_of_tile, t_ctx):
    t, d = x1.shape
    nw = t // MOE_TILE
    nw_ctx = t_ctx // MOE_TILE
    n_tiles = y.shape[0] // MOE_TILE
    win = lambda s: jnp.maximum(s - n_tiles, 0)
    return pl.pallas_call(
        functools.partial(_combine_kernel, alpha, n_tiles, nw_ctx),
        grid_spec=pltpu.PrefetchScalarGridSpec(
            num_scalar_prefetch=4, grid=(n_tiles + nw,),
            in_specs=[
                pl.BlockSpec((nw, N_EXPERTS, MOE_TILE), lambda s, *_: (0, 0, 0)),
                pl.BlockSpec((nw, N_EXPERTS, MOE_TILE), lambda s, *_: (0, 0, 0)),
                pl.BlockSpec((MOE_TILE, d), lambda s, *_: (jnp.minimum(s, n_tiles - 1), 0)),
                pl.BlockSpec((MOE_TILE, d), lambda s, *_: (win(s), 0)),
                pl.BlockSpec((None, None, 1, d), lambda s, *_: (row_of_tile(win(s)), 5, 0, 0)),
                pl.BlockSpec((1, d), lambda s, *_: (0, 0)),
                pl.BlockSpec((1, d), lambda s, *_: (0, 0)),
            ],
            out_specs=[
                pl.BlockSpec((MOE_TILE, d), lambda s, *_: (jnp.minimum(win(s), nw_ctx - 1), 0)),
                pl.BlockSpec((MOE_TILE, d), lambda s, *_: (jnp.maximum(win(s) - nw_ctx, 0), 0)),
            ],
            scratch_shapes=[pltpu.VMEM((t, d), F32)],
        ),
        out_shape=[jax.ShapeDtypeStruct((t_ctx, d), F32), jax.ShapeDtypeStruct((t - t_ctx, d), F32)],
        compiler_params=_cparams(("arbitrary",)), name="moe_combine",
    )(*meta, rank_t, gate_t, y, x1, mod_l, ln_g, ln_b)


def _moe_layer(gates, h2, x1, mod_l, wg, wu, wd, ln_g, ln_b, alpha, row_of_tile, t_ctx):
    t = h2.shape[0]
    n_tiles = -(-(2 * t + N_EXPERTS * (MOE_TILE - 1)) // MOE_TILE)
    rank_t, gate_t, before = _route(gates)
    before = before[:, :, 0].astype(jnp.int32)
    counts = jnp.sum(gates > 0.0, axis=1).astype(jnp.int32)
    tiles_e = (counts + MOE_TILE - 1) // MOE_TILE
    tile_end = jnp.cumsum(tiles_e)
    n_used = tile_end[-1]
    j = jnp.arange(n_tiles, dtype=jnp.int32)
    te = jnp.minimum(jnp.sum(j[:, None] >= tile_end[None, :], axis=1), N_EXPERTS - 1).astype(jnp.int32)
    base = (j - jnp.take(tile_end - tiles_e, te)) * MOE_TILE
    last = jnp.minimum(base + MOE_TILE - 1, jnp.take(counts, te) - 1)
    before_t = jnp.take(before, te, axis=1)
    used = j < n_used
    wlo = jnp.where(used, jnp.sum(before_t <= base[None, :], axis=0) - 1, 1).astype(jnp.int32)
    whi = jnp.where(used, jnp.sum(before_t <= last[None, :], axis=0) - 1, 0).astype(jnp.int32)
    chg = jnp.concatenate([jnp.ones((1,), jnp.int32), (te[1:] != te[:-1]).astype(jnp.int32)])
    nused = n_used.reshape(1).astype(jnp.int32)
    meta = (te, base.astype(jnp.int32), wlo, whi)
    xs = _moe_gather(meta, rank_t, h2, n_tiles)
    act = _moe_up(te, chg, nused, xs, wg, wu)
    y = _moe_down(te, chg, nused, act, wd)
    return _moe_combine(meta, rank_t, gate_t, y, x1, mod_l, ln_g, ln_b, alpha, row_of_tile, t_ctx)


def _rope_tables(seq):
    half = GLA_DK // 2
    n = half // 2
    t = np.arange(seq)
    inv = ROPE_BASE ** (-np.arange(n, dtype=np.float64) / n)
    cos = np.zeros((seq, GLA_DK), np.float64)
    sin = np.zeros((seq, GLA_DK), np.float64)
    for a, pos in enumerate((t // GRID_W, t % GRID_W)):
        ang = pos[:, None].astype(np.float64) * inv[None, :]
        base = a * half
        cos[:, base:base + n] = np.cos(ang)
        cos[:, base + n:base + half] = np.cos(ang)
        sin[:, base:base + n] = -np.sin(ang)
        sin[:, base + n:base + half] = np.sin(ang)
    tile = lambda m: jnp.asarray(np.tile(m, (1, GLA_HEADS)), F32)
    return tile(cos), tile(sin)


def _chunk_w_in(w):
    bf = lambda a: a.astype(BF16)
    gd = jnp.pad(w[:, :, 1536:1568], ((0, 0), (0, 0), (0, LANES - 2 * GLA_RANK)))
    kv = jnp.stack([w[:, :, 2080:2592], w[:, :, 2592:3104]], axis=1)
    starts = (0, 512, 1024, 1568, 3104, 3616, 4128, 4640, 5152, 5664, 6176, 6688)
    return bf(gd), bf(kv), bf(jnp.stack([w[:, :, s:s + CHUNK_W] for s in starts], axis=1))


def _gate_up_blocks(w_gup):
    qkw = w_gup.shape[-1]
    fwd = jnp.pad(w_gup[:, 0], ((0, 0), (0, 0), (0, qkw)))
    bwd = jnp.pad(w_gup[:, 1], ((0, 0), (0, 0), (qkw, 0)))
    return jnp.pad(jnp.concatenate([fwd, bwd], axis=1), ((0, 0), (0, LANES - 2 * GLA_RANK), (0, 0)))


def kernel(x_prompt, x_sample, cache_na_k, cache_na_v, state_gla, c, c_ctx, w_mod, b_mod, w_in, gla_w_gup, gla_b_g, gla_norm, w_br_gla, na_rpb, w_br_na, cv_w, cv_b, cv_ln_g, cv_ln_b, w_br_cv, w_out, ln_g, ln_b, ffd_w_gate, ffd_w_up, ffd_w_down, moe_w_router, moe_w_gate, moe_w_up, moe_w_down):
    nb_c, l_c, d = x_prompt.shape
    nb_l, l_l, _ = x_sample.shape
    depth = w_mod.shape[0]
    t_ctx, t_lat = nb_c * l_c, nb_l * l_l
    t_all = t_ctx + t_lat
    alpha = (2 * depth) ** 0.25
    tm = 512
    row_of_tile = _row_of_tile_fn(t_ctx, l_l, tm)

    x_pair = (x_prompt.reshape(t_ctx, d), x_sample.reshape(t_lat, d))
    cond8 = jnp.concatenate([c_ctx[None, :], c, jnp.zeros((8 - 1 - nb_l, d), F32)], axis=0)
    mod = _modulation(cond8, w_mod, b_mod).reshape(depth, 8, 6, 1, d)
    rope = _rope_tables(l_l)
    rows_lat = l_l // GRID_W
    kr = min(NA_WIN_R, rows_lat)
    s0_all = state_gla.reshape(nb_l, depth, 2, 2, LANES, GLA_DV)
    w_gd, w_kv, w_chunks = _chunk_w_in(w_in)
    wg_blks = _gate_up_blocks(gla_w_gup)
    toep = _na_bias_tables(na_rpb)
    w_br = [w.astype(BF16) for w in (w_br_gla, w_br_na, w_br_cv, w_out)]

    caches = None
    ss = []
    for l in range(depth):
        mod_l = mod[l]
        gd, kc, vc, kl, vl, zb = _input_projection(x_pair, mod_l, (w_gd[l], w_kv[l], w_chunks[l]), l, depth, l_c,
                                                   tm, row_of_tile, caches)
        caches = (kc, vc)

        wg_blk = wg_blks[l]
        bg = gla_b_g[l].reshape(1, CHUNK_W)
        gn = gla_norm[l].reshape(1, CHUNK_W)
        ya, s_fin = _gla_branch(gd, zb, wg_blk, bg, gn, latent=False, seq=l_c, nb=nb_c, tok0=0, t_total=t_all)
        ya = _gla_branch(gd, zb, wg_blk, bg, gn, latent=True, seq=l_l, nb=nb_l, tok0=t_ctx, t_total=t_all,
                         rope=rope, s0=s0_all[:, l], ya_prev=ya)

        yb = _attn_context(zb, kc, vc, l, nb_c, l_c, t_all)
        yb = _attn_latent(zb, kl, vl, cache_na_k, cache_na_v, l, toep, yb, nb_l, l_l, t_ctx)

        conv_args = (zb, cv_w[l], cv_b[l][None, :], cv_ln_g[l][None, :], cv_ln_b[l][None, :])
        yc = _conv_branch(*conv_args, seq=l_c, nb=nb_c, tok0=0, t_total=t_all)
        yc = _conv_branch(*conv_args, seq=l_l, nb=nb_l, tok0=t_ctx, t_total=t_all, prev=yc)

        moe = l % 2 == 1
        w_router = None
        if moe:
            w_router = moe_w_router[l // 2].T
        merged = _merge(ya, yb, yc, zb, x_pair, mod_l, w_br[0][l], w_br[1][l], w_br[2][l], w_br[3][l],
                        ln_g[l, 0][None, :], ln_b[l, 0][None, :], alpha, tm, row_of_tile, w_router)
        ln2 = (ln_g[l, 1][None, :], ln_b[l, 1][None, :])
        if moe:
            x1, h2, gates = merged
            x_pair = _moe_layer(gates, h2, x1, mod_l, moe_w_gate[l // 2], moe_w_up[l // 2], moe_w_down[l // 2],
                                *ln2, alpha, _row_of_tile_fn(t_ctx, l_l, MOE_TILE), t_ctx)
        else:
            x1, h2 = merged
            i = l // 2
            x_pair = _dense_mixer(h2, x1, mod_l, ffd_w_gate[i].astype(BF16), ffd_w_up[i].astype(BF16),
                                  ffd_w_down[i].astype(BF16), *ln2, alpha, tm, 2, row_of_tile, t_ctx)
        ss.append(s_fin.reshape(nb_c, 2, GLA_HEADS, GLA_DK, GLA_DV))

    y_prompt = x_pair[0].reshape(nb_c, l_c, d)
    y_sample = x_pair[1].reshape(nb_l, l_l, d)
    cache_shape = (nb_c, depth, l_c, NA_HEADS, NA_DH)
    return (y_prompt, y_sample, caches[0].reshape(cache_shape), caches[1].reshape(cache_shape), jnp.stack(ss, 1))


def _row_of_tile_fn(t_ctx, l_lat, tm):
    n_ctx = t_ctx // tm

    def row_of_tile(i):
        return jnp.where(i < n_ctx, 0, 1 + (i - n_ctx) // (l_lat // tm))

    return row_of_tile
```

```python
import functools

import numpy as np
import jax
import jax.numpy as jnp
from jax import lax
from jax.experimental import pallas as pl
from jax.experimental.pallas import tpu as pltpu

F32 = jnp.float32
BF16 = jnp.bfloat16

GRID_W = 64
GLA_HEADS = 4
GLA_DK = 64
GLA_DV = 128
GLA_RANK = 16
GLA_TAU = 16.0
GLA_CHUNK = 64
GLA_GROUP = 256
NA_HEADS = 8
NA_DH = 64
NA_WIN_R = 8
NA_WIN_C = 16
CONV_CH = 512
CONV_K = 31
N_EXPERTS = 8
ROPE_BASE = 10000.0
LN_EPS = 1e-5

D_MODEL = 1024
LANES = 128
SUBLANES = 8
CHUNK_W = 512
N_BF16_CHUNKS = 12
NEG_BIG = -1e30
VMEM_LIMIT = 56 * 1024 * 1024


def _cparams(sem, vmem=VMEM_LIMIT):
    return pltpu.CompilerParams(dimension_semantics=sem, vmem_limit_bytes=vmem)


def _dot(a, b):
    return jnp.dot(a, b, preferred_element_type=F32)


def _split(x):
    hi = x.astype(BF16)
    return hi, (x - hi.astype(F32)).astype(BF16)


def _dot_split(a, b):
    a_hi, a_lo = _split(a)
    b_hi, b_lo = _split(b)
    return _dot(a_hi, b_hi) + _dot(a_lo, b_hi) + _dot(a_hi, b_lo)


def _dot_split_rhs(a, b):
    b_hi, b_lo = _split(b)
    return _dot(a, b_hi) + _dot(a, b_lo)


def _dot_nt(a, b):
    return lax.dot_general(a, b, (((1,), (1,)), ((), ())), preferred_element_type=F32)


def _sigmoid(x):
    return 0.5 * jnp.tanh(0.5 * x) + 0.5


def _silu(x):
    return x * _sigmoid(x)


def _layer_norm(x, g, b):
    mu = jnp.mean(x, axis=-1, keepdims=True)
    xc = x - mu
    var = jnp.mean(xc * xc, axis=-1, keepdims=True)
    return xc * lax.rsqrt(var + LN_EPS) * g + b


def _mod_kernel(c_ref, w_ref, b_ref, o_ref):
    c = c_ref[...]
    s = _silu(c).astype(BF16)
    o_ref[...] = _dot(s, w_ref[...].astype(BF16)) + b_ref[...]


def _modulation(cond8, w_mod, b_mod):
    depth, d, n = w_mod.shape
    tn = 1024
    return pl.pallas_call(
        _mod_kernel,
        grid=(depth, n // tn),
        in_specs=[
            pl.BlockSpec((8, d), lambda l, j: (0, 0)),
            pl.BlockSpec((None, d, tn), lambda l, j: (l, 0, j)),
            pl.BlockSpec((None, 1, tn), lambda l, j: (l, 0, j)),
        ],
        out_specs=pl.BlockSpec((None, 8, tn), lambda l, j: (l, 0, j)),
        out_shape=jax.ShapeDtypeStruct((depth, 8, n), F32),
        compiler_params=_cparams(("arbitrary", "arbitrary")),
        name="modulation",
    )(cond8, w_mod, b_mod.reshape(depth, 1, n))


def _inproj_kernel(n_ctx_tiles, aliased, *refs):
    if aliased:
        (xc_ref, xl_ref, sc_ref, sh_ref, wgd_ref, wkv_ref, w_ref, _kc_in, _vc_in,
         gd_ref, kc_ref, vc_ref, kl_ref, vl_ref, zb_ref, h_ref) = refs
    else:
        (xc_ref, xl_ref, sc_ref, sh_ref, wgd_ref, wkv_ref, w_ref,
         gd_ref, kc_ref, vc_ref, kl_ref, vl_ref, zb_ref, h_ref) = refs
    is_ctx = pl.program_id(0) < n_ctx_tiles
    x = jnp.where(is_ctx, xc_ref[...], xl_ref[...])
    h_ref[...] = (x * (1.0 + sc_ref[...]) + sh_ref[...]).astype(BF16)
    gd_ref[...] = _dot(h_ref[...], wgd_ref[...])
    k = _dot(h_ref[...], wkv_ref[0])
    v = _dot(h_ref[...], wkv_ref[1])
    seqs, l_c, _ = kc_ref.shape

    @pl.when(is_ctx)
    def _():
        for s in range(seqs):
            kc_ref[s] = k[s * l_c:(s + 1) * l_c]
            vc_ref[s] = v[s * l_c:(s + 1) * l_c]

    @pl.when(jnp.logical_not(is_ctx))
    def _():
        kl_ref[...] = k
        vl_ref[...] = v

    for j in range(N_BF16_CHUNKS):
        zb_ref[j] = _dot(h_ref[...], w_ref[j]).astype(BF16)


def _mod_spec(piece, row_of_tile):
    return pl.BlockSpec((None, None, 1, D_MODEL), lambda i: (row_of_tile(i), piece, 0, 0))


def _pair_specs(tm, d, n_ctx_tiles):
    return [pl.BlockSpec((tm, d), lambda i, *_: (jnp.minimum(i, n_ctx_tiles - 1), 0)),
            pl.BlockSpec((tm, d), lambda i, *_: (jnp.maximum(i - n_ctx_tiles, 0), 0))]


def _input_projection(x_pair, mod_l, w_parts, layer, depth, l_c, tm, row_of_tile, caches=None):
    xc, xl = x_pair
    w_gd, w_kv, w_chunks = w_parts
    d = xc.shape[1]
    t_ctx, t_lat = xc.shape[0], xl.shape[0]
    t = t_ctx + t_lat
    n_ctx_tiles = t_ctx // tm
    seqs = tm // l_c
    cache_spec = pl.BlockSpec((seqs, None, l_c, CHUNK_W),
                              lambda i: (jnp.minimum(i, n_ctx_tiles - 1), layer, 0, 0))
    lat_spec = pl.BlockSpec((tm, CHUNK_W), lambda i: (jnp.maximum(i - n_ctx_tiles, 0), 0))
    const = lambda shape: pl.BlockSpec(shape, lambda i: tuple(0 for _ in shape), pipeline_mode=pl.Buffered(1))
    in_specs = _pair_specs(tm, d, n_ctx_tiles) + [
        _mod_spec(1, row_of_tile), _mod_spec(0, row_of_tile),
        const(w_gd.shape), const(w_kv.shape), const(w_chunks.shape),
    ]
    args = [xc, xl, mod_l, mod_l, w_gd, w_kv, w_chunks]
    aliases = {}
    if caches is not None:
        in_specs += [pl.BlockSpec(memory_space=pl.ANY), pl.BlockSpec(memory_space=pl.ANY)]
        args += list(caches)
        aliases = {7: 1, 8: 2}
    cache_shape = jax.ShapeDtypeStruct((t_ctx // l_c, depth, l_c, CHUNK_W), F32)
    return pl.pallas_call(
        functools.partial(_inproj_kernel, n_ctx_tiles, caches is not None),
        grid=(t // tm,),
        in_specs=in_specs,
        out_specs=[
            pl.BlockSpec((tm, LANES), lambda i: (i, 0)),
            cache_spec, cache_spec, lat_spec, lat_spec,
            pl.BlockSpec((N_BF16_CHUNKS, tm, CHUNK_W), lambda i: (0, i, 0)),
        ],
        out_shape=[
            jax.ShapeDtypeStruct((t, LANES), F32),
            cache_shape, cache_shape,
            jax.ShapeDtypeStruct((t_lat, CHUNK_W), F32), jax.ShapeDtypeStruct((t_lat, CHUNK_W), F32),
            jax.ShapeDtypeStruct((N_BF16_CHUNKS, t, CHUNK_W), BF16),
        ],
        scratch_shapes=[pltpu.VMEM((tm, d), BF16)],
        input_output_aliases=aliases,
        compiler_params=_cparams(("arbitrary",)),
        name="input_projection",
    )(*args)


def _gla_kernel(latent, seq, *refs):
    if latent:
        (qk_ref, v_ref, ra_ref, gd_ref, wg_ref, bg_ref, gn_ref, cos_ref, sin_ref, s0_ref, _alias,
         ya_ref, q_s, k_s, g_s, acc_s, st_s) = refs
    else:
        (qk_ref, v_ref, ra_ref, gd_ref, wg_ref, bg_ref, gn_ref,
         ya_ref, sfin_ref, q_s, k_s, g_s, acc_s, st_s) = refs
    c = GLA_CHUNK
    n_chunks = seq // c
    qkw = GLA_HEADS * GLA_DK

    q = qk_ref[:, 0:qkw].astype(F32) * (GLA_DK ** -0.5)
    k = qk_ref[:, qkw:2 * qkw].astype(F32)
    if latent:
        lane = lax.broadcasted_iota(jnp.int32, (1, qkw), 1)
        first_half = (lane % 32) < 16

        def rope(x):
            swapped = jnp.where(first_half, pltpu.roll(x, qkw - 16, 1), pltpu.roll(x, 16, 1))
            return x * cos_ref[...] + swapped * sin_ref[...]

        q, k = rope(q), rope(k)
    q_s[...] = q
    k_s[...] = k

    glog = _dot_split(gd_ref[...], wg_ref[...]) + bg_ref[...]
    g_s[...] = (jnp.minimum(glog, 0.0) - jnp.log(1.0 + jnp.exp(-jnp.abs(glog)))) * (1.0 / GLA_TAU)
    acc_s[...] = jnp.zeros_like(acc_s)
    if latent:
        st_s[...] = s0_ref[...]
    else:
        st_s[...] = jnp.zeros_like(st_s)

    grp = GLA_GROUP
    cpg = grp // c
    n_groups = seq // grp
    grow = lax.broadcasted_iota(jnp.int32, (grp, grp), 0)
    gcol = lax.broadcasted_iota(jnp.int32, (grp, grp), 1)
    same_chunk = (grow // c) == (gcol // c)
    keep = (same_chunk & (gcol <= grow), same_chunk & (gcol >= grow))
    tri = (keep[0].astype(BF16), keep[1].astype(BF16))
    lane128 = lax.broadcasted_iota(jnp.int32, (1, LANES), 1)
    head_mask = (lane128 < GLA_DK, lane128 >= GLA_DK)
    urow = lax.broadcasted_iota(jnp.int32, (cpg * LANES, GLA_DV), 0)
    srow = lax.broadcasted_iota(jnp.int32, (LANES, GLA_DV), 0)
    chunk_of_col = lax.broadcasted_iota(jnp.int32, (LANES, grp), 1) // c
    chunk_of_row = lax.broadcasted_iota(jnp.int32, (grp, LANES), 0) // c
    pad_rows = jnp.zeros((SUBLANES - cpg, qkw), F32)

    def group_step(i, carry):
        for d in range(2):
            gi = i if d == 0 else n_groups - 1 - i
            rows = pl.ds(pl.multiple_of(gi * grp, grp), grp)
            cum = _dot_split_rhs(tri[d], g_s[rows, d * qkw:(d + 1) * qkw])
            edge = c - 1 if d == 0 else 0
            lasts = [cum[ci * c + edge:ci * c + edge + 1, :] for ci in range(cpg)]
            last = jnp.concatenate([jnp.broadcast_to(r, (c, qkw)) for r in lasts], axis=0)
            ref = 0.5 * last
            q = q_s[rows, :]
            k = k_s[rows, :]
            qe = q * jnp.exp(cum - ref)
            ke = k * jnp.exp(ref - cum)
            qe2 = q * jnp.exp(cum)
            ke2 = k * jnp.exp(last - cum)
            last8 = jnp.concatenate(lasts + [pad_rows], axis=0)
            for p in range(2):
                sl = slice(p * LANES, (p + 1) * LANES)
                ke_p = ke[:, sl].astype(BF16)
                ke2_t = ke2[:, sl].T
                ke2_blk = jnp.concatenate(
                    [jnp.where(chunk_of_col == ci, ke2_t, 0.0) for ci in range(cpg)], axis=0).astype(BF16)
                dec = jnp.exp(last8[:, sl]).T
                psl = slice(2 * p * GLA_DV, (2 * p + 2) * GLA_DV)
                v_pair = v_ref[rows, psl]
                u = _dot(ke2_blk, v_pair)
                upd = jnp.where(urow % LANES < GLA_DK, u[:, :GLA_DV], u[:, GLA_DV:])
                s = st_s[d, p]
                s_in = [None] * cpg
                for ci in (range(cpg) if d == 0 else range(cpg - 1, -1, -1)):
                    s_in[ci] = jnp.concatenate(
                        [jnp.where(srow < GLA_DK, s, 0.0), jnp.where(srow >= GLA_DK, s, 0.0)], axis=1).astype(BF16)
                    s = dec[:, ci:ci + 1] * s + upd[ci * LANES:(ci + 1) * LANES]
                st_s[d, p] = s
                s_stack = jnp.concatenate(s_in, axis=0)
                a2_blk = jnp.concatenate(
                    [jnp.where(chunk_of_row == ci, qe2[:, sl], 0.0) for ci in range(cpg)], axis=1).astype(BF16)
                intra = []
                for hh in range(2):
                    a = jnp.where(head_mask[hh], qe[:, sl], 0.0).astype(BF16)
                    att = jnp.where(keep[d], _dot_nt(a, ke_p), 0.0).astype(BF16)
                    intra.append(_dot(att, v_pair[:, hh * GLA_DV:(hh + 1) * GLA_DV]))
                acc_s[rows, psl] += jnp.concatenate(intra, axis=1) + _dot(a2_blk, s_stack)
        return carry

    lax.fori_loop(0, n_groups, group_step, 0)

    def epilogue(i, carry):
        rows = pl.ds(pl.multiple_of(i * c, c), c)
        for h in range(GLA_HEADS):
            vsl = slice(h * GLA_DV, (h + 1) * GLA_DV)
            o = acc_s[rows, vsl]
            o = o * lax.rsqrt(jnp.mean(o * o, axis=-1, keepdims=True) + LN_EPS) * gn_ref[:, vsl]
            ya_ref[rows, vsl] = (o * _silu(ra_ref[rows, vsl].astype(F32))).astype(BF16)
        return carry

    lax.fori_loop(0, n_chunks, epilogue, 0)
    if not latent:
        sfin_ref[...] = st_s[...]


def _gla_branch(gd, zb, wg_blk, bg, gn, *, latent, seq, nb, tok0, t_total, rope=None, s0=None, ya_prev=None):
    b0 = tok0 // seq
    qkw = GLA_HEADS * GLA_DK
    in_specs = [
        pl.BlockSpec((None, seq, CHUNK_W), lambda b: (0, b0 + b, 0)),
        pl.BlockSpec((None, seq, CHUNK_W), lambda b: (1, b0 + b, 0)),
        pl.BlockSpec((None, seq, CHUNK_W), lambda b: (2, b0 + b, 0)),
        pl.BlockSpec((seq, LANES), lambda b: (b0 + b, 0)),
        pl.BlockSpec((LANES, CHUNK_W), lambda b: (0, 0)),
        pl.BlockSpec((1, CHUNK_W), lambda b: (0, 0)),
        pl.BlockSpec((1, CHUNK_W), lambda b: (0, 0)),
    ]
    args = [zb, zb, zb, gd, wg_blk, bg, gn]
    scratch = [
        pltpu.VMEM((seq, qkw), F32), pltpu.VMEM((seq, qkw), F32), pltpu.VMEM((seq, CHUNK_W), F32),
        pltpu.VMEM((seq, CHUNK_W), F32), pltpu.VMEM((2, 2, LANES, GLA_DV), F32),
    ]
    ya_shape = jax.ShapeDtypeStruct((t_total, CHUNK_W), BF16)
    ya_spec = pl.BlockSpec((seq, CHUNK_W), lambda b: (b0 + b, 0))
    if latent:
        cos_t, sin_t = rope
        in_specs += [
            pl.BlockSpec((seq, qkw), lambda b: (0, 0)),
            pl.BlockSpec((seq, qkw), lambda b: (0, 0)),
            pl.BlockSpec((None, 2, 2, LANES, GLA_DV), lambda b: (b, 0, 0, 0, 0)),
            pl.BlockSpec(memory_space=pl.ANY),
        ]
        args += [cos_t, sin_t, s0, ya_prev]
        return pl.pallas_call(
            functools.partial(_gla_kernel, True, seq),
            grid=(nb,), in_specs=in_specs, out_specs=ya_spec, out_shape=ya_shape,
            scratch_shapes=scratch, input_output_aliases={len(args) - 1: 0},
            compiler_params=_cparams(("arbitrary",)), name="gla_latent",
        )(*args)
    return pl.pallas_call(
        functools.partial(_gla_kernel, False, seq),
        grid=(nb,), in_specs=in_specs,
        out_specs=[ya_spec, pl.BlockSpec((None, 2, 2, LANES, GLA_DV), lambda b: (b, 0, 0, 0, 0))],
        out_shape=[ya_shape, jax.ShapeDtypeStruct((nb, 2, 2, LANES, GLA_DV), F32)],
        scratch_shapes=scratch,
        compiler_params=_cparams(("arbitrary",)), name="gla_context",
    )(*args)


def _attn_ctx_kernel(q_ref, k_ref, v_ref, o_ref):
    lane = lax.broadcasted_iota(jnp.int32, (1, LANES), 1)
    masks = (lane < NA_DH, lane >= NA_DH)
    scale = NA_DH ** -0.5
    seq = q_ref.shape[0]
    for p in range(NA_HEADS // 2):
        sl = slice(p * LANES, (p + 1) * LANES)
        q2 = _stack_heads(q_ref[:, sl], masks)
        kp = k_ref[:, sl].astype(BF16)
        vp = v_ref[:, sl].astype(BF16)
        s = _dot_nt(q2, kp) * scale
        e = jnp.exp(s - jnp.max(s, axis=-1, keepdims=True))
        o2 = _dot(e.astype(BF16), vp) / jnp.sum(e, axis=-1, keepdims=True)
        o_ref[:, sl] = jnp.where(masks[0], o2[0:seq], o2[seq:2 * seq]).astype(BF16)


def _stack_heads(qp, masks):
    zero = jnp.zeros_like(qp)
    return jnp.concatenate([jnp.where(masks[0], qp, zero), jnp.where(masks[1], qp, zero)], axis=0)


def _attn_context(zb, kc, vc, layer, nb, seq, t_total):
    return pl.pallas_call(
        _attn_ctx_kernel,
        grid=(nb,),
        in_specs=[
            pl.BlockSpec((None, seq, CHUNK_W), lambda b: (3, b, 0)),
            pl.BlockSpec((None, None, seq, CHUNK_W), lambda b: (b, layer, 0, 0)),
            pl.BlockSpec((None, None, seq, CHUNK_W), lambda b: (b, layer, 0, 0)),
        ],
        out_specs=pl.BlockSpec((seq, CHUNK_W), lambda b: (b, 0)),
        out_shape=jax.ShapeDtypeStruct((t_total, CHUNK_W), BF16),
        compiler_params=_cparams(("arbitrary",)), name="attention_context",
    )(zb, kc, vc)


def _na_kernel(rows_total, kr, q_ref, kl_ref, vl_ref, kc_ref, vc_ref, toep_ref, _alias, o_ref):
    r = pl.program_id(1)
    rs = jnp.clip(r - kr // 2, 0, rows_total - kr)
    krows = pl.ds(pl.multiple_of(rs * GRID_W, GRID_W), kr * GRID_W)
    dr0 = rs - r + NA_WIN_R - 1
    lane = lax.broadcasted_iota(jnp.int32, (1, LANES), 1)
    masks = (lane < NA_DH, lane >= NA_DH)
    scale = NA_DH ** -0.5
    nq = q_ref.shape[0]
    for p in range(NA_HEADS // 2):
        sl = slice(p * LANES, (p + 1) * LANES)
        q2 = _stack_heads(q_ref[:, sl], masks)
        klp = kl_ref[krows, sl].astype(BF16)
        vlp = vl_ref[krows, sl].astype(BF16)
        kcp = kc_ref[:, sl].astype(BF16)
        vcp = vc_ref[:, sl].astype(BF16)
        bias = jnp.concatenate(
            [jnp.concatenate([toep_ref[2 * p + hh, dr0 + 2 * m] for m in range(kr // 2)], axis=1)
             for hh in range(2)], axis=0)
        s_loc = _dot_nt(q2, klp) * scale + bias
        s_ctx = _dot_nt(q2, kcp) * scale
        m = jnp.maximum(jnp.max(s_loc, axis=-1, keepdims=True), jnp.max(s_ctx, axis=-1, keepdims=True))
        e_loc = jnp.exp(s_loc - m)
        e_ctx = jnp.exp(s_ctx - m)
        den = jnp.sum(e_loc, axis=-1, keepdims=True) + jnp.sum(e_ctx, axis=-1, keepdims=True)
        o2 = (_dot(e_loc.astype(BF16), vlp) + _dot(e_ctx.astype(BF16), vcp)) / den
        o_ref[:, sl] = jnp.where(masks[0], o2[0:nq], o2[nq:2 * nq]).astype(BF16)


def _na_bias_tables(rpb):
    depth, heads, nr, nc = rpb.shape
    qc = np.arange(GRID_W)[:, None]
    kc = np.arange(GRID_W)[None, :]
    cs = np.clip(qc - NA_WIN_C // 2, 0, GRID_W - NA_WIN_C)
    valid = (kc >= cs) & (kc < cs + NA_WIN_C)
    period = 2 * GRID_W
    lead = GRID_W - 1 - (NA_WIN_C - 1)
    v = jnp.pad(rpb.astype(F32), ((0, 0), (0, 0), (0, 0), (lead, period - lead - nc)))
    skew = jnp.tile(v, (1, 1, 1, GRID_W))[..., :GRID_W * (period - 1)].reshape(depth, heads, nr, GRID_W, period - 1)
    toep = jnp.where(valid, skew[..., GRID_W - 1:], NEG_BIG)
    return jnp.concatenate([toep[:, :, :-1], toep[:, :, 1:]], axis=-1)


def _attn_latent(zb, kl, vl, cache_k, cache_v, layer, toep, att_prev, nb, seq, tok0):
    rows_total = seq // GRID_W
    kr = min(NA_WIN_R, rows_total)
    q0 = tok0 // GRID_W
    past = cache_k.shape[2]
    ck = cache_k.reshape(cache_k.shape[0], cache_k.shape[1], past, NA_HEADS * NA_DH)
    cv = cache_v.reshape(ck.shape)
    return pl.pallas_call(
        functools.partial(_na_kernel, rows_total, kr),
        grid=(nb, rows_total),
        in_specs=[
            pl.BlockSpec((None, GRID_W, CHUNK_W), lambda b, r: (3, q0 + b * rows_total + r, 0)),
            pl.BlockSpec((seq, CHUNK_W), lambda b, r: (b, 0)),
            pl.BlockSpec((seq, CHUNK_W), lambda b, r: (b, 0)),
            pl.BlockSpec((None, None, past, CHUNK_W), lambda b, r: (b, layer, 0, 0)),
            pl.BlockSpec((None, None, past, CHUNK_W), lambda b, r: (b, layer, 0, 0)),
            pl.BlockSpec((None,) + toep.shape[1:], lambda b, r: (layer, 0, 0, 0, 0), pipeline_mode=pl.Buffered(1)),
            pl.BlockSpec(memory_space=pl.ANY),
        ],
        out_specs=pl.BlockSpec((GRID_W, CHUNK_W), lambda b, r: (q0 + b * rows_total + r, 0)),
        out_shape=jax.ShapeDtypeStruct(att_prev.shape, BF16),
        input_output_aliases={6: 0},
        compiler_params=_cparams(("arbitrary", "arbitrary")), name="attention_latent",
    )(zb, kl, vl, ck, cv, toep, att_prev)


CONV_PAD = 16
CONV_ROWS = 64


def _conv_kernel(seq, aliased, *refs):
    if aliased:
        a_ref, gt_ref, w_ref, cb_ref, lg_ref, lb_ref, _alias, o_ref, z_s, sh_s = refs
    else:
        a_ref, gt_ref, w_ref, cb_ref, lg_ref, lb_ref, o_ref, z_s, sh_s = refs
    z_s[0:CONV_PAD, :] = jnp.zeros((CONV_PAD, CONV_CH), F32)
    z_s[CONV_PAD + seq:2 * CONV_PAD + seq, :] = jnp.zeros((CONV_PAD, CONV_CH), F32)
    z_s[CONV_PAD:CONV_PAD + seq, :] = a_ref[...].astype(F32) * _sigmoid(gt_ref[...].astype(F32))
    ext = seq + 2 * CONV_PAD - SUBLANES
    for b in range(1, SUBLANES):
        for r0 in range(0, ext, CONV_ROWS):
            n = min(CONV_ROWS, ext - r0)
            sh_s[b - 1, r0:r0 + n, :] = z_s[r0 + b:r0 + b + n, :]
    off = CONV_PAD - CONV_K // 2
    grp = (CONV_ROWS // SUBLANES, SUBLANES, CONV_CH)
    for t0 in range(0, seq, CONV_ROWS):
        acc = jnp.zeros(grp, F32)
        for k in range(CONV_K):
            a, b = divmod(off + k, SUBLANES)
            r0 = t0 + a * SUBLANES
            tap = z_s[r0:r0 + CONV_ROWS, :] if b == 0 else sh_s[b - 1, r0:r0 + CONV_ROWS, :]
            acc = acc + tap.reshape(grp) * w_ref[k][None]
        y = _layer_norm(acc.reshape(CONV_ROWS, CONV_CH) + cb_ref[...], lg_ref[...], lb_ref[...])
        o_ref[t0:t0 + CONV_ROWS, :] = _silu(y).astype(BF16)


def _conv_branch(zb, cv_w, cv_b, ln_g, ln_b, *, seq, nb, tok0, t_total, prev=None):
    b0 = tok0 // seq
    in_specs = [
        pl.BlockSpec((None, seq, CHUNK_W), lambda b: (4, b0 + b, 0)),
        pl.BlockSpec((None, seq, CHUNK_W), lambda b: (5, b0 + b, 0)),
        pl.BlockSpec((CONV_K, SUBLANES, CONV_CH), lambda b: (0, 0, 0)),
        pl.BlockSpec((1, CONV_CH), lambda b: (0, 0)),
        pl.BlockSpec((1, CONV_CH), lambda b: (0, 0)),
        pl.BlockSpec((1, CONV_CH), lambda b: (0, 0)),
    ]
    args = [zb, zb, jnp.broadcast_to(cv_w[:, None, :], (CONV_K, SUBLANES, CONV_CH)), cv_b, ln_g, ln_b]
    aliases = {}
    if prev is not None:
        in_specs.append(pl.BlockSpec(memory_space=pl.ANY))
        args.append(prev)
        aliases = {6: 0}
    return pl.pallas_call(
        functools.partial(_conv_kernel, seq, prev is not None),
        grid=(nb,), in_specs=in_specs,
        out_specs=pl.BlockSpec((seq, CONV_CH), lambda b: (b0 + b, 0)),
        out_shape=jax.ShapeDtypeStruct((t_total, CONV_CH), BF16),
        scratch_shapes=[pltpu.VMEM((seq + 2 * CONV_PAD, CONV_CH), F32),
                        pltpu.VMEM((SUBLANES - 1, seq + 2 * CONV_PAD - SUBLANES, CONV_CH), F32)],
        input_output_aliases=aliases,
        compiler_params=_cparams(("arbitrary",)), name="conv_module_%d" % seq,
    )(*args)


def _merge_kernel(alpha, n_ctx_tiles, with_router, *refs):
    if with_router:
        (ya_ref, yb_ref, yc_ref, mg_ref, xc_ref, xl_ref, g1_ref, sc2_ref, sh2_ref, wa_ref, wb_ref, wc_ref, wo_ref,
         lg_ref, lb_ref, wr_ref, x1_ref, h2_ref, gates_ref) = refs
    else:
        (ya_ref, yb_ref, yc_ref, mg_ref, xc_ref, xl_ref, g1_ref, sc2_ref, sh2_ref, wa_ref, wb_ref, wc_ref, wo_ref,
         lg_ref, lb_ref, x1_ref, h2_ref) = refs
    x = jnp.where(pl.program_id(0) < n_ctx_tiles, xc_ref[...], xl_ref[...])
    halves = []
    for n in range(2):
        m = None
        for j, (y_ref, w_ref) in enumerate(((ya_ref, wa_ref), (yb_ref, wb_ref), (yc_ref, wc_ref))):
            proj = _dot(y_ref[...], w_ref[:, n * CHUNK_W:(n + 1) * CHUNK_W])
            term = _sigmoid(mg_ref[2 * j + n].astype(F32)) * proj
            m = term if m is None else m + term
        halves.append(m.astype(BF16))
    mix = _dot(halves[0], wo_ref[0:CHUNK_W, :]) + _dot(halves[1], wo_ref[CHUNK_W:2 * CHUNK_W, :])
    x1 = _layer_norm(alpha * x + g1_ref[...] * mix, lg_ref[...], lb_ref[...])
    x1_ref[...] = x1
    h2 = x1 * (1.0 + sc2_ref[...]) + sh2_ref[...]
    h2_ref[...] = h2.astype(BF16)
    if with_router:
        w_hi, w_lo = _split(wr_ref[...])
        h_hi, h_lo = _split(h2)
        lg = _dot_nt(w_hi, h_hi) + _dot_nt(w_hi, h_lo) + _dot_nt(w_lo, h_hi)
        eid = lax.broadcasted_iota(jnp.int32, lg.shape, 0)
        m1 = jnp.max(lg, axis=0, keepdims=True)
        i1 = jnp.min(jnp.where(lg == m1, eid, N_EXPERTS), axis=0, keepdims=True)
        lg2 = jnp.where(eid == i1, -jnp.inf, lg)
        m2 = jnp.max(lg2, axis=0, keepdims=True)
        i2 = jnp.min(jnp.where(lg2 == m2, eid, N_EXPERTS), axis=0, keepdims=True)
        e2 = jnp.exp(m2 - m1)
        w1 = 1.0 / (1.0 + e2)
        gates_ref[...] = jnp.where(eid == i1, w1, 0.0) + jnp.where(eid == i2, e2 * w1, 0.0)


def _merge(ya, yb, yc, zb, x_pair, mod_l, wa, wb, wc, wo, ln_g, ln_b, alpha, tm, row_of_tile, w_router=None):
    xc, xl = x_pair
    d = xc.shape[1]
    t = xc.shape[0] + xl.shape[0]
    n_ctx_tiles = xc.shape[0] // tm
    full = lambda shape: pl.BlockSpec(shape, lambda i: tuple(0 for _ in shape), pipeline_mode=pl.Buffered(1))
    in_specs = [
        pl.BlockSpec((tm, CHUNK_W), lambda i: (i, 0)),
        pl.BlockSpec((tm, CHUNK_W), lambda i: (i, 0)),
        pl.BlockSpec((tm, CHUNK_W), lambda i: (i, 0)),
        pl.BlockSpec((6, tm, CHUNK_W), lambda i: (1, i, 0)),
    ] + _pair_specs(tm, d, n_ctx_tiles) + [
        _mod_spec(2, row_of_tile), _mod_spec(4, row_of_tile), _mod_spec(3, row_of_tile),
        full(wa.shape), full(wb.shape), full(wc.shape), full(wo.shape), full((1, d)), full((1, d)),
    ]
    args = [ya, yb, yc, zb, xc, xl, mod_l, mod_l, mod_l, wa, wb, wc, wo, ln_g, ln_b]
    out_specs = [pl.BlockSpec((tm, d), lambda i: (i, 0)), pl.BlockSpec((tm, d), lambda i: (i, 0))]
    out_shape = [jax.ShapeDtypeStruct((t, d), F32), jax.ShapeDtypeStruct((t, d), BF16)]
    if w_router is not None:
        in_specs.append(full(w_router.shape))
        args.append(w_router)
        out_specs.append(pl.BlockSpec((N_EXPERTS, tm), lambda i: (0, i)))
        out_shape.append(jax.ShapeDtypeStruct((N_EXPERTS, t), F32))
    return pl.pallas_call(
        functools.partial(_merge_kernel, alpha, n_ctx_tiles, w_router is not None),
        grid=(t // tm,), in_specs=in_specs, out_specs=out_specs, out_shape=out_shape,
        compiler_params=_cparams(("arbitrary",)), name="merge",
    )(*args)


def _ffn_kernel(alpha, n_ctx_tiles, n_split, h_ref, x1_ref, g2_ref, wg_ref, wu_ref, wd_ref, lg_ref, lb_ref,
                oc_ref, ol_ref):
    h = h_ref[...]
    tf = wg_ref.shape[1] // n_split
    y = None
    for f in range(n_split):
        cols = slice(f * tf, (f + 1) * tf)
        act = _silu(_dot(h, wg_ref[:, cols])) * _dot(h, wu_ref[:, cols])
        part = _dot(act.astype(BF16), wd_ref[cols, :])
        y = part if y is None else y + part
    out = _layer_norm(alpha * x1_ref[...] + g2_ref[...] * y, lg_ref[...], lb_ref[...])
    is_ctx = pl.program_id(0) < n_ctx_tiles

    @pl.when(is_ctx)
    def _():
        oc_ref[...] = out

    @pl.when(jnp.logical_not(is_ctx))
    def _():
        ol_ref[...] = out


def _dense_mixer(h2, x1, mod_l, wg, wu, wd, ln_g, ln_b, alpha, tm, n_split, row_of_tile, t_ctx):
    t, d = x1.shape
    n_ctx_tiles = t_ctx // tm
    const = lambda shape: pl.BlockSpec(shape, lambda i: tuple(0 for _ in shape), pipeline_mode=pl.Buffered(1))
    return pl.pallas_call(
        functools.partial(_ffn_kernel, alpha, n_ctx_tiles, n_split),
        grid=(t // tm,),
        in_specs=[
            pl.BlockSpec((tm, d), lambda i: (i, 0)),
            pl.BlockSpec((tm, d), lambda i: (i, 0)),
            _mod_spec(5, row_of_tile),
            const(wg.shape), const(wu.shape), const(wd.shape), const((1, d)), const((1, d)),
        ],
        out_specs=_pair_specs(tm, d, n_ctx_tiles),
        out_shape=[jax.ShapeDtypeStruct((t_ctx, d), F32), jax.ShapeDtypeStruct((t - t_ctx, d), F32)],
        compiler_params=_cparams(("arbitrary",)), name="dense_mixer",
    )(h2, x1, mod_l, wg, wu, wd, ln_g, ln_b)


MOE_TILE = 256
MOE_TF = 1792


def _route_kernel(g_ref, rank_ref, gate_ref, before_ref, run_s):
    w = pl.program_id(0)

    @pl.when(w == 0)
    def _():
        run_s[...] = jnp.zeros_like(run_s)

    g = g_ref[...]
    sel = g > 0.0
    row = lax.broadcasted_iota(jnp.int32, (MOE_TILE, MOE_TILE), 0)
    col = lax.broadcasted_iota(jnp.int32, (MOE_TILE, MOE_TILE), 1)
    earlier = (row < col).astype(BF16)
    ones = jnp.where(sel, 1.0, 0.0)
    rank = _dot(ones.astype(BF16), earlier) + run_s[:, 0:1]
    rank_ref[...] = jnp.where(sel, rank, -1.0)
    gate_ref[...] = g
    before_ref[...] = run_s[...]
    run_s[...] += jnp.sum(ones, axis=1, keepdims=True)


def _route(gates_t):
    t = gates_t.shape[1]
    nw = t // MOE_TILE
    return pl.pallas_call(
        _route_kernel,
        grid=(nw,),
        in_specs=[pl.BlockSpec((N_EXPERTS, MOE_TILE), lambda w: (0, w))],
        out_specs=[
            pl.BlockSpec((None, N_EXPERTS, MOE_TILE), lambda w: (w, 0, 0)),
            pl.BlockSpec((None, N_EXPERTS, MOE_TILE), lambda w: (w, 0, 0)),
            pl.BlockSpec((None, N_EXPERTS, LANES), lambda w: (w, 0, 0)),
        ],
        out_shape=[
            jax.ShapeDtypeStruct((nw, N_EXPERTS, MOE_TILE), F32),
            jax.ShapeDtypeStruct((nw, N_EXPERTS, MOE_TILE), F32),
            jax.ShapeDtypeStruct((nw, N_EXPERTS, LANES), F32),
        ],
        scratch_shapes=[pltpu.VMEM((N_EXPERTS, LANES), F32)],
        compiler_params=_cparams(("arbitrary",)), name="moe_route",
    )(gates_t)


def _tile_onehot(rank_ref, w, e, base):
    rank_row = rank_ref[w, pl.ds(e, 1), :]
    rows = lax.broadcasted_iota(jnp.int32, (MOE_TILE, MOE_TILE), 0).astype(F32) + base.astype(F32)
    return rank_row == rows


def _gather_kernel(te_ref, base_ref, wlo_ref, whi_ref, rank_ref, x_ref, o_ref):
    j = pl.program_id(0)
    e, base = te_ref[j], base_ref[j]
    o_ref[...] = jnp.zeros_like(o_ref)

    def window(w, carry):
        p = jnp.where(_tile_onehot(rank_ref, w, e, base), 1.0, 0.0).astype(BF16)
        x_w = x_ref[pl.ds(pl.multiple_of(w * MOE_TILE, MOE_TILE), MOE_TILE), :]
        o_ref[...] += _dot(p, x_w).astype(BF16)
        return carry

    lax.fori_loop(wlo_ref[j], whi_ref[j] + 1, window, 0)


def _moe_gather(meta, rank_t, h2, n_tiles):
    t, d = h2.shape
    nw = t // MOE_TILE
    return pl.pallas_call(
        _gather_kernel,
        grid_spec=pltpu.PrefetchScalarGridSpec(
            num_scalar_prefetch=4, grid=(n_tiles,),
            in_specs=[
                pl.BlockSpec((nw, N_EXPERTS, MOE_TILE), lambda j, *_: (0, 0, 0)),
                pl.BlockSpec((t, d), lambda j, *_: (0, 0), pipeline_mode=pl.Buffered(1)),
            ],
            out_specs=pl.BlockSpec((MOE_TILE, d), lambda j, *_: (j, 0)),
        ),
        out_shape=jax.ShapeDtypeStruct((n_tiles * MOE_TILE, d), BF16),
        compiler_params=_cparams(("arbitrary",)), name="moe_gather",
    )(*meta, rank_t, h2)


F8 = jnp.float8_e4m3fn
F8_TARGET = 224.0
F8_TINY = 1e-30


def _f8_quantize(x):
    m = jnp.max(jnp.max(jnp.abs(x), axis=1, keepdims=True), axis=0, keepdims=True)
    m = jnp.maximum(m, F8_TINY)
    return (x * (F8_TARGET / m)).astype(F8), m * (1.0 / F8_TARGET)


def _moe_up_kernel(te_ref, chg_ref, nused_ref, x_ref, wg_ref, wu_ref, o_ref, wg_s, wu_s, inv_s):
    j = pl.program_id(1)

    @pl.when(chg_ref[j] == 1)
    def _():
        for row, (w_ref, w_s) in enumerate(((wg_ref, wg_s), (wu_ref, wu_s))):
            w_s[...], inv = _f8_quantize(w_ref[...])
            inv_s[row:row + 1, :] = jnp.broadcast_to(inv, (1, LANES))

    @pl.when(j < nused_ref[0])
    def _():
        x8, inv_x = _f8_quantize(x_ref[...].astype(F32))
        a = _dot(x8, wg_s[...]) * (inv_x * inv_s[0:1, 0:1])
        u = _dot(x8, wu_s[...]) * (inv_x * inv_s[1:2, 0:1])
        o_ref[...] = (_silu(a) * u).astype(BF16)

    @pl.when(j >= nused_ref[0])
    def _():
        o_ref[...] = jnp.zeros_like(o_ref)


def _moe_up(te, chg, nused, xs, wg, wu):
    rows, d = xs.shape
    n_tiles = rows // MOE_TILE
    ff = wg.shape[2]
    return pl.pallas_call(
        _moe_up_kernel,
        grid_spec=pltpu.PrefetchScalarGridSpec(
            num_scalar_prefetch=3, grid=(ff // MOE_TF, n_tiles),
            in_specs=[
                pl.BlockSpec((MOE_TILE, d), lambda f, j, *_: (j, 0)),
                pl.BlockSpec((None, d, MOE_TF), lambda f, j, te, *_: (te[j], 0, f)),
                pl.BlockSpec((None, d, MOE_TF), lambda f, j, te, *_: (te[j], 0, f)),
            ],
            out_specs=pl.BlockSpec((MOE_TILE, MOE_TF), lambda f, j, *_: (j, f)),
            scratch_shapes=[pltpu.VMEM((d, MOE_TF), F8), pltpu.VMEM((d, MOE_TF), F8),
                            pltpu.VMEM((SUBLANES, LANES), F32)],
        ),
        out_shape=jax.ShapeDtypeStruct((rows, ff), BF16),
        compiler_params=_cparams(("arbitrary", "arbitrary")), name="moe_up",
    )(te, chg, nused, xs, wg, wu)


def _moe_down_kernel(te_ref, chg_ref, nused_ref, a_ref, wd_ref, o_ref, wd_s, inv_s):
    j = pl.program_id(0)

    @pl.when(chg_ref[j] == 1)
    def _():
        wd_s[...], inv = _f8_quantize(wd_ref[...])
        inv_s[0:1, :] = jnp.broadcast_to(inv, (1, LANES))

    @pl.when(j < nused_ref[0])
    def _():
        a8, inv_a = _f8_quantize(a_ref[...].astype(F32))
        o_ref[...] = (_dot(a8, wd_s[...]) * (inv_a * inv_s[0:1, 0:1])).astype(BF16)

    @pl.when(j >= nused_ref[0])
    def _():
        o_ref[...] = jnp.zeros_like(o_ref)


def _moe_down(te, chg, nused, act, wd):
    rows, ff = act.shape
    d = wd.shape[2]
    return pl.pallas_call(
        _moe_down_kernel,
        grid_spec=pltpu.PrefetchScalarGridSpec(
            num_scalar_prefetch=3, grid=(rows // MOE_TILE,),
            in_specs=[
                pl.BlockSpec((MOE_TILE, ff), lambda j, *_: (j, 0)),
                pl.BlockSpec((None, ff, d), lambda j, te, *_: (te[j], 0, 0)),
            ],
            out_specs=pl.BlockSpec((MOE_TILE, d), lambda j, *_: (j, 0)),
            scratch_shapes=[pltpu.VMEM((ff, d), F8), pltpu.VMEM((SUBLANES, LANES), F32)],
        ),
        out_shape=jax.ShapeDtypeStruct((rows, d), BF16),
        compiler_params=_cparams(("arbitrary",)), name="moe_down",
    )(te, chg, nused, act, wd)


def _combine_kernel(alpha, n_tiles, nw_ctx, te_ref, base_ref, wlo_ref, whi_ref, rank_ref, gate_ref, y_ref, x1_ref,
                    g2_ref, lg_ref, lb_ref, oc_ref, ol_ref, acc_s):
    s = pl.program_id(0)

    @pl.when(s == 0)
    def _():
        acc_s[...] = jnp.zeros_like(acc_s)

    @pl.when(s < n_tiles)
    def _():
        e, base = te_ref[s], base_ref[s]

        def window(w, carry):
            hit = _tile_onehot(rank_ref, w, e, base)
            q = jnp.where(hit, gate_ref[w, pl.ds(e, 1), :], 0.0).T.astype(BF16)
            rows = pl.ds(pl.multiple_of(w * MOE_TILE, MOE_TILE), MOE_TILE)
            acc_s[rows, :] += _dot(q, y_ref[...])
            return carry

        lax.fori_loop(wlo_ref[s], whi_ref[s] + 1, window, 0)

    @pl.when(s >= n_tiles)
    def _():
        w = s - n_tiles
        rows = pl.ds(pl.multiple_of(w * MOE_TILE, MOE_TILE), MOE_TILE)
        out = _layer_norm(alpha * x1_ref[...] + g2_ref[...] * acc_s[rows, :], lg_ref[...], lb_ref[...])

        @pl.when(w < nw_ctx)
        def _():
            oc_ref[...] = out

        @pl.when(w >= nw_ctx)
        def _():
            ol_ref[...] = out


def _moe_combine(meta, rank_t, gate_t, y, x1, mod_l, ln_g, ln_b, alpha, row_of_tile, t_ctx):
    t, d = x1.shape
    nw = t // MOE_TILE
    nw_ctx = t_ctx // MOE_TILE
    n_tiles = y.shape[0] // MOE_TILE
    win = lambda s: jnp.maximum(s - n_tiles, 0)
    return pl.pallas_call(
        functools.partial(_combine_kernel, alpha, n_tiles, nw_ctx),
        grid_spec=pltpu.PrefetchScalarGridSpec(
            num_scalar_prefetch=4, grid=(n_tiles + nw,),
            in_specs=[
                pl.BlockSpec((nw, N_EXPERTS, MOE_TILE), lambda s, *_: (0, 0, 0)),
                pl.BlockSpec((nw, N_EXPERTS, MOE_TILE), lambda s, *_: (0, 0, 0)),
                pl.BlockSpec((MOE_TILE, d), lambda s, *_: (jnp.minimum(s, n_tiles - 1), 0)),
                pl.BlockSpec((MOE_TILE, d), lambda s, *_: (win(s), 0)),
                pl.BlockSpec((None, None, 1, d), lambda s, *_: (row_of_tile(win(s)), 5, 0, 0)),
                pl.BlockSpec((1, d), lambda s, *_: (0, 0)),
                pl.BlockSpec((1, d), lambda s, *_: (0, 0)),
            ],
            out_specs=[
                pl.BlockSpec((MOE_TILE, d), lambda s, *_: (jnp.minimum(win(s), nw_ctx - 1), 0)),
                pl.BlockSpec((MOE_TILE, d), lambda s, *_: (jnp.maximum(win(s) - nw_ctx, 0), 0)),
            ],
            scratch_shapes=[pltpu.VMEM((t, d), F32)],
        ),
        out_shape=[jax.ShapeDtypeStruct((t_ctx, d), F32), jax.ShapeDtypeStruct((t - t_ctx, d), F32)],
        compiler_params=_cparams(("arbitrary",)), name="moe_combine",
    )(*meta, rank_t, gate_t, y, x1, mod_l, ln_g, ln_b)


def _moe_layer(gates, h2, x1, mod_l, wg, wu, wd, ln_g, ln_b, alpha, row_of_tile, t_ctx):
    t = h2.shape[0]
    n_tiles = -(-(2 * t + N_EXPERTS * (MOE_TILE - 1)) // MOE_TILE)
    rank_t, gate_t, before = _route(gates)
    before = before[:, :, 0].astype(jnp.int32)
    counts = jnp.sum(gates > 0.0, axis=1).astype(jnp.int32)
    tiles_e = (counts + MOE_TILE - 1) // MOE_TILE
    tile_end = jnp.cumsum(tiles_e)
    n_used = tile_end[-1]
    j = jnp.arange(n_tiles, dtype=jnp.int32)
    te = jnp.minimum(jnp.sum(j[:, None] >= tile_end[None, :], axis=1), N_EXPERTS - 1).astype(jnp.int32)
    base = (j - jnp.take(tile_end - tiles_e, te)) * MOE_TILE
    last = jnp.minimum(base + MOE_TILE - 1, jnp.take(counts, te) - 1)
    before_t = jnp.take(before, te, axis=1)
    used = j < n_used
    wlo = jnp.where(used, jnp.sum(before_t <= base[None, :], axis=0) - 1, 1).astype(jnp.int32)
    whi = jnp.where(used, jnp.sum(before_t <= last[None, :], axis=0) - 1, 0).astype(jnp.int32)
    chg = jnp.concatenate([jnp.ones((1,), jnp.int32), (te[1:] != te[:-1]).astype(jnp.int32)])
    nused = n_used.reshape(1).astype(jnp.int32)
    meta = (te, base.astype(jnp.int32), wlo, whi)
    xs = _moe_gather(meta, rank_t, h2, n_tiles)
    act = _moe_up(te, chg, nused, xs, wg, wu)
    y = _moe_down(te, chg, nused, act, wd)
    return _moe_combine(meta, rank_t, gate_t, y, x1, mod_l, ln_g, ln_b, alpha, row_of_tile, t_ctx)


def _rope_tables(seq):
    half = GLA_DK // 2
    n = half // 2
    t = np.arange(seq)
    inv = ROPE_BASE ** (-np.arange(n, dtype=np.float64) / n)
    cos = np.zeros((seq, GLA_DK), np.float64)
    sin = np.zeros((seq, GLA_DK), np.float64)
    for a, pos in enumerate((t // GRID_W, t % GRID_W)):
        ang = pos[:, None].astype(np.float64) * inv[None, :]
        base = a * half
        cos[:, base:base + n] = np.cos(ang)
        cos[:, base + n:base + half] = np.cos(ang)
        sin[:, base:base + n] = -np.sin(ang)
        sin[:, base + n:base + half] = np.sin(ang)
    tile = lambda m: jnp.asarray(np.tile(m, (1, GLA_HEADS)), F32)
    return tile(cos), tile(sin)


W_IN_GD = 1536
W_IN_KV = (2080, 2592)
W_IN_CHUNKS = (0, 512, 1024, 1568, 3104, 3616, 4128, 4640, 5152, 5664, 6176, 6688)
W_IN_ROWS = 256


def _w_in_cols(w_ref, start, width):
    lo = start - start % LANES
    hi = min(-(-(start + width) // LANES) * LANES, w_ref.shape[1])
    return w_ref[:, lo:hi][:, start - lo:start - lo + width]


def _w_in_prep_kernel(w_ref, gd_ref, kv_ref, ch_ref):
    rows = w_ref.shape[0]
    gd = _w_in_cols(w_ref, W_IN_GD, 2 * GLA_RANK)
    gd_ref[...] = jnp.concatenate([gd, jnp.zeros((rows, LANES - 2 * GLA_RANK), F32)], axis=1).astype(BF16)
    for j, s in enumerate(W_IN_KV):
        kv_ref[j] = _w_in_cols(w_ref, s, CHUNK_W).astype(BF16)
    for j, s in enumerate(W_IN_CHUNKS):
        ch_ref[j] = _w_in_cols(w_ref, s, CHUNK_W).astype(BF16)


def _chunk_w_in(w):
    depth, d, cols = w.shape
    return pl.pallas_call(
        _w_in_prep_kernel,
        grid=(depth, d // W_IN_ROWS),
        in_specs=[pl.BlockSpec((None, W_IN_ROWS, cols), lambda l, i: (l, i, 0))],
        out_specs=[
            pl.BlockSpec((None, W_IN_ROWS, LANES), lambda l, i: (l, i, 0)),
            pl.BlockSpec((None, len(W_IN_KV), W_IN_ROWS, CHUNK_W), lambda l, i: (l, 0, i, 0)),
            pl.BlockSpec((None, N_BF16_CHUNKS, W_IN_ROWS, CHUNK_W), lambda l, i: (l, 0, i, 0)),
        ],
        out_shape=[
            jax.ShapeDtypeStruct((depth, d, LANES), BF16),
            jax.ShapeDtypeStruct((depth, len(W_IN_KV), d, CHUNK_W), BF16),
            jax.ShapeDtypeStruct((depth, N_BF16_CHUNKS, d, CHUNK_W), BF16),
        ],
        compiler_params=_cparams(("arbitrary", "arbitrary")), name="w_in_relayout",
    )(w)


def _gate_up_blocks(w_gup):
    qkw = w_gup.shape[-1]
    fwd = jnp.pad(w_gup[:, 0], ((0, 0), (0, 0), (0, qkw)))
    bwd = jnp.pad(w_gup[:, 1], ((0, 0), (0, 0), (qkw, 0)))
    return jnp.pad(jnp.concatenate([fwd, bwd], axis=1), ((0, 0), (0, LANES - 2 * GLA_RANK), (0, 0)))


def kernel(x_prompt, x_sample, cache_na_k, cache_na_v, state_gla, c, c_ctx, w_mod, b_mod, w_in, gla_w_gup, gla_b_g, gla_norm, w_br_gla, na_rpb, w_br_na, cv_w, cv_b, cv_ln_g, cv_ln_b, w_br_cv, w_out, ln_g, ln_b, ffd_w_gate, ffd_w_up, ffd_w_down, moe_w_router, moe_w_gate, moe_w_up, moe_w_down):
    nb_c, l_c, d = x_prompt.shape
    nb_l, l_l, _ = x_sample.shape
    depth = w_mod.shape[0]
    t_ctx, t_lat = nb_c * l_c, nb_l * l_l
    t_all = t_ctx + t_lat
    alpha = (2 * depth) ** 0.25
    tm = 512
    row_of_tile = _row_of_tile_fn(t_ctx, l_l, tm)

    x_pair = (x_prompt.reshape(t_ctx, d), x_sample.reshape(t_lat, d))
    cond8 = jnp.concatenate([c_ctx[None, :], c, jnp.zeros((8 - 1 - nb_l, d), F32)], axis=0)
    mod = _modulation(cond8, w_mod, b_mod).reshape(depth, 8, 6, 1, d)
    rope = _rope_tables(l_l)
    rows_lat = l_l // GRID_W
    kr = min(NA_WIN_R, rows_lat)
    s0_all = state_gla.reshape(nb_l, depth, 2, 2, LANES, GLA_DV)
    w_gd, w_kv, w_chunks = _chunk_w_in(w_in)
    wg_blks = _gate_up_blocks(gla_w_gup)
    toep = _na_bias_tables(na_rpb)
    w_br = [w.astype(BF16) for w in (w_br_gla, w_br_na, w_br_cv, w_out)]

    caches = None
    ss = []
    for l in range(depth):
        mod_l = mod[l]
        gd, kc, vc, kl, vl, zb = _input_projection(x_pair, mod_l, (w_gd[l], w_kv[l], w_chunks[l]), l, depth, l_c,
                                                   tm, row_of_tile, caches)
        caches = (kc, vc)

        wg_blk = wg_blks[l]
        bg = gla_b_g[l].reshape(1, CHUNK_W)
        gn = gla_norm[l].reshape(1, CHUNK_W)
        ya, s_fin = _gla_branch(gd, zb, wg_blk, bg, gn, latent=False, seq=l_c, nb=nb_c, tok0=0, t_total=t_all)
        ya = _gla_branch(gd, zb, wg_blk, bg, gn, latent=True, seq=l_l, nb=nb_l, tok0=t_ctx, t_total=t_all,
                         rope=rope, s0=s0_all[:, l], ya_prev=ya)

        yb = _attn_context(zb, kc, vc, l, nb_c, l_c, t_all)
        yb = _attn_latent(zb, kl, vl, cache_na_k, cache_na_v, l, toep, yb, nb_l, l_l, t_ctx)

        conv_args = (zb, cv_w[l], cv_b[l][None, :], cv_ln_g[l][None, :], cv_ln_b[l][None, :])
        yc = _conv_branch(*conv_args, seq=l_c, nb=nb_c, tok0=0, t_total=t_all)
        yc = _conv_branch(*conv_args, seq=l_l, nb=nb_l, tok0=t_ctx, t_total=t_all, prev=yc)

        moe = l % 2 == 1
        w_router = None
        if moe:
            w_router = moe_w_router[l // 2].T
        merged = _merge(ya, yb, yc, zb, x_pair, mod_l, w_br[0][l], w_br[1][l], w_br[2][l], w_br[3][l],
                        ln_g[l, 0][None, :], ln_b[l, 0][None, :], alpha, tm, row_of_tile, w_router)
        ln2 = (ln_g[l, 1][None, :], ln_b[l, 1][None, :])
        if moe:
            x1, h2, gates = merged
            x_pair = _moe_layer(gates, h2, x1, mod_l, moe_w_gate[l // 2], moe_w_up[l // 2], moe_w_down[l // 2],
                                *ln2, alpha, _row_of_tile_fn(t_ctx, l_l, MOE_TILE), t_ctx)
        else:
            x1, h2 = merged
            i = l // 2
            x_pair = _dense_mixer(h2, x1, mod_l, ffd_w_gate[i].astype(BF16), ffd_w_up[i].astype(BF16),
                                  ffd_w_down[i].astype(BF16), *ln2, alpha, tm, 2, row_of_tile, t_ctx)
        ss.append(s_fin.reshape(nb_c, 2, GLA_HEADS, GLA_DK, GLA_DV))

    y_prompt = x_pair[0].reshape(nb_c, l_c, d)
    y_sample = x_pair[1].reshape(nb_l, l_l, d)
    cache_shape = (nb_c, depth, l_c, NA_HEADS, NA_DH)
    return (y_prompt, y_sample, caches[0].reshape(cache_shape), caches[1].reshape(cache_shape), jnp.stack(ss, 1))


def _row_of_tile_fn(t_ctx, l_lat, tm):
    n_ctx = t_ctx // tm

    def row_of_tile(i):
        return jnp.where(i < n_ctx, 0, 1 + (i - n_ctx) // (l_lat // tm))

    return row_of_tile
```

```python
import functools

import numpy as np
import jax
import jax.numpy as jnp
from jax import lax
from jax.experimental import pallas as pl
from jax.experimental.pallas import tpu as pltpu

F32 = jnp.float32
BF16 = jnp.bfloat16

GRID_W = 64
GLA_HEADS = 4
GLA_DK = 64
GLA_DV = 128
GLA_RANK = 16
GLA_TAU = 16.0
GLA_CHUNK = 64
GLA_GROUP = 256
NA_HEADS = 8
NA_DH = 64
NA_WIN_R = 8
NA_WIN_C = 16
CONV_CH = 512
CONV_K = 31
N_EXPERTS = 8
ROPE_BASE = 10000.0
LN_EPS = 1e-5

D_MODEL = 1024
LANES = 128
SUBLANES = 8
CHUNK_W = 512
N_BF16_CHUNKS = 12
NEG_BIG = -1e30
VMEM_LIMIT = 56 * 1024 * 1024


def _cparams(sem, vmem=VMEM_LIMIT):
    return pltpu.CompilerParams(dimension_semantics=sem, vmem_limit_bytes=vmem)


def _dot(a, b):
    return jnp.dot(a, b, preferred_element_type=F32)


def _split(x):
    hi = x.astype(BF16)
    return hi, (x - hi.astype(F32)).astype(BF16)


def _dot_split(a, b):
    a_hi, a_lo = _split(a)
    b_hi, b_lo = _split(b)
    return _dot(a_hi, b_hi) + _dot(a_lo, b_hi) + _dot(a_hi, b_lo)


def _dot_split_rhs(a, b):
    b_hi, b_lo = _split(b)
    return _dot(a, b_hi) + _dot(a, b_lo)


def _dot_nt(a, b):
    return lax.dot_general(a, b, (((1,), (1,)), ((), ())), preferred_element_type=F32)


def _sigmoid(x):
    return 0.5 * jnp.tanh(0.5 * x) + 0.5


def _silu(x):
    return x * _sigmoid(x)


def _layer_norm(x, g, b):
    mu = jnp.mean(x, axis=-1, keepdims=True)
    xc = x - mu
    var = jnp.mean(xc * xc, axis=-1, keepdims=True)
    return xc * lax.rsqrt(var + LN_EPS) * g + b


def _mod_kernel(c_ref, w_ref, b_ref, o_ref):
    c = c_ref[...]
    s = _silu(c).astype(BF16)
    o_ref[...] = _dot(s, w_ref[...].astype(BF16)) + b_ref[...]


def _modulation(cond8, w_mod, b_mod):
    depth, d, n = w_mod.shape
    tn = 1024
    return pl.pallas_call(
        _mod_kernel,
        grid=(depth, n // tn),
        in_specs=[
            pl.BlockSpec((8, d), lambda l, j: (0, 0)),
            pl.BlockSpec((None, d, tn), lambda l, j: (l, 0, j)),
            pl.BlockSpec((None, 1, tn), lambda l, j: (l, 0, j)),
        ],
        out_specs=pl.BlockSpec((None, 8, tn), lambda l, j: (l, 0, j)),
        out_shape=jax.ShapeDtypeStruct((depth, 8, n), F32),
        compiler_params=_cparams(("arbitrary", "arbitrary")),
        name="modulation",
    )(cond8, w_mod, b_mod.reshape(depth, 1, n))


def _inproj_kernel(n_ctx_tiles, aliased, *refs):
    if aliased:
        (xc_ref, xl_ref, sc_ref, sh_ref, wgd_ref, wkv_ref, w_ref, _kc_in, _vc_in,
         gd_ref, kc_ref, vc_ref, kl_ref, vl_ref, zb_ref, h_ref) = refs
    else:
        (xc_ref, xl_ref, sc_ref, sh_ref, wgd_ref, wkv_ref, w_ref,
         gd_ref, kc_ref, vc_ref, kl_ref, vl_ref, zb_ref, h_ref) = refs
    is_ctx = pl.program_id(0) < n_ctx_tiles
    x = jnp.where(is_ctx, xc_ref[...], xl_ref[...])
    h_ref[...] = (x * (1.0 + sc_ref[...]) + sh_ref[...]).astype(BF16)
    gd_ref[...] = _dot_nt(h_ref[...], wgd_ref[...])
    k = _dot_nt(h_ref[...], wkv_ref[0])
    v = _dot_nt(h_ref[...], wkv_ref[1])
    seqs, l_c, _ = kc_ref.shape

    @pl.when(is_ctx)
    def _():
        for s in range(seqs):
            kc_ref[s] = k[s * l_c:(s + 1) * l_c]
            vc_ref[s] = v[s * l_c:(s + 1) * l_c]

    @pl.when(jnp.logical_not(is_ctx))
    def _():
        kl_ref[...] = k
        vl_ref[...] = v

    for j in range(N_BF16_CHUNKS):
        zb_ref[j] = _dot_nt(h_ref[...], w_ref[j]).astype(BF16)


def _mod_spec(piece, row_of_tile):
    return pl.BlockSpec((None, None, 1, D_MODEL), lambda i: (row_of_tile(i), piece, 0, 0))


def _pair_specs(tm, d, n_ctx_tiles):
    return [pl.BlockSpec((tm, d), lambda i, *_: (jnp.minimum(i, n_ctx_tiles - 1), 0)),
            pl.BlockSpec((tm, d), lambda i, *_: (jnp.maximum(i - n_ctx_tiles, 0), 0))]


def _input_projection(x_pair, mod_l, w_parts, layer, depth, l_c, tm, row_of_tile, caches=None):
    xc, xl = x_pair
    w_gd, w_kv, w_chunks = w_parts
    d = xc.shape[1]
    t_ctx, t_lat = xc.shape[0], xl.shape[0]
    t = t_ctx + t_lat
    n_ctx_tiles = t_ctx // tm
    seqs = tm // l_c
    cache_spec = pl.BlockSpec((seqs, None, l_c, CHUNK_W),
                              lambda i: (jnp.minimum(i, n_ctx_tiles - 1), layer, 0, 0))
    lat_spec = pl.BlockSpec((tm, CHUNK_W), lambda i: (jnp.maximum(i - n_ctx_tiles, 0), 0))
    const = lambda shape: pl.BlockSpec(shape, lambda i: tuple(0 for _ in shape), pipeline_mode=pl.Buffered(1))
    in_specs = _pair_specs(tm, d, n_ctx_tiles) + [
        _mod_spec(1, row_of_tile), _mod_spec(0, row_of_tile),
        const(w_gd.shape), const(w_kv.shape), const(w_chunks.shape),
    ]
    args = [xc, xl, mod_l, mod_l, w_gd, w_kv, w_chunks]
    aliases = {}
    if caches is not None:
        in_specs += [pl.BlockSpec(memory_space=pl.ANY), pl.BlockSpec(memory_space=pl.ANY)]
        args += list(caches)
        aliases = {7: 1, 8: 2}
    cache_shape = jax.ShapeDtypeStruct((t_ctx // l_c, depth, l_c, CHUNK_W), F32)
    return pl.pallas_call(
        functools.partial(_inproj_kernel, n_ctx_tiles, caches is not None),
        grid=(t // tm,),
        in_specs=in_specs,
        out_specs=[
            pl.BlockSpec((tm, LANES), lambda i: (i, 0)),
            cache_spec, cache_spec, lat_spec, lat_spec,
            pl.BlockSpec((N_BF16_CHUNKS, tm, CHUNK_W), lambda i: (0, i, 0)),
        ],
        out_shape=[
            jax.ShapeDtypeStruct((t, LANES), F32),
            cache_shape, cache_shape,
            jax.ShapeDtypeStruct((t_lat, CHUNK_W), F32), jax.ShapeDtypeStruct((t_lat, CHUNK_W), F32),
            jax.ShapeDtypeStruct((N_BF16_CHUNKS, t, CHUNK_W), BF16),
        ],
        scratch_shapes=[pltpu.VMEM((tm, d), BF16)],
        input_output_aliases=aliases,
        compiler_params=_cparams(("arbitrary",)),
        name="input_projection",
    )(*args)


def _gla_kernel(latent, seq, *refs):
    if latent:
        (qk_ref, v_ref, ra_ref, gd_ref, wg_ref, bg_ref, gn_ref, cos_ref, sin_ref, s0_ref, _alias,
         ya_ref, q_s, k_s, g_s, acc_s, st_s) = refs
    else:
        (qk_ref, v_ref, ra_ref, gd_ref, wg_ref, bg_ref, gn_ref,
         ya_ref, sfin_ref, q_s, k_s, g_s, acc_s, st_s) = refs
    c = GLA_CHUNK
    n_chunks = seq // c
    qkw = GLA_HEADS * GLA_DK

    q = qk_ref[:, 0:qkw].astype(F32) * (GLA_DK ** -0.5)
    k = qk_ref[:, qkw:2 * qkw].astype(F32)
    if latent:
        lane = lax.broadcasted_iota(jnp.int32, (1, qkw), 1)
        first_half = (lane % 32) < 16

        def rope(x):
            swapped = jnp.where(first_half, pltpu.roll(x, qkw - 16, 1), pltpu.roll(x, 16, 1))
            return x * cos_ref[...] + swapped * sin_ref[...]

        q, k = rope(q), rope(k)
    q_s[...] = q
    k_s[...] = k

    glog = _dot_split(gd_ref[...], wg_ref[...]) + bg_ref[...]
    g_s[...] = (jnp.minimum(glog, 0.0) - jnp.log(1.0 + jnp.exp(-jnp.abs(glog)))) * (1.0 / GLA_TAU)
    acc_s[...] = jnp.zeros_like(acc_s)
    if latent:
        st_s[...] = s0_ref[...]
    else:
        st_s[...] = jnp.zeros_like(st_s)

    grp = GLA_GROUP
    cpg = grp // c
    n_groups = seq // grp
    grow = lax.broadcasted_iota(jnp.int32, (grp, grp), 0)
    gcol = lax.broadcasted_iota(jnp.int32, (grp, grp), 1)
    same_chunk = (grow // c) == (gcol // c)
    keep = (same_chunk & (gcol <= grow), same_chunk & (gcol >= grow))
    tri = (keep[0].astype(BF16), keep[1].astype(BF16))
    lane128 = lax.broadcasted_iota(jnp.int32, (1, LANES), 1)
    head_mask = (lane128 < GLA_DK, lane128 >= GLA_DK)
    urow = lax.broadcasted_iota(jnp.int32, (cpg * LANES, GLA_DV), 0)
    srow = lax.broadcasted_iota(jnp.int32, (LANES, GLA_DV), 0)
    chunk_of_col = lax.broadcasted_iota(jnp.int32, (LANES, grp), 1) // c
    chunk_of_row = lax.broadcasted_iota(jnp.int32, (grp, LANES), 0) // c
    pad_rows = jnp.zeros((SUBLANES - cpg, qkw), F32)

    def group_step(i, carry):
        for d in range(2):
            gi = i if d == 0 else n_groups - 1 - i
            rows = pl.ds(pl.multiple_of(gi * grp, grp), grp)
            cum = _dot_split_rhs(tri[d], g_s[rows, d * qkw:(d + 1) * qkw])
            edge = c - 1 if d == 0 else 0
            lasts = [cum[ci * c + edge:ci * c + edge + 1, :] for ci in range(cpg)]
            last = jnp.concatenate([jnp.broadcast_to(r, (c, qkw)) for r in lasts], axis=0)
            ref = 0.5 * last
            q = q_s[rows, :]
            k = k_s[rows, :]
            qe = q * jnp.exp(cum - ref)
            ke = k * jnp.exp(ref - cum)
            qe2 = q * jnp.exp(cum)
            ke2 = k * jnp.exp(last - cum)
            last8 = jnp.concatenate(lasts + [pad_rows], axis=0)
            for p in range(2):
                sl = slice(p * LANES, (p + 1) * LANES)
                ke_p = ke[:, sl].astype(BF16)
                ke2_t = ke2[:, sl].T
                ke2_blk = jnp.concatenate(
                    [jnp.where(chunk_of_col == ci, ke2_t, 0.0) for ci in range(cpg)], axis=0).astype(BF16)
                dec = jnp.exp(last8[:, sl]).T
                psl = slice(2 * p * GLA_DV, (2 * p + 2) * GLA_DV)
                v_pair = v_ref[rows, psl]
                u = _dot(ke2_blk, v_pair)
                upd = jnp.where(urow % LANES < GLA_DK, u[:, :GLA_DV], u[:, GLA_DV:])
                s = st_s[d, p]
                s_in = [None] * cpg
                for ci in (range(cpg) if d == 0 else range(cpg - 1, -1, -1)):
                    s_in[ci] = jnp.concatenate(
                        [jnp.where(srow < GLA_DK, s, 0.0), jnp.where(srow >= GLA_DK, s, 0.0)], axis=1).astype(BF16)
                    s = dec[:, ci:ci + 1] * s + upd[ci * LANES:(ci + 1) * LANES]
                st_s[d, p] = s
                s_stack = jnp.concatenate(s_in, axis=0)
                a2_blk = jnp.concatenate(
                    [jnp.where(chunk_of_row == ci, qe2[:, sl], 0.0) for ci in range(cpg)], axis=1).astype(BF16)
                intra = []
                for hh in range(2):
                    a = jnp.where(head_mask[hh], qe[:, sl], 0.0).astype(BF16)
                    att = jnp.where(keep[d], _dot_nt(a, ke_p), 0.0).astype(BF16)
                    intra.append(_dot(att, v_pair[:, hh * GLA_DV:(hh + 1) * GLA_DV]))
                acc_s[rows, psl] += jnp.concatenate(intra, axis=1) + _dot(a2_blk, s_stack)
        return carry

    lax.fori_loop(0, n_groups, group_step, 0)

    def epilogue(i, carry):
        rows = pl.ds(pl.multiple_of(i * c, c), c)
        for h in range(GLA_HEADS):
            vsl = slice(h * GLA_DV, (h + 1) * GLA_DV)
            o = acc_s[rows, vsl]
            o = o * lax.rsqrt(jnp.mean(o * o, axis=-1, keepdims=True) + LN_EPS) * gn_ref[:, vsl]
            ya_ref[rows, vsl] = (o * _silu(ra_ref[rows, vsl].astype(F32))).astype(BF16)
        return carry

    lax.fori_loop(0, n_chunks, epilogue, 0)
    if not latent:
        sfin_ref[...] = st_s[...]


def _gla_branch(gd, zb, wg_blk, bg, gn, *, latent, seq, nb, tok0, t_total, rope=None, s0=None, ya_prev=None):
    b0 = tok0 // seq
    qkw = GLA_HEADS * GLA_DK
    in_specs = [
        pl.BlockSpec((None, seq, CHUNK_W), lambda b: (0, b0 + b, 0)),
        pl.BlockSpec((None, seq, CHUNK_W), lambda b: (1, b0 + b, 0)),
        pl.BlockSpec((None, seq, CHUNK_W), lambda b: (2, b0 + b, 0)),
        pl.BlockSpec((seq, LANES), lambda b: (b0 + b, 0)),
        pl.BlockSpec((LANES, CHUNK_W), lambda b: (0, 0)),
        pl.BlockSpec((1, CHUNK_W), lambda b: (0, 0)),
        pl.BlockSpec((1, CHUNK_W), lambda b: (0, 0)),
    ]
    args = [zb, zb, zb, gd, wg_blk, bg, gn]
    scratch = [
        pltpu.VMEM((seq, qkw), F32), pltpu.VMEM((seq, qkw), F32), pltpu.VMEM((seq, CHUNK_W), F32),
        pltpu.VMEM((seq, CHUNK_W), F32), pltpu.VMEM((2, 2, LANES, GLA_DV), F32),
    ]
    ya_shape = jax.ShapeDtypeStruct((t_total, CHUNK_W), BF16)
    ya_spec = pl.BlockSpec((seq, CHUNK_W), lambda b: (b0 + b, 0))
    if latent:
        cos_t, sin_t = rope
        in_specs += [
            pl.BlockSpec((seq, qkw), lambda b: (0, 0)),
            pl.BlockSpec((seq, qkw), lambda b: (0, 0)),
            pl.BlockSpec((None, 2, 2, LANES, GLA_DV), lambda b: (b, 0, 0, 0, 0)),
            pl.BlockSpec(memory_space=pl.ANY),
        ]
        args += [cos_t, sin_t, s0, ya_prev]
        return pl.pallas_call(
            functools.partial(_gla_kernel, True, seq),
            grid=(nb,), in_specs=in_specs, out_specs=ya_spec, out_shape=ya_shape,
            scratch_shapes=scratch, input_output_aliases={len(args) - 1: 0},
            compiler_params=_cparams(("arbitrary",)), name="gla_latent",
        )(*args)
    return pl.pallas_call(
        functools.partial(_gla_kernel, False, seq),
        grid=(nb,), in_specs=in_specs,
        out_specs=[ya_spec, pl.BlockSpec((None, 2, 2, LANES, GLA_DV), lambda b: (b, 0, 0, 0, 0))],
        out_shape=[ya_shape, jax.ShapeDtypeStruct((nb, 2, 2, LANES, GLA_DV), F32)],
        scratch_shapes=scratch,
        compiler_params=_cparams(("arbitrary",)), name="gla_context",
    )(*args)


def _attn_ctx_kernel(q_ref, k_ref, v_ref, o_ref):
    lane = lax.broadcasted_iota(jnp.int32, (1, LANES), 1)
    masks = (lane < NA_DH, lane >= NA_DH)
    scale = NA_DH ** -0.5
    seq = q_ref.shape[0]
    for p in range(NA_HEADS // 2):
        sl = slice(p * LANES, (p + 1) * LANES)
        q2 = _stack_heads(q_ref[:, sl], masks)
        kp = k_ref[:, sl].astype(BF16)
        vp = v_ref[:, sl].astype(BF16)
        s = _dot_nt(q2, kp) * scale
        e = jnp.exp(s - jnp.max(s, axis=-1, keepdims=True))
        o2 = _dot(e.astype(BF16), vp) / jnp.sum(e, axis=-1, keepdims=True)
        o_ref[:, sl] = jnp.where(masks[0], o2[0:seq], o2[seq:2 * seq]).astype(BF16)


def _stack_heads(qp, masks):
    zero = jnp.zeros_like(qp)
    return jnp.concatenate([jnp.where(masks[0], qp, zero), jnp.where(masks[1], qp, zero)], axis=0)


def _attn_context(zb, kc, vc, layer, nb, seq, t_total):
    return pl.pallas_call(
        _attn_ctx_kernel,
        grid=(nb,),
        in_specs=[
            pl.BlockSpec((None, seq, CHUNK_W), lambda b: (3, b, 0)),
            pl.BlockSpec((None, None, seq, CHUNK_W), lambda b: (b, layer, 0, 0)),
            pl.BlockSpec((None, None, seq, CHUNK_W), lambda b: (b, layer, 0, 0)),
        ],
        out_specs=pl.BlockSpec((seq, CHUNK_W), lambda b: (b, 0)),
        out_shape=jax.ShapeDtypeStruct((t_total, CHUNK_W), BF16),
        compiler_params=_cparams(("arbitrary",)), name="attention_context",
    )(zb, kc, vc)


def _na_kernel(rows_total, kr, q_ref, kl_ref, vl_ref, kc_ref, vc_ref, toep_ref, _alias, o_ref):
    r = pl.program_id(1)
    rs = jnp.clip(r - kr // 2, 0, rows_total - kr)
    krows = pl.ds(pl.multiple_of(rs * GRID_W, GRID_W), kr * GRID_W)
    dr0 = rs - r + NA_WIN_R - 1
    lane = lax.broadcasted_iota(jnp.int32, (1, LANES), 1)
    masks = (lane < NA_DH, lane >= NA_DH)
    scale = NA_DH ** -0.5
    nq = q_ref.shape[0]
    for p in range(NA_HEADS // 2):
        sl = slice(p * LANES, (p + 1) * LANES)
        q2 = _stack_heads(q_ref[:, sl], masks)
        klp = kl_ref[krows, sl].astype(BF16)
        vlp = vl_ref[krows, sl].astype(BF16)
        kcp = kc_ref[:, sl].astype(BF16)
        vcp = vc_ref[:, sl].astype(BF16)
        bias = jnp.concatenate(
            [jnp.concatenate([toep_ref[2 * p + hh, dr0 + 2 * m] for m in range(kr // 2)], axis=1)
             for hh in range(2)], axis=0)
        s_loc = _dot_nt(q2, klp) * scale + bias
        s_ctx = _dot_nt(q2, kcp) * scale
        m = jnp.maximum(jnp.max(s_loc, axis=-1, keepdims=True), jnp.max(s_ctx, axis=-1, keepdims=True))
        e_loc = jnp.exp(s_loc - m)
        e_ctx = jnp.exp(s_ctx - m)
        den = jnp.sum(e_loc, axis=-1, keepdims=True) + jnp.sum(e_ctx, axis=-1, keepdims=True)
        o2 = (_dot(e_loc.astype(BF16), vlp) + _dot(e_ctx.astype(BF16), vcp)) / den
        o_ref[:, sl] = jnp.where(masks[0], o2[0:nq], o2[nq:2 * nq]).astype(BF16)


def _na_bias_tables(rpb):
    nc = rpb.shape[-1]
    qc = np.arange(GRID_W)[:, None]
    kc = np.arange(GRID_W)[None, :]
    cs = np.clip(qc - NA_WIN_C // 2, 0, GRID_W - NA_WIN_C)
    valid = (kc >= cs) & (kc < cs + NA_WIN_C)
    tap = np.clip(kc - qc, -(NA_WIN_C - 1), NA_WIN_C - 1) + NA_WIN_C - 1
    pick = jnp.asarray(tap[..., None] == np.arange(nc), F32)
    toep = jnp.einsum('dhab,qkb->dhaqk', rpb.astype(F32), pick, precision=lax.Precision.HIGHEST)
    toep = jnp.where(valid, toep, NEG_BIG)
    return jnp.concatenate([toep[:, :, :-1], toep[:, :, 1:]], axis=-1)


def _attn_latent(zb, kl, vl, cache_k, cache_v, layer, toep, att_prev, nb, seq, tok0):
    rows_total = seq // GRID_W
    kr = min(NA_WIN_R, rows_total)
    q0 = tok0 // GRID_W
    past = cache_k.shape[2]
    ck = cache_k.reshape(cache_k.shape[0], cache_k.shape[1], past, NA_HEADS * NA_DH)
    cv = cache_v.reshape(ck.shape)
    return pl.pallas_call(
        functools.partial(_na_kernel, rows_total, kr),
        grid=(nb, rows_total),
        in_specs=[
            pl.BlockSpec((None, GRID_W, CHUNK_W), lambda b, r: (3, q0 + b * rows_total + r, 0)),
            pl.BlockSpec((seq, CHUNK_W), lambda b, r: (b, 0)),
            pl.BlockSpec((seq, CHUNK_W), lambda b, r: (b, 0)),
            pl.BlockSpec((None, None, past, CHUNK_W), lambda b, r: (b, layer, 0, 0)),
            pl.BlockSpec((None, None, past, CHUNK_W), lambda b, r: (b, layer, 0, 0)),
            pl.BlockSpec((None,) + toep.shape[1:], lambda b, r: (layer, 0, 0, 0, 0), pipeline_mode=pl.Buffered(1)),
            pl.BlockSpec(memory_space=pl.ANY),
        ],
        out_specs=pl.BlockSpec((GRID_W, CHUNK_W), lambda b, r: (q0 + b * rows_total + r, 0)),
        out_shape=jax.ShapeDtypeStruct(att_prev.shape, BF16),
        input_output_aliases={6: 0},
        compiler_params=_cparams(("arbitrary", "arbitrary")), name="attention_latent",
    )(zb, kl, vl, ck, cv, toep, att_prev)


CONV_PAD = 16
CONV_ROWS = 64


def _conv_kernel(seq, aliased, *refs):
    if aliased:
        a_ref, gt_ref, w_ref, cb_ref, lg_ref, lb_ref, _alias, o_ref, z_s, sh_s = refs
    else:
        a_ref, gt_ref, w_ref, cb_ref, lg_ref, lb_ref, o_ref, z_s, sh_s = refs
    z_s[0:CONV_PAD, :] = jnp.zeros((CONV_PAD, CONV_CH), F32)
    z_s[CONV_PAD + seq:2 * CONV_PAD + seq, :] = jnp.zeros((CONV_PAD, CONV_CH), F32)
    z_s[CONV_PAD:CONV_PAD + seq, :] = a_ref[...].astype(F32) * _sigmoid(gt_ref[...].astype(F32))
    ext = seq + 2 * CONV_PAD - SUBLANES
    for b in range(1, SUBLANES):
        for r0 in range(0, ext, CONV_ROWS):
            n = min(CONV_ROWS, ext - r0)
            sh_s[b - 1, r0:r0 + n, :] = z_s[r0 + b:r0 + b + n, :]
    off = CONV_PAD - CONV_K // 2
    grp = (CONV_ROWS // SUBLANES, SUBLANES, CONV_CH)
    for t0 in range(0, seq, CONV_ROWS):
        acc = jnp.zeros(grp, F32)
        for k in range(CONV_K):
            a, b = divmod(off + k, SUBLANES)
            r0 = t0 + a * SUBLANES
            tap = z_s[r0:r0 + CONV_ROWS, :] if b == 0 else sh_s[b - 1, r0:r0 + CONV_ROWS, :]
            acc = acc + tap.reshape(grp) * w_ref[k][None]
        y = _layer_norm(acc.reshape(CONV_ROWS, CONV_CH) + cb_ref[...], lg_ref[...], lb_ref[...])
        o_ref[t0:t0 + CONV_ROWS, :] = _silu(y).astype(BF16)


def _conv_branch(zb, cv_w, cv_b, ln_g, ln_b, *, seq, nb, tok0, t_total, prev=None):
    b0 = tok0 // seq
    in_specs = [
        pl.BlockSpec((None, seq, CHUNK_W), lambda b: (4, b0 + b, 0)),
        pl.BlockSpec((None, seq, CHUNK_W), lambda b: (5, b0 + b, 0)),
        pl.BlockSpec((CONV_K, SUBLANES, CONV_CH), lambda b: (0, 0, 0)),
        pl.BlockSpec((1, CONV_CH), lambda b: (0, 0)),
        pl.BlockSpec((1, CONV_CH), lambda b: (0, 0)),
        pl.BlockSpec((1, CONV_CH), lambda b: (0, 0)),
    ]
    args = [zb, zb, jnp.broadcast_to(cv_w[:, None, :], (CONV_K, SUBLANES, CONV_CH)), cv_b, ln_g, ln_b]
    aliases = {}
    if prev is not None:
        in_specs.append(pl.BlockSpec(memory_space=pl.ANY))
        args.append(prev)
        aliases = {6: 0}
    return pl.pallas_call(
        functools.partial(_conv_kernel, seq, prev is not None),
        grid=(nb,), in_specs=in_specs,
        out_specs=pl.BlockSpec((seq, CONV_CH), lambda b: (b0 + b, 0)),
        out_shape=jax.ShapeDtypeStruct((t_total, CONV_CH), BF16),
        scratch_shapes=[pltpu.VMEM((seq + 2 * CONV_PAD, CONV_CH), F32),
                        pltpu.VMEM((SUBLANES - 1, seq + 2 * CONV_PAD - SUBLANES, CONV_CH), F32)],
        input_output_aliases=aliases,
        compiler_params=_cparams(("arbitrary",)), name="conv_module_%d" % seq,
    )(*args)


def _merge_kernel(alpha, n_ctx_tiles, with_router, *refs):
    if with_router:
        (ya_ref, yb_ref, yc_ref, mg_ref, xc_ref, xl_ref, g1_ref, sc2_ref, sh2_ref, wa_ref, wb_ref, wc_ref, wo_ref,
         lg_ref, lb_ref, wr_ref, x1_ref, h2_ref, gates_ref) = refs
    else:
        (ya_ref, yb_ref, yc_ref, mg_ref, xc_ref, xl_ref, g1_ref, sc2_ref, sh2_ref, wa_ref, wb_ref, wc_ref, wo_ref,
         lg_ref, lb_ref, x1_ref, h2_ref) = refs
    x = jnp.where(pl.program_id(0) < n_ctx_tiles, xc_ref[...], xl_ref[...])
    halves = []
    for n in range(2):
        m = None
        for j, (y_ref, w_ref) in enumerate(((ya_ref, wa_ref), (yb_ref, wb_ref), (yc_ref, wc_ref))):
            proj = _dot(y_ref[...], w_ref[:, n * CHUNK_W:(n + 1) * CHUNK_W])
            term = _sigmoid(mg_ref[2 * j + n].astype(F32)) * proj
            m = term if m is None else m + term
        halves.append(m.astype(BF16))
    mix = _dot(halves[0], wo_ref[0:CHUNK_W, :]) + _dot(halves[1], wo_ref[CHUNK_W:2 * CHUNK_W, :])
    x1 = _layer_norm(alpha * x + g1_ref[...] * mix, lg_ref[...], lb_ref[...])
    x1_ref[...] = x1
    h2 = x1 * (1.0 + sc2_ref[...]) + sh2_ref[...]
    h2_ref[...] = h2.astype(BF16)
    if with_router:
        w_hi, w_lo = _split(wr_ref[...])
        h_hi, h_lo = _split(h2)
        lg = _dot_nt(w_hi, h_hi) + _dot_nt(w_hi, h_lo) + _dot_nt(w_lo, h_hi)
        eid = lax.broadcasted_iota(jnp.int32, lg.shape, 0)
        m1 = jnp.max(lg, axis=0, keepdims=True)
        i1 = jnp.min(jnp.where(lg == m1, eid, N_EXPERTS), axis=0, keepdims=True)
        lg2 = jnp.where(eid == i1, -jnp.inf, lg)
        m2 = jnp.max(lg2, axis=0, keepdims=True)
        i2 = jnp.min(jnp.where(lg2 == m2, eid, N_EXPERTS), axis=0, keepdims=True)
        e2 = jnp.exp(m2 - m1)
        w1 = 1.0 / (1.0 + e2)
        gates_ref[...] = jnp.where(eid == i1, w1, 0.0) + jnp.where(eid == i2, e2 * w1, 0.0)


def _merge(ya, yb, yc, zb, x_pair, mod_l, wa, wb, wc, wo, ln_g, ln_b, alpha, tm, row_of_tile, w_router=None):
    xc, xl = x_pair
    d = xc.shape[1]
    t = xc.shape[0] + xl.shape[0]
    n_ctx_tiles = xc.shape[0] // tm
    full = lambda shape: pl.BlockSpec(shape, lambda i: tuple(0 for _ in shape), pipeline_mode=pl.Buffered(1))
    in_specs = [
        pl.BlockSpec((tm, CHUNK_W), lambda i: (i, 0)),
        pl.BlockSpec((tm, CHUNK_W), lambda i: (i, 0)),
        pl.BlockSpec((tm, CHUNK_W), lambda i: (i, 0)),
        pl.BlockSpec((6, tm, CHUNK_W), lambda i: (1, i, 0)),
    ] + _pair_specs(tm, d, n_ctx_tiles) + [
        _mod_spec(2, row_of_tile), _mod_spec(4, row_of_tile), _mod_spec(3, row_of_tile),
        full(wa.shape), full(wb.shape), full(wc.shape), full(wo.shape), full((1, d)), full((1, d)),
    ]
    args = [ya, yb, yc, zb, xc, xl, mod_l, mod_l, mod_l, wa, wb, wc, wo, ln_g, ln_b]
    out_specs = [pl.BlockSpec((tm, d), lambda i: (i, 0)), pl.BlockSpec((tm, d), lambda i: (i, 0))]
    out_shape = [jax.ShapeDtypeStruct((t, d), F32), jax.ShapeDtypeStruct((t, d), BF16)]
    if w_router is not None:
        in_specs.append(full(w_router.shape))
        args.append(w_router)
        out_specs.append(pl.BlockSpec((N_EXPERTS, tm), lambda i: (0, i)))
        out_shape.append(jax.ShapeDtypeStruct((N_EXPERTS, t), F32))
    return pl.pallas_call(
        functools.partial(_merge_kernel, alpha, n_ctx_tiles, w_router is not None),
        grid=(t // tm,), in_specs=in_specs, out_specs=out_specs, out_shape=out_shape,
        compiler_params=_cparams(("arbitrary",)), name="merge",
    )(*args)


def _ffn_kernel(alpha, n_ctx_tiles, n_split, h_ref, x1_ref, g2_ref, wg_ref, wu_ref, wd_ref, lg_ref, lb_ref,
                oc_ref, ol_ref):
    h = h_ref[...]
    tf = wg_ref.shape[1] // n_split
    y = None
    for f in range(n_split):
        cols = slice(f * tf, (f + 1) * tf)
        act = _silu(_dot(h, wg_ref[:, cols])) * _dot(h, wu_ref[:, cols])
        part = _dot(act.astype(BF16), wd_ref[cols, :])
        y = part if y is None else y + part
    out = _layer_norm(alpha * x1_ref[...] + g2_ref[...] * y, lg_ref[...], lb_ref[...])
    is_ctx = pl.program_id(0) < n_ctx_tiles

    @pl.when(is_ctx)
    def _():
        oc_ref[...] = out

    @pl.when(jnp.logical_not(is_ctx))
    def _():
        ol_ref[...] = out


def _dense_mixer(h2, x1, mod_l, wg, wu, wd, ln_g, ln_b, alpha, tm, n_split, row_of_tile, t_ctx):
    t, d = x1.shape
    n_ctx_tiles = t_ctx // tm
    const = lambda shape: pl.BlockSpec(shape, lambda i: tuple(0 for _ in shape), pipeline_mode=pl.Buffered(1))
    return pl.pallas_call(
        functools.partial(_ffn_kernel, alpha, n_ctx_tiles, n_split),
        grid=(t // tm,),
        in_specs=[
            pl.BlockSpec((tm, d), lambda i: (i, 0)),
            pl.BlockSpec((tm, d), lambda i: (i, 0)),
            _mod_spec(5, row_of_tile),
            const(wg.shape), const(wu.shape), const(wd.shape), const((1, d)), const((1, d)),
        ],
        out_specs=_pair_specs(tm, d, n_ctx_tiles),
        out_shape=[jax.ShapeDtypeStruct((t_ctx, d), F32), jax.ShapeDtypeStruct((t - t_ctx, d), F32)],
        compiler_params=_cparams(("arbitrary",)), name="dense_mixer",
    )(h2, x1, mod_l, wg, wu, wd, ln_g, ln_b)


MOE_TILE = 256
MOE_TF = 1792


def _route_kernel(g_ref, rank_ref, gate_ref, before_ref, run_s):
    w = pl.program_id(0)

    @pl.when(w == 0)
    def _():
        run_s[...] = jnp.zeros_like(run_s)

    g = g_ref[...]
    sel = g > 0.0
    row = lax.broadcasted_iota(jnp.int32, (MOE_TILE, MOE_TILE), 0)
    col = lax.broadcasted_iota(jnp.int32, (MOE_TILE, MOE_TILE), 1)
    earlier = (row < col).astype(BF16)
    ones = jnp.where(sel, 1.0, 0.0)
    rank = _dot(ones.astype(BF16), earlier) + run_s[:, 0:1]
    rank_ref[...] = jnp.where(sel, rank, -1.0)
    gate_ref[...] = g
    before_ref[...] = run_s[...]
    run_s[...] += jnp.sum(ones, axis=1, keepdims=True)


def _route(gates_t):
    t = gates_t.shape[1]
    nw = t // MOE_TILE
    return pl.pallas_call(
        _route_kernel,
        grid=(nw,),
        in_specs=[pl.BlockSpec((N_EXPERTS, MOE_TILE), lambda w: (0, w))],
        out_specs=[
            pl.BlockSpec((None, N_EXPERTS, MOE_TILE), lambda w: (w, 0, 0)),
            pl.BlockSpec((None, N_EXPERTS, MOE_TILE), lambda w: (w, 0, 0)),
            pl.BlockSpec((None, N_EXPERTS, LANES), lambda w: (w, 0, 0)),
        ],
        out_shape=[
            jax.ShapeDtypeStruct((nw, N_EXPERTS, MOE_TILE), F32),
            jax.ShapeDtypeStruct((nw, N_EXPERTS, MOE_TILE), F32),
            jax.ShapeDtypeStruct((nw, N_EXPERTS, LANES), F32),
        ],
        scratch_shapes=[pltpu.VMEM((N_EXPERTS, LANES), F32)],
        compiler_params=_cparams(("arbitrary",)), name="moe_route",
    )(gates_t)


def _tile_onehot(rank_ref, w, e, base):
    rank_row = rank_ref[w, pl.ds(e, 1), :]
    rows = lax.broadcasted_iota(jnp.int32, (MOE_TILE, MOE_TILE), 0).astype(F32) + base.astype(F32)
    return rank_row == rows


def _gather_kernel(te_ref, base_ref, wlo_ref, whi_ref, rank_ref, x_ref, o_ref):
    j = pl.program_id(0)
    e, base = te_ref[j], base_ref[j]
    o_ref[...] = jnp.zeros_like(o_ref)

    def window(w, carry):
        p = jnp.where(_tile_onehot(rank_ref, w, e, base), 1.0, 0.0).astype(BF16)
        x_w = x_ref[pl.ds(pl.multiple_of(w * MOE_TILE, MOE_TILE), MOE_TILE), :]
        o_ref[...] += _dot(p, x_w).astype(BF16)
        return carry

    lax.fori_loop(wlo_ref[j], whi_ref[j] + 1, window, 0)


def _moe_gather(meta, rank_t, h2, n_tiles):
    t, d = h2.shape
    nw = t // MOE_TILE
    return pl.pallas_call(
        _gather_kernel,
        grid_spec=pltpu.PrefetchScalarGridSpec(
            num_scalar_prefetch=4, grid=(n_tiles,),
            in_specs=[
                pl.BlockSpec((nw, N_EXPERTS, MOE_TILE), lambda j, *_: (0, 0, 0)),
                pl.BlockSpec((t, d), lambda j, *_: (0, 0), pipeline_mode=pl.Buffered(1)),
            ],
            out_specs=pl.BlockSpec((MOE_TILE, d), lambda j, *_: (j, 0)),
        ),
        out_shape=jax.ShapeDtypeStruct((n_tiles * MOE_TILE, d), BF16),
        compiler_params=_cparams(("arbitrary",)), name="moe_gather",
    )(*meta, rank_t, h2)


F8 = jnp.float8_e4m3fn
F8_TARGET = 224.0
F8_TINY = 1e-30


def _f8_quantize(x):
    m = jnp.max(jnp.max(jnp.abs(x), axis=1, keepdims=True), axis=0, keepdims=True)
    m = jnp.maximum(m, F8_TINY)
    return (x * (F8_TARGET / m)).astype(F8), m * (1.0 / F8_TARGET)


def _moe_up_kernel(te_ref, chg_ref, nused_ref, x_ref, wg_ref, wu_ref, o_ref, wg_s, wu_s, inv_s):
    j = pl.program_id(1)

    @pl.when(chg_ref[j] == 1)
    def _():
        for row, (w_ref, w_s) in enumerate(((wg_ref, wg_s), (wu_ref, wu_s))):
            w_s[...], inv = _f8_quantize(w_ref[...])
            inv_s[row:row + 1, :] = jnp.broadcast_to(inv, (1, LANES))

    @pl.when(j < nused_ref[0])
    def _():
        x8, inv_x = _f8_quantize(x_ref[...].astype(F32))
        a = _dot(x8, wg_s[...]) * (inv_x * inv_s[0:1, 0:1])
        u = _dot(x8, wu_s[...]) * (inv_x * inv_s[1:2, 0:1])
        o_ref[...] = (_silu(a) * u).astype(BF16)

    @pl.when(j >= nused_ref[0])
    def _():
        o_ref[...] = jnp.zeros_like(o_ref)


def _moe_up(te, chg, nused, xs, wg, wu):
    rows, d = xs.shape
    n_tiles = rows // MOE_TILE
    ff = wg.shape[2]
    return pl.pallas_call(
        _moe_up_kernel,
        grid_spec=pltpu.PrefetchScalarGridSpec(
            num_scalar_prefetch=3, grid=(ff // MOE_TF, n_tiles),
            in_specs=[
                pl.BlockSpec((MOE_TILE, d), lambda f, j, *_: (j, 0)),
                pl.BlockSpec((None, d, MOE_TF), lambda f, j, te, *_: (te[j], 0, f)),
                pl.BlockSpec((None, d, MOE_TF), lambda f, j, te, *_: (te[j], 0, f)),
            ],
            out_specs=pl.BlockSpec((MOE_TILE, MOE_TF), lambda f, j, *_: (j, f)),
            scratch_shapes=[pltpu.VMEM((d, MOE_TF), F8), pltpu.VMEM((d, MOE_TF), F8),
                            pltpu.VMEM((SUBLANES, LANES), F32)],
        ),
        out_shape=jax.ShapeDtypeStruct((rows, ff), BF16),
        compiler_params=_cparams(("arbitrary", "arbitrary")), name="moe_up",
    )(te, chg, nused, xs, wg, wu)


def _moe_down_kernel(te_ref, chg_ref, nused_ref, a_ref, wd_ref, o_ref, wd_s, inv_s):
    j = pl.program_id(0)

    @pl.when(chg_ref[j] == 1)
    def _():
        wd_s[...], inv = _f8_quantize(wd_ref[...])
        inv_s[0:1, :] = jnp.broadcast_to(inv, (1, LANES))

    @pl.when(j < nused_ref[0])
    def _():
        a8, inv_a = _f8_quantize(a_ref[...].astype(F32))
        o_ref[...] = (_dot(a8, wd_s[...]) * (inv_a * inv_s[0:1, 0:1])).astype(BF16)

    @pl.when(j >= nused_ref[0])
    def _():
        o_ref[...] = jnp.zeros_like(o_ref)


def _moe_down(te, chg, nused, act, wd):
    rows, ff = act.shape
    d = wd.shape[2]
    return pl.pallas_call(
        _moe_down_kernel,
        grid_spec=pltpu.PrefetchScalarGridSpec(
            num_scalar_prefetch=3, grid=(rows // MOE_TILE,),
            in_specs=[
                pl.BlockSpec((MOE_TILE, ff), lambda j, *_: (j, 0)),
                pl.BlockSpec((None, ff, d), lambda j, te, *_: (te[j], 0, 0)),
            ],
            out_specs=pl.BlockSpec((MOE_TILE, d), lambda j, *_: (j, 0)),
            scratch_shapes=[pltpu.VMEM((ff, d), F8), pltpu.VMEM((SUBLANES, LANES), F32)],
        ),
        out_shape=jax.ShapeDtypeStruct((rows, d), BF16),
        compiler_params=_cparams(("arbitrary",)), name="moe_down",
    )(te, chg, nused, act, wd)


def _combine_kernel(alpha, n_tiles, nw_ctx, te_ref, base_ref, wlo_ref, whi_ref, rank_ref, gate_ref, y_ref, x1_ref,
                    g2_ref, lg_ref, lb_ref, oc_ref, ol_ref, acc_s):
    s = pl.program_id(0)

    @pl.when(s == 0)
    def _():
        acc_s[...] = jnp.zeros_like(acc_s)

    @pl.when(s < n_tiles)
    def _():
        e, base = te_ref[s], base_ref[s]

        def window(w, carry):
            hit = _tile_onehot(rank_ref, w, e, base)
            q = jnp.where(hit, gate_ref[w, pl.ds(e, 1), :], 0.0).T.astype(BF16)
            rows = pl.ds(pl.multiple_of(w * MOE_TILE, MOE_TILE), MOE_TILE)
            acc_s[rows, :] += _dot(q, y_ref[...])
            return carry

        lax.fori_loop(wlo_ref[s], whi_ref[s] + 1, window, 0)

    @pl.when(s >= n_tiles)
    def _():
        w = s - n_tiles
        rows = pl.ds(pl.multiple_of(w * MOE_TILE, MOE_TILE), MOE_TILE)
        out = _layer_norm(alpha * x1_ref[...] + g2_ref[...] * acc_s[rows, :], lg_ref[...], lb_ref[...])

        @pl.when(w < nw_ctx)
        def _():
            oc_ref[...] = out

        @pl.when(w >= nw_ctx)
        def _():
            ol_ref[...] = out


def _moe_combine(meta, rank_t, gate_t, y, x1, mod_l, ln_g, ln_b, alpha, row_of_tile, t_ctx):
    t, d = x1.shape
    nw = t // MOE_TILE
    nw_ctx = t_ctx // MOE_TILE
    n_tiles = y.shape[0] // MOE_TILE
    win = lambda s: jnp.maximum(s - n_tiles, 0)
    return pl.pallas_call(
        functools.partial(_combine_kernel, alpha, n_tiles, nw_ctx),
        grid_spec=pltpu.PrefetchScalarGridSpec(
            num_scalar_prefetch=4, grid=(n_tiles + nw,),
            in_specs=[
                pl.BlockSpec((nw, N_EXPERTS, MOE_TILE), lambda s, *_: (0, 0, 0)),
                pl.BlockSpec((nw, N_EXPERTS, MOE_TILE), lambda s, *_: (0, 0, 0)),
                pl.BlockSpec((MOE_TILE, d), lambda s, *_: (jnp.minimum(s, n_tiles - 1), 0)),
                pl.BlockSpec((MOE_TILE, d), lambda s, *_: (win(s), 0)),
                pl.BlockSpec((None, None, 1, d), lambda s, *_: (row_of_tile(win(s)), 5, 0, 0)),
                pl.BlockSpec((1, d), lambda s, *_: (0, 0)),
                pl.BlockSpec((1, d), lambda s, *_: (0, 0)),
            ],
            out_specs=[
                pl.BlockSpec((MOE_TILE, d), lambda s, *_: (jnp.minimum(win(s), nw_ctx - 1), 0)),
                pl.BlockSpec((MOE_TILE, d), lambda s, *_: (jnp.maximum(win(s) - nw_ctx, 0), 0)),
            ],
            scratch_shapes=[pltpu.VMEM((t, d), F32)],
        ),
        out_shape=[jax.ShapeDtypeStruct((t_ctx, d), F32), jax.ShapeDtypeStruct((t - t_ctx, d), F32)],
        compiler_params=_cparams(("arbitrary",)), name="moe_combine",
    )(*meta, rank_t, gate_t, y, x1, mod_l, ln_g, ln_b)


def _moe_layer(gates, h2, x1, mod_l, wg, wu, wd, ln_g, ln_b, alpha, row_of_tile, t_ctx):
    t = h2.shape[0]
    n_tiles = -(-(2 * t + N_EXPERTS * (MOE_TILE - 1)) // MOE_TILE)
    rank_t, gate_t, before = _route(gates)
    before = before[:, :, 0].astype(jnp.int32)
    counts = jnp.sum(gates > 0.0, axis=1).astype(jnp.int32)
    tiles_e = (counts + MOE_TILE - 1) // MOE_TILE
    tile_end = jnp.cumsum(tiles_e)
    n_used = tile_end[-1]
    j = jnp.arange(n_tiles, dtype=jnp.int32)
    te = jnp.minimum(jnp.sum(j[:, None] >= tile_end[None, :], axis=1), N_EXPERTS - 1).astype(jnp.int32)
    base = (j - jnp.take(tile_end - tiles_e, te)) * MOE_TILE
    last = jnp.minimum(base + MOE_TILE - 1, jnp.take(counts, te) - 1)
    before_t = jnp.take(before, te, axis=1)
    used = j < n_used
    wlo = jnp.where(used, jnp.sum(before_t <= base[None, :], axis=0) - 1, 1).astype(jnp.int32)
    whi = jnp.where(used, jnp.sum(before_t <= last[None, :], axis=0) - 1, 0).astype(jnp.int32)
    chg = jnp.concatenate([jnp.ones((1,), jnp.int32), (te[1:] != te[:-1]).astype(jnp.int32)])
    nused = n_used.reshape(1).astype(jnp.int32)
    meta = (te, base.astype(jnp.int32), wlo, whi)
    xs = _moe_gather(meta, rank_t, h2, n_tiles)
    act = _moe_up(te, chg, nused, xs, wg, wu)
    y = _moe_down(te, chg, nused, act, wd)
    return _moe_combine(meta, rank_t, gate_t, y, x1, mod_l, ln_g, ln_b, alpha, row_of_tile, t_ctx)


def _rope_tables(seq):
    half = GLA_DK // 2
    n = half // 2
    t = np.arange(seq)
    inv = ROPE_BASE ** (-np.arange(n, dtype=np.float64) / n)
    cos = np.zeros((seq, GLA_DK), np.float64)
    sin = np.zeros((seq, GLA_DK), np.float64)
    for a, pos in enumerate((t // GRID_W, t % GRID_W)):
        ang = pos[:, None].astype(np.float64) * inv[None, :]
        base = a * half
        cos[:, base:base + n] = np.cos(ang)
        cos[:, base + n:base + half] = np.cos(ang)
        sin[:, base:base + n] = -np.sin(ang)
        sin[:, base + n:base + half] = np.sin(ang)
    tile = lambda m: jnp.asarray(np.tile(m, (1, GLA_HEADS)), F32)
    return tile(cos), tile(sin)


W_IN_GD = 1536
W_IN_KV = (2080, 2592)
W_IN_CHUNKS = (0, 512, 1024, 1568, 3104, 3616, 4128, 4640, 5152, 5664, 6176, 6688)


def _chunk_w_in(w):
    bf = lambda a: a.astype(BF16)
    wt = jnp.swapaxes(w, 1, 2)
    gd = jnp.pad(wt[:, W_IN_GD:W_IN_GD + 2 * GLA_RANK], ((0, 0), (0, LANES - 2 * GLA_RANK), (0, 0)))
    kv = jnp.stack([wt[:, s:s + CHUNK_W] for s in W_IN_KV], axis=1)
    return bf(gd), bf(kv), bf(jnp.stack([wt[:, s:s + CHUNK_W] for s in W_IN_CHUNKS], axis=1))


def _gate_up_blocks(w_gup):
    qkw = w_gup.shape[-1]
    fwd = jnp.pad(w_gup[:, 0], ((0, 0), (0, 0), (0, qkw)))
    bwd = jnp.pad(w_gup[:, 1], ((0, 0), (0, 0), (qkw, 0)))
    return jnp.pad(jnp.concatenate([fwd, bwd], axis=1), ((0, 0), (0, LANES - 2 * GLA_RANK), (0, 0)))


def kernel(x_prompt, x_sample, cache_na_k, cache_na_v, state_gla, c, c_ctx, w_mod, b_mod, w_in, gla_w_gup, gla_b_g, gla_norm, w_br_gla, na_rpb, w_br_na, cv_w, cv_b, cv_ln_g, cv_ln_b, w_br_cv, w_out, ln_g, ln_b, ffd_w_gate, ffd_w_up, ffd_w_down, moe_w_router, moe_w_gate, moe_w_up, moe_w_down):
    nb_c, l_c, d = x_prompt.shape
    nb_l, l_l, _ = x_sample.shape
    depth = w_mod.shape[0]
    t_ctx, t_lat = nb_c * l_c, nb_l * l_l
    t_all = t_ctx + t_lat
    alpha = (2 * depth) ** 0.25
    tm = 512
    row_of_tile = _row_of_tile_fn(t_ctx, l_l, tm)

    x_pair = (x_prompt.reshape(t_ctx, d), x_sample.reshape(t_lat, d))
    cond8 = jnp.concatenate([c_ctx[None, :], c, jnp.zeros((8 - 1 - nb_l, d), F32)], axis=0)
    mod = _modulation(cond8, w_mod, b_mod).reshape(depth, 8, 6, 1, d)
    rope = _rope_tables(l_l)
    rows_lat = l_l // GRID_W
    kr = min(NA_WIN_R, rows_lat)
    s0_all = state_gla.reshape(nb_l, depth, 2, 2, LANES, GLA_DV)
    w_gd, w_kv, w_chunks = _chunk_w_in(w_in)
    wg_blks = _gate_up_blocks(gla_w_gup)
    toep = _na_bias_tables(na_rpb)
    w_br = [w.astype(BF16) for w in (w_br_gla, w_br_na, w_br_cv, w_out)]

    caches = None
    ss = []
    for l in range(depth):
        mod_l = mod[l]
        gd, kc, vc, kl, vl, zb = _input_projection(x_pair, mod_l, (w_gd[l], w_kv[l], w_chunks[l]), l, depth, l_c,
                                                   tm, row_of_tile, caches)
        caches = (kc, vc)

        wg_blk = wg_blks[l]
        bg = gla_b_g[l].reshape(1, CHUNK_W)
        gn = gla_norm[l].reshape(1, CHUNK_W)
        ya, s_fin = _gla_branch(gd, zb, wg_blk, bg, gn, latent=False, seq=l_c, nb=nb_c, tok0=0, t_total=t_all)
        ya = _gla_branch(gd, zb, wg_blk, bg, gn, latent=True, seq=l_l, nb=nb_l, tok0=t_ctx, t_total=t_all,
                         rope=rope, s0=s0_all[:, l], ya_prev=ya)

        yb = _attn_context(zb, kc, vc, l, nb_c, l_c, t_all)
        yb = _attn_latent(zb, kl, vl, cache_na_k, cache_na_v, l, toep, yb, nb_l, l_l, t_ctx)

        conv_args = (zb, cv_w[l], cv_b[l][None, :], cv_ln_g[l][None, :], cv_ln_b[l][None, :])
        yc = _conv_branch(*conv_args, seq=l_c, nb=nb_c, tok0=0, t_total=t_all)
        yc = _conv_branch(*conv_args, seq=l_l, nb=nb_l, tok0=t_ctx, t_total=t_all, prev=yc)

        moe = l % 2 == 1
        w_router = None
        if moe:
            w_router = moe_w_router[l // 2].T
        merged = _merge(ya, yb, yc, zb, x_pair, mod_l, w_br[0][l], w_br[1][l], w_br[2][l], w_br[3][l],
                        ln_g[l, 0][None, :], ln_b[l, 0][None, :], alpha, tm, row_of_tile, w_router)
        ln2 = (ln_g[l, 1][None, :], ln_b[l, 1][None, :])
        if moe:
            x1, h2, gates = merged
            x_pair = _moe_layer(gates, h2, x1, mod_l, moe_w_gate[l // 2], moe_w_up[l // 2], moe_w_down[l // 2],
                                *ln2, alpha, _row_of_tile_fn(t_ctx, l_l, MOE_TILE), t_ctx)
        else:
            x1, h2 = merged
            i = l // 2
            x_pair = _dense_mixer(h2, x1, mod_l, ffd_w_gate[i].astype(BF16), ffd_w_up[i].astype(BF16),
                                  ffd_w_down[i].astype(BF16), *ln2, alpha, tm, 2, row_of_tile, t_ctx)
        ss.append(s_fin.reshape(nb_c, 2, GLA_HEADS, GLA_DK, GLA_DV))

    y_prompt = x_pair[0].reshape(nb_c, l_c, d)
    y_sample = x_pair[1].reshape(nb_l, l_l, d)
    cache_shape = (nb_c, depth, l_c, NA_HEADS, NA_DH)
    return (y_prompt, y_sample, caches[0].reshape(cache_shape), caches[1].reshape(cache_shape), jnp.stack(ss, 1))


def _row_of_tile_fn(t_ctx, l_lat, tm):
    n_ctx = t_ctx // tm

    def row_of_tile(i):
        return jnp.where(i < n_ctx, 0, 1 + (i - n_ctx) // (l_lat // tm))

    return row_of_tile
```

```python
import functools

import numpy as np
import jax
import jax.numpy as jnp
from jax import lax
from jax.experimental import pallas as pl
from jax.experimental.pallas import tpu as pltpu

F32 = jnp.float32
BF16 = jnp.bfloat16

GRID_W = 64
GLA_HEADS = 4
GLA_DK = 64
GLA_DV = 128
GLA_RANK = 16
GLA_TAU = 16.0
GLA_CHUNK = 64
GLA_GROUP = 256
NA_HEADS = 8
NA_DH = 64
NA_WIN_R = 8
NA_WIN_C = 16
CONV_CH = 512
CONV_K = 31
N_EXPERTS = 8
ROPE_BASE = 10000.0
LN_EPS = 1e-5

D_MODEL = 1024
LANES = 128
SUBLANES = 8
CHUNK_W = 512
N_BF16_CHUNKS = 12
NEG_BIG = -1e30
VMEM_LIMIT = 56 * 1024 * 1024


def _cparams(sem, vmem=VMEM_LIMIT):
    return pltpu.CompilerParams(dimension_semantics=sem, vmem_limit_bytes=vmem)


def _dot(a, b):
    return jnp.dot(a, b, preferred_element_type=F32)


def _split(x):
    hi = x.astype(BF16)
    return hi, (x - hi.astype(F32)).astype(BF16)


def _dot_split(a, b):
    a_hi, a_lo = _split(a)
    b_hi, b_lo = _split(b)
    return _dot(a_hi, b_hi) + _dot(a_lo, b_hi) + _dot(a_hi, b_lo)


def _dot_split_rhs(a, b):
    b_hi, b_lo = _split(b)
    return _dot(a, b_hi) + _dot(a, b_lo)


def _dot_nt(a, b):
    return lax.dot_general(a, b, (((1,), (1,)), ((), ())), preferred_element_type=F32)


def _sigmoid(x):
    return 0.5 * jnp.tanh(0.5 * x) + 0.5


def _silu(x):
    return x * _sigmoid(x)


def _layer_norm(x, g, b):
    mu = jnp.mean(x, axis=-1, keepdims=True)
    xc = x - mu
    var = jnp.mean(xc * xc, axis=-1, keepdims=True)
    return xc * lax.rsqrt(var + LN_EPS) * g + b


def _mod_kernel(c_ref, w_ref, b_ref, o_ref):
    c = c_ref[...]
    s = _silu(c).astype(BF16)
    o_ref[...] = _dot(s, w_ref[...].astype(BF16)) + b_ref[...]


def _modulation(cond8, w_mod, b_mod):
    depth, d, n = w_mod.shape
    tn = 1024
    return pl.pallas_call(
        _mod_kernel,
        grid=(depth, n // tn),
        in_specs=[
            pl.BlockSpec((8, d), lambda l, j: (0, 0)),
            pl.BlockSpec((None, d, tn), lambda l, j: (l, 0, j)),
            pl.BlockSpec((None, 1, tn), lambda l, j: (l, 0, j)),
        ],
        out_specs=pl.BlockSpec((None, 8, tn), lambda l, j: (l, 0, j)),
        out_shape=jax.ShapeDtypeStruct((depth, 8, n), F32),
        compiler_params=_cparams(("arbitrary", "arbitrary")),
        name="modulation",
    )(cond8, w_mod, b_mod.reshape(depth, 1, n))


def _inproj_kernel(n_ctx_tiles, aliased, *refs):
    if aliased:
        (xc_ref, xl_ref, sc_ref, sh_ref, wgd_ref, wkv_ref, w_ref, _kc_in, _vc_in,
         gd_ref, kc_ref, vc_ref, kl_ref, vl_ref, zb_ref, h_ref) = refs
    else:
        (xc_ref, xl_ref, sc_ref, sh_ref, wgd_ref, wkv_ref, w_ref,
         gd_ref, kc_ref, vc_ref, kl_ref, vl_ref, zb_ref, h_ref) = refs
    is_ctx = pl.program_id(0) < n_ctx_tiles
    x = jnp.where(is_ctx, xc_ref[...], xl_ref[...])
    h_ref[...] = (x * (1.0 + sc_ref[...]) + sh_ref[...]).astype(BF16)
    gd_ref[...] = _dot_nt(h_ref[...], wgd_ref[...])
    k = _dot_nt(h_ref[...], wkv_ref[0])
    v = _dot_nt(h_ref[...], wkv_ref[1])
    seqs, l_c, _ = kc_ref.shape

    @pl.when(is_ctx)
    def _():
        for s in range(seqs):
            kc_ref[s] = k[s * l_c:(s + 1) * l_c]
            vc_ref[s] = v[s * l_c:(s + 1) * l_c]

    @pl.when(jnp.logical_not(is_ctx))
    def _():
        kl_ref[...] = k
        vl_ref[...] = v

    for j in range(N_BF16_CHUNKS):
        zb_ref[j] = _dot_nt(h_ref[...], w_ref[j]).astype(BF16)


def _mod_spec(piece, row_of_tile):
    return pl.BlockSpec((None, None, 1, D_MODEL), lambda i: (row_of_tile(i), piece, 0, 0))


def _pair_specs(tm, d, n_ctx_tiles):
    return [pl.BlockSpec((tm, d), lambda i, *_: (jnp.minimum(i, n_ctx_tiles - 1), 0)),
            pl.BlockSpec((tm, d), lambda i, *_: (jnp.maximum(i - n_ctx_tiles, 0), 0))]


def _input_projection(x_pair, mod_l, w_parts, layer, depth, l_c, tm, row_of_tile, caches=None):
    xc, xl = x_pair
    w_gd, w_kv, w_chunks = w_parts
    d = xc.shape[1]
    t_ctx, t_lat = xc.shape[0], xl.shape[0]
    t = t_ctx + t_lat
    n_ctx_tiles = t_ctx // tm
    seqs = tm // l_c
    cache_spec = pl.BlockSpec((seqs, None, l_c, CHUNK_W),
                              lambda i: (jnp.minimum(i, n_ctx_tiles - 1), layer, 0, 0))
    lat_spec = pl.BlockSpec((tm, CHUNK_W), lambda i: (jnp.maximum(i - n_ctx_tiles, 0), 0))
    const = lambda a: pl.BlockSpec((None,) + a.shape[1:], lambda i: (layer,) + (0,) * (a.ndim - 1),
                                   pipeline_mode=pl.Buffered(1))
    in_specs = _pair_specs(tm, d, n_ctx_tiles) + [
        _mod_spec(1, row_of_tile), _mod_spec(0, row_of_tile),
        const(w_gd), const(w_kv), const(w_chunks),
    ]
    args = [xc, xl, mod_l, mod_l, w_gd, w_kv, w_chunks]
    aliases = {}
    if caches is not None:
        in_specs += [pl.BlockSpec(memory_space=pl.ANY), pl.BlockSpec(memory_space=pl.ANY)]
        args += list(caches)
        aliases = {7: 1, 8: 2}
    cache_shape = jax.ShapeDtypeStruct((t_ctx // l_c, depth, l_c, CHUNK_W), F32)
    return pl.pallas_call(
        functools.partial(_inproj_kernel, n_ctx_tiles, caches is not None),
        grid=(t // tm,),
        in_specs=in_specs,
        out_specs=[
            pl.BlockSpec((tm, LANES), lambda i: (i, 0)),
            cache_spec, cache_spec, lat_spec, lat_spec,
            pl.BlockSpec((N_BF16_CHUNKS, tm, CHUNK_W), lambda i: (0, i, 0)),
        ],
        out_shape=[
            jax.ShapeDtypeStruct((t, LANES), F32),
            cache_shape, cache_shape,
            jax.ShapeDtypeStruct((t_lat, CHUNK_W), F32), jax.ShapeDtypeStruct((t_lat, CHUNK_W), F32),
            jax.ShapeDtypeStruct((N_BF16_CHUNKS, t, CHUNK_W), BF16),
        ],
        scratch_shapes=[pltpu.VMEM((tm, d), BF16)],
        input_output_aliases=aliases,
        compiler_params=_cparams(("arbitrary",)),
        name="input_projection",
    )(*args)


def _gla_kernel(latent, seq, state_alias, *refs):
    if latent:
        (qk_ref, v_ref, ra_ref, gd_ref, wg_ref, bg_ref, gn_ref, cos_ref, sin_ref, s0_ref, _alias,
         ya_ref, q_s, k_s, g_s, acc_s, st_s) = refs
    elif state_alias:
        (qk_ref, v_ref, ra_ref, gd_ref, wg_ref, bg_ref, gn_ref, _alias,
         ya_ref, sfin_ref, q_s, k_s, g_s, acc_s, st_s) = refs
    else:
        (qk_ref, v_ref, ra_ref, gd_ref, wg_ref, bg_ref, gn_ref,
         ya_ref, sfin_ref, q_s, k_s, g_s, acc_s, st_s) = refs
    c = GLA_CHUNK
    n_chunks = seq // c
    qkw = GLA_HEADS * GLA_DK

    q = qk_ref[:, 0:qkw].astype(F32) * (GLA_DK ** -0.5)
    k = qk_ref[:, qkw:2 * qkw].astype(F32)
    if latent:
        lane = lax.broadcasted_iota(jnp.int32, (1, qkw), 1)
        first_half = (lane % 32) < 16

        def rope(x):
            swapped = jnp.where(first_half, pltpu.roll(x, qkw - 16, 1), pltpu.roll(x, 16, 1))
            return x * cos_ref[...] + swapped * sin_ref[...]

        q, k = rope(q), rope(k)
    q_s[...] = q
    k_s[...] = k

    glog = _dot_split(gd_ref[...], wg_ref[...]) + bg_ref[...]
    g_s[...] = (jnp.minimum(glog, 0.0) - jnp.log(1.0 + jnp.exp(-jnp.abs(glog)))) * (1.0 / GLA_TAU)
    acc_s[...] = jnp.zeros_like(acc_s)
    if latent:
        st_s[...] = s0_ref[...]
    else:
        st_s[...] = jnp.zeros_like(st_s)

    grp = GLA_GROUP
    cpg = grp // c
    n_groups = seq // grp
    grow = lax.broadcasted_iota(jnp.int32, (grp, grp), 0)
    gcol = lax.broadcasted_iota(jnp.int32, (grp, grp), 1)
    same_chunk = (grow // c) == (gcol // c)
    keep = (same_chunk & (gcol <= grow), same_chunk & (gcol >= grow))
    tri = (keep[0].astype(BF16), keep[1].astype(BF16))
    lane128 = lax.broadcasted_iota(jnp.int32, (1, LANES), 1)
    head_mask = (lane128 < GLA_DK, lane128 >= GLA_DK)
    urow = lax.broadcasted_iota(jnp.int32, (cpg * LANES, GLA_DV), 0)
    srow = lax.broadcasted_iota(jnp.int32, (LANES, GLA_DV), 0)
    chunk_of_col = lax.broadcasted_iota(jnp.int32, (LANES, grp), 1) // c
    chunk_of_row = lax.broadcasted_iota(jnp.int32, (grp, LANES), 0) // c
    pad_rows = jnp.zeros((SUBLANES - cpg, qkw), F32)

    def group_step(i, carry):
        for d in range(2):
            gi = i if d == 0 else n_groups - 1 - i
            rows = pl.ds(pl.multiple_of(gi * grp, grp), grp)
            cum = _dot_split_rhs(tri[d], g_s[rows, d * qkw:(d + 1) * qkw])
            edge = c - 1 if d == 0 else 0
            lasts = [cum[ci * c + edge:ci * c + edge + 1, :] for ci in range(cpg)]
            last = jnp.concatenate([jnp.broadcast_to(r, (c, qkw)) for r in lasts], axis=0)
            ref = 0.5 * last
            q = q_s[rows, :]
            k = k_s[rows, :]
            qe = q * jnp.exp(cum - ref)
            ke = k * jnp.exp(ref - cum)
            qe2 = q * jnp.exp(cum)
            ke2 = k * jnp.exp(last - cum)
            last8 = jnp.concatenate(lasts + [pad_rows], axis=0)
            for p in range(2):
                sl = slice(p * LANES, (p + 1) * LANES)
                ke_p = ke[:, sl].astype(BF16)
                ke2_t = ke2[:, sl].T
                ke2_blk = jnp.concatenate(
                    [jnp.where(chunk_of_col == ci, ke2_t, 0.0) for ci in range(cpg)], axis=0).astype(BF16)
                dec = jnp.exp(last8[:, sl]).T
                psl = slice(2 * p * GLA_DV, (2 * p + 2) * GLA_DV)
                v_pair = v_ref[rows, psl]
                u = _dot(ke2_blk, v_pair)
                upd = jnp.where(urow % LANES < GLA_DK, u[:, :GLA_DV], u[:, GLA_DV:])
                s = st_s[d, p]
                s_in = [None] * cpg
                for ci in (range(cpg) if d == 0 else range(cpg - 1, -1, -1)):
                    s_in[ci] = jnp.concatenate(
                        [jnp.where(srow < GLA_DK, s, 0.0), jnp.where(srow >= GLA_DK, s, 0.0)], axis=1).astype(BF16)
                    s = dec[:, ci:ci + 1] * s + upd[ci * LANES:(ci + 1) * LANES]
                st_s[d, p] = s
                s_stack = jnp.concatenate(s_in, axis=0)
                a2_blk = jnp.concatenate(
                    [jnp.where(chunk_of_row == ci, qe2[:, sl], 0.0) for ci in range(cpg)], axis=1).astype(BF16)
                intra = []
                for hh in range(2):
                    a = jnp.where(head_mask[hh], qe[:, sl], 0.0).astype(BF16)
                    att = jnp.where(keep[d], _dot_nt(a, ke_p), 0.0).astype(BF16)
                    intra.append(_dot(att, v_pair[:, hh * GLA_DV:(hh + 1) * GLA_DV]))
                acc_s[rows, psl] += jnp.concatenate(intra, axis=1) + _dot(a2_blk, s_stack)
        return carry

    lax.fori_loop(0, n_groups, group_step, 0)

    def epilogue(i, carry):
        rows = pl.ds(pl.multiple_of(i * c, c), c)
        for h in range(GLA_HEADS):
            vsl = slice(h * GLA_DV, (h + 1) * GLA_DV)
            o = acc_s[rows, vsl]
            o = o * lax.rsqrt(jnp.mean(o * o, axis=-1, keepdims=True) + LN_EPS) * gn_ref[:, vsl]
            ya_ref[rows, vsl] = (o * _silu(ra_ref[rows, vsl].astype(F32))).astype(BF16)
        return carry

    lax.fori_loop(0, n_chunks, epilogue, 0)
    if not latent:
        sfin_ref[...] = st_s[...]


def _gla_branch(gd, zb, wg_blk, bg, gn, *, latent, seq, nb, tok0, t_total, rope=None, s0=None, ya_prev=None,
                state_slot=None):
    b0 = tok0 // seq
    qkw = GLA_HEADS * GLA_DK
    in_specs = [
        pl.BlockSpec((None, seq, CHUNK_W), lambda b: (0, b0 + b, 0)),
        pl.BlockSpec((None, seq, CHUNK_W), lambda b: (1, b0 + b, 0)),
        pl.BlockSpec((None, seq, CHUNK_W), lambda b: (2, b0 + b, 0)),
        pl.BlockSpec((seq, LANES), lambda b: (b0 + b, 0)),
        pl.BlockSpec((LANES, CHUNK_W), lambda b: (0, 0)),
        pl.BlockSpec((1, CHUNK_W), lambda b: (0, 0)),
        pl.BlockSpec((1, CHUNK_W), lambda b: (0, 0)),
    ]
    args = [zb, zb, zb, gd, wg_blk, bg, gn]
    scratch = [
        pltpu.VMEM((seq, qkw), F32), pltpu.VMEM((seq, qkw), F32), pltpu.VMEM((seq, CHUNK_W), F32),
        pltpu.VMEM((seq, CHUNK_W), F32), pltpu.VMEM((2, 2, LANES, GLA_DV), F32),
    ]
    ya_shape = jax.ShapeDtypeStruct((t_total, CHUNK_W), BF16)
    ya_spec = pl.BlockSpec((seq, CHUNK_W), lambda b: (b0 + b, 0))
    if latent:
        cos_t, sin_t = rope
        in_specs += [
            pl.BlockSpec((seq, qkw), lambda b: (0, 0)),
            pl.BlockSpec((seq, qkw), lambda b: (0, 0)),
            pl.BlockSpec((None, 2, 2, LANES, GLA_DV), lambda b: (b, 0, 0, 0, 0)),
            pl.BlockSpec(memory_space=pl.ANY),
        ]
        args += [cos_t, sin_t, s0, ya_prev]
        return pl.pallas_call(
            functools.partial(_gla_kernel, True, seq, False),
            grid=(nb,), in_specs=in_specs, out_specs=ya_spec, out_shape=ya_shape,
            scratch_shapes=scratch, input_output_aliases={len(args) - 1: 0},
            compiler_params=_cparams(("arbitrary",)), name="gla_latent",
        )(*args)
    layer, depth, states = state_slot
    aliases = {}
    if states is not None:
        in_specs.append(pl.BlockSpec(memory_space=pl.ANY))
        args.append(states)
        aliases = {len(args) - 1: 1}
    return pl.pallas_call(
        functools.partial(_gla_kernel, False, seq, states is not None),
        grid=(nb,), in_specs=in_specs,
        out_specs=[ya_spec, pl.BlockSpec((None, None, 2, 2, LANES, GLA_DV), lambda b: (b, layer, 0, 0, 0, 0))],
        out_shape=[ya_shape, jax.ShapeDtypeStruct((nb, depth, 2, 2, LANES, GLA_DV), F32)],
        scratch_shapes=scratch, input_output_aliases=aliases,
        compiler_params=_cparams(("arbitrary",)), name="gla_context",
    )(*args)


def _attn_ctx_kernel(q_ref, k_ref, v_ref, o_ref):
    lane = lax.broadcasted_iota(jnp.int32, (1, LANES), 1)
    masks = (lane < NA_DH, lane >= NA_DH)
    scale = NA_DH ** -0.5
    seq = q_ref.shape[0]
    for p in range(NA_HEADS // 2):
        sl = slice(p * LANES, (p + 1) * LANES)
        q2 = _stack_heads(q_ref[:, sl], masks)
        kp = k_ref[:, sl].astype(BF16)
        vp = v_ref[:, sl].astype(BF16)
        s = _dot_nt(q2, kp) * scale
        e = jnp.exp(s - jnp.max(s, axis=-1, keepdims=True))
        o2 = _dot(e.astype(BF16), vp) / jnp.sum(e, axis=-1, keepdims=True)
        o_ref[:, sl] = jnp.where(masks[0], o2[0:seq], o2[seq:2 * seq]).astype(BF16)


def _stack_heads(qp, masks):
    zero = jnp.zeros_like(qp)
    return jnp.concatenate([jnp.where(masks[0], qp, zero), jnp.where(masks[1], qp, zero)], axis=0)


def _attn_context(zb, kc, vc, layer, nb, seq, t_total):
    return pl.pallas_call(
        _attn_ctx_kernel,
        grid=(nb,),
        in_specs=[
            pl.BlockSpec((None, seq, CHUNK_W), lambda b: (3, b, 0)),
            pl.BlockSpec((None, None, seq, CHUNK_W), lambda b: (b, layer, 0, 0)),
            pl.BlockSpec((None, None, seq, CHUNK_W), lambda b: (b, layer, 0, 0)),
        ],
        out_specs=pl.BlockSpec((seq, CHUNK_W), lambda b: (b, 0)),
        out_shape=jax.ShapeDtypeStruct((t_total, CHUNK_W), BF16),
        compiler_params=_cparams(("arbitrary",)), name="attention_context",
    )(zb, kc, vc)


def _na_kernel(rows_total, kr, q_ref, kl_ref, vl_ref, kc_ref, vc_ref, toep_ref, _alias, o_ref):
    r = pl.program_id(1)
    rs = jnp.clip(r - kr // 2, 0, rows_total - kr)
    krows = pl.ds(pl.multiple_of(rs * GRID_W, GRID_W), kr * GRID_W)
    dr0 = rs - r + NA_WIN_R - 1
    lane = lax.broadcasted_iota(jnp.int32, (1, LANES), 1)
    masks = (lane < NA_DH, lane >= NA_DH)
    scale = NA_DH ** -0.5
    nq = q_ref.shape[0]
    for p in range(NA_HEADS // 2):
        sl = slice(p * LANES, (p + 1) * LANES)
        q2 = _stack_heads(q_ref[:, sl], masks)
        klp = kl_ref[krows, sl].astype(BF16)
        vlp = vl_ref[krows, sl].astype(BF16)
        kcp = kc_ref[:, sl].astype(BF16)
        vcp = vc_ref[:, sl].astype(BF16)
        bias = jnp.concatenate(
            [jnp.concatenate([toep_ref[2 * p + hh, dr0 + 2 * m] for m in range(kr // 2)], axis=1)
             for hh in range(2)], axis=0)
        s_loc = _dot_nt(q2, klp) * scale + bias
        s_ctx = _dot_nt(q2, kcp) * scale
        m = jnp.maximum(jnp.max(s_loc, axis=-1, keepdims=True), jnp.max(s_ctx, axis=-1, keepdims=True))
        e_loc = jnp.exp(s_loc - m)
        e_ctx = jnp.exp(s_ctx - m)
        den = jnp.sum(e_loc, axis=-1, keepdims=True) + jnp.sum(e_ctx, axis=-1, keepdims=True)
        o2 = (_dot(e_loc.astype(BF16), vlp) + _dot(e_ctx.astype(BF16), vcp)) / den
        o_ref[:, sl] = jnp.where(masks[0], o2[0:nq], o2[nq:2 * nq]).astype(BF16)


def _na_bias_tables(rpb):
    nc = rpb.shape[-1]
    qc = np.arange(GRID_W)[:, None]
    kc = np.arange(GRID_W)[None, :]
    cs = np.clip(qc - NA_WIN_C // 2, 0, GRID_W - NA_WIN_C)
    valid = (kc >= cs) & (kc < cs + NA_WIN_C)
    tap = np.clip(kc - qc, -(NA_WIN_C - 1), NA_WIN_C - 1) + NA_WIN_C - 1
    pick = jnp.asarray(tap[..., None] == np.arange(nc), F32)
    toep = jnp.einsum('dhab,qkb->dhaqk', rpb.astype(F32), pick, precision=lax.Precision.HIGHEST)
    toep = jnp.where(valid, toep, NEG_BIG)
    return jnp.concatenate([toep[:, :, :-1], toep[:, :, 1:]], axis=-1)


def _attn_latent(zb, kl, vl, cache_k, cache_v, layer, toep, att_prev, nb, seq, tok0):
    rows_total = seq // GRID_W
    kr = min(NA_WIN_R, rows_total)
    q0 = tok0 // GRID_W
    past = cache_k.shape[2]
    ck = cache_k.reshape(cache_k.shape[0], cache_k.shape[1], past, NA_HEADS * NA_DH)
    cv = cache_v.reshape(ck.shape)
    return pl.pallas_call(
        functools.partial(_na_kernel, rows_total, kr),
        grid=(nb, rows_total),
        in_specs=[
            pl.BlockSpec((None, GRID_W, CHUNK_W), lambda b, r: (3, q0 + b * rows_total + r, 0)),
            pl.BlockSpec((seq, CHUNK_W), lambda b, r: (b, 0)),
            pl.BlockSpec((seq, CHUNK_W), lambda b, r: (b, 0)),
            pl.BlockSpec((None, None, past, CHUNK_W), lambda b, r: (b, layer, 0, 0)),
            pl.BlockSpec((None, None, past, CHUNK_W), lambda b, r: (b, layer, 0, 0)),
            pl.BlockSpec((None,) + toep.shape[1:], lambda b, r: (layer, 0, 0, 0, 0), pipeline_mode=pl.Buffered(1)),
            pl.BlockSpec(memory_space=pl.ANY),
        ],
        out_specs=pl.BlockSpec((GRID_W, CHUNK_W), lambda b, r: (q0 + b * rows_total + r, 0)),
        out_shape=jax.ShapeDtypeStruct(att_prev.shape, BF16),
        input_output_aliases={6: 0},
        compiler_params=_cparams(("arbitrary", "arbitrary")), name="attention_latent",
    )(zb, kl, vl, ck, cv, toep, att_prev)


CONV_PAD = 16
CONV_ROWS = 64


def _conv_kernel(seq, aliased, *refs):
    if aliased:
        a_ref, gt_ref, w_ref, cb_ref, lg_ref, lb_ref, _alias, o_ref, z_s, sh_s = refs
    else:
        a_ref, gt_ref, w_ref, cb_ref, lg_ref, lb_ref, o_ref, z_s, sh_s = refs
    z_s[0:CONV_PAD, :] = jnp.zeros((CONV_PAD, CONV_CH), F32)
    z_s[CONV_PAD + seq:2 * CONV_PAD + seq, :] = jnp.zeros((CONV_PAD, CONV_CH), F32)
    z_s[CONV_PAD:CONV_PAD + seq, :] = a_ref[...].astype(F32) * _sigmoid(gt_ref[...].astype(F32))
    ext = seq + 2 * CONV_PAD - SUBLANES
    for b in range(1, SUBLANES):
        for r0 in range(0, ext, CONV_ROWS):
            n = min(CONV_ROWS, ext - r0)
            sh_s[b - 1, r0:r0 + n, :] = z_s[r0 + b:r0 + b + n, :]
    off = CONV_PAD - CONV_K // 2
    grp = (CONV_ROWS // SUBLANES, SUBLANES, CONV_CH)
    for t0 in range(0, seq, CONV_ROWS):
        acc = jnp.zeros(grp, F32)
        for k in range(CONV_K):
            a, b = divmod(off + k, SUBLANES)
            r0 = t0 + a * SUBLANES
            tap = z_s[r0:r0 + CONV_ROWS, :] if b == 0 else sh_s[b - 1, r0:r0 + CONV_ROWS, :]
            acc = acc + tap.reshape(grp) * w_ref[k][None]
        y = _layer_norm(acc.reshape(CONV_ROWS, CONV_CH) + cb_ref[...], lg_ref[...], lb_ref[...])
        o_ref[t0:t0 + CONV_ROWS, :] = _silu(y).astype(BF16)


def _conv_branch(zb, cv_w, cv_b, ln_g, ln_b, *, seq, nb, tok0, t_total, prev=None):
    b0 = tok0 // seq
    in_specs = [
        pl.BlockSpec((None, seq, CHUNK_W), lambda b: (4, b0 + b, 0)),
        pl.BlockSpec((None, seq, CHUNK_W), lambda b: (5, b0 + b, 0)),
        pl.BlockSpec((CONV_K, SUBLANES, CONV_CH), lambda b: (0, 0, 0)),
        pl.BlockSpec((1, CONV_CH), lambda b: (0, 0)),
        pl.BlockSpec((1, CONV_CH), lambda b: (0, 0)),
        pl.BlockSpec((1, CONV_CH), lambda b: (0, 0)),
    ]
    args = [zb, zb, jnp.broadcast_to(cv_w[:, None, :], (CONV_K, SUBLANES, CONV_CH)), cv_b, ln_g, ln_b]
    aliases = {}
    if prev is not None:
        in_specs.append(pl.BlockSpec(memory_space=pl.ANY))
        args.append(prev)
        aliases = {6: 0}
    return pl.pallas_call(
        functools.partial(_conv_kernel, seq, prev is not None),
        grid=(nb,), in_specs=in_specs,
        out_specs=pl.BlockSpec((seq, CONV_CH), lambda b: (b0 + b, 0)),
        out_shape=jax.ShapeDtypeStruct((t_total, CONV_CH), BF16),
        scratch_shapes=[pltpu.VMEM((seq + 2 * CONV_PAD, CONV_CH), F32),
                        pltpu.VMEM((SUBLANES - 1, seq + 2 * CONV_PAD - SUBLANES, CONV_CH), F32)],
        input_output_aliases=aliases,
        compiler_params=_cparams(("arbitrary",)), name="conv_module_%d" % seq,
    )(*args)


def _merge_kernel(alpha, n_ctx_tiles, with_router, *refs):
    if with_router:
        (ya_ref, yb_ref, yc_ref, mg_ref, xc_ref, xl_ref, g1_ref, sc2_ref, sh2_ref, wa_ref, wb_ref, wc_ref, wo_ref,
         lg_ref, lb_ref, wr_ref, x1_ref, h2_ref, gates_ref) = refs
    else:
        (ya_ref, yb_ref, yc_ref, mg_ref, xc_ref, xl_ref, g1_ref, sc2_ref, sh2_ref, wa_ref, wb_ref, wc_ref, wo_ref,
         lg_ref, lb_ref, x1_ref, h2_ref) = refs
    x = jnp.where(pl.program_id(0) < n_ctx_tiles, xc_ref[...], xl_ref[...])
    halves = []
    for n in range(2):
        m = None
        for j, (y_ref, w_ref) in enumerate(((ya_ref, wa_ref), (yb_ref, wb_ref), (yc_ref, wc_ref))):
            proj = _dot(y_ref[...], w_ref[:, n * CHUNK_W:(n + 1) * CHUNK_W])
            term = _sigmoid(mg_ref[2 * j + n].astype(F32)) * proj
            m = term if m is None else m + term
        halves.append(m.astype(BF16))
    mix = _dot(halves[0], wo_ref[0:CHUNK_W, :]) + _dot(halves[1], wo_ref[CHUNK_W:2 * CHUNK_W, :])
    x1 = _layer_norm(alpha * x + g1_ref[...] * mix, lg_ref[...], lb_ref[...])
    x1_ref[...] = x1
    h2 = x1 * (1.0 + sc2_ref[...]) + sh2_ref[...]
    h2_ref[...] = h2.astype(BF16)
    if with_router:
        w_hi, w_lo = _split(wr_ref[...])
        h_hi, h_lo = _split(h2)
        lg = _dot_nt(w_hi, h_hi) + _dot_nt(w_hi, h_lo) + _dot_nt(w_lo, h_hi)
        eid = lax.broadcasted_iota(jnp.int32, lg.shape, 0)
        m1 = jnp.max(lg, axis=0, keepdims=True)
        i1 = jnp.min(jnp.where(lg == m1, eid, N_EXPERTS), axis=0, keepdims=True)
        lg2 = jnp.where(eid == i1, -jnp.inf, lg)
        m2 = jnp.max(lg2, axis=0, keepdims=True)
        i2 = jnp.min(jnp.where(lg2 == m2, eid, N_EXPERTS), axis=0, keepdims=True)
        e2 = jnp.exp(m2 - m1)
        w1 = 1.0 / (1.0 + e2)
        gates_ref[...] = jnp.where(eid == i1, w1, 0.0) + jnp.where(eid == i2, e2 * w1, 0.0)


def _merge(ya, yb, yc, zb, x_pair, mod_l, wa, wb, wc, wo, ln_g, ln_b, alpha, tm, row_of_tile, w_router=None):
    xc, xl = x_pair
    d = xc.shape[1]
    t = xc.shape[0] + xl.shape[0]
    n_ctx_tiles = xc.shape[0] // tm
    full = lambda shape: pl.BlockSpec(shape, lambda i: tuple(0 for _ in shape), pipeline_mode=pl.Buffered(1))
    in_specs = [
        pl.BlockSpec((tm, CHUNK_W), lambda i: (i, 0)),
        pl.BlockSpec((tm, CHUNK_W), lambda i: (i, 0)),
        pl.BlockSpec((tm, CHUNK_W), lambda i: (i, 0)),
        pl.BlockSpec((6, tm, CHUNK_W), lambda i: (1, i, 0)),
    ] + _pair_specs(tm, d, n_ctx_tiles) + [
        _mod_spec(2, row_of_tile), _mod_spec(4, row_of_tile), _mod_spec(3, row_of_tile),
        full(wa.shape), full(wb.shape), full(wc.shape), full(wo.shape), full((1, d)), full((1, d)),
    ]
    args = [ya, yb, yc, zb, xc, xl, mod_l, mod_l, mod_l, wa, wb, wc, wo, ln_g, ln_b]
    out_specs = [pl.BlockSpec((tm, d), lambda i: (i, 0)), pl.BlockSpec((tm, d), lambda i: (i, 0))]
    out_shape = [jax.ShapeDtypeStruct((t, d), F32), jax.ShapeDtypeStruct((t, d), BF16)]
    if w_router is not None:
        in_specs.append(full(w_router.shape))
        args.append(w_router)
        out_specs.append(pl.BlockSpec((N_EXPERTS, tm), lambda i: (0, i)))
        out_shape.append(jax.ShapeDtypeStruct((N_EXPERTS, t), F32))
    return pl.pallas_call(
        functools.partial(_merge_kernel, alpha, n_ctx_tiles, w_router is not None),
        grid=(t // tm,), in_specs=in_specs, out_specs=out_specs, out_shape=out_shape,
        compiler_params=_cparams(("arbitrary",)), name="merge",
    )(*args)


def _ffn_kernel(alpha, n_ctx_tiles, n_split, h_ref, x1_ref, g2_ref, wg_ref, wu_ref, wd_ref, lg_ref, lb_ref,
                oc_ref, ol_ref):
    h = h_ref[...]
    tf = wg_ref.shape[1] // n_split
    y = None
    for f in range(n_split):
        cols = slice(f * tf, (f + 1) * tf)
        act = _silu(_dot(h, wg_ref[:, cols])) * _dot(h, wu_ref[:, cols])
        part = _dot(act.astype(BF16), wd_ref[cols, :])
        y = part if y is None else y + part
    out = _layer_norm(alpha * x1_ref[...] + g2_ref[...] * y, lg_ref[...], lb_ref[...])
    is_ctx = pl.program_id(0) < n_ctx_tiles

    @pl.when(is_ctx)
    def _():
        oc_ref[...] = out

    @pl.when(jnp.logical_not(is_ctx))
    def _():
        ol_ref[...] = out


def _dense_mixer(h2, x1, mod_l, wg, wu, wd, ln_g, ln_b, alpha, tm, n_split, row_of_tile, t_ctx):
    t, d = x1.shape
    n_ctx_tiles = t_ctx // tm
    const = lambda shape: pl.BlockSpec(shape, lambda i: tuple(0 for _ in shape), pipeline_mode=pl.Buffered(1))
    return pl.pallas_call(
        functools.partial(_ffn_kernel, alpha, n_ctx_tiles, n_split),
        grid=(t // tm,),
        in_specs=[
            pl.BlockSpec((tm, d), lambda i: (i, 0)),
            pl.BlockSpec((tm, d), lambda i: (i, 0)),
            _mod_spec(5, row_of_tile),
            const(wg.shape), const(wu.shape), const(wd.shape), const((1, d)), const((1, d)),
        ],
        out_specs=_pair_specs(tm, d, n_ctx_tiles),
        out_shape=[jax.ShapeDtypeStruct((t_ctx, d), F32), jax.ShapeDtypeStruct((t - t_ctx, d), F32)],
        compiler_params=_cparams(("arbitrary",)), name="dense_mixer",
    )(h2, x1, mod_l, wg, wu, wd, ln_g, ln_b)


MOE_TILE = 256
MOE_TF = 1792


def _route_kernel(g_ref, rank_ref, gate_ref, before_ref, run_s):
    w = pl.program_id(0)

    @pl.when(w == 0)
    def _():
        run_s[...] = jnp.zeros_like(run_s)

    g = g_ref[...]
    sel = g > 0.0
    row = lax.broadcasted_iota(jnp.int32, (MOE_TILE, MOE_TILE), 0)
    col = lax.broadcasted_iota(jnp.int32, (MOE_TILE, MOE_TILE), 1)
    earlier = (row < col).astype(BF16)
    ones = jnp.where(sel, 1.0, 0.0)
    rank = _dot(ones.astype(BF16), earlier) + run_s[:, 0:1]
    rank_ref[...] = jnp.where(sel, rank, -1.0)
    gate_ref[...] = g
    before_ref[...] = run_s[...]
    run_s[...] += jnp.sum(ones, axis=1, keepdims=True)


def _route(gates_t):
    t = gates_t.shape[1]
    nw = t // MOE_TILE
    return pl.pallas_call(
        _route_kernel,
        grid=(nw,),
        in_specs=[pl.BlockSpec((N_EXPERTS, MOE_TILE), lambda w: (0, w))],
        out_specs=[
            pl.BlockSpec((None, N_EXPERTS, MOE_TILE), lambda w: (w, 0, 0)),
            pl.BlockSpec((None, N_EXPERTS, MOE_TILE), lambda w: (w, 0, 0)),
            pl.BlockSpec((None, N_EXPERTS, LANES), lambda w: (w, 0, 0)),
        ],
        out_shape=[
            jax.ShapeDtypeStruct((nw, N_EXPERTS, MOE_TILE), F32),
            jax.ShapeDtypeStruct((nw, N_EXPERTS, MOE_TILE), F32),
            jax.ShapeDtypeStruct((nw, N_EXPERTS, LANES), F32),
        ],
        scratch_shapes=[pltpu.VMEM((N_EXPERTS, LANES), F32)],
        compiler_params=_cparams(("arbitrary",)), name="moe_route",
    )(gates_t)


def _tile_onehot(rank_ref, w, e, base):
    rank_row = rank_ref[w, pl.ds(e, 1), :]
    rows = lax.broadcasted_iota(jnp.int32, (MOE_TILE, MOE_TILE), 0).astype(F32) + base.astype(F32)
    return rank_row == rows


def _gather_kernel(te_ref, base_ref, wlo_ref, whi_ref, rank_ref, x_ref, o_ref):
    j = pl.program_id(0)
    e, base = te_ref[j], base_ref[j]
    o_ref[...] = jnp.zeros_like(o_ref)

    def window(w, carry):
        p = jnp.where(_tile_onehot(rank_ref, w, e, base), 1.0, 0.0).astype(BF16)
        x_w = x_ref[pl.ds(pl.multiple_of(w * MOE_TILE, MOE_TILE), MOE_TILE), :]
        o_ref[...] += _dot(p, x_w).astype(BF16)
        return carry

    lax.fori_loop(wlo_ref[j], whi_ref[j] + 1, window, 0)


def _moe_gather(meta, rank_t, h2, n_tiles):
    t, d = h2.shape
    nw = t // MOE_TILE
    return pl.pallas_call(
        _gather_kernel,
        grid_spec=pltpu.PrefetchScalarGridSpec(
            num_scalar_prefetch=4, grid=(n_tiles,),
            in_specs=[
                pl.BlockSpec((nw, N_EXPERTS, MOE_TILE), lambda j, *_: (0, 0, 0)),
                pl.BlockSpec((t, d), lambda j, *_: (0, 0), pipeline_mode=pl.Buffered(1)),
            ],
            out_specs=pl.BlockSpec((MOE_TILE, d), lambda j, *_: (j, 0)),
        ),
        out_shape=jax.ShapeDtypeStruct((n_tiles * MOE_TILE, d), BF16),
        compiler_params=_cparams(("arbitrary",)), name="moe_gather",
    )(*meta, rank_t, h2)


F8 = jnp.float8_e4m3fn
F8_TARGET = 224.0
F8_TINY = 1e-30


def _f8_quantize(x):
    m = jnp.max(jnp.max(jnp.abs(x), axis=1, keepdims=True), axis=0, keepdims=True)
    m = jnp.maximum(m, F8_TINY)
    return (x * (F8_TARGET / m)).astype(F8), m * (1.0 / F8_TARGET)


def _moe_up_kernel(te_ref, chg_ref, nused_ref, x_ref, wg_ref, wu_ref, o_ref, wg_s, wu_s, inv_s):
    j = pl.program_id(1)

    @pl.when(chg_ref[j] == 1)
    def _():
        for row, (w_ref, w_s) in enumerate(((wg_ref, wg_s), (wu_ref, wu_s))):
            w_s[...], inv = _f8_quantize(w_ref[...])
            inv_s[row:row + 1, :] = jnp.broadcast_to(inv, (1, LANES))

    @pl.when(j < nused_ref[0])
    def _():
        x8, inv_x = _f8_quantize(x_ref[...].astype(F32))
        a = _dot(x8, wg_s[...]) * (inv_x * inv_s[0:1, 0:1])
        u = _dot(x8, wu_s[...]) * (inv_x * inv_s[1:2, 0:1])
        o_ref[...] = (_silu(a) * u).astype(BF16)

    @pl.when(j >= nused_ref[0])
    def _():
        o_ref[...] = jnp.zeros_like(o_ref)


def _moe_up(te, chg, nused, xs, wg, wu):
    rows, d = xs.shape
    n_tiles = rows // MOE_TILE
    ff = wg.shape[2]
    return pl.pallas_call(
        _moe_up_kernel,
        grid_spec=pltpu.PrefetchScalarGridSpec(
            num_scalar_prefetch=3, grid=(ff // MOE_TF, n_tiles),
            in_specs=[
                pl.BlockSpec((MOE_TILE, d), lambda f, j, *_: (j, 0)),
                pl.BlockSpec((None, d, MOE_TF), lambda f, j, te, *_: (te[j], 0, f)),
                pl.BlockSpec((None, d, MOE_TF), lambda f, j, te, *_: (te[j], 0, f)),
            ],
            out_specs=pl.BlockSpec((MOE_TILE, MOE_TF), lambda f, j, *_: (j, f)),
            scratch_shapes=[pltpu.VMEM((d, MOE_TF), F8), pltpu.VMEM((d, MOE_TF), F8),
                            pltpu.VMEM((SUBLANES, LANES), F32)],
        ),
        out_shape=jax.ShapeDtypeStruct((rows, ff), BF16),
        compiler_params=_cparams(("arbitrary", "arbitrary")), name="moe_up",
    )(te, chg, nused, xs, wg, wu)


def _moe_down_kernel(te_ref, chg_ref, nused_ref, a_ref, wd_ref, o_ref, wd_s, inv_s):
    j = pl.program_id(0)

    @pl.when(chg_ref[j] == 1)
    def _():
        wd_s[...], inv = _f8_quantize(wd_ref[...])
        inv_s[0:1, :] = jnp.broadcast_to(inv, (1, LANES))

    @pl.when(j < nused_ref[0])
    def _():
        a8, inv_a = _f8_quantize(a_ref[...].astype(F32))
        o_ref[...] = (_dot(a8, wd_s[...]) * (inv_a * inv_s[0:1, 0:1])).astype(BF16)

    @pl.when(j >= nused_ref[0])
    def _():
        o_ref[...] = jnp.zeros_like(o_ref)


def _moe_down(te, chg, nused, act, wd):
    rows, ff = act.shape
    d = wd.shape[2]
    return pl.pallas_call(
        _moe_down_kernel,
        grid_spec=pltpu.PrefetchScalarGridSpec(
            num_scalar_prefetch=3, grid=(rows // MOE_TILE,),
            in_specs=[
                pl.BlockSpec((MOE_TILE, ff), lambda j, *_: (j, 0)),
                pl.BlockSpec((None, ff, d), lambda j, te, *_: (te[j], 0, 0)),
            ],
            out_specs=pl.BlockSpec((MOE_TILE, d), lambda j, *_: (j, 0)),
            scratch_shapes=[pltpu.VMEM((ff, d), F8), pltpu.VMEM((SUBLANES, LANES), F32)],
        ),
        out_shape=jax.ShapeDtypeStruct((rows, d), BF16),
        compiler_params=_cparams(("arbitrary",)), name="moe_down",
    )(te, chg, nused, act, wd)


def _combine_kernel(alpha, nw_ctx, blk_a_ref, blk_b_ref, rel_ref, two_ref, rank_ref, gate_ref, *refs):
    ya_refs, yb_refs = refs[:N_EXPERTS], refs[N_EXPERTS:2 * N_EXPERTS]
    x1_ref, g2_ref, lg_ref, lb_ref, oc_ref, ol_ref, acc_s = refs[2 * N_EXPERTS:]
    w = pl.program_id(0)
    row_id = lax.broadcasted_iota(jnp.int32, (MOE_TILE, MOE_TILE), 0).astype(F32)

    def weights(e, rel):
        hit = rank_ref[e:e + 1, :] == row_id + rel.astype(F32)
        return jnp.where(hit, gate_ref[e:e + 1, :], 0.0).T.astype(BF16)

    acc = None
    for e in range(N_EXPERTS):
        part = _dot(weights(e, rel_ref[w * N_EXPERTS + e]), ya_refs[e][...])
        acc = part if acc is None else acc + part
    acc_s[...] = acc
    for e in range(N_EXPERTS):
        @pl.when(two_ref[w * N_EXPERTS + e] == 1)
        def _():
            acc_s[...] += _dot(weights(e, rel_ref[w * N_EXPERTS + e] + MOE_TILE), yb_refs[e][...])

    out = _layer_norm(alpha * x1_ref[...] + g2_ref[...] * acc_s[...], lg_ref[...], lb_ref[...])

    @pl.when(w < nw_ctx)
    def _():
        oc_ref[...] = out

    @pl.when(w >= nw_ctx)
    def _():
        ol_ref[...] = out


def _moe_combine(bands, rank_t, gate_t, y, x1, mod_l, ln_g, ln_b, alpha, row_of_tile, t_ctx):
    t, d = x1.shape
    nw = t // MOE_TILE
    nw_ctx = t_ctx // MOE_TILE

    def y_spec(which, e):
        return pl.BlockSpec((MOE_TILE, d), lambda w, *pre: (pre[which][w * N_EXPERTS + e], 0))

    win_spec = pl.BlockSpec((None, N_EXPERTS, MOE_TILE), lambda w, *_: (w, 0, 0))
    return pl.pallas_call(
        functools.partial(_combine_kernel, alpha, nw_ctx),
        grid_spec=pltpu.PrefetchScalarGridSpec(
            num_scalar_prefetch=4, grid=(nw,),
            in_specs=[win_spec, win_spec]
            + [y_spec(0, e) for e in range(N_EXPERTS)] + [y_spec(1, e) for e in range(N_EXPERTS)]
            + [
                pl.BlockSpec((MOE_TILE, d), lambda w, *_: (w, 0)),
                pl.BlockSpec((None, None, 1, d), lambda w, *_: (row_of_tile(w), 5, 0, 0)),
                pl.BlockSpec((1, d), lambda w, *_: (0, 0)),
                pl.BlockSpec((1, d), lambda w, *_: (0, 0)),
            ],
            out_specs=[
                pl.BlockSpec((MOE_TILE, d), lambda w, *_: (jnp.minimum(w, nw_ctx - 1), 0)),
                pl.BlockSpec((MOE_TILE, d), lambda w, *_: (jnp.maximum(w - nw_ctx, 0), 0)),
            ],
            scratch_shapes=[pltpu.VMEM((MOE_TILE, d), F32)],
        ),
        out_shape=[jax.ShapeDtypeStruct((t_ctx, d), F32), jax.ShapeDtypeStruct((t - t_ctx, d), F32)],
        compiler_params=_cparams(("arbitrary",)), name="moe_combine",
    )(*bands, rank_t, gate_t, *([y] * (2 * N_EXPERTS)), x1, mod_l, ln_g, ln_b)


def _moe_layer(gates, h2, x1, mod_l, wg, wu, wd, ln_g, ln_b, alpha, row_of_tile, t_ctx):
    t = h2.shape[0]
    n_tiles = -(-(2 * t + N_EXPERTS * (MOE_TILE - 1)) // MOE_TILE)
    rank_t, gate_t, before = _route(gates)
    before = before[:, :, 0].astype(jnp.int32)
    counts = jnp.sum(gates > 0.0, axis=1).astype(jnp.int32)
    tiles_e = (counts + MOE_TILE - 1) // MOE_TILE
    tile_end = jnp.cumsum(tiles_e)
    n_used = tile_end[-1]
    j = jnp.arange(n_tiles, dtype=jnp.int32)
    te = jnp.minimum(jnp.sum(j[:, None] >= tile_end[None, :], axis=1), N_EXPERTS - 1).astype(jnp.int32)
    base = (j - jnp.take(tile_end - tiles_e, te)) * MOE_TILE
    last = jnp.minimum(base + MOE_TILE - 1, jnp.take(counts, te) - 1)
    before_t = jnp.take(before, te, axis=1)
    used = j < n_used
    wlo = jnp.where(used, jnp.sum(before_t <= base[None, :], axis=0) - 1, 1).astype(jnp.int32)
    whi = jnp.where(used, jnp.sum(before_t <= last[None, :], axis=0) - 1, 0).astype(jnp.int32)
    chg = jnp.concatenate([jnp.ones((1,), jnp.int32), (te[1:] != te[:-1]).astype(jnp.int32)])
    nused = n_used.reshape(1).astype(jnp.int32)
    meta = (te, base.astype(jnp.int32), wlo, whi)
    xs = _moe_gather(meta, rank_t, h2, n_tiles)
    act = _moe_up(te, chg, nused, xs, wg, wu)
    y = _moe_down(te, chg, nused, act, wd)
    start_row = (tile_end - tiles_e) * MOE_TILE
    n_we = jnp.concatenate([before[1:], counts[None, :]], axis=0) - before
    first = start_row[None, :] + before
    blk_a = jnp.minimum(first // MOE_TILE, n_tiles - 1)
    two = ((n_we > 0) & ((first + n_we - 1) // MOE_TILE > blk_a)).astype(jnp.int32)
    rel = blk_a * MOE_TILE - start_row[None, :]
    bands = tuple(a.reshape(-1).astype(jnp.int32) for a in (blk_a, blk_a + two, rel, two))
    return _moe_combine(bands, rank_t, gate_t, y, x1, mod_l, ln_g, ln_b, alpha, row_of_tile, t_ctx)


def _rope_tables(seq):
    half = GLA_DK // 2
    n = half // 2
    t = np.arange(seq)
    inv = ROPE_BASE ** (-np.arange(n, dtype=np.float64) / n)
    cos = np.zeros((seq, GLA_DK), np.float64)
    sin = np.zeros((seq, GLA_DK), np.float64)
    for a, pos in enumerate((t // GRID_W, t % GRID_W)):
        ang = pos[:, None].astype(np.float64) * inv[None, :]
        base = a * half
        cos[:, base:base + n] = np.cos(ang)
        cos[:, base + n:base + half] = np.cos(ang)
        sin[:, base:base + n] = -np.sin(ang)
        sin[:, base + n:base + half] = np.sin(ang)
    tile = lambda m: jnp.asarray(np.tile(m, (1, GLA_HEADS)), F32)
    return tile(cos), tile(sin)


W_IN_GD = 1536
W_IN_KV = (2080, 2592)
W_IN_CHUNKS = (0, 512, 1024, 1568, 3104, 3616, 4128, 4640, 5152, 5664, 6176, 6688)


def _chunk_w_in(w):
    bf = lambda a: a.astype(BF16)
    wt = jnp.swapaxes(w, 1, 2)
    gd = jnp.pad(wt[:, W_IN_GD:W_IN_GD + 2 * GLA_RANK], ((0, 0), (0, LANES - 2 * GLA_RANK), (0, 0)))
    kv = jnp.stack([wt[:, s:s + CHUNK_W] for s in W_IN_KV], axis=1)
    return bf(gd), bf(kv), bf(jnp.stack([wt[:, s:s + CHUNK_W] for s in W_IN_CHUNKS], axis=1))


def _gate_up_blocks(w_gup):
    qkw = w_gup.shape[-1]
    fwd = jnp.pad(w_gup[:, 0], ((0, 0), (0, 0), (0, qkw)))
    bwd = jnp.pad(w_gup[:, 1], ((0, 0), (0, 0), (qkw, 0)))
    return jnp.pad(jnp.concatenate([fwd, bwd], axis=1), ((0, 0), (0, LANES - 2 * GLA_RANK), (0, 0)))


def kernel(x_prompt, x_sample, cache_na_k, cache_na_v, state_gla, c, c_ctx, w_mod, b_mod, w_in, gla_w_gup, gla_b_g, gla_norm, w_br_gla, na_rpb, w_br_na, cv_w, cv_b, cv_ln_g, cv_ln_b, w_br_cv, w_out, ln_g, ln_b, ffd_w_gate, ffd_w_up, ffd_w_down, moe_w_router, moe_w_gate, moe_w_up, moe_w_down):
    nb_c, l_c, d = x_prompt.shape
    nb_l, l_l, _ = x_sample.shape
    depth = w_mod.shape[0]
    t_ctx, t_lat = nb_c * l_c, nb_l * l_l
    t_all = t_ctx + t_lat
    alpha = (2 * depth) ** 0.25
    tm = 512
    row_of_tile = _row_of_tile_fn(t_ctx, l_l, tm)

    x_pair = (x_prompt.reshape(t_ctx, d), x_sample.reshape(t_lat, d))
    cond8 = jnp.concatenate([c_ctx[None, :], c, jnp.zeros((8 - 1 - nb_l, d), F32)], axis=0)
    mod = _modulation(cond8, w_mod, b_mod).reshape(depth, 8, 6, 1, d)
    rope = _rope_tables(l_l)
    rows_lat = l_l // GRID_W
    kr = min(NA_WIN_R, rows_lat)
    s0_all = state_gla.reshape(nb_l, depth, 2, 2, LANES, GLA_DV)
    w_gd, w_kv, w_chunks = _chunk_w_in(w_in)
    wg_blks = _gate_up_blocks(gla_w_gup)
    toep = _na_bias_tables(na_rpb)
    w_br = [w.astype(BF16) for w in (w_br_gla, w_br_na, w_br_cv, w_out)]

    caches = None
    states = None
    for l in range(depth):
        mod_l = mod[l]
        gd, kc, vc, kl, vl, zb = _input_projection(x_pair, mod_l, (w_gd, w_kv, w_chunks), l, depth, l_c,
                                                   tm, row_of_tile, caches)
        caches = (kc, vc)

        wg_blk = wg_blks[l]
        bg = gla_b_g[l].reshape(1, CHUNK_W)
        gn = gla_norm[l].reshape(1, CHUNK_W)
        ya, states = _gla_branch(gd, zb, wg_blk, bg, gn, latent=False, seq=l_c, nb=nb_c, tok0=0, t_total=t_all,
                                 state_slot=(l, depth, states))
        ya = _gla_branch(gd, zb, wg_blk, bg, gn, latent=True, seq=l_l, nb=nb_l, tok0=t_ctx, t_total=t_all,
                         rope=rope, s0=s0_all[:, l], ya_prev=ya)

        yb = _attn_context(zb, kc, vc, l, nb_c, l_c, t_all)
        yb = _attn_latent(zb, kl, vl, cache_na_k, cache_na_v, l, toep, yb, nb_l, l_l, t_ctx)

        conv_args = (zb, cv_w[l], cv_b[l][None, :], cv_ln_g[l][None, :], cv_ln_b[l][None, :])
        yc = _conv_branch(*conv_args, seq=l_c, nb=nb_c, tok0=0, t_total=t_all)
        yc = _conv_branch(*conv_args, seq=l_l, nb=nb_l, tok0=t_ctx, t_total=t_all, prev=yc)

        moe = l % 2 == 1
        w_router = None
        if moe:
            w_router = moe_w_router[l // 2].T
        merged = _merge(ya, yb, yc, zb, x_pair, mod_l, w_br[0][l], w_br[1][l], w_br[2][l], w_br[3][l],
                        ln_g[l, 0][None, :], ln_b[l, 0][None, :], alpha, tm, row_of_tile, w_router)
        ln2 = (ln_g[l, 1][None, :], ln_b[l, 1][None, :])
        if moe:
            x1, h2, gates = merged
            x_pair = _moe_layer(gates, h2, x1, mod_l, moe_w_gate[l // 2], moe_w_up[l // 2], moe_w_down[l // 2],
                                *ln2, alpha, _row_of_tile_fn(t_ctx, l_l, MOE_TILE), t_ctx)
        else:
            x1, h2 = merged
            i = l // 2
            x_pair = _dense_mixer(h2, x1, mod_l, ffd_w_gate[i].astype(BF16), ffd_w_up[i].astype(BF16),
                                  ffd_w_down[i].astype(BF16), *ln2, alpha, tm, 2, row_of_tile, t_ctx)

    y_prompt = x_pair[0].reshape(nb_c, l_c, d)
    y_sample = x_pair[1].reshape(nb_l, l_l, d)
    cache_shape = (nb_c, depth, l_c, NA_HEADS, NA_DH)
    return (y_prompt, y_sample, caches[0].reshape(cache_shape), caches[1].reshape(cache_shape),
            states.reshape(nb_c, depth, 2, GLA_HEADS, GLA_DK, GLA_DV))


def _row_of_tile_fn(t_ctx, l_lat, tm):
    n_ctx = t_ctx // tm

    def row_of_tile(i):
        return jnp.where(i < n_ctx, 0, 1 + (i - n_ctx) // (l_lat // tm))

    return row_of_tile
```

```python
import functools

import numpy as np
import jax
import jax.numpy as jnp
from jax import lax
from jax.experimental import pallas as pl
from jax.experimental.pallas import tpu as pltpu

F32 = jnp.float32
BF16 = jnp.bfloat16

GRID_W = 64
GLA_HEADS = 4
GLA_DK = 64
GLA_DV = 128
GLA_RANK = 16
GLA_TAU = 16.0
GLA_CHUNK = 64
GLA_GROUP = 256
NA_HEADS = 8
NA_DH = 64
NA_WIN_R = 8
NA_WIN_C = 16
CONV_CH = 512
CONV_K = 31
N_EXPERTS = 8
ROPE_BASE = 10000.0
LN_EPS = 1e-5

D_MODEL = 1024
LANES = 128
SUBLANES = 8
CHUNK_W = 512
N_BF16_CHUNKS = 12
NEG_BIG = -1e30
VMEM_LIMIT = 56 * 1024 * 1024


def _cparams(sem, vmem=VMEM_LIMIT):
    return pltpu.CompilerParams(dimension_semantics=sem, vmem_limit_bytes=vmem)


def _dot(a, b):
    return jnp.dot(a, b, preferred_element_type=F32)


def _split(x):
    hi = x.astype(BF16)
    return hi, (x - hi.astype(F32)).astype(BF16)


def _dot_split(a, b):
    a_hi, a_lo = _split(a)
    b_hi, b_lo = _split(b)
    return _dot(a_hi, b_hi) + _dot(a_lo, b_hi) + _dot(a_hi, b_lo)


def _dot_split_rhs(a, b):
    b_hi, b_lo = _split(b)
    return _dot(a, b_hi) + _dot(a, b_lo)


def _dot_nt(a, b):
    return lax.dot_general(a, b, (((1,), (1,)), ((), ())), preferred_element_type=F32)


def _sigmoid(x):
    return 0.5 * jnp.tanh(0.5 * x) + 0.5


def _silu(x):
    return x * _sigmoid(x)


def _layer_norm(x, g, b):
    mu = jnp.mean(x, axis=-1, keepdims=True)
    xc = x - mu
    var = jnp.mean(xc * xc, axis=-1, keepdims=True)
    return xc * lax.rsqrt(var + LN_EPS) * g + b


def _mod_kernel(c_ref, w_ref, b_ref, o_ref):
    c = c_ref[...]
    s = _silu(c).astype(BF16)
    o_ref[...] = _dot(s, w_ref[...].astype(BF16)) + b_ref[...]


def _modulation(cond8, w_mod, b_mod):
    depth, d, n = w_mod.shape
    tn = 1024
    return pl.pallas_call(
        _mod_kernel,
        grid=(depth, n // tn),
        in_specs=[
            pl.BlockSpec((8, d), lambda l, j: (0, 0)),
            pl.BlockSpec((None, d, tn), lambda l, j: (l, 0, j)),
            pl.BlockSpec((None, 1, tn), lambda l, j: (l, 0, j)),
        ],
        out_specs=pl.BlockSpec((None, 8, tn), lambda l, j: (l, 0, j)),
        out_shape=jax.ShapeDtypeStruct((depth, 8, n), F32),
        compiler_params=_cparams(("arbitrary", "arbitrary")),
        name="modulation",
    )(cond8, w_mod, b_mod.reshape(depth, 1, n))


def _inproj_kernel(n_ctx_tiles, aliased, *refs):
    if aliased:
        (xc_ref, xl_ref, sc_ref, sh_ref, wgd_ref, wkv_ref, w_ref, _kc_in, _vc_in,
         gd_ref, kc_ref, vc_ref, kl_ref, vl_ref, zb_ref, h_ref) = refs
    else:
        (xc_ref, xl_ref, sc_ref, sh_ref, wgd_ref, wkv_ref, w_ref,
         gd_ref, kc_ref, vc_ref, kl_ref, vl_ref, zb_ref, h_ref) = refs
    is_ctx = pl.program_id(0) < n_ctx_tiles
    x = jnp.where(is_ctx, xc_ref[...], xl_ref[...])
    h_ref[...] = (x * (1.0 + sc_ref[...]) + sh_ref[...]).astype(BF16)
    gd_ref[...] = _dot_nt(h_ref[...], wgd_ref[...])
    k = _dot_nt(h_ref[...], wkv_ref[0])
    v = _dot_nt(h_ref[...], wkv_ref[1])
    seqs, l_c, _ = kc_ref.shape

    @pl.when(is_ctx)
    def _():
        for s in range(seqs):
            kc_ref[s] = k[s * l_c:(s + 1) * l_c]
            vc_ref[s] = v[s * l_c:(s + 1) * l_c]

    @pl.when(jnp.logical_not(is_ctx))
    def _():
        kl_ref[...] = k
        vl_ref[...] = v

    for j in range(N_BF16_CHUNKS):
        zb_ref[j] = _dot_nt(h_ref[...], w_ref[j]).astype(BF16)


def _mod_spec(piece, row_of_tile):
    return pl.BlockSpec((None, None, 1, D_MODEL), lambda i: (row_of_tile(i), piece, 0, 0))


def _pair_specs(tm, d, n_ctx_tiles):
    return [pl.BlockSpec((tm, d), lambda i, *_: (jnp.minimum(i, n_ctx_tiles - 1), 0)),
            pl.BlockSpec((tm, d), lambda i, *_: (jnp.maximum(i - n_ctx_tiles, 0), 0))]


def _input_projection(x_pair, mod_l, w_parts, layer, depth, l_c, tm, row_of_tile, caches=None):
    xc, xl = x_pair
    w_gd, w_kv, w_chunks = w_parts
    d = xc.shape[1]
    t_ctx, t_lat = xc.shape[0], xl.shape[0]
    t = t_ctx + t_lat
    n_ctx_tiles = t_ctx // tm
    seqs = tm // l_c
    cache_spec = pl.BlockSpec((seqs, None, l_c, CHUNK_W),
                              lambda i: (jnp.minimum(i, n_ctx_tiles - 1), layer, 0, 0))
    lat_spec = pl.BlockSpec((tm, CHUNK_W), lambda i: (jnp.maximum(i - n_ctx_tiles, 0), 0))
    const = lambda a: pl.BlockSpec((None,) + a.shape[1:], lambda i: (layer,) + (0,) * (a.ndim - 1),
                                   pipeline_mode=pl.Buffered(1))
    in_specs = _pair_specs(tm, d, n_ctx_tiles) + [
        _mod_spec(1, row_of_tile), _mod_spec(0, row_of_tile),
        const(w_gd), const(w_kv), const(w_chunks),
    ]
    args = [xc, xl, mod_l, mod_l, w_gd, w_kv, w_chunks]
    aliases = {}
    if caches is not None:
        in_specs += [pl.BlockSpec(memory_space=pl.ANY), pl.BlockSpec(memory_space=pl.ANY)]
        args += list(caches)
        aliases = {7: 1, 8: 2}
    cache_shape = jax.ShapeDtypeStruct((t_ctx // l_c, depth, l_c, CHUNK_W), F32)
    return pl.pallas_call(
        functools.partial(_inproj_kernel, n_ctx_tiles, caches is not None),
        grid=(t // tm,),
        in_specs=in_specs,
        out_specs=[
            pl.BlockSpec((tm, LANES), lambda i: (i, 0)),
            cache_spec, cache_spec, lat_spec, lat_spec,
            pl.BlockSpec((N_BF16_CHUNKS, tm, CHUNK_W), lambda i: (0, i, 0)),
        ],
        out_shape=[
            jax.ShapeDtypeStruct((t, LANES), F32),
            cache_shape, cache_shape,
            jax.ShapeDtypeStruct((t_lat, CHUNK_W), F32), jax.ShapeDtypeStruct((t_lat, CHUNK_W), F32),
            jax.ShapeDtypeStruct((N_BF16_CHUNKS, t, CHUNK_W), BF16),
        ],
        scratch_shapes=[pltpu.VMEM((tm, d), BF16)],
        input_output_aliases=aliases,
        compiler_params=_cparams(("arbitrary",)),
        name="input_projection",
    )(*args)


def _gla_kernel(latent, seq, state_alias, *refs):
    if latent:
        (qk_ref, v_ref, ra_ref, gd_ref, wg_ref, bg_ref, gn_ref, cos_ref, sin_ref, s0_ref, _alias,
         ya_ref, q_s, k_s, g_s, acc_s, st_s) = refs
    elif state_alias:
        (qk_ref, v_ref, ra_ref, gd_ref, wg_ref, bg_ref, gn_ref, _alias,
         ya_ref, sfin_ref, q_s, k_s, g_s, acc_s, st_s) = refs
    else:
        (qk_ref, v_ref, ra_ref, gd_ref, wg_ref, bg_ref, gn_ref,
         ya_ref, sfin_ref, q_s, k_s, g_s, acc_s, st_s) = refs
    c = GLA_CHUNK
    n_chunks = seq // c
    qkw = GLA_HEADS * GLA_DK

    q = qk_ref[:, 0:qkw].astype(F32) * (GLA_DK ** -0.5)
    k = qk_ref[:, qkw:2 * qkw].astype(F32)
    if latent:
        lane = lax.broadcasted_iota(jnp.int32, (1, qkw), 1)
        first_half = (lane % 32) < 16

        def rope(x):
            swapped = jnp.where(first_half, pltpu.roll(x, qkw - 16, 1), pltpu.roll(x, 16, 1))
            return x * cos_ref[...] + swapped * sin_ref[...]

        q, k = rope(q), rope(k)
    q_s[...] = q
    k_s[...] = k

    glog = _dot_split(gd_ref[...], wg_ref[...]) + bg_ref[...]
    g_s[...] = (jnp.minimum(glog, 0.0) - jnp.log(1.0 + jnp.exp(-jnp.abs(glog)))) * (1.0 / GLA_TAU)
    acc_s[...] = jnp.zeros_like(acc_s)
    if latent:
        st_s[...] = s0_ref[...]
    else:
        st_s[...] = jnp.zeros_like(st_s)

    grp = GLA_GROUP
    cpg = grp // c
    n_groups = seq // grp
    grow = lax.broadcasted_iota(jnp.int32, (grp, grp), 0)
    gcol = lax.broadcasted_iota(jnp.int32, (grp, grp), 1)
    same_chunk = (grow // c) == (gcol // c)
    keep = (same_chunk & (gcol <= grow), same_chunk & (gcol >= grow))
    tri = (keep[0].astype(BF16), keep[1].astype(BF16))
    lane128 = lax.broadcasted_iota(jnp.int32, (1, LANES), 1)
    head_mask = (lane128 < GLA_DK, lane128 >= GLA_DK)
    urow = lax.broadcasted_iota(jnp.int32, (cpg * LANES, GLA_DV), 0)
    srow = lax.broadcasted_iota(jnp.int32, (LANES, GLA_DV), 0)
    chunk_of_col = lax.broadcasted_iota(jnp.int32, (LANES, grp), 1) // c
    chunk_of_row = lax.broadcasted_iota(jnp.int32, (grp, LANES), 0) // c
    pad_rows = jnp.zeros((SUBLANES - cpg, qkw), F32)

    def group_step(i, carry):
        for d in range(2):
            gi = i if d == 0 else n_groups - 1 - i
            rows = pl.ds(pl.multiple_of(gi * grp, grp), grp)
            cum = _dot_split_rhs(tri[d], g_s[rows, d * qkw:(d + 1) * qkw])
            edge = c - 1 if d == 0 else 0
            lasts = [cum[ci * c + edge:ci * c + edge + 1, :] for ci in range(cpg)]
            last = jnp.concatenate([jnp.broadcast_to(r, (c, qkw)) for r in lasts], axis=0)
            ref = 0.5 * last
            q = q_s[rows, :]
            k = k_s[rows, :]
            qe = q * jnp.exp(cum - ref)
            ke = k * jnp.exp(ref - cum)
            qe2 = q * jnp.exp(cum)
            ke2 = k * jnp.exp(last - cum)
            last8 = jnp.concatenate(lasts + [pad_rows], axis=0)
            for p in range(2):
                sl = slice(p * LANES, (p + 1) * LANES)
                ke_p = ke[:, sl].astype(BF16)
                ke2_t = ke2[:, sl].T
                ke2_blk = jnp.concatenate(
                    [jnp.where(chunk_of_col == ci, ke2_t, 0.0) for ci in range(cpg)], axis=0).astype(BF16)
                dec = jnp.exp(last8[:, sl]).T
                psl = slice(2 * p * GLA_DV, (2 * p + 2) * GLA_DV)
                v_pair = v_ref[rows, psl]
                u = _dot(ke2_blk, v_pair)
                upd = jnp.where(urow % LANES < GLA_DK, u[:, :GLA_DV], u[:, GLA_DV:])
                s = st_s[d, p]
                s_in = [None] * cpg
                for ci in (range(cpg) if d == 0 else range(cpg - 1, -1, -1)):
                    s_in[ci] = jnp.concatenate(
                        [jnp.where(srow < GLA_DK, s, 0.0), jnp.where(srow >= GLA_DK, s, 0.0)], axis=1).astype(BF16)
                    s = dec[:, ci:ci + 1] * s + upd[ci * LANES:(ci + 1) * LANES]
                st_s[d, p] = s
                s_stack = jnp.concatenate(s_in, axis=0)
                a2_blk = jnp.concatenate(
                    [jnp.where(chunk_of_row == ci, qe2[:, sl], 0.0) for ci in range(cpg)], axis=1).astype(BF16)
                intra = []
                for hh in range(2):
                    a = jnp.where(head_mask[hh], qe[:, sl], 0.0).astype(BF16)
                    att = jnp.where(keep[d], _dot_nt(a, ke_p), 0.0).astype(BF16)
                    intra.append(_dot(att, v_pair[:, hh * GLA_DV:(hh + 1) * GLA_DV]))
                acc_s[rows, psl] += jnp.concatenate(intra, axis=1) + _dot(a2_blk, s_stack)
        return carry

    lax.fori_loop(0, n_groups, group_step, 0)

    def epilogue(i, carry):
        rows = pl.ds(pl.multiple_of(i * c, c), c)
        for h in range(GLA_HEADS):
            vsl = slice(h * GLA_DV, (h + 1) * GLA_DV)
            o = acc_s[rows, vsl]
            o = o * lax.rsqrt(jnp.mean(o * o, axis=-1, keepdims=True) + LN_EPS) * gn_ref[:, vsl]
            ya_ref[rows, vsl] = (o * _silu(ra_ref[rows, vsl].astype(F32))).astype(BF16)
        return carry

    lax.fori_loop(0, n_chunks, epilogue, 0)
    if not latent:
        sfin_ref[...] = st_s[...]


def _gla_branch(gd, zb, wg_blk, bg, gn, *, latent, seq, nb, tok0, t_total, rope=None, s0=None, ya_prev=None,
                state_slot=None):
    b0 = tok0 // seq
    qkw = GLA_HEADS * GLA_DK
    in_specs = [
        pl.BlockSpec((None, seq, CHUNK_W), lambda b: (0, b0 + b, 0)),
        pl.BlockSpec((None, seq, CHUNK_W), lambda b: (1, b0 + b, 0)),
        pl.BlockSpec((None, seq, CHUNK_W), lambda b: (2, b0 + b, 0)),
        pl.BlockSpec((seq, LANES), lambda b: (b0 + b, 0)),
        pl.BlockSpec((LANES, CHUNK_W), lambda b: (0, 0)),
        pl.BlockSpec((1, CHUNK_W), lambda b: (0, 0)),
        pl.BlockSpec((1, CHUNK_W), lambda b: (0, 0)),
    ]
    args = [zb, zb, zb, gd, wg_blk, bg, gn]
    scratch = [
        pltpu.VMEM((seq, qkw), F32), pltpu.VMEM((seq, qkw), F32), pltpu.VMEM((seq, CHUNK_W), F32),
        pltpu.VMEM((seq, CHUNK_W), F32), pltpu.VMEM((2, 2, LANES, GLA_DV), F32),
    ]
    ya_shape = jax.ShapeDtypeStruct((t_total, CHUNK_W), BF16)
    ya_spec = pl.BlockSpec((seq, CHUNK_W), lambda b: (b0 + b, 0))
    if latent:
        cos_t, sin_t = rope
        in_specs += [
            pl.BlockSpec((seq, qkw), lambda b: (0, 0)),
            pl.BlockSpec((seq, qkw), lambda b: (0, 0)),
            pl.BlockSpec((None, 2, 2, LANES, GLA_DV), lambda b: (b, 0, 0, 0, 0)),
            pl.BlockSpec(memory_space=pl.ANY),
        ]
        args += [cos_t, sin_t, s0, ya_prev]
        return pl.pallas_call(
            functools.partial(_gla_kernel, True, seq, False),
            grid=(nb,), in_specs=in_specs, out_specs=ya_spec, out_shape=ya_shape,
            scratch_shapes=scratch, input_output_aliases={len(args) - 1: 0},
            compiler_params=_cparams(("arbitrary",)), name="gla_latent",
        )(*args)
    layer, depth, states = state_slot
    aliases = {}
    if states is not None:
        in_specs.append(pl.BlockSpec(memory_space=pl.ANY))
        args.append(states)
        aliases = {len(args) - 1: 1}
    return pl.pallas_call(
        functools.partial(_gla_kernel, False, seq, states is not None),
        grid=(nb,), in_specs=in_specs,
        out_specs=[ya_spec, pl.BlockSpec((None, None, 2, 2, LANES, GLA_DV), lambda b: (b, layer, 0, 0, 0, 0))],
        out_shape=[ya_shape, jax.ShapeDtypeStruct((nb, depth, 2, 2, LANES, GLA_DV), F32)],
        scratch_shapes=scratch, input_output_aliases=aliases,
        compiler_params=_cparams(("arbitrary",)), name="gla_context",
    )(*args)


ATTN_CTX_SEQS = 2


def _attn_ctx_kernel(q_ref, k_ref, v_ref, o_ref):
    lane = lax.broadcasted_iota(jnp.int32, (1, LANES), 1)
    masks = (lane < NA_DH, lane >= NA_DH)
    scale = NA_DH ** -0.5
    n_seq, seq, _ = k_ref.shape
    for b in range(n_seq):
        rows = slice(b * seq, (b + 1) * seq)
        for p in range(NA_HEADS // 2):
            sl = slice(p * LANES, (p + 1) * LANES)
            q2 = _stack_heads(q_ref[rows, sl], masks)
            kp = k_ref[b, :, sl].astype(BF16)
            vp = v_ref[b, :, sl].astype(BF16)
            s = _dot_nt(q2, kp) * scale
            e = jnp.exp(s - jnp.max(s, axis=-1, keepdims=True))
            o2 = _dot(e.astype(BF16), vp) / jnp.sum(e, axis=-1, keepdims=True)
            o_ref[rows, sl] = jnp.where(masks[0], o2[0:seq], o2[seq:2 * seq]).astype(BF16)


def _stack_heads(qp, masks):
    zero = jnp.zeros_like(qp)
    return jnp.concatenate([jnp.where(masks[0], qp, zero), jnp.where(masks[1], qp, zero)], axis=0)


def _attn_context(zb, kc, vc, layer, nb, seq, t_total):
    per = ATTN_CTX_SEQS
    return pl.pallas_call(
        _attn_ctx_kernel,
        grid=(nb // per,),
        in_specs=[
            pl.BlockSpec((None, per * seq, CHUNK_W), lambda b: (3, b, 0)),
            pl.BlockSpec((per, None, seq, CHUNK_W), lambda b: (b, layer, 0, 0)),
            pl.BlockSpec((per, None, seq, CHUNK_W), lambda b: (b, layer, 0, 0)),
        ],
        out_specs=pl.BlockSpec((per * seq, CHUNK_W), lambda b: (b, 0)),
        out_shape=jax.ShapeDtypeStruct((t_total, CHUNK_W), BF16),
        compiler_params=_cparams(("arbitrary",)), name="attention_context",
    )(zb, kc, vc)


def _na_kernel(rows_total, kr, q_ref, kl_ref, vl_ref, kc_ref, vc_ref, toep_ref, _alias, o_ref):
    r = pl.program_id(1)
    rs = jnp.clip(r - kr // 2, 0, rows_total - kr)
    krows = pl.ds(pl.multiple_of(rs * GRID_W, GRID_W), kr * GRID_W)
    dr0 = rs - r + NA_WIN_R - 1
    lane = lax.broadcasted_iota(jnp.int32, (1, LANES), 1)
    masks = (lane < NA_DH, lane >= NA_DH)
    scale = NA_DH ** -0.5
    nq = q_ref.shape[0]
    for p in range(NA_HEADS // 2):
        sl = slice(p * LANES, (p + 1) * LANES)
        q2 = _stack_heads(q_ref[:, sl], masks)
        klp = kl_ref[krows, sl].astype(BF16)
        vlp = vl_ref[krows, sl].astype(BF16)
        kcp = kc_ref[:, sl].astype(BF16)
        vcp = vc_ref[:, sl].astype(BF16)
        bias = jnp.concatenate(
            [jnp.concatenate([toep_ref[2 * p + hh, dr0 + 2 * m] for m in range(kr // 2)], axis=1)
             for hh in range(2)], axis=0)
        s_loc = _dot_nt(q2, klp) * scale + bias
        s_ctx = _dot_nt(q2, kcp) * scale
        m = jnp.maximum(jnp.max(s_loc, axis=-1, keepdims=True), jnp.max(s_ctx, axis=-1, keepdims=True))
        e_loc = jnp.exp(s_loc - m)
        e_ctx = jnp.exp(s_ctx - m)
        den = jnp.sum(e_loc, axis=-1, keepdims=True) + jnp.sum(e_ctx, axis=-1, keepdims=True)
        o2 = (_dot(e_loc.astype(BF16), vlp) + _dot(e_ctx.astype(BF16), vcp)) / den
        o_ref[:, sl] = jnp.where(masks[0], o2[0:nq], o2[nq:2 * nq]).astype(BF16)


def _na_bias_tables(rpb):
    nc = rpb.shape[-1]
    qc = np.arange(GRID_W)[:, None]
    kc = np.arange(GRID_W)[None, :]
    cs = np.clip(qc - NA_WIN_C // 2, 0, GRID_W - NA_WIN_C)
    valid = (kc >= cs) & (kc < cs + NA_WIN_C)
    tap = np.clip(kc - qc, -(NA_WIN_C - 1), NA_WIN_C - 1) + NA_WIN_C - 1
    pick = jnp.asarray(tap[..., None] == np.arange(nc), F32)
    toep = jnp.einsum('dhab,qkb->dhaqk', rpb.astype(F32), pick, precision=lax.Precision.HIGHEST)
    toep = jnp.where(valid, toep, NEG_BIG)
    return jnp.concatenate([toep[:, :, :-1], toep[:, :, 1:]], axis=-1)


def _attn_latent(zb, kl, vl, cache_k, cache_v, layer, toep, att_prev, nb, seq, tok0):
    rows_total = seq // GRID_W
    kr = min(NA_WIN_R, rows_total)
    q0 = tok0 // GRID_W
    past = cache_k.shape[2]
    ck = cache_k.reshape(cache_k.shape[0], cache_k.shape[1], past, NA_HEADS * NA_DH)
    cv = cache_v.reshape(ck.shape)
    return pl.pallas_call(
        functools.partial(_na_kernel, rows_total, kr),
        grid=(nb, rows_total),
        in_specs=[
            pl.BlockSpec((None, GRID_W, CHUNK_W), lambda b, r: (3, q0 + b * rows_total + r, 0)),
            pl.BlockSpec((seq, CHUNK_W), lambda b, r: (b, 0)),
            pl.BlockSpec((seq, CHUNK_W), lambda b, r: (b, 0)),
            pl.BlockSpec((None, None, past, CHUNK_W), lambda b, r: (b, layer, 0, 0)),
            pl.BlockSpec((None, None, past, CHUNK_W), lambda b, r: (b, layer, 0, 0)),
            pl.BlockSpec((None,) + toep.shape[1:], lambda b, r: (layer, 0, 0, 0, 0), pipeline_mode=pl.Buffered(1)),
            pl.BlockSpec(memory_space=pl.ANY),
        ],
        out_specs=pl.BlockSpec((GRID_W, CHUNK_W), lambda b, r: (q0 + b * rows_total + r, 0)),
        out_shape=jax.ShapeDtypeStruct(att_prev.shape, BF16),
        input_output_aliases={6: 0},
        compiler_params=_cparams(("arbitrary", "arbitrary")), name="attention_latent",
    )(zb, kl, vl, ck, cv, toep, att_prev)


CONV_PAD = 16
CONV_ROWS = 64


def _conv_kernel(seq, aliased, *refs):
    if aliased:
        a_ref, gt_ref, w_ref, cb_ref, lg_ref, lb_ref, _alias, o_ref, z_s, sh_s = refs
    else:
        a_ref, gt_ref, w_ref, cb_ref, lg_ref, lb_ref, o_ref, z_s, sh_s = refs
    z_s[0:CONV_PAD, :] = jnp.zeros((CONV_PAD, CONV_CH), F32)
    z_s[CONV_PAD + seq:2 * CONV_PAD + seq, :] = jnp.zeros((CONV_PAD, CONV_CH), F32)
    z_s[CONV_PAD:CONV_PAD + seq, :] = a_ref[...].astype(F32) * _sigmoid(gt_ref[...].astype(F32))
    ext = seq + 2 * CONV_PAD - SUBLANES
    for b in range(1, SUBLANES):
        for r0 in range(0, ext, CONV_ROWS):
            n = min(CONV_ROWS, ext - r0)
            sh_s[b - 1, r0:r0 + n, :] = z_s[r0 + b:r0 + b + n, :]
    off = CONV_PAD - CONV_K // 2
    grp = (CONV_ROWS // SUBLANES, SUBLANES, CONV_CH)
    for t0 in range(0, seq, CONV_ROWS):
        acc = jnp.zeros(grp, F32)
        for k in range(CONV_K):
            a, b = divmod(off + k, SUBLANES)
            r0 = t0 + a * SUBLANES
            tap = z_s[r0:r0 + CONV_ROWS, :] if b == 0 else sh_s[b - 1, r0:r0 + CONV_ROWS, :]
            acc = acc + tap.reshape(grp) * w_ref[k][None]
        y = _layer_norm(acc.reshape(CONV_ROWS, CONV_CH) + cb_ref[...], lg_ref[...], lb_ref[...])
        o_ref[t0:t0 + CONV_ROWS, :] = _silu(y).astype(BF16)


def _conv_branch(zb, cv_w, cv_b, ln_g, ln_b, *, seq, nb, tok0, t_total, prev=None):
    b0 = tok0 // seq
    in_specs = [
        pl.BlockSpec((None, seq, CHUNK_W), lambda b: (4, b0 + b, 0)),
        pl.BlockSpec((None, seq, CHUNK_W), lambda b: (5, b0 + b, 0)),
        pl.BlockSpec((CONV_K, SUBLANES, CONV_CH), lambda b: (0, 0, 0)),
        pl.BlockSpec((1, CONV_CH), lambda b: (0, 0)),
        pl.BlockSpec((1, CONV_CH), lambda b: (0, 0)),
        pl.BlockSpec((1, CONV_CH), lambda b: (0, 0)),
    ]
    args = [zb, zb, jnp.broadcast_to(cv_w[:, None, :], (CONV_K, SUBLANES, CONV_CH)), cv_b, ln_g, ln_b]
    aliases = {}
    if prev is not None:
        in_specs.append(pl.BlockSpec(memory_space=pl.ANY))
        args.append(prev)
        aliases = {6: 0}
    return pl.pallas_call(
        functools.partial(_conv_kernel, seq, prev is not None),
        grid=(nb,), in_specs=in_specs,
        out_specs=pl.BlockSpec((seq, CONV_CH), lambda b: (b0 + b, 0)),
        out_shape=jax.ShapeDtypeStruct((t_total, CONV_CH), BF16),
        scratch_shapes=[pltpu.VMEM((seq + 2 * CONV_PAD, CONV_CH), F32),
                        pltpu.VMEM((SUBLANES - 1, seq + 2 * CONV_PAD - SUBLANES, CONV_CH), F32)],
        input_output_aliases=aliases,
        compiler_params=_cparams(("arbitrary",)), name="conv_module_%d" % seq,
    )(*args)


def _merge_kernel(alpha, n_ctx_tiles, with_router, *refs):
    if with_router:
        (ya_ref, yb_ref, yc_ref, mg_ref, xc_ref, xl_ref, g1_ref, sc2_ref, sh2_ref, wa_ref, wb_ref, wc_ref, wo_ref,
         lg_ref, lb_ref, wr_ref, x1_ref, h2_ref, gates_ref) = refs
    else:
        (ya_ref, yb_ref, yc_ref, mg_ref, xc_ref, xl_ref, g1_ref, sc2_ref, sh2_ref, wa_ref, wb_ref, wc_ref, wo_ref,
         lg_ref, lb_ref, x1_ref, h2_ref) = refs
    x = jnp.where(pl.program_id(0) < n_ctx_tiles, xc_ref[...], xl_ref[...])
    halves = []
    for n in range(2):
        m = None
        for j, (y_ref, w_ref) in enumerate(((ya_ref, wa_ref), (yb_ref, wb_ref), (yc_ref, wc_ref))):
            proj = _dot(y_ref[...], w_ref[:, n * CHUNK_W:(n + 1) * CHUNK_W])
            term = _sigmoid(mg_ref[2 * j + n].astype(F32)) * proj
            m = term if m is None else m + term
        halves.append(m.astype(BF16))
    mix = _dot(halves[0], wo_ref[0:CHUNK_W, :]) + _dot(halves[1], wo_ref[CHUNK_W:2 * CHUNK_W, :])
    x1 = _layer_norm(alpha * x + g1_ref[...] * mix, lg_ref[...], lb_ref[...])
    x1_ref[...] = x1
    h2 = x1 * (1.0 + sc2_ref[...]) + sh2_ref[...]
    h2_ref[...] = h2.astype(BF16)
    if with_router:
        w_hi, w_lo = _split(wr_ref[...])
        h_hi, h_lo = _split(h2)
        lg = _dot_nt(w_hi, h_hi) + _dot_nt(w_hi, h_lo) + _dot_nt(w_lo, h_hi)
        eid = lax.broadcasted_iota(jnp.int32, lg.shape, 0)
        m1 = jnp.max(lg, axis=0, keepdims=True)
        i1 = jnp.min(jnp.where(lg == m1, eid, N_EXPERTS), axis=0, keepdims=True)
        lg2 = jnp.where(eid == i1, -jnp.inf, lg)
        m2 = jnp.max(lg2, axis=0, keepdims=True)
        i2 = jnp.min(jnp.where(lg2 == m2, eid, N_EXPERTS), axis=0, keepdims=True)
        e2 = jnp.exp(m2 - m1)
        w1 = 1.0 / (1.0 + e2)
        gates_ref[...] = jnp.where(eid == i1, w1, 0.0) + jnp.where(eid == i2, e2 * w1, 0.0)


def _merge(ya, yb, yc, zb, x_pair, mod_l, wa, wb, wc, wo, ln_g, ln_b, alpha, tm, row_of_tile, w_router=None):
    xc, xl = x_pair
    d = xc.shape[1]
    t = xc.shape[0] + xl.shape[0]
    n_ctx_tiles = xc.shape[0] // tm
    full = lambda shape: pl.BlockSpec(shape, lambda i: tuple(0 for _ in shape), pipeline_mode=pl.Buffered(1))
    in_specs = [
        pl.BlockSpec((tm, CHUNK_W), lambda i: (i, 0)),
        pl.BlockSpec((tm, CHUNK_W), lambda i: (i, 0)),
        pl.BlockSpec((tm, CHUNK_W), lambda i: (i, 0)),
        pl.BlockSpec((6, tm, CHUNK_W), lambda i: (1, i, 0)),
    ] + _pair_specs(tm, d, n_ctx_tiles) + [
        _mod_spec(2, row_of_tile), _mod_spec(4, row_of_tile), _mod_spec(3, row_of_tile),
        full(wa.shape), full(wb.shape), full(wc.shape), full(wo.shape), full((1, d)), full((1, d)),
    ]
    args = [ya, yb, yc, zb, xc, xl, mod_l, mod_l, mod_l, wa, wb, wc, wo, ln_g, ln_b]
    out_specs = [pl.BlockSpec((tm, d), lambda i: (i, 0)), pl.BlockSpec((tm, d), lambda i: (i, 0))]
    out_shape = [jax.ShapeDtypeStruct((t, d), F32), jax.ShapeDtypeStruct((t, d), BF16)]
    if w_router is not None:
        in_specs.append(full(w_router.shape))
        args.append(w_router)
        out_specs.append(pl.BlockSpec((N_EXPERTS, tm), lambda i: (0, i)))
        out_shape.append(jax.ShapeDtypeStruct((N_EXPERTS, t), F32))
    return pl.pallas_call(
        functools.partial(_merge_kernel, alpha, n_ctx_tiles, w_router is not None),
        grid=(t // tm,), in_specs=in_specs, out_specs=out_specs, out_shape=out_shape,
        compiler_params=_cparams(("arbitrary",)), name="merge",
    )(*args)


def _ffn_kernel(alpha, n_ctx_tiles, n_split, h_ref, x1_ref, g2_ref, wg_ref, wu_ref, wd_ref, lg_ref, lb_ref,
                oc_ref, ol_ref):
    h = h_ref[...]
    tf = wg_ref.shape[1] // n_split
    y = None
    for f in range(n_split):
        cols = slice(f * tf, (f + 1) * tf)
        act = _silu(_dot(h, wg_ref[:, cols])) * _dot(h, wu_ref[:, cols])
        part = _dot(act.astype(BF16), wd_ref[cols, :])
        y = part if y is None else y + part
    out = _layer_norm(alpha * x1_ref[...] + g2_ref[...] * y, lg_ref[...], lb_ref[...])
    is_ctx = pl.program_id(0) < n_ctx_tiles

    @pl.when(is_ctx)
    def _():
        oc_ref[...] = out

    @pl.when(jnp.logical_not(is_ctx))
    def _():
        ol_ref[...] = out


def _dense_mixer(h2, x1, mod_l, wg, wu, wd, ln_g, ln_b, alpha, tm, n_split, row_of_tile, t_ctx):
    t, d = x1.shape
    n_ctx_tiles = t_ctx // tm
    const = lambda shape: pl.BlockSpec(shape, lambda i: tuple(0 for _ in shape), pipeline_mode=pl.Buffered(1))
    return pl.pallas_call(
        functools.partial(_ffn_kernel, alpha, n_ctx_tiles, n_split),
        grid=(t // tm,),
        in_specs=[
            pl.BlockSpec((tm, d), lambda i: (i, 0)),
            pl.BlockSpec((tm, d), lambda i: (i, 0)),
            _mod_spec(5, row_of_tile),
            const(wg.shape), const(wu.shape), const(wd.shape), const((1, d)), const((1, d)),
        ],
        out_specs=_pair_specs(tm, d, n_ctx_tiles),
        out_shape=[jax.ShapeDtypeStruct((t_ctx, d), F32), jax.ShapeDtypeStruct((t - t_ctx, d), F32)],
        compiler_params=_cparams(("arbitrary",)), name="dense_mixer",
    )(h2, x1, mod_l, wg, wu, wd, ln_g, ln_b)


MOE_TILE = 256
MOE_TF = 1792
GATHER_TILES = 2


def _route_kernel(g_ref, rank_ref, gate_ref, before_ref, run_s):
    w = pl.program_id(0)

    @pl.when(w == 0)
    def _():
        run_s[...] = jnp.zeros_like(run_s)

    g = g_ref[...]
    sel = g > 0.0
    row = lax.broadcasted_iota(jnp.int32, (MOE_TILE, MOE_TILE), 0)
    col = lax.broadcasted_iota(jnp.int32, (MOE_TILE, MOE_TILE), 1)
    earlier = (row < col).astype(BF16)
    ones = jnp.where(sel, 1.0, 0.0)
    rank = _dot(ones.astype(BF16), earlier) + run_s[:, 0:1]
    rank_ref[...] = jnp.where(sel, rank, -1.0)
    gate_ref[...] = g
    before_ref[...] = run_s[...]
    run_s[...] += jnp.sum(ones, axis=1, keepdims=True)


def _route(gates_t):
    t = gates_t.shape[1]
    nw = t // MOE_TILE
    return pl.pallas_call(
        _route_kernel,
        grid=(nw,),
        in_specs=[pl.BlockSpec((N_EXPERTS, MOE_TILE), lambda w: (0, w))],
        out_specs=[
            pl.BlockSpec((None, N_EXPERTS, MOE_TILE), lambda w: (w, 0, 0)),
            pl.BlockSpec((None, N_EXPERTS, MOE_TILE), lambda w: (w, 0, 0)),
            pl.BlockSpec((None, N_EXPERTS, LANES), lambda w: (w, 0, 0)),
        ],
        out_shape=[
            jax.ShapeDtypeStruct((nw, N_EXPERTS, MOE_TILE), F32),
            jax.ShapeDtypeStruct((nw, N_EXPERTS, MOE_TILE), F32),
            jax.ShapeDtypeStruct((nw, N_EXPERTS, LANES), F32),
        ],
        scratch_shapes=[pltpu.VMEM((N_EXPERTS, LANES), F32)],
        compiler_params=_cparams(("arbitrary",)), name="moe_route",
    )(gates_t)


def _tile_onehot(rank_ref, w, e, base):
    rank_row = rank_ref[w, pl.ds(e, 1), :]
    rows = lax.broadcasted_iota(jnp.int32, (MOE_TILE, MOE_TILE), 0).astype(F32) + base.astype(F32)
    return rank_row == rows


def _gather_kernel(te_ref, base_ref, wlo_ref, whi_ref, rank_ref, x_ref, o_ref):
    o_ref[...] = jnp.zeros_like(o_ref)
    for u in range(GATHER_TILES):
        j = pl.program_id(0) * GATHER_TILES + u
        e, base = te_ref[j], base_ref[j]
        rows = slice(u * MOE_TILE, (u + 1) * MOE_TILE)

        def window(w, carry):
            p = jnp.where(_tile_onehot(rank_ref, w, e, base), 1.0, 0.0).astype(BF16)
            x_w = x_ref[pl.ds(pl.multiple_of(w * MOE_TILE, MOE_TILE), MOE_TILE), :]
            o_ref[rows, :] += _dot(p, x_w).astype(BF16)
            return carry

        lax.fori_loop(wlo_ref[j], whi_ref[j] + 1, window, 0)


def _moe_gather(meta, rank_t, h2, n_tiles):
    t, d = h2.shape
    nw = t // MOE_TILE
    return pl.pallas_call(
        _gather_kernel,
        grid_spec=pltpu.PrefetchScalarGridSpec(
            num_scalar_prefetch=4, grid=(n_tiles // GATHER_TILES,),
            in_specs=[
                pl.BlockSpec((nw, N_EXPERTS, MOE_TILE), lambda j, *_: (0, 0, 0)),
                pl.BlockSpec((t, d), lambda j, *_: (0, 0), pipeline_mode=pl.Buffered(1)),
            ],
            out_specs=pl.BlockSpec((GATHER_TILES * MOE_TILE, d), lambda j, *_: (j, 0)),
        ),
        out_shape=jax.ShapeDtypeStruct((n_tiles * MOE_TILE, d), BF16),
        compiler_params=_cparams(("arbitrary",)), name="moe_gather",
    )(*meta, rank_t, h2)


F8 = jnp.float8_e4m3fn
F8_TARGET = 224.0
F8_TINY = 1e-30


def _f8_quantize(x):
    m = jnp.max(jnp.max(jnp.abs(x), axis=1, keepdims=True), axis=0, keepdims=True)
    m = jnp.maximum(m, F8_TINY)
    return (x * (F8_TARGET / m)).astype(F8), m * (1.0 / F8_TARGET)


def _moe_up_kernel(te_ref, chg_ref, nused_ref, x_ref, wg_ref, wu_ref, o_ref, wg_s, wu_s, inv_s):
    j = pl.program_id(1)

    @pl.when(chg_ref[j] == 1)
    def _():
        for row, (w_ref, w_s) in enumerate(((wg_ref, wg_s), (wu_ref, wu_s))):
            w_s[...], inv = _f8_quantize(w_ref[...])
            inv_s[row:row + 1, :] = jnp.broadcast_to(inv, (1, LANES))

    @pl.when(j < nused_ref[0])
    def _():
        x8, inv_x = _f8_quantize(x_ref[...].astype(F32))
        a = _dot(x8, wg_s[...]) * (inv_x * inv_s[0:1, 0:1])
        u = _dot(x8, wu_s[...]) * (inv_x * inv_s[1:2, 0:1])
        o_ref[...] = (_silu(a) * u).astype(BF16)

    @pl.when(j >= nused_ref[0])
    def _():
        o_ref[...] = jnp.zeros_like(o_ref)


def _moe_up(te, chg, nused, xs, wg, wu):
    rows, d = xs.shape
    n_tiles = rows // MOE_TILE
    ff = wg.shape[2]
    return pl.pallas_call(
        _moe_up_kernel,
        grid_spec=pltpu.PrefetchScalarGridSpec(
            num_scalar_prefetch=3, grid=(ff // MOE_TF, n_tiles),
            in_specs=[
                pl.BlockSpec((MOE_TILE, d), lambda f, j, *_: (j, 0)),
                pl.BlockSpec((None, d, MOE_TF), lambda f, j, te, *_: (te[j], 0, f)),
                pl.BlockSpec((None, d, MOE_TF), lambda f, j, te, *_: (te[j], 0, f)),
            ],
            out_specs=pl.BlockSpec((MOE_TILE, MOE_TF), lambda f, j, *_: (j, f)),
            scratch_shapes=[pltpu.VMEM((d, MOE_TF), F8), pltpu.VMEM((d, MOE_TF), F8),
                            pltpu.VMEM((SUBLANES, LANES), F32)],
        ),
        out_shape=jax.ShapeDtypeStruct((rows, ff), BF16),
        compiler_params=_cparams(("arbitrary", "arbitrary")), name="moe_up",
    )(te, chg, nused, xs, wg, wu)


def _moe_down_kernel(te_ref, chg_ref, nused_ref, a_ref, wd_ref, o_ref, wd_s, inv_s):
    j = pl.program_id(0)

    @pl.when(chg_ref[j] == 1)
    def _():
        wd_s[...], inv = _f8_quantize(wd_ref[...])
        inv_s[0:1, :] = jnp.broadcast_to(inv, (1, LANES))

    @pl.when(j < nused_ref[0])
    def _():
        a8, inv_a = _f8_quantize(a_ref[...].astype(F32))
        o_ref[...] = (_dot(a8, wd_s[...]) * (inv_a * inv_s[0:1, 0:1])).astype(BF16)

    @pl.when(j >= nused_ref[0])
    def _():
        o_ref[...] = jnp.zeros_like(o_ref)


def _moe_down(te, chg, nused, act, wd):
    rows, ff = act.shape
    d = wd.shape[2]
    return pl.pallas_call(
        _moe_down_kernel,
        grid_spec=pltpu.PrefetchScalarGridSpec(
            num_scalar_prefetch=3, grid=(rows // MOE_TILE,),
            in_specs=[
                pl.BlockSpec((MOE_TILE, ff), lambda j, *_: (j, 0)),
                pl.BlockSpec((None, ff, d), lambda j, te, *_: (te[j], 0, 0)),
            ],
            out_specs=pl.BlockSpec((MOE_TILE, d), lambda j, *_: (j, 0)),
            scratch_shapes=[pltpu.VMEM((ff, d), F8), pltpu.VMEM((SUBLANES, LANES), F32)],
        ),
        out_shape=jax.ShapeDtypeStruct((rows, d), BF16),
        compiler_params=_cparams(("arbitrary",)), name="moe_down",
    )(te, chg, nused, act, wd)


def _combine_kernel(alpha, nw_ctx, blk_a_ref, blk_b_ref, rel_ref, two_ref, rank_ref, gate_ref, *refs):
    ya_refs, yb_refs = refs[:N_EXPERTS], refs[N_EXPERTS:2 * N_EXPERTS]
    x1_ref, g2_ref, lg_ref, lb_ref, oc_ref, ol_ref, acc_s = refs[2 * N_EXPERTS:]
    w = pl.program_id(0)
    row_id = lax.broadcasted_iota(jnp.int32, (MOE_TILE, MOE_TILE), 0).astype(F32)

    def weights(e, rel):
        hit = rank_ref[e:e + 1, :] == row_id + rel.astype(F32)
        return jnp.where(hit, gate_ref[e:e + 1, :], 0.0).T.astype(BF16)

    acc = None
    for e in range(N_EXPERTS):
        part = _dot(weights(e, rel_ref[w * N_EXPERTS + e]), ya_refs[e][...])
        acc = part if acc is None else acc + part
    acc_s[...] = acc
    for e in range(N_EXPERTS):
        @pl.when(two_ref[w * N_EXPERTS + e] == 1)
        def _():
            acc_s[...] += _dot(weights(e, rel_ref[w * N_EXPERTS + e] + MOE_TILE), yb_refs[e][...])

    out = _layer_norm(alpha * x1_ref[...] + g2_ref[...] * acc_s[...], lg_ref[...], lb_ref[...])

    @pl.when(w < nw_ctx)
    def _():
        oc_ref[...] = out

    @pl.when(w >= nw_ctx)
    def _():
        ol_ref[...] = out


def _moe_combine(bands, rank_t, gate_t, y, x1, mod_l, ln_g, ln_b, alpha, row_of_tile, t_ctx):
    t, d = x1.shape
    nw = t // MOE_TILE
    nw_ctx = t_ctx // MOE_TILE

    def y_spec(which, e):
        return pl.BlockSpec((MOE_TILE, d), lambda w, *pre: (pre[which][w * N_EXPERTS + e], 0))

    win_spec = pl.BlockSpec((None, N_EXPERTS, MOE_TILE), lambda w, *_: (w, 0, 0))
    return pl.pallas_call(
        functools.partial(_combine_kernel, alpha, nw_ctx),
        grid_spec=pltpu.PrefetchScalarGridSpec(
            num_scalar_prefetch=4, grid=(nw,),
            in_specs=[win_spec, win_spec]
            + [y_spec(0, e) for e in range(N_EXPERTS)] + [y_spec(1, e) for e in range(N_EXPERTS)]
            + [
                pl.BlockSpec((MOE_TILE, d), lambda w, *_: (w, 0)),
                pl.BlockSpec((None, None, 1, d), lambda w, *_: (row_of_tile(w), 5, 0, 0)),
                pl.BlockSpec((1, d), lambda w, *_: (0, 0)),
                pl.BlockSpec((1, d), lambda w, *_: (0, 0)),
            ],
            out_specs=[
                pl.BlockSpec((MOE_TILE, d), lambda w, *_: (jnp.minimum(w, nw_ctx - 1), 0)),
                pl.BlockSpec((MOE_TILE, d), lambda w, *_: (jnp.maximum(w - nw_ctx, 0), 0)),
            ],
            scratch_shapes=[pltpu.VMEM((MOE_TILE, d), F32)],
        ),
        out_shape=[jax.ShapeDtypeStruct((t_ctx, d), F32), jax.ShapeDtypeStruct((t - t_ctx, d), F32)],
        compiler_params=_cparams(("arbitrary",)), name="moe_combine",
    )(*bands, rank_t, gate_t, *([y] * (2 * N_EXPERTS)), x1, mod_l, ln_g, ln_b)


def _moe_layer(gates, h2, x1, mod_l, wg, wu, wd, ln_g, ln_b, alpha, row_of_tile, t_ctx):
    t = h2.shape[0]
    n_tiles = -(-(2 * t + N_EXPERTS * (MOE_TILE - 1)) // MOE_TILE)
    n_tiles += -n_tiles % GATHER_TILES
    rank_t, gate_t, before = _route(gates)
    before = before[:, :, 0].astype(jnp.int32)
    counts = jnp.sum(gates > 0.0, axis=1).astype(jnp.int32)
    tiles_e = (counts + MOE_TILE - 1) // MOE_TILE
    tile_end = jnp.cumsum(tiles_e)
    n_used = tile_end[-1]
    j = jnp.arange(n_tiles, dtype=jnp.int32)
    te = jnp.minimum(jnp.sum(j[:, None] >= tile_end[None, :], axis=1), N_EXPERTS - 1).astype(jnp.int32)
    base = (j - jnp.take(tile_end - tiles_e, te)) * MOE_TILE
    last = jnp.minimum(base + MOE_TILE - 1, jnp.take(counts, te) - 1)
    before_t = jnp.take(before, te, axis=1)
    used = j < n_used
    wlo = jnp.where(used, jnp.sum(before_t <= base[None, :], axis=0) - 1, 1).astype(jnp.int32)
    whi = jnp.where(used, jnp.sum(before_t <= last[None, :], axis=0) - 1, 0).astype(jnp.int32)
    chg = jnp.concatenate([jnp.ones((1,), jnp.int32), (te[1:] != te[:-1]).astype(jnp.int32)])
    nused = n_used.reshape(1).astype(jnp.int32)
    meta = (te, base.astype(jnp.int32), wlo, whi)
    xs = _moe_gather(meta, rank_t, h2, n_tiles)
    act = _moe_up(te, chg, nused, xs, wg, wu)
    y = _moe_down(te, chg, nused, act, wd)
    start_row = (tile_end - tiles_e) * MOE_TILE
    n_we = jnp.concatenate([before[1:], counts[None, :]], axis=0) - before
    first = start_row[None, :] + before
    blk_a = jnp.minimum(first // MOE_TILE, n_tiles - 1)
    two = ((n_we > 0) & ((first + n_we - 1) // MOE_TILE > blk_a)).astype(jnp.int32)
    rel = blk_a * MOE_TILE - start_row[None, :]
    bands = tuple(a.reshape(-1).astype(jnp.int32) for a in (blk_a, blk_a + two, rel, two))
    return _moe_combine(bands, rank_t, gate_t, y, x1, mod_l, ln_g, ln_b, alpha, row_of_tile, t_ctx)


def _rope_tables(seq):
    half = GLA_DK // 2
    n = half // 2
    t = np.arange(seq)
    inv = ROPE_BASE ** (-np.arange(n, dtype=np.float64) / n)
    cos = np.zeros((seq, GLA_DK), np.float64)
    sin = np.zeros((seq, GLA_DK), np.float64)
    for a, pos in enumerate((t // GRID_W, t % GRID_W)):
        ang = pos[:, None].astype(np.float64) * inv[None, :]
        base = a * half
        cos[:, base:base + n] = np.cos(ang)
        cos[:, base + n:base + half] = np.cos(ang)
        sin[:, base:base + n] = -np.sin(ang)
        sin[:, base + n:base + half] = np.sin(ang)
    tile = lambda m: jnp.asarray(np.tile(m, (1, GLA_HEADS)), F32)
    return tile(cos), tile(sin)


W_IN_GD = 1536
W_IN_KV = (2080, 2592)
W_IN_CHUNKS = (0, 512, 1024, 1568, 3104, 3616, 4128, 4640, 5152, 5664, 6176, 6688)


def _chunk_w_in(w):
    bf = lambda a: a.astype(BF16)
    wt = jnp.swapaxes(w, 1, 2)
    gd = jnp.pad(wt[:, W_IN_GD:W_IN_GD + 2 * GLA_RANK], ((0, 0), (0, LANES - 2 * GLA_RANK), (0, 0)))
    kv = jnp.stack([wt[:, s:s + CHUNK_W] for s in W_IN_KV], axis=1)
    return bf(gd), bf(kv), bf(jnp.stack([wt[:, s:s + CHUNK_W] for s in W_IN_CHUNKS], axis=1))


def _gate_up_blocks(w_gup):
    qkw = w_gup.shape[-1]
    fwd = jnp.pad(w_gup[:, 0], ((0, 0), (0, 0), (0, qkw)))
    bwd = jnp.pad(w_gup[:, 1], ((0, 0), (0, 0), (qkw, 0)))
    return jnp.pad(jnp.concatenate([fwd, bwd], axis=1), ((0, 0), (0, LANES - 2 * GLA_RANK), (0, 0)))


def kernel(x_prompt, x_sample, cache_na_k, cache_na_v, state_gla, c, c_ctx, w_mod, b_mod, w_in, gla_w_gup, gla_b_g, gla_norm, w_br_gla, na_rpb, w_br_na, cv_w, cv_b, cv_ln_g, cv_ln_b, w_br_cv, w_out, ln_g, ln_b, ffd_w_gate, ffd_w_up, ffd_w_down, moe_w_router, moe_w_gate, moe_w_up, moe_w_down):
    nb_c, l_c, d = x_prompt.shape
    nb_l, l_l, _ = x_sample.shape
    depth = w_mod.shape[0]
    t_ctx, t_lat = nb_c * l_c, nb_l * l_l
    t_all = t_ctx + t_lat
    alpha = (2 * depth) ** 0.25
    tm = 512
    row_of_tile = _row_of_tile_fn(t_ctx, l_l, tm)

    x_pair = (x_prompt.reshape(t_ctx, d), x_sample.reshape(t_lat, d))
    cond8 = jnp.concatenate([c_ctx[None, :], c, jnp.zeros((8 - 1 - nb_l, d), F32)], axis=0)
    mod = _modulation(cond8, w_mod, b_mod).reshape(depth, 8, 6, 1, d)
    rope = _rope_tables(l_l)
    rows_lat = l_l // GRID_W
    kr = min(NA_WIN_R, rows_lat)
    s0_all = state_gla.reshape(nb_l, depth, 2, 2, LANES, GLA_DV)
    w_gd, w_kv, w_chunks = _chunk_w_in(w_in)
    wg_blks = _gate_up_blocks(gla_w_gup)
    toep = _na_bias_tables(na_rpb)
    w_br = [w.astype(BF16) for w in (w_br_gla, w_br_na, w_br_cv, w_out)]

    caches = None
    states = None
    for l in range(depth):
        mod_l = mod[l]
        gd, kc, vc, kl, vl, zb = _input_projection(x_pair, mod_l, (w_gd, w_kv, w_chunks), l, depth, l_c,
                                                   tm, row_of_tile, caches)
        caches = (kc, vc)

        wg_blk = wg_blks[l]
        bg = gla_b_g[l].reshape(1, CHUNK_W)
        gn = gla_norm[l].reshape(1, CHUNK_W)
        ya, states = _gla_branch(gd, zb, wg_blk, bg, gn, latent=False, seq=l_c, nb=nb_c, tok0=0, t_total=t_all,
                                 state_slot=(l, depth, states))
        ya = _gla_branch(gd, zb, wg_blk, bg, gn, latent=True, seq=l_l, nb=nb_l, tok0=t_ctx, t_total=t_all,
                         rope=rope, s0=s0_all[:, l], ya_prev=ya)

        yb = _attn_context(zb, kc, vc, l, nb_c, l_c, t_all)
        yb = _attn_latent(zb, kl, vl, cache_na_k, cache_na_v, l, toep, yb, nb_l, l_l, t_ctx)

        conv_args = (zb, cv_w[l], cv_b[l][None, :], cv_ln_g[l][None, :], cv_ln_b[l][None, :])
        yc = _conv_branch(*conv_args, seq=l_c, nb=nb_c, tok0=0, t_total=t_all)
        yc = _conv_branch(*conv_args, seq=l_l, nb=nb_l, tok0=t_ctx, t_total=t_all, prev=yc)

        moe = l % 2 == 1
        w_router = None
        if moe:
            w_router = moe_w_router[l // 2].T
        merged = _merge(ya, yb, yc, zb, x_pair, mod_l, w_br[0][l], w_br[1][l], w_br[2][l], w_br[3][l],
                        ln_g[l, 0][None, :], ln_b[l, 0][None, :], alpha, tm, row_of_tile, w_router)
        ln2 = (ln_g[l, 1][None, :], ln_b[l, 1][None, :])
        if moe:
            x1, h2, gates = merged
            x_pair = _moe_layer(gates, h2, x1, mod_l, moe_w_gate[l // 2], moe_w_up[l // 2], moe_w_down[l // 2],
                                *ln2, alpha, _row_of_tile_fn(t_ctx, l_l, MOE_TILE), t_ctx)
        else:
            x1, h2 = merged
            i = l // 2
            x_pair = _dense_mixer(h2, x1, mod_l, ffd_w_gate[i].astype(BF16), ffd_w_up[i].astype(BF16),
                                  ffd_w_down[i].astype(BF16), *ln2, alpha, tm, 1, row_of_tile, t_ctx)

    y_prompt = x_pair[0].reshape(nb_c, l_c, d)
    y_sample = x_pair[1].reshape(nb_l, l_l, d)
    cache_shape = (nb_c, depth, l_c, NA_HEADS, NA_DH)
    return (y_prompt, y_sample, caches[0].reshape(cache_shape), caches[1].reshape(cache_shape),
            states.reshape(nb_c, depth, 2, GLA_HEADS, GLA_DK, GLA_DV))


def _row_of_tile_fn(t_ctx, l_lat, tm):
    n_ctx = t_ctx // tm

    def row_of_tile(i):
        return jnp.where(i < n_ctx, 0, 1 + (i - n_ctx) // (l_lat // tm))

    return row_of_tile
```

```python
import functools

import numpy as np
import jax
import jax.numpy as jnp
from jax import lax
from jax.experimental import pallas as pl
from jax.experimental.pallas import tpu as pltpu

F32 = jnp.float32
BF16 = jnp.bfloat16

GRID_W = 64
GLA_HEADS = 4
GLA_DK = 64
GLA_DV = 128
GLA_RANK = 16
GLA_TAU = 16.0
GLA_CHUNK = 64
GLA_GROUP = 256
GLA_CTX_SEQS = 2
NA_HEADS = 8
NA_DH = 64
NA_WIN_R = 8
NA_WIN_C = 16
CONV_CH = 512
CONV_K = 31
N_EXPERTS = 8
ROPE_BASE = 10000.0
LN_EPS = 1e-5

D_MODEL = 1024
LANES = 128
SUBLANES = 8
CHUNK_W = 512
N_BF16_CHUNKS = 12
NEG_BIG = -1e30
VMEM_LIMIT = 56 * 1024 * 1024


def _cparams(sem, vmem=VMEM_LIMIT):
    return pltpu.CompilerParams(dimension_semantics=sem, vmem_limit_bytes=vmem)


def _dot(a, b):
    return jnp.dot(a, b, preferred_element_type=F32)


def _split(x):
    hi = x.astype(BF16)
    return hi, (x - hi.astype(F32)).astype(BF16)


def _dot_split(a, b):
    a_hi, a_lo = _split(a)
    b_hi, b_lo = _split(b)
    return _dot(a_hi, b_hi) + _dot(a_lo, b_hi) + _dot(a_hi, b_lo)


def _dot_split_rhs(a, b):
    b_hi, b_lo = _split(b)
    return _dot(a, b_hi) + _dot(a, b_lo)


def _dot_nt(a, b):
    return lax.dot_general(a, b, (((1,), (1,)), ((), ())), preferred_element_type=F32)


def _sigmoid(x):
    return 0.5 * jnp.tanh(0.5 * x) + 0.5


def _silu(x):
    return x * _sigmoid(x)


def _layer_norm(x, g, b):
    mu = jnp.mean(x, axis=-1, keepdims=True)
    xc = x - mu
    var = jnp.mean(xc * xc, axis=-1, keepdims=True)
    return xc * lax.rsqrt(var + LN_EPS) * g + b


def _mod_kernel(c_ref, w_ref, b_ref, o_ref):
    c = c_ref[...]
    s = _silu(c).astype(BF16)
    o_ref[...] = _dot(s, w_ref[...].astype(BF16)) + b_ref[...]


def _modulation(cond8, w_mod, b_mod):
    depth, d, n = w_mod.shape
    tn = 1024
    return pl.pallas_call(
        _mod_kernel,
        grid=(depth, n // tn),
        in_specs=[
            pl.BlockSpec((8, d), lambda l, j: (0, 0)),
            pl.BlockSpec((None, d, tn), lambda l, j: (l, 0, j)),
            pl.BlockSpec((None, 1, tn), lambda l, j: (l, 0, j)),
        ],
        out_specs=pl.BlockSpec((None, 8, tn), lambda l, j: (l, 0, j)),
        out_shape=jax.ShapeDtypeStruct((depth, 8, n), F32),
        compiler_params=_cparams(("arbitrary", "arbitrary")),
        name="modulation",
    )(cond8, w_mod, b_mod.reshape(depth, 1, n))


def _inproj_kernel(n_ctx_tiles, aliased, *refs):
    if aliased:
        (xc_ref, xl_ref, sc_ref, sh_ref, w_ref, _kc_in, _vc_in,
         gd_ref, kc_ref, vc_ref, kl_ref, vl_ref, zb_ref, h_ref) = refs
    else:
        (xc_ref, xl_ref, sc_ref, sh_ref, w_ref,
         gd_ref, kc_ref, vc_ref, kl_ref, vl_ref, zb_ref, h_ref) = refs
    is_ctx = pl.program_id(0) < n_ctx_tiles
    x = jnp.where(is_ctx, xc_ref[...], xl_ref[...])
    h_ref[...] = (x * (1.0 + sc_ref[...]) + sh_ref[...]).astype(BF16)

    def project(start, width):
        return _dot_nt(h_ref[...], w_ref[start:start + width, :])

    gd_ref[...] = project(W_IN_GD, LANES)
    k = project(W_IN_KV[0], CHUNK_W)
    v = project(W_IN_KV[1], CHUNK_W)
    seqs, l_c, _ = kc_ref.shape

    @pl.when(is_ctx)
    def _():
        for s in range(seqs):
            kc_ref[s] = k[s * l_c:(s + 1) * l_c]
            vc_ref[s] = v[s * l_c:(s + 1) * l_c]

    @pl.when(jnp.logical_not(is_ctx))
    def _():
        kl_ref[...] = k
        vl_ref[...] = v

    for j, start in enumerate(W_IN_CHUNKS):
        zb_ref[j] = project(start, CHUNK_W).astype(BF16)


def _mod_spec(piece, row_of_tile):
    return pl.BlockSpec((None, None, 1, D_MODEL), lambda i: (row_of_tile(i), piece, 0, 0))


def _pair_specs(tm, d, n_ctx_tiles):
    return [pl.BlockSpec((tm, d), lambda i, *_: (jnp.minimum(i, n_ctx_tiles - 1), 0)),
            pl.BlockSpec((tm, d), lambda i, *_: (jnp.maximum(i - n_ctx_tiles, 0), 0))]


def _input_projection(x_pair, mod_l, w_t, layer, depth, l_c, tm, row_of_tile, caches=None):
    xc, xl = x_pair
    d = xc.shape[1]
    t_ctx, t_lat = xc.shape[0], xl.shape[0]
    t = t_ctx + t_lat
    n_ctx_tiles = t_ctx // tm
    seqs = tm // l_c
    cache_spec = pl.BlockSpec((seqs, None, l_c, CHUNK_W),
                              lambda i: (jnp.minimum(i, n_ctx_tiles - 1), layer, 0, 0))
    lat_spec = pl.BlockSpec((tm, CHUNK_W), lambda i: (jnp.maximum(i - n_ctx_tiles, 0), 0))
    in_specs = _pair_specs(tm, d, n_ctx_tiles) + [
        _mod_spec(1, row_of_tile), _mod_spec(0, row_of_tile),
        pl.BlockSpec((None,) + w_t.shape[1:], lambda i: (layer, 0, 0), pipeline_mode=pl.Buffered(1)),
    ]
    args = [xc, xl, mod_l, mod_l, w_t]
    aliases = {}
    if caches is not None:
        in_specs += [pl.BlockSpec(memory_space=pl.ANY), pl.BlockSpec(memory_space=pl.ANY)]
        args += list(caches)
        aliases = {5: 1, 6: 2}
    cache_shape = jax.ShapeDtypeStruct((t_ctx // l_c, depth, l_c, CHUNK_W), F32)
    return pl.pallas_call(
        functools.partial(_inproj_kernel, n_ctx_tiles, caches is not None),
        grid=(t // tm,),
        in_specs=in_specs,
        out_specs=[
            pl.BlockSpec((tm, LANES), lambda i: (i, 0)),
            cache_spec, cache_spec, lat_spec, lat_spec,
            pl.BlockSpec((N_BF16_CHUNKS, tm, CHUNK_W), lambda i: (0, i, 0)),
        ],
        out_shape=[
            jax.ShapeDtypeStruct((t, LANES), F32),
            cache_shape, cache_shape,
            jax.ShapeDtypeStruct((t_lat, CHUNK_W), F32), jax.ShapeDtypeStruct((t_lat, CHUNK_W), F32),
            jax.ShapeDtypeStruct((N_BF16_CHUNKS, t, CHUNK_W), BF16),
        ],
        scratch_shapes=[pltpu.VMEM((tm, d), BF16)],
        input_output_aliases=aliases,
        compiler_params=_cparams(("arbitrary",)),
        name="input_projection",
    )(*args)


def _gla_kernel(latent, seq, n_seq, state_alias, *refs):
    if latent:
        (qk_ref, v_ref, ra_ref, gd_ref, wg_ref, bg_ref, gn_ref, cos_ref, sin_ref, s0_ref, _alias,
         ya_ref, q_s, k_s, g_s, acc_s, st_s) = refs
    elif state_alias:
        (qk_ref, v_ref, ra_ref, gd_ref, wg_ref, bg_ref, gn_ref, _alias,
         ya_ref, sfin_ref, q_s, k_s, g_s, acc_s, st_s) = refs
    else:
        (qk_ref, v_ref, ra_ref, gd_ref, wg_ref, bg_ref, gn_ref,
         ya_ref, sfin_ref, q_s, k_s, g_s, acc_s, st_s) = refs
    c = GLA_CHUNK
    n_chunks = n_seq * seq // c
    qkw = GLA_HEADS * GLA_DK

    q = qk_ref[:, 0:qkw].astype(F32) * (GLA_DK ** -0.5)
    k = qk_ref[:, qkw:2 * qkw].astype(F32)
    if latent:
        lane = lax.broadcasted_iota(jnp.int32, (1, qkw), 1)
        first_half = (lane % 32) < 16

        def rope(x):
            swapped = jnp.where(first_half, pltpu.roll(x, qkw - 16, 1), pltpu.roll(x, 16, 1))
            return x * cos_ref[...] + swapped * sin_ref[...]

        q, k = rope(q), rope(k)
    q_s[...] = q
    k_s[...] = k

    glog = _dot_split(gd_ref[...], wg_ref[...]) + bg_ref[...]
    g_s[...] = (jnp.minimum(glog, 0.0) - jnp.log(1.0 + jnp.exp(-jnp.abs(glog)))) * (1.0 / GLA_TAU)
    acc_s[...] = jnp.zeros_like(acc_s)
    if latent:
        st_s[0] = s0_ref[...]
    else:
        st_s[...] = jnp.zeros_like(st_s)

    grp = GLA_GROUP
    cpg = grp // c
    n_groups = seq // grp
    grow = lax.broadcasted_iota(jnp.int32, (grp, grp), 0)
    gcol = lax.broadcasted_iota(jnp.int32, (grp, grp), 1)
    same_chunk = (grow // c) == (gcol // c)
    keep = (same_chunk & (gcol <= grow), same_chunk & (gcol >= grow))
    tri = (keep[0].astype(BF16), keep[1].astype(BF16))
    lane128 = lax.broadcasted_iota(jnp.int32, (1, LANES), 1)
    head_mask = (lane128 < GLA_DK, lane128 >= GLA_DK)
    urow = lax.broadcasted_iota(jnp.int32, (cpg * LANES, GLA_DV), 0)
    srow = lax.broadcasted_iota(jnp.int32, (LANES, GLA_DV), 0)
    chunk_of_col = lax.broadcasted_iota(jnp.int32, (LANES, grp), 1) // c
    chunk_of_row = lax.broadcasted_iota(jnp.int32, (grp, LANES), 0) // c
    pad_rows = jnp.zeros((SUBLANES - cpg, qkw), F32)

    def group_step(i, carry):
        for sq, d in ((sq, d) for sq in range(n_seq) for d in range(2)):
            gi = i if d == 0 else n_groups - 1 - i
            rows = pl.ds(pl.multiple_of(sq * seq + gi * grp, grp), grp)
            cum = _dot_split_rhs(tri[d], g_s[rows, d * qkw:(d + 1) * qkw])
            edge = c - 1 if d == 0 else 0
            lasts = [cum[ci * c + edge:ci * c + edge + 1, :] for ci in range(cpg)]
            last = jnp.concatenate([jnp.broadcast_to(r, (c, qkw)) for r in lasts], axis=0)
            ref = 0.5 * last
            q = q_s[rows, :]
            k = k_s[rows, :]
            qe = q * jnp.exp(cum - ref)
            ke = k * jnp.exp(ref - cum)
            qe2 = q * jnp.exp(cum)
            ke2 = k * jnp.exp(last - cum)
            last8 = jnp.concatenate(lasts + [pad_rows], axis=0)
            for p in range(2):
                sl = slice(p * LANES, (p + 1) * LANES)
                ke_p = ke[:, sl].astype(BF16)
                ke2_t = ke2[:, sl].T
                ke2_blk = jnp.concatenate(
                    [jnp.where(chunk_of_col == ci, ke2_t, 0.0) for ci in range(cpg)], axis=0).astype(BF16)
                dec = jnp.exp(last8[:, sl]).T
                psl = slice(2 * p * GLA_DV, (2 * p + 2) * GLA_DV)
                v_pair = v_ref[rows, psl]
                u = _dot(ke2_blk, v_pair)
                upd = jnp.where(urow % LANES < GLA_DK, u[:, :GLA_DV], u[:, GLA_DV:])
                s = st_s[sq, d, p]
                s_in = [None] * cpg
                for ci in (range(cpg) if d == 0 else range(cpg - 1, -1, -1)):
                    s_in[ci] = jnp.concatenate(
                        [jnp.where(srow < GLA_DK, s, 0.0), jnp.where(srow >= GLA_DK, s, 0.0)], axis=1).astype(BF16)
                    s = dec[:, ci:ci + 1] * s + upd[ci * LANES:(ci + 1) * LANES]
                st_s[sq, d, p] = s
                s_stack = jnp.concatenate(s_in, axis=0)
                a2_blk = jnp.concatenate(
                    [jnp.where(chunk_of_row == ci, qe2[:, sl], 0.0) for ci in range(cpg)], axis=1).astype(BF16)
                intra = []
                for hh in range(2):
                    a = jnp.where(head_mask[hh], qe[:, sl], 0.0).astype(BF16)
                    att = jnp.where(keep[d], _dot_nt(a, ke_p), 0.0).astype(BF16)
                    intra.append(_dot(att, v_pair[:, hh * GLA_DV:(hh + 1) * GLA_DV]))
                acc_s[rows, psl] += jnp.concatenate(intra, axis=1) + _dot(a2_blk, s_stack)
        return carry

    lax.fori_loop(0, n_groups, group_step, 0)

    def epilogue(i, carry):
        rows = pl.ds(pl.multiple_of(i * c, c), c)
        for h in range(GLA_HEADS):
            vsl = slice(h * GLA_DV, (h + 1) * GLA_DV)
            o = acc_s[rows, vsl]
            o = o * lax.rsqrt(jnp.mean(o * o, axis=-1, keepdims=True) + LN_EPS) * gn_ref[:, vsl]
            ya_ref[rows, vsl] = (o * _silu(ra_ref[rows, vsl].astype(F32))).astype(BF16)
        return carry

    lax.fori_loop(0, n_chunks, epilogue, 0)
    if not latent:
        sfin_ref[...] = st_s[...]


def _gla_branch(gd, zb, wg_blk, bg, gn, *, latent, seq, nb, tok0, t_total, rope=None, s0=None, ya_prev=None,
                state_slot=None):
    n_seq = 1 if latent else GLA_CTX_SEQS
    rows = n_seq * seq
    b0 = tok0 // rows
    qkw = GLA_HEADS * GLA_DK
    in_specs = [
        pl.BlockSpec((None, rows, CHUNK_W), lambda b: (0, b0 + b, 0)),
        pl.BlockSpec((None, rows, CHUNK_W), lambda b: (1, b0 + b, 0)),
        pl.BlockSpec((None, rows, CHUNK_W), lambda b: (2, b0 + b, 0)),
        pl.BlockSpec((rows, LANES), lambda b: (b0 + b, 0)),
        pl.BlockSpec((LANES, CHUNK_W), lambda b: (0, 0)),
        pl.BlockSpec((1, CHUNK_W), lambda b: (0, 0)),
        pl.BlockSpec((1, CHUNK_W), lambda b: (0, 0)),
    ]
    args = [zb, zb, zb, gd, wg_blk, bg, gn]
    scratch = [
        pltpu.VMEM((rows, qkw), F32), pltpu.VMEM((rows, qkw), F32), pltpu.VMEM((rows, CHUNK_W), F32),
        pltpu.VMEM((rows, CHUNK_W), F32), pltpu.VMEM((n_seq, 2, 2, LANES, GLA_DV), F32),
    ]
    ya_shape = jax.ShapeDtypeStruct((t_total, CHUNK_W), BF16)
    ya_spec = pl.BlockSpec((rows, CHUNK_W), lambda b: (b0 + b, 0))
    if latent:
        cos_t, sin_t = rope
        in_specs += [
            pl.BlockSpec((seq, qkw), lambda b: (0, 0)),
            pl.BlockSpec((seq, qkw), lambda b: (0, 0)),
            pl.BlockSpec((None, 2, 2, LANES, GLA_DV), lambda b: (b, 0, 0, 0, 0)),
            pl.BlockSpec(memory_space=pl.ANY),
        ]
        args += [cos_t, sin_t, s0, ya_prev]
        return pl.pallas_call(
            functools.partial(_gla_kernel, True, seq, n_seq, False),
            grid=(nb,), in_specs=in_specs, out_specs=ya_spec, out_shape=ya_shape,
            scratch_shapes=scratch, input_output_aliases={len(args) - 1: 0},
            compiler_params=_cparams(("arbitrary",)), name="gla_latent",
        )(*args)
    layer, depth, states = state_slot
    aliases = {}
    if states is not None:
        in_specs.append(pl.BlockSpec(memory_space=pl.ANY))
        args.append(states)
        aliases = {len(args) - 1: 1}
    return pl.pallas_call(
        functools.partial(_gla_kernel, False, seq, n_seq, states is not None),
        grid=(nb // n_seq,), in_specs=in_specs,
        out_specs=[ya_spec, pl.BlockSpec((n_seq, None, 2, 2, LANES, GLA_DV), lambda b: (b, layer, 0, 0, 0, 0))],
        out_shape=[ya_shape, jax.ShapeDtypeStruct((nb, depth, 2, 2, LANES, GLA_DV), F32)],
        scratch_shapes=scratch, input_output_aliases=aliases,
        compiler_params=_cparams(("arbitrary",)), name="gla_context",
    )(*args)


ATTN_CTX_SEQS = 2


def _attn_ctx_kernel(q_ref, k_ref, v_ref, o_ref):
    lane = lax.broadcasted_iota(jnp.int32, (1, LANES), 1)
    masks = (lane < NA_DH, lane >= NA_DH)
    scale = NA_DH ** -0.5
    n_seq, seq, _ = k_ref.shape
    for b in range(n_seq):
        rows = slice(b * seq, (b + 1) * seq)
        for p in range(NA_HEADS // 2):
            sl = slice(p * LANES, (p + 1) * LANES)
            q2 = _stack_heads(q_ref[rows, sl], masks)
            kp = k_ref[b, :, sl].astype(BF16)
            vp = v_ref[b, :, sl].astype(BF16)
            s = _dot_nt(q2, kp) * scale
            e = jnp.exp(s - jnp.max(s, axis=-1, keepdims=True))
            o2 = _dot(e.astype(BF16), vp) / jnp.sum(e, axis=-1, keepdims=True)
            o_ref[rows, sl] = jnp.where(masks[0], o2[0:seq], o2[seq:2 * seq]).astype(BF16)


def _stack_heads(qp, masks):
    zero = jnp.zeros_like(qp)
    return jnp.concatenate([jnp.where(masks[0], qp, zero), jnp.where(masks[1], qp, zero)], axis=0)


def _attn_context(zb, kc, vc, layer, nb, seq, t_total):
    per = ATTN_CTX_SEQS
    return pl.pallas_call(
        _attn_ctx_kernel,
        grid=(nb // per,),
        in_specs=[
            pl.BlockSpec((None, per * seq, CHUNK_W), lambda b: (3, b, 0)),
            pl.BlockSpec((per, None, seq, CHUNK_W), lambda b: (b, layer, 0, 0)),
            pl.BlockSpec((per, None, seq, CHUNK_W), lambda b: (b, layer, 0, 0)),
        ],
        out_specs=pl.BlockSpec((per * seq, CHUNK_W), lambda b: (b, 0)),
        out_shape=jax.ShapeDtypeStruct((t_total, CHUNK_W), BF16),
        compiler_params=_cparams(("arbitrary",)), name="attention_context",
    )(zb, kc, vc)


NA_ROWS_PER_STEP = 2


def _na_kernel(rows_total, kr, q_ref, kl_ref, vl_ref, kc_ref, vc_ref, toep_ref, _alias, o_ref):
    lane = lax.broadcasted_iota(jnp.int32, (1, LANES), 1)
    masks = (lane < NA_DH, lane >= NA_DH)
    scale = NA_DH ** -0.5
    nq = GRID_W
    for rr in range(NA_ROWS_PER_STEP):
        r = pl.program_id(1) * NA_ROWS_PER_STEP + rr
        qrows = slice(rr * nq, (rr + 1) * nq)
        rs = jnp.clip(r - kr // 2, 0, rows_total - kr)
        krows = pl.ds(pl.multiple_of(rs * GRID_W, GRID_W), kr * GRID_W)
        dr0 = rs - r + NA_WIN_R - 1
        for p in range(NA_HEADS // 2):
            sl = slice(p * LANES, (p + 1) * LANES)
            q2 = _stack_heads(q_ref[qrows, sl], masks)
            klp = kl_ref[krows, sl].astype(BF16)
            vlp = vl_ref[krows, sl].astype(BF16)
            kcp = kc_ref[:, sl].astype(BF16)
            vcp = vc_ref[:, sl].astype(BF16)
            bias = jnp.concatenate(
                [jnp.concatenate([toep_ref[2 * p + hh, dr0 + 2 * m] for m in range(kr // 2)], axis=1)
                 for hh in range(2)], axis=0)
            s_loc = _dot_nt(q2, klp) * scale + bias
            s_ctx = _dot_nt(q2, kcp) * scale
            m = jnp.maximum(jnp.max(s_loc, axis=-1, keepdims=True), jnp.max(s_ctx, axis=-1, keepdims=True))
            e_loc = jnp.exp(s_loc - m)
            e_ctx = jnp.exp(s_ctx - m)
            den = jnp.sum(e_loc, axis=-1, keepdims=True) + jnp.sum(e_ctx, axis=-1, keepdims=True)
            o2 = (_dot(e_loc.astype(BF16), vlp) + _dot(e_ctx.astype(BF16), vcp)) / den
            o_ref[qrows, sl] = jnp.where(masks[0], o2[0:nq], o2[nq:2 * nq]).astype(BF16)


def _na_bias_tables(rpb):
    nc = rpb.shape[-1]
    qc = np.arange(GRID_W)[:, None]
    kc = np.arange(GRID_W)[None, :]
    cs = np.clip(qc - NA_WIN_C // 2, 0, GRID_W - NA_WIN_C)
    valid = (kc >= cs) & (kc < cs + NA_WIN_C)
    tap = np.clip(kc - qc, -(NA_WIN_C - 1), NA_WIN_C - 1) + NA_WIN_C - 1
    pick = jnp.asarray(tap[..., None] == np.arange(nc), F32)
    toep = jnp.einsum('dhab,qkb->dhaqk', rpb.astype(F32), pick, precision=lax.Precision.HIGHEST)
    toep = jnp.where(valid, toep, NEG_BIG)
    return jnp.concatenate([toep[:, :, :-1], toep[:, :, 1:]], axis=-1)


def _attn_latent(zb, kl, vl, cache_k, cache_v, layer, toep, att_prev, nb, seq, tok0):
    rows_total = seq // GRID_W
    kr = min(NA_WIN_R, rows_total)
    nq = NA_ROWS_PER_STEP * GRID_W
    steps = rows_total // NA_ROWS_PER_STEP
    q0 = tok0 // nq
    past = cache_k.shape[2]
    ck = cache_k.reshape(cache_k.shape[0], cache_k.shape[1], past, NA_HEADS * NA_DH)
    cv = cache_v.reshape(ck.shape)
    return pl.pallas_call(
        functools.partial(_na_kernel, rows_total, kr),
        grid=(nb, steps),
        in_specs=[
            pl.BlockSpec((None, nq, CHUNK_W), lambda b, r: (3, q0 + b * steps + r, 0)),
            pl.BlockSpec((seq, CHUNK_W), lambda b, r: (b, 0)),
            pl.BlockSpec((seq, CHUNK_W), lambda b, r: (b, 0)),
            pl.BlockSpec((None, None, past, CHUNK_W), lambda b, r: (b, layer, 0, 0)),
            pl.BlockSpec((None, None, past, CHUNK_W), lambda b, r: (b, layer, 0, 0)),
            pl.BlockSpec((None,) + toep.shape[1:], lambda b, r: (layer, 0, 0, 0, 0), pipeline_mode=pl.Buffered(1)),
            pl.BlockSpec(memory_space=pl.ANY),
        ],
        out_specs=pl.BlockSpec((nq, CHUNK_W), lambda b, r: (q0 + b * steps + r, 0)),
        out_shape=jax.ShapeDtypeStruct(att_prev.shape, BF16),
        input_output_aliases={6: 0},
        compiler_params=_cparams(("arbitrary", "arbitrary")), name="attention_latent",
    )(zb, kl, vl, ck, cv, toep, att_prev)


CONV_PAD = 16
CONV_ROWS = 64


def _conv_kernel(seq, aliased, *refs):
    if aliased:
        a_ref, gt_ref, w_ref, cb_ref, lg_ref, lb_ref, _alias, o_ref, z_s, sh_s = refs
    else:
        a_ref, gt_ref, w_ref, cb_ref, lg_ref, lb_ref, o_ref, z_s, sh_s = refs
    z_s[0:CONV_PAD, :] = jnp.zeros((CONV_PAD, CONV_CH), F32)
    z_s[CONV_PAD + seq:2 * CONV_PAD + seq, :] = jnp.zeros((CONV_PAD, CONV_CH), F32)
    z_s[CONV_PAD:CONV_PAD + seq, :] = a_ref[...].astype(F32) * _sigmoid(gt_ref[...].astype(F32))
    ext = seq + 2 * CONV_PAD - SUBLANES
    for b in range(1, SUBLANES):
        for r0 in range(0, ext, CONV_ROWS):
            n = min(CONV_ROWS, ext - r0)
            sh_s[b - 1, r0:r0 + n, :] = z_s[r0 + b:r0 + b + n, :]
    off = CONV_PAD - CONV_K // 2
    grp = (CONV_ROWS // SUBLANES, SUBLANES, CONV_CH)
    for t0 in range(0, seq, CONV_ROWS):
        acc = jnp.zeros(grp, F32)
        for k in range(CONV_K):
            a, b = divmod(off + k, SUBLANES)
            r0 = t0 + a * SUBLANES
            tap = z_s[r0:r0 + CONV_ROWS, :] if b == 0 else sh_s[b - 1, r0:r0 + CONV_ROWS, :]
            acc = acc + tap.reshape(grp) * w_ref[k][None]
        y = _layer_norm(acc.reshape(CONV_ROWS, CONV_CH) + cb_ref[...], lg_ref[...], lb_ref[...])
        o_ref[t0:t0 + CONV_ROWS, :] = _silu(y).astype(BF16)


def _conv_branch(zb, cv_w, cv_b, ln_g, ln_b, *, seq, nb, tok0, t_total, prev=None):
    b0 = tok0 // seq
    in_specs = [
        pl.BlockSpec((None, seq, CHUNK_W), lambda b: (4, b0 + b, 0)),
        pl.BlockSpec((None, seq, CHUNK_W), lambda b: (5, b0 + b, 0)),
        pl.BlockSpec((CONV_K, SUBLANES, CONV_CH), lambda b: (0, 0, 0)),
        pl.BlockSpec((1, CONV_CH), lambda b: (0, 0)),
        pl.BlockSpec((1, CONV_CH), lambda b: (0, 0)),
        pl.BlockSpec((1, CONV_CH), lambda b: (0, 0)),
    ]
    args = [zb, zb, jnp.broadcast_to(cv_w[:, None, :], (CONV_K, SUBLANES, CONV_CH)), cv_b, ln_g, ln_b]
    aliases = {}
    if prev is not None:
        in_specs.append(pl.BlockSpec(memory_space=pl.ANY))
        args.append(prev)
        aliases = {6: 0}
    return pl.pallas_call(
        functools.partial(_conv_kernel, seq, prev is not None),
        grid=(nb,), in_specs=in_specs,
        out_specs=pl.BlockSpec((seq, CONV_CH), lambda b: (b0 + b, 0)),
        out_shape=jax.ShapeDtypeStruct((t_total, CONV_CH), BF16),
        scratch_shapes=[pltpu.VMEM((seq + 2 * CONV_PAD, CONV_CH), F32),
                        pltpu.VMEM((SUBLANES - 1, seq + 2 * CONV_PAD - SUBLANES, CONV_CH), F32)],
        input_output_aliases=aliases,
        compiler_params=_cparams(("arbitrary",)), name="conv_module_%d" % seq,
    )(*args)


def _merge_kernel(alpha, n_ctx_tiles, with_router, *refs):
    if with_router:
        (ya_ref, yb_ref, yc_ref, mg_ref, xc_ref, xl_ref, g1_ref, sc2_ref, sh2_ref, wa_ref, wb_ref, wc_ref, wo_ref,
         lg_ref, lb_ref, wr_ref, x1_ref, h2_ref, gates_ref) = refs
    else:
        (ya_ref, yb_ref, yc_ref, mg_ref, xc_ref, xl_ref, g1_ref, sc2_ref, sh2_ref, wa_ref, wb_ref, wc_ref, wo_ref,
         lg_ref, lb_ref, x1_ref, h2_ref) = refs
    x = jnp.where(pl.program_id(0) < n_ctx_tiles, xc_ref[...], xl_ref[...])
    halves = []
    for n in range(2):
        m = None
        for j, (y_ref, w_ref) in enumerate(((ya_ref, wa_ref), (yb_ref, wb_ref), (yc_ref, wc_ref))):
            proj = _dot(y_ref[...], w_ref[:, n * CHUNK_W:(n + 1) * CHUNK_W])
            term = _sigmoid(mg_ref[2 * j + n].astype(F32)) * proj
            m = term if m is None else m + term
        halves.append(m.astype(BF16))
    mix = _dot(halves[0], wo_ref[0:CHUNK_W, :]) + _dot(halves[1], wo_ref[CHUNK_W:2 * CHUNK_W, :])
    x1 = _layer_norm(alpha * x + g1_ref[...] * mix, lg_ref[...], lb_ref[...])
    x1_ref[...] = x1
    h2 = x1 * (1.0 + sc2_ref[...]) + sh2_ref[...]
    h2_ref[...] = h2.astype(BF16)
    if with_router:
        w_hi, w_lo = _split(wr_ref[...])
        h_hi, h_lo = _split(h2)
        lg = _dot_nt(w_hi, h_hi) + _dot_nt(w_hi, h_lo) + _dot_nt(w_lo, h_hi)
        eid = lax.broadcasted_iota(jnp.int32, lg.shape, 0)
        m1 = jnp.max(lg, axis=0, keepdims=True)
        i1 = jnp.min(jnp.where(lg == m1, eid, N_EXPERTS), axis=0, keepdims=True)
        lg2 = jnp.where(eid == i1, -jnp.inf, lg)
        m2 = jnp.max(lg2, axis=0, keepdims=True)
        i2 = jnp.min(jnp.where(lg2 == m2, eid, N_EXPERTS), axis=0, keepdims=True)
        e2 = jnp.exp(m2 - m1)
        w1 = 1.0 / (1.0 + e2)
        gates_ref[...] = jnp.where(eid == i1, w1, 0.0) + jnp.where(eid == i2, e2 * w1, 0.0)


def _merge(ya, yb, yc, zb, x_pair, mod_l, wa, wb, wc, wo, ln_g, ln_b, alpha, tm, row_of_tile, w_router=None):
    xc, xl = x_pair
    d = xc.shape[1]
    t = xc.shape[0] + xl.shape[0]
    n_ctx_tiles = xc.shape[0] // tm
    full = lambda shape: pl.BlockSpec(shape, lambda i: tuple(0 for _ in shape), pipeline_mode=pl.Buffered(1))
    in_specs = [
        pl.BlockSpec((tm, CHUNK_W), lambda i: (i, 0)),
        pl.BlockSpec((tm, CHUNK_W), lambda i: (i, 0)),
        pl.BlockSpec((tm, CHUNK_W), lambda i: (i, 0)),
        pl.BlockSpec((6, tm, CHUNK_W), lambda i: (1, i, 0)),
    ] + _pair_specs(tm, d, n_ctx_tiles) + [
        _mod_spec(2, row_of_tile), _mod_spec(4, row_of_tile), _mod_spec(3, row_of_tile),
        full(wa.shape), full(wb.shape), full(wc.shape), full(wo.shape), full((1, d)), full((1, d)),
    ]
    args = [ya, yb, yc, zb, xc, xl, mod_l, mod_l, mod_l, wa, wb, wc, wo, ln_g, ln_b]
    out_specs = [pl.BlockSpec((tm, d), lambda i: (i, 0)), pl.BlockSpec((tm, d), lambda i: (i, 0))]
    out_shape = [jax.ShapeDtypeStruct((t, d), F32), jax.ShapeDtypeStruct((t, d), BF16)]
    if w_router is not None:
        in_specs.append(full(w_router.shape))
        args.append(w_router)
        out_specs.append(pl.BlockSpec((N_EXPERTS, tm), lambda i: (0, i)))
        out_shape.append(jax.ShapeDtypeStruct((N_EXPERTS, t), F32))
    return pl.pallas_call(
        functools.partial(_merge_kernel, alpha, n_ctx_tiles, w_router is not None),
        grid=(t // tm,), in_specs=in_specs, out_specs=out_specs, out_shape=out_shape,
        compiler_params=_cparams(("arbitrary",)), name="merge",
    )(*args)


def _ffn_kernel(alpha, n_ctx_tiles, n_split, h_ref, x1_ref, g2_ref, wg_ref, wu_ref, wd_ref, lg_ref, lb_ref,
                oc_ref, ol_ref):
    h = h_ref[...]
    tf = wg_ref.shape[1] // n_split
    y = None
    for f in range(n_split):
        cols = slice(f * tf, (f + 1) * tf)
        act = _silu(_dot(h, wg_ref[:, cols])) * _dot(h, wu_ref[:, cols])
        part = _dot(act.astype(BF16), wd_ref[cols, :])
        y = part if y is None else y + part
    out = _layer_norm(alpha * x1_ref[...] + g2_ref[...] * y, lg_ref[...], lb_ref[...])
    is_ctx = pl.program_id(0) < n_ctx_tiles

    @pl.when(is_ctx)
    def _():
        oc_ref[...] = out

    @pl.when(jnp.logical_not(is_ctx))
    def _():
        ol_ref[...] = out


def _dense_mixer(h2, x1, mod_l, wg, wu, wd, ln_g, ln_b, alpha, tm, n_split, row_of_tile, t_ctx):
    t, d = x1.shape
    n_ctx_tiles = t_ctx // tm
    const = lambda shape: pl.BlockSpec(shape, lambda i: tuple(0 for _ in shape), pipeline_mode=pl.Buffered(1))
    return pl.pallas_call(
        functools.partial(_ffn_kernel, alpha, n_ctx_tiles, n_split),
        grid=(t // tm,),
        in_specs=[
            pl.BlockSpec((tm, d), lambda i: (i, 0)),
            pl.BlockSpec((tm, d), lambda i: (i, 0)),
            _mod_spec(5, row_of_tile),
            const(wg.shape), const(wu.shape), const(wd.shape), const((1, d)), const((1, d)),
        ],
        out_specs=_pair_specs(tm, d, n_ctx_tiles),
        out_shape=[jax.ShapeDtypeStruct((t_ctx, d), F32), jax.ShapeDtypeStruct((t - t_ctx, d), F32)],
        compiler_params=_cparams(("arbitrary",)), name="dense_mixer",
    )(h2, x1, mod_l, wg, wu, wd, ln_g, ln_b)


MOE_TILE = 256
MOE_TF = 1792
GATHER_TILES = 2


def _route_kernel(g_ref, rank_ref, gate_ref, before_ref, run_s):
    w = pl.program_id(0)

    @pl.when(w == 0)
    def _():
        run_s[...] = jnp.zeros_like(run_s)

    g = g_ref[...]
    sel = g > 0.0
    row = lax.broadcasted_iota(jnp.int32, (MOE_TILE, MOE_TILE), 0)
    col = lax.broadcasted_iota(jnp.int32, (MOE_TILE, MOE_TILE), 1)
    earlier = (row < col).astype(BF16)
    ones = jnp.where(sel, 1.0, 0.0)
    rank = _dot(ones.astype(BF16), earlier) + run_s[:, 0:1]
    rank_ref[...] = jnp.where(sel, rank, -1.0)
    gate_ref[...] = g
    before_ref[...] = run_s[...]
    run_s[...] += jnp.sum(ones, axis=1, keepdims=True)


def _route(gates_t):
    t = gates_t.shape[1]
    nw = t // MOE_TILE
    return pl.pallas_call(
        _route_kernel,
        grid=(nw,),
        in_specs=[pl.BlockSpec((N_EXPERTS, MOE_TILE), lambda w: (0, w))],
        out_specs=[
            pl.BlockSpec((None, N_EXPERTS, MOE_TILE), lambda w: (w, 0, 0)),
            pl.BlockSpec((None, N_EXPERTS, MOE_TILE), lambda w: (w, 0, 0)),
            pl.BlockSpec((None, N_EXPERTS, LANES), lambda w: (w, 0, 0)),
        ],
        out_shape=[
            jax.ShapeDtypeStruct((nw, N_EXPERTS, MOE_TILE), F32),
            jax.ShapeDtypeStruct((nw, N_EXPERTS, MOE_TILE), F32),
            jax.ShapeDtypeStruct((nw, N_EXPERTS, LANES), F32),
        ],
        scratch_shapes=[pltpu.VMEM((N_EXPERTS, LANES), F32)],
        compiler_params=_cparams(("arbitrary",)), name="moe_route",
    )(gates_t)


def _tile_onehot(rank_ref, w, e, base):
    rank_row = rank_ref[w, pl.ds(e, 1), :]
    rows = lax.broadcasted_iota(jnp.int32, (MOE_TILE, MOE_TILE), 0).astype(F32) + base.astype(F32)
    return rank_row == rows


def _gather_kernel(te_ref, base_ref, wlo_ref, whi_ref, rank_ref, x_ref, o_ref):
    o_ref[...] = jnp.zeros_like(o_ref)
    for u in range(GATHER_TILES):
        j = pl.program_id(0) * GATHER_TILES + u
        e, base = te_ref[j], base_ref[j]
        rows = slice(u * MOE_TILE, (u + 1) * MOE_TILE)

        def window(w, carry):
            p = jnp.where(_tile_onehot(rank_ref, w, e, base), 1.0, 0.0).astype(BF16)
            x_w = x_ref[pl.ds(pl.multiple_of(w * MOE_TILE, MOE_TILE), MOE_TILE), :]
            o_ref[rows, :] += _dot(p, x_w).astype(BF16)
            return carry

        lax.fori_loop(wlo_ref[j], whi_ref[j] + 1, window, 0)


def _moe_gather(meta, rank_t, h2, n_tiles):
    t, d = h2.shape
    nw = t // MOE_TILE
    return pl.pallas_call(
        _gather_kernel,
        grid_spec=pltpu.PrefetchScalarGridSpec(
            num_scalar_prefetch=4, grid=(n_tiles // GATHER_TILES,),
            in_specs=[
                pl.BlockSpec((nw, N_EXPERTS, MOE_TILE), lambda j, *_: (0, 0, 0)),
                pl.BlockSpec((t, d), lambda j, *_: (0, 0), pipeline_mode=pl.Buffered(1)),
            ],
            out_specs=pl.BlockSpec((GATHER_TILES * MOE_TILE, d), lambda j, *_: (j, 0)),
        ),
        out_shape=jax.ShapeDtypeStruct((n_tiles * MOE_TILE, d), BF16),
        compiler_params=_cparams(("arbitrary",)), name="moe_gather",
    )(*meta, rank_t, h2)


F8 = jnp.float8_e4m3fn
F8_TARGET = 224.0
F8_TINY = 1e-30


def _f8_quantize(x):
    m = jnp.max(jnp.max(jnp.abs(x), axis=1, keepdims=True), axis=0, keepdims=True)
    m = jnp.maximum(m, F8_TINY)
    return (x * (F8_TARGET / m)).astype(F8), m * (1.0 / F8_TARGET)


def _moe_up_kernel(te_ref, chg_ref, nused_ref, x_ref, wg_ref, wu_ref, o_ref, wg_s, wu_s, inv_s):
    j = pl.program_id(1)

    @pl.when(chg_ref[j] == 1)
    def _():
        for row, (w_ref, w_s) in enumerate(((wg_ref, wg_s), (wu_ref, wu_s))):
            w_s[...], inv = _f8_quantize(w_ref[...])
            inv_s[row:row + 1, :] = jnp.broadcast_to(inv, (1, LANES))

    @pl.when(j < nused_ref[0])
    def _():
        x8, inv_x = _f8_quantize(x_ref[...].astype(F32))
        a = _dot(x8, wg_s[...]) * (inv_x * inv_s[0:1, 0:1])
        u = _dot(x8, wu_s[...]) * (inv_x * inv_s[1:2, 0:1])
        o_ref[...] = (_silu(a) * u).astype(BF16)

    @pl.when(j >= nused_ref[0])
    def _():
        o_ref[...] = jnp.zeros_like(o_ref)


def _moe_up(te, chg, nused, xs, wg, wu):
    rows, d = xs.shape
    n_tiles = rows // MOE_TILE
    ff = wg.shape[2]
    return pl.pallas_call(
        _moe_up_kernel,
        grid_spec=pltpu.PrefetchScalarGridSpec(
            num_scalar_prefetch=3, grid=(ff // MOE_TF, n_tiles),
            in_specs=[
                pl.BlockSpec((MOE_TILE, d), lambda f, j, *_: (j, 0)),
                pl.BlockSpec((None, d, MOE_TF), lambda f, j, te, *_: (te[j], 0, f)),
                pl.BlockSpec((None, d, MOE_TF), lambda f, j, te, *_: (te[j], 0, f)),
            ],
            out_specs=pl.BlockSpec((MOE_TILE, MOE_TF), lambda f, j, *_: (j, f)),
            scratch_shapes=[pltpu.VMEM((d, MOE_TF), F8), pltpu.VMEM((d, MOE_TF), F8),
                            pltpu.VMEM((SUBLANES, LANES), F32)],
        ),
        out_shape=jax.ShapeDtypeStruct((rows, ff), BF16),
        compiler_params=_cparams(("arbitrary", "arbitrary")), name="moe_up",
    )(te, chg, nused, xs, wg, wu)


def _moe_down_kernel(te_ref, chg_ref, nused_ref, a_ref, wd_ref, o_ref, wd_s, inv_s):
    j = pl.program_id(0)

    @pl.when(chg_ref[j] == 1)
    def _():
        wd_s[...], inv = _f8_quantize(wd_ref[...])
        inv_s[0:1, :] = jnp.broadcast_to(inv, (1, LANES))

    @pl.when(j < nused_ref[0])
    def _():
        a8, inv_a = _f8_quantize(a_ref[...].astype(F32))
        o_ref[...] = (_dot(a8, wd_s[...]) * (inv_a * inv_s[0:1, 0:1])).astype(BF16)

    @pl.when(j >= nused_ref[0])
    def _():
        o_ref[...] = jnp.zeros_like(o_ref)


def _moe_down(te, chg, nused, act, wd):
    rows, ff = act.shape
    d = wd.shape[2]
    return pl.pallas_call(
        _moe_down_kernel,
        grid_spec=pltpu.PrefetchScalarGridSpec(
            num_scalar_prefetch=3, grid=(rows // MOE_TILE,),
            in_specs=[
                pl.BlockSpec((MOE_TILE, ff), lambda j, *_: (j, 0)),
                pl.BlockSpec((None, ff, d), lambda j, te, *_: (te[j], 0, 0)),
            ],
            out_specs=pl.BlockSpec((MOE_TILE, d), lambda j, *_: (j, 0)),
            scratch_shapes=[pltpu.VMEM((ff, d), F8), pltpu.VMEM((SUBLANES, LANES), F32)],
        ),
        out_shape=jax.ShapeDtypeStruct((rows, d), BF16),
        compiler_params=_cparams(("arbitrary",)), name="moe_down",
    )(te, chg, nused, act, wd)


def _combine_kernel(alpha, nw_ctx, blk_a_ref, blk_b_ref, rel_ref, two_ref, rank_ref, gate_ref, *refs):
    ya_refs, yb_refs = refs[:N_EXPERTS], refs[N_EXPERTS:2 * N_EXPERTS]
    x1_ref, g2_ref, lg_ref, lb_ref, oc_ref, ol_ref, acc_s = refs[2 * N_EXPERTS:]
    w = pl.program_id(0)
    row_id = lax.broadcasted_iota(jnp.int32, (MOE_TILE, MOE_TILE), 0).astype(F32)

    def weights(e, rel):
        hit = rank_ref[e:e + 1, :] == row_id + rel.astype(F32)
        return jnp.where(hit, gate_ref[e:e + 1, :], 0.0).T.astype(BF16)

    acc = None
    for e in range(N_EXPERTS):
        part = _dot(weights(e, rel_ref[w * N_EXPERTS + e]), ya_refs[e][...])
        acc = part if acc is None else acc + part
    acc_s[...] = acc
    for e in range(N_EXPERTS):
        @pl.when(two_ref[w * N_EXPERTS + e] == 1)
        def _():
            acc_s[...] += _dot(weights(e, rel_ref[w * N_EXPERTS + e] + MOE_TILE), yb_refs[e][...])

    out = _layer_norm(alpha * x1_ref[...] + g2_ref[...] * acc_s[...], lg_ref[...], lb_ref[...])

    @pl.when(w < nw_ctx)
    def _():
        oc_ref[...] = out

    @pl.when(w >= nw_ctx)
    def _():
        ol_ref[...] = out


def _moe_combine(bands, rank_t, gate_t, y, x1, mod_l, ln_g, ln_b, alpha, row_of_tile, t_ctx):
    t, d = x1.shape
    nw = t // MOE_TILE
    nw_ctx = t_ctx // MOE_TILE

    def y_spec(which, e):
        return pl.BlockSpec((MOE_TILE, d), lambda w, *pre: (pre[which][w * N_EXPERTS + e], 0))

    win_spec = pl.BlockSpec((None, N_EXPERTS, MOE_TILE), lambda w, *_: (w, 0, 0))
    return pl.pallas_call(
        functools.partial(_combine_kernel, alpha, nw_ctx),
        grid_spec=pltpu.PrefetchScalarGridSpec(
            num_scalar_prefetch=4, grid=(nw,),
            in_specs=[win_spec, win_spec]
            + [y_spec(0, e) for e in range(N_EXPERTS)] + [y_spec(1, e) for e in range(N_EXPERTS)]
            + [
                pl.BlockSpec((MOE_TILE, d), lambda w, *_: (w, 0)),
                pl.BlockSpec((None, None, 1, d), lambda w, *_: (row_of_tile(w), 5, 0, 0)),
                pl.BlockSpec((1, d), lambda w, *_: (0, 0)),
                pl.BlockSpec((1, d), lambda w, *_: (0, 0)),
            ],
            out_specs=[
                pl.BlockSpec((MOE_TILE, d), lambda w, *_: (jnp.minimum(w, nw_ctx - 1), 0)),
                pl.BlockSpec((MOE_TILE, d), lambda w, *_: (jnp.maximum(w - nw_ctx, 0), 0)),
            ],
            scratch_shapes=[pltpu.VMEM((MOE_TILE, d), F32)],
        ),
        out_shape=[jax.ShapeDtypeStruct((t_ctx, d), F32), jax.ShapeDtypeStruct((t - t_ctx, d), F32)],
        compiler_params=_cparams(("arbitrary",)), name="moe_combine",
    )(*bands, rank_t, gate_t, *([y] * (2 * N_EXPERTS)), x1, mod_l, ln_g, ln_b)


def _moe_layer(gates, h2, x1, mod_l, wg, wu, wd, ln_g, ln_b, alpha, row_of_tile, t_ctx):
    t = h2.shape[0]
    n_tiles = -(-(2 * t + N_EXPERTS * (MOE_TILE - 1)) // MOE_TILE)
    n_tiles += -n_tiles % GATHER_TILES
    rank_t, gate_t, before = _route(gates)
    before = before[:, :, 0].astype(jnp.int32)
    counts = jnp.sum(gates > 0.0, axis=1).astype(jnp.int32)
    tiles_e = (counts + MOE_TILE - 1) // MOE_TILE
    tile_end = jnp.cumsum(tiles_e)
    n_used = tile_end[-1]
    j = jnp.arange(n_tiles, dtype=jnp.int32)
    te = jnp.minimum(jnp.sum(j[:, None] >= tile_end[None, :], axis=1), N_EXPERTS - 1).astype(jnp.int32)
    base = (j - jnp.take(tile_end - tiles_e, te)) * MOE_TILE
    last = jnp.minimum(base + MOE_TILE - 1, jnp.take(counts, te) - 1)
    before_t = jnp.take(before, te, axis=1)
    used = j < n_used
    wlo = jnp.where(used, jnp.sum(before_t <= base[None, :], axis=0) - 1, 1).astype(jnp.int32)
    whi = jnp.where(used, jnp.sum(before_t <= last[None, :], axis=0) - 1, 0).astype(jnp.int32)
    chg = jnp.concatenate([jnp.ones((1,), jnp.int32), (te[1:] != te[:-1]).astype(jnp.int32)])
    nused = n_used.reshape(1).astype(jnp.int32)
    meta = (te, base.astype(jnp.int32), wlo, whi)
    xs = _moe_gather(meta, rank_t, h2, n_tiles)
    act = _moe_up(te, chg, nused, xs, wg, wu)
    y = _moe_down(te, chg, nused, act, wd)
    start_row = (tile_end - tiles_e) * MOE_TILE
    n_we = jnp.concatenate([before[1:], counts[None, :]], axis=0) - before
    first = start_row[None, :] + before
    blk_a = jnp.minimum(first // MOE_TILE, n_tiles - 1)
    two = ((n_we > 0) & ((first + n_we - 1) // MOE_TILE > blk_a)).astype(jnp.int32)
    rel = blk_a * MOE_TILE - start_row[None, :]
    bands = tuple(a.reshape(-1).astype(jnp.int32) for a in (blk_a, blk_a + two, rel, two))
    return _moe_combine(bands, rank_t, gate_t, y, x1, mod_l, ln_g, ln_b, alpha, row_of_tile, t_ctx)


def _rope_tables(seq):
    half = GLA_DK // 2
    n = half // 2
    t = np.arange(seq)
    inv = ROPE_BASE ** (-np.arange(n, dtype=np.float64) / n)
    cos = np.zeros((seq, GLA_DK), np.float64)
    sin = np.zeros((seq, GLA_DK), np.float64)
    for a, pos in enumerate((t // GRID_W, t % GRID_W)):
        ang = pos[:, None].astype(np.float64) * inv[None, :]
        base = a * half
        cos[:, base:base + n] = np.cos(ang)
        cos[:, base + n:base + half] = np.cos(ang)
        sin[:, base:base + n] = -np.sin(ang)
        sin[:, base + n:base + half] = np.sin(ang)
    tile = lambda m: jnp.asarray(np.tile(m, (1, GLA_HEADS)), F32)
    return tile(cos), tile(sin)


W_IN_GD = 1536
W_IN_KV = (2080, 2592)
W_IN_CHUNKS = (0, 512, 1024, 1568, 3104, 3616, 4128, 4640, 5152, 5664, 6176, 6688)


def _transpose_w_in(w):
    return jnp.swapaxes(w, 1, 2).astype(BF16)


def _gate_up_blocks(w_gup):
    qkw = w_gup.shape[-1]
    fwd = jnp.pad(w_gup[:, 0], ((0, 0), (0, 0), (0, qkw)))
    bwd = jnp.pad(w_gup[:, 1], ((0, 0), (0, 0), (qkw, 0)))
    return jnp.pad(jnp.concatenate([fwd, bwd], axis=1), ((0, 0), (0, LANES - 2 * GLA_RANK), (0, 0)))


def kernel(x_prompt, x_sample, cache_na_k, cache_na_v, state_gla, c, c_ctx, w_mod, b_mod, w_in, gla_w_gup, gla_b_g, gla_norm, w_br_gla, na_rpb, w_br_na, cv_w, cv_b, cv_ln_g, cv_ln_b, w_br_cv, w_out, ln_g, ln_b, ffd_w_gate, ffd_w_up, ffd_w_down, moe_w_router, moe_w_gate, moe_w_up, moe_w_down):
    nb_c, l_c, d = x_prompt.shape
    nb_l, l_l, _ = x_sample.shape
    depth = w_mod.shape[0]
    t_ctx, t_lat = nb_c * l_c, nb_l * l_l
    t_all = t_ctx + t_lat
    alpha = (2 * depth) ** 0.25
    tm = 512
    row_of_tile = _row_of_tile_fn(t_ctx, l_l, tm)

    x_pair = (x_prompt.reshape(t_ctx, d), x_sample.reshape(t_lat, d))
    cond8 = jnp.concatenate([c_ctx[None, :], c, jnp.zeros((8 - 1 - nb_l, d), F32)], axis=0)
    mod = _modulation(cond8, w_mod, b_mod).reshape(depth, 8, 6, 1, d)
    rope = _rope_tables(l_l)
    rows_lat = l_l // GRID_W
    kr = min(NA_WIN_R, rows_lat)
    s0_all = state_gla.reshape(nb_l, depth, 2, 2, LANES, GLA_DV)
    w_t = _transpose_w_in(w_in)
    wg_blks = _gate_up_blocks(gla_w_gup)
    toep = _na_bias_tables(na_rpb)
    w_br = [w.astype(BF16) for w in (w_br_gla, w_br_na, w_br_cv, w_out)]

    caches = None
    states = None
    for l in range(depth):
        mod_l = mod[l]
        gd, kc, vc, kl, vl, zb = _input_projection(x_pair, mod_l, w_t, l, depth, l_c, tm, row_of_tile, caches)
        caches = (kc, vc)

        wg_blk = wg_blks[l]
        bg = gla_b_g[l].reshape(1, CHUNK_W)
        gn = gla_norm[l].reshape(1, CHUNK_W)
        ya, states = _gla_branch(gd, zb, wg_blk, bg, gn, latent=False, seq=l_c, nb=nb_c, tok0=0, t_total=t_all,
                                 state_slot=(l, depth, states))
        ya = _gla_branch(gd, zb, wg_blk, bg, gn, latent=True, seq=l_l, nb=nb_l, tok0=t_ctx, t_total=t_all,
                         rope=rope, s0=s0_all[:, l], ya_prev=ya)

        yb = _attn_context(zb, kc, vc, l, nb_c, l_c, t_all)
        yb = _attn_latent(zb, kl, vl, cache_na_k, cache_na_v, l, toep, yb, nb_l, l_l, t_ctx)

        conv_args = (zb, cv_w[l], cv_b[l][None, :], cv_ln_g[l][None, :], cv_ln_b[l][None, :])
        yc = _conv_branch(*conv_args, seq=l_c, nb=nb_c, tok0=0, t_total=t_all)
        yc = _conv_branch(*conv_args, seq=l_l, nb=nb_l, tok0=t_ctx, t_total=t_all, prev=yc)

        moe = l % 2 == 1
        w_router = None
        if moe:
            w_router = moe_w_router[l // 2].T
        merged = _merge(ya, yb, yc, zb, x_pair, mod_l, w_br[0][l], w_br[1][l], w_br[2][l], w_br[3][l],
                        ln_g[l, 0][None, :], ln_b[l, 0][None, :], alpha, tm, row_of_tile, w_router)
        ln2 = (ln_g[l, 1][None, :], ln_b[l, 1][None, :])
        if moe:
            x1, h2, gates = merged
            x_pair = _moe_layer(gates, h2, x1, mod_l, moe_w_gate[l // 2], moe_w_up[l // 2], moe_w_down[l // 2],
                                *ln2, alpha, _row_of_tile_fn(t_ctx, l_l, MOE_TILE), t_ctx)
        else:
            x1, h2 = merged
            i = l // 2
            x_pair = _dense_mixer(h2, x1, mod_l, ffd_w_gate[i].astype(BF16), ffd_w_up[i].astype(BF16),
                                  ffd_w_down[i].astype(BF16), *ln2, alpha, tm, 1, row_of_tile, t_ctx)

    y_prompt = x_pair[0].reshape(nb_c, l_c, d)
    y_sample = x_pair[1].reshape(nb_l, l_l, d)
    cache_shape = (nb_c, depth, l_c, NA_HEADS, NA_DH)
    return (y_prompt, y_sample, caches[0].reshape(cache_shape), caches[1].reshape(cache_shape),
            states.reshape(nb_c, depth, 2, GLA_HEADS, GLA_DK, GLA_DV))


def _row_of_tile_fn(t_ctx, l_lat, tm):
    n_ctx = t_ctx // tm

    def row_of_tile(i):
        return jnp.where(i < n_ctx, 0, 1 + (i - n_ctx) // (l_lat // tm))

    return row_of_tile
```

```python
import functools

import numpy as np
import jax
import jax.numpy as jnp
from jax import lax
from jax.experimental import pallas as pl
from jax.experimental.pallas import tpu as pltpu

F32 = jnp.float32
BF16 = jnp.bfloat16

GRID_W = 64
GLA_HEADS = 4
GLA_DK = 64
GLA_DV = 128
GLA_RANK = 16
GLA_TAU = 16.0
GLA_CHUNK = 64
GLA_GROUP = 256
GLA_CTX_SEQS = 2
GLA_LAT_SEQS = 2
NA_HEADS = 8
NA_DH = 64
NA_WIN_R = 8
NA_WIN_C = 16
CONV_CH = 512
CONV_K = 31
N_EXPERTS = 8
ROPE_BASE = 10000.0
LN_EPS = 1e-5

D_MODEL = 1024
LANES = 128
SUBLANES = 8
CHUNK_W = 512
N_BF16_CHUNKS = 12
NEG_BIG = -1e30
VMEM_LIMIT = 56 * 1024 * 1024


def _cparams(sem, vmem=VMEM_LIMIT):
    return pltpu.CompilerParams(dimension_semantics=sem, vmem_limit_bytes=vmem)


def _dot(a, b):
    return jnp.dot(a, b, preferred_element_type=F32)


def _split(x):
    hi = x.astype(BF16)
    return hi, (x - hi.astype(F32)).astype(BF16)


def _dot_split(a, b):
    a_hi, a_lo = _split(a)
    b_hi, b_lo = _split(b)
    return _dot(a_hi, b_hi) + _dot(a_lo, b_hi) + _dot(a_hi, b_lo)


def _dot_split_rhs(a, b):
    b_hi, b_lo = _split(b)
    return _dot(a, b_hi) + _dot(a, b_lo)


def _dot_nt(a, b):
    return lax.dot_general(a, b, (((1,), (1,)), ((), ())), preferred_element_type=F32)


def _sigmoid(x):
    return 0.5 * jnp.tanh(0.5 * x) + 0.5


def _silu(x):
    return x * _sigmoid(x)


def _layer_norm(x, g, b):
    mu = jnp.mean(x, axis=-1, keepdims=True)
    xc = x - mu
    var = jnp.mean(xc * xc, axis=-1, keepdims=True)
    return xc * lax.rsqrt(var + LN_EPS) * g + b


def _mod_kernel(c_ref, w_ref, b_ref, o_ref):
    c = c_ref[...]
    s = _silu(c).astype(BF16)
    o_ref[...] = _dot(s, w_ref[...].astype(BF16)) + b_ref[...]


def _modulation(cond8, w_mod, b_mod):
    depth, d, n = w_mod.shape
    tn = 1024
    return pl.pallas_call(
        _mod_kernel,
        grid=(depth, n // tn),
        in_specs=[
            pl.BlockSpec((8, d), lambda l, j: (0, 0)),
            pl.BlockSpec((None, d, tn), lambda l, j: (l, 0, j)),
            pl.BlockSpec((None, 1, tn), lambda l, j: (l, 0, j)),
        ],
        out_specs=pl.BlockSpec((None, 8, tn), lambda l, j: (l, 0, j)),
        out_shape=jax.ShapeDtypeStruct((depth, 8, n), F32),
        compiler_params=_cparams(("arbitrary", "arbitrary")),
        name="modulation",
    )(cond8, w_mod, b_mod.reshape(depth, 1, n))


def _inproj_kernel(n_ctx_tiles, aliased, *refs):
    if aliased:
        (xc_ref, xl_ref, sc_ref, sh_ref, w_ref, _kc_in, _vc_in,
         gd_ref, kc_ref, vc_ref, kl_ref, vl_ref, zb_ref, h_ref) = refs
    else:
        (xc_ref, xl_ref, sc_ref, sh_ref, w_ref,
         gd_ref, kc_ref, vc_ref, kl_ref, vl_ref, zb_ref, h_ref) = refs
    is_ctx = pl.program_id(0) < n_ctx_tiles
    x = jnp.where(is_ctx, xc_ref[...], xl_ref[...])
    h_ref[...] = (x * (1.0 + sc_ref[...]) + sh_ref[...]).astype(BF16)

    def project(start, width):
        return _dot_nt(h_ref[...], w_ref[start:start + width, :])

    gd_ref[...] = project(W_IN_GD, LANES)
    k = project(W_IN_KV[0], CHUNK_W)
    v = project(W_IN_KV[1], CHUNK_W)
    seqs, l_c, _ = kc_ref.shape

    @pl.when(is_ctx)
    def _():
        for s in range(seqs):
            kc_ref[s] = k[s * l_c:(s + 1) * l_c]
            vc_ref[s] = v[s * l_c:(s + 1) * l_c]

    @pl.when(jnp.logical_not(is_ctx))
    def _():
        kl_ref[...] = k
        vl_ref[...] = v

    for j, start in enumerate(W_IN_CHUNKS):
        zb_ref[j] = project(start, CHUNK_W).astype(BF16)


def _mod_spec(piece, row_of_tile):
    return pl.BlockSpec((None, None, 1, D_MODEL), lambda i: (row_of_tile(i), piece, 0, 0))


def _pair_specs(tm, d, n_ctx_tiles):
    return [pl.BlockSpec((tm, d), lambda i, *_: (jnp.minimum(i, n_ctx_tiles - 1), 0)),
            pl.BlockSpec((tm, d), lambda i, *_: (jnp.maximum(i - n_ctx_tiles, 0), 0))]


def _input_projection(x_pair, mod_l, w_t, layer, depth, l_c, tm, row_of_tile, caches=None):
    xc, xl = x_pair
    d = xc.shape[1]
    t_ctx, t_lat = xc.shape[0], xl.shape[0]
    t = t_ctx + t_lat
    n_ctx_tiles = t_ctx // tm
    seqs = tm // l_c
    cache_spec = pl.BlockSpec((seqs, None, l_c, CHUNK_W),
                              lambda i: (jnp.minimum(i, n_ctx_tiles - 1), layer, 0, 0))
    lat_spec = pl.BlockSpec((tm, CHUNK_W), lambda i: (jnp.maximum(i - n_ctx_tiles, 0), 0))
    in_specs = _pair_specs(tm, d, n_ctx_tiles) + [
        _mod_spec(1, row_of_tile), _mod_spec(0, row_of_tile),
        pl.BlockSpec((None,) + w_t.shape[1:], lambda i: (layer, 0, 0), pipeline_mode=pl.Buffered(1)),
    ]
    args = [xc, xl, mod_l, mod_l, w_t]
    aliases = {}
    if caches is not None:
        in_specs += [pl.BlockSpec(memory_space=pl.ANY), pl.BlockSpec(memory_space=pl.ANY)]
        args += list(caches)
        aliases = {5: 1, 6: 2}
    cache_shape = jax.ShapeDtypeStruct((t_ctx // l_c, depth, l_c, CHUNK_W), F32)
    return pl.pallas_call(
        functools.partial(_inproj_kernel, n_ctx_tiles, caches is not None),
        grid=(t // tm,),
        in_specs=in_specs,
        out_specs=[
            pl.BlockSpec((tm, LANES), lambda i: (i, 0)),
            cache_spec, cache_spec, lat_spec, lat_spec,
            pl.BlockSpec((N_BF16_CHUNKS, tm, CHUNK_W), lambda i: (0, i, 0)),
        ],
        out_shape=[
            jax.ShapeDtypeStruct((t, LANES), F32),
            cache_shape, cache_shape,
            jax.ShapeDtypeStruct((t_lat, CHUNK_W), F32), jax.ShapeDtypeStruct((t_lat, CHUNK_W), F32),
            jax.ShapeDtypeStruct((N_BF16_CHUNKS, t, CHUNK_W), BF16),
        ],
        scratch_shapes=[pltpu.VMEM((tm, d), BF16)],
        input_output_aliases=aliases,
        compiler_params=_cparams(("arbitrary",)),
        name="input_projection",
    )(*args)


def _gla_kernel(latent, seq, n_seq, state_alias, *refs):
    if latent:
        (qk_ref, v_ref, ra_ref, gd_ref, wg_ref, bg_ref, gn_ref, cos_ref, sin_ref, s0_ref, _alias,
         ya_ref, q_s, k_s, g_s, acc_s, st_s) = refs
    elif state_alias:
        (qk_ref, v_ref, ra_ref, gd_ref, wg_ref, bg_ref, gn_ref, _alias,
         ya_ref, sfin_ref, q_s, k_s, g_s, acc_s, st_s) = refs
    else:
        (qk_ref, v_ref, ra_ref, gd_ref, wg_ref, bg_ref, gn_ref,
         ya_ref, sfin_ref, q_s, k_s, g_s, acc_s, st_s) = refs
    c = GLA_CHUNK
    n_chunks = n_seq * seq // c
    qkw = GLA_HEADS * GLA_DK

    q = qk_ref[:, 0:qkw].astype(F32) * (GLA_DK ** -0.5)
    k = qk_ref[:, qkw:2 * qkw].astype(F32)
    if latent:
        lane = lax.broadcasted_iota(jnp.int32, (1, qkw), 1)
        first_half = (lane % 32) < 16

        def rope(x):
            swapped = jnp.where(first_half, pltpu.roll(x, qkw - 16, 1), pltpu.roll(x, 16, 1))
            cos = jnp.concatenate([cos_ref[...]] * n_seq, axis=0)
            sin = jnp.concatenate([sin_ref[...]] * n_seq, axis=0)
            return x * cos + swapped * sin

        q, k = rope(q), rope(k)
    q_s[...] = q
    k_s[...] = k

    glog = _dot_split(gd_ref[...], wg_ref[...]) + bg_ref[...]
    g_s[...] = (jnp.minimum(glog, 0.0) - jnp.log(1.0 + jnp.exp(-jnp.abs(glog)))) * (1.0 / GLA_TAU)
    acc_s[...] = jnp.zeros_like(acc_s)
    if latent:
        st_s[...] = s0_ref[...]
    else:
        st_s[...] = jnp.zeros_like(st_s)

    grp = GLA_GROUP
    cpg = grp // c
    n_groups = seq // grp
    grow = lax.broadcasted_iota(jnp.int32, (grp, grp), 0)
    gcol = lax.broadcasted_iota(jnp.int32, (grp, grp), 1)
    same_chunk = (grow // c) == (gcol // c)
    keep = (same_chunk & (gcol <= grow), same_chunk & (gcol >= grow))
    tri = (keep[0].astype(BF16), keep[1].astype(BF16))
    lane128 = lax.broadcasted_iota(jnp.int32, (1, LANES), 1)
    head_mask = (lane128 < GLA_DK, lane128 >= GLA_DK)
    urow = lax.broadcasted_iota(jnp.int32, (cpg * LANES, GLA_DV), 0)
    srow = lax.broadcasted_iota(jnp.int32, (LANES, GLA_DV), 0)
    chunk_of_col = lax.broadcasted_iota(jnp.int32, (LANES, grp), 1) // c
    chunk_of_row = lax.broadcasted_iota(jnp.int32, (grp, LANES), 0) // c
    pad_rows = jnp.zeros((SUBLANES - cpg, qkw), F32)

    def group_step(i, carry):
        for sq, d in ((sq, d) for sq in range(n_seq) for d in range(2)):
            gi = i if d == 0 else n_groups - 1 - i
            rows = pl.ds(pl.multiple_of(sq * seq + gi * grp, grp), grp)
            cum = _dot_split_rhs(tri[d], g_s[rows, d * qkw:(d + 1) * qkw])
            edge = c - 1 if d == 0 else 0
            lasts = [cum[ci * c + edge:ci * c + edge + 1, :] for ci in range(cpg)]
            last = jnp.concatenate([jnp.broadcast_to(r, (c, qkw)) for r in lasts], axis=0)
            ref = 0.5 * last
            q = q_s[rows, :]
            k = k_s[rows, :]
            qe = q * jnp.exp(cum - ref)
            ke = k * jnp.exp(ref - cum)
            qe2 = q * jnp.exp(cum)
            ke2 = k * jnp.exp(last - cum)
            last8 = jnp.concatenate(lasts + [pad_rows], axis=0)
            for p in range(2):
                sl = slice(p * LANES, (p + 1) * LANES)
                ke_p = ke[:, sl].astype(BF16)
                ke2_t = ke2[:, sl].T
                ke2_blk = jnp.concatenate(
                    [jnp.where(chunk_of_col == ci, ke2_t, 0.0) for ci in range(cpg)], axis=0).astype(BF16)
                dec = jnp.exp(last8[:, sl]).T
                psl = slice(2 * p * GLA_DV, (2 * p + 2) * GLA_DV)
                v_pair = v_ref[rows, psl]
                u = _dot(ke2_blk, v_pair)
                upd = jnp.where(urow % LANES < GLA_DK, u[:, :GLA_DV], u[:, GLA_DV:])
                s = st_s[sq, d, p]
                s_in = [None] * cpg
                for ci in (range(cpg) if d == 0 else range(cpg - 1, -1, -1)):
                    s_in[ci] = jnp.concatenate(
                        [jnp.where(srow < GLA_DK, s, 0.0), jnp.where(srow >= GLA_DK, s, 0.0)], axis=1).astype(BF16)
                    s = dec[:, ci:ci + 1] * s + upd[ci * LANES:(ci + 1) * LANES]
                st_s[sq, d, p] = s
                s_stack = jnp.concatenate(s_in, axis=0)
                a2_blk = jnp.concatenate(
                    [jnp.where(chunk_of_row == ci, qe2[:, sl], 0.0) for ci in range(cpg)], axis=1).astype(BF16)
                intra = []
                for hh in range(2):
                    a = jnp.where(head_mask[hh], qe[:, sl], 0.0).astype(BF16)
                    att = jnp.where(keep[d], _dot_nt(a, ke_p), 0.0).astype(BF16)
                    intra.append(_dot(att, v_pair[:, hh * GLA_DV:(hh + 1) * GLA_DV]))
                acc_s[rows, psl] += jnp.concatenate(intra, axis=1) + _dot(a2_blk, s_stack)
        return carry

    lax.fori_loop(0, n_groups, group_step, 0)

    def epilogue(i, carry):
        rows = pl.ds(pl.multiple_of(i * c, c), c)
        for h in range(GLA_HEADS):
            vsl = slice(h * GLA_DV, (h + 1) * GLA_DV)
            o = acc_s[rows, vsl]
            o = o * lax.rsqrt(jnp.mean(o * o, axis=-1, keepdims=True) + LN_EPS) * gn_ref[:, vsl]
            ya_ref[rows, vsl] = (o * _silu(ra_ref[rows, vsl].astype(F32))).astype(BF16)
        return carry

    lax.fori_loop(0, n_chunks, epilogue, 0)
    if not latent:
        sfin_ref[...] = st_s[...]


def _gla_branch(gd, zb, wg_blk, bg, gn, *, latent, seq, nb, tok0, t_total, rope=None, s0=None, ya_prev=None,
                state_slot=None):
    n_seq = GLA_LAT_SEQS if latent else GLA_CTX_SEQS
    rows = n_seq * seq
    b0 = tok0 // rows
    qkw = GLA_HEADS * GLA_DK
    in_specs = [
        pl.BlockSpec((None, rows, CHUNK_W), lambda b: (0, b0 + b, 0)),
        pl.BlockSpec((None, rows, CHUNK_W), lambda b: (1, b0 + b, 0)),
        pl.BlockSpec((None, rows, CHUNK_W), lambda b: (2, b0 + b, 0)),
        pl.BlockSpec((rows, LANES), lambda b: (b0 + b, 0)),
        pl.BlockSpec((LANES, CHUNK_W), lambda b: (0, 0)),
        pl.BlockSpec((1, CHUNK_W), lambda b: (0, 0)),
        pl.BlockSpec((1, CHUNK_W), lambda b: (0, 0)),
    ]
    args = [zb, zb, zb, gd, wg_blk, bg, gn]
    scratch = [
        pltpu.VMEM((rows, qkw), F32), pltpu.VMEM((rows, qkw), F32), pltpu.VMEM((rows, CHUNK_W), F32),
        pltpu.VMEM((rows, CHUNK_W), F32), pltpu.VMEM((n_seq, 2, 2, LANES, GLA_DV), F32),
    ]
    ya_shape = jax.ShapeDtypeStruct((t_total, CHUNK_W), BF16)
    ya_spec = pl.BlockSpec((rows, CHUNK_W), lambda b: (b0 + b, 0))
    if latent:
        cos_t, sin_t = rope
        in_specs += [
            pl.BlockSpec((seq, qkw), lambda b: (0, 0)),
            pl.BlockSpec((seq, qkw), lambda b: (0, 0)),
            pl.BlockSpec((n_seq, 2, 2, LANES, GLA_DV), lambda b: (b, 0, 0, 0, 0)),
            pl.BlockSpec(memory_space=pl.ANY),
        ]
        args += [cos_t, sin_t, s0, ya_prev]
        return pl.pallas_call(
            functools.partial(_gla_kernel, True, seq, n_seq, False),
            grid=(nb // n_seq,), in_specs=in_specs, out_specs=ya_spec, out_shape=ya_shape,
            scratch_shapes=scratch, input_output_aliases={len(args) - 1: 0},
            compiler_params=_cparams(("arbitrary",)), name="gla_latent",
        )(*args)
    layer, depth, states = state_slot
    aliases = {}
    if states is not None:
        in_specs.append(pl.BlockSpec(memory_space=pl.ANY))
        args.append(states)
        aliases = {len(args) - 1: 1}
    return pl.pallas_call(
        functools.partial(_gla_kernel, False, seq, n_seq, states is not None),
        grid=(nb // n_seq,), in_specs=in_specs,
        out_specs=[ya_spec, pl.BlockSpec((n_seq, None, 2, 2, LANES, GLA_DV), lambda b: (b, layer, 0, 0, 0, 0))],
        out_shape=[ya_shape, jax.ShapeDtypeStruct((nb, depth, 2, 2, LANES, GLA_DV), F32)],
        scratch_shapes=scratch, input_output_aliases=aliases,
        compiler_params=_cparams(("arbitrary",)), name="gla_context",
    )(*args)


ATTN_CTX_SEQS = 2


def _attn_ctx_kernel(q_ref, k_ref, v_ref, o_ref):
    lane = lax.broadcasted_iota(jnp.int32, (1, LANES), 1)
    masks = (lane < NA_DH, lane >= NA_DH)
    scale = NA_DH ** -0.5
    n_seq, seq, _ = k_ref.shape
    for b in range(n_seq):
        rows = slice(b * seq, (b + 1) * seq)
        for p in range(NA_HEADS // 2):
            sl = slice(p * LANES, (p + 1) * LANES)
            q2 = _stack_heads(q_ref[rows, sl], masks)
            kp = k_ref[b, :, sl].astype(BF16)
            vp = v_ref[b, :, sl].astype(BF16)
            s = _dot_nt(q2, kp) * scale
            e = jnp.exp(s - jnp.max(s, axis=-1, keepdims=True))
            o2 = _dot(e.astype(BF16), vp) / jnp.sum(e, axis=-1, keepdims=True)
            o_ref[rows, sl] = jnp.where(masks[0], o2[0:seq], o2[seq:2 * seq]).astype(BF16)


def _stack_heads(qp, masks):
    zero = jnp.zeros_like(qp)
    return jnp.concatenate([jnp.where(masks[0], qp, zero), jnp.where(masks[1], qp, zero)], axis=0)


def _attn_context(zb, kc, vc, layer, nb, seq, t_total):
    per = ATTN_CTX_SEQS
    return pl.pallas_call(
        _attn_ctx_kernel,
        grid=(nb // per,),
        in_specs=[
            pl.BlockSpec((None, per * seq, CHUNK_W), lambda b: (3, b, 0)),
            pl.BlockSpec((per, None, seq, CHUNK_W), lambda b: (b, layer, 0, 0)),
            pl.BlockSpec((per, None, seq, CHUNK_W), lambda b: (b, layer, 0, 0)),
        ],
        out_specs=pl.BlockSpec((per * seq, CHUNK_W), lambda b: (b, 0)),
        out_shape=jax.ShapeDtypeStruct((t_total, CHUNK_W), BF16),
        compiler_params=_cparams(("arbitrary",)), name="attention_context",
    )(zb, kc, vc)


NA_ROWS_PER_STEP = 2


def _na_kernel(rows_total, kr, q_ref, kl_ref, vl_ref, kc_ref, vc_ref, toep_ref, _alias, o_ref):
    lane = lax.broadcasted_iota(jnp.int32, (1, LANES), 1)
    masks = (lane < NA_DH, lane >= NA_DH)
    scale = NA_DH ** -0.5
    nq = GRID_W
    for rr in range(NA_ROWS_PER_STEP):
        r = pl.program_id(1) * NA_ROWS_PER_STEP + rr
        qrows = slice(rr * nq, (rr + 1) * nq)
        rs = jnp.clip(r - kr // 2, 0, rows_total - kr)
        krows = pl.ds(pl.multiple_of(rs * GRID_W, GRID_W), kr * GRID_W)
        dr0 = rs - r + NA_WIN_R - 1
        for p in range(NA_HEADS // 2):
            sl = slice(p * LANES, (p + 1) * LANES)
            q2 = _stack_heads(q_ref[qrows, sl], masks)
            klp = kl_ref[krows, sl].astype(BF16)
            vlp = vl_ref[krows, sl].astype(BF16)
            kcp = kc_ref[:, sl].astype(BF16)
            vcp = vc_ref[:, sl].astype(BF16)
            bias = jnp.concatenate(
                [jnp.concatenate([toep_ref[2 * p + hh, dr0 + 2 * m] for m in range(kr // 2)], axis=1)
                 for hh in range(2)], axis=0)
            s_loc = _dot_nt(q2, klp) * scale + bias
            s_ctx = _dot_nt(q2, kcp) * scale
            m = jnp.maximum(jnp.max(s_loc, axis=-1, keepdims=True), jnp.max(s_ctx, axis=-1, keepdims=True))
            e_loc = jnp.exp(s_loc - m)
            e_ctx = jnp.exp(s_ctx - m)
            den = jnp.sum(e_loc, axis=-1, keepdims=True) + jnp.sum(e_ctx, axis=-1, keepdims=True)
            o2 = (_dot(e_loc.astype(BF16), vlp) + _dot(e_ctx.astype(BF16), vcp)) / den
            o_ref[qrows, sl] = jnp.where(masks[0], o2[0:nq], o2[nq:2 * nq]).astype(BF16)


def _na_bias_tables(rpb):
    nc = rpb.shape[-1]
    qc = np.arange(GRID_W)[:, None]
    kc = np.arange(GRID_W)[None, :]
    cs = np.clip(qc - NA_WIN_C // 2, 0, GRID_W - NA_WIN_C)
    valid = (kc >= cs) & (kc < cs + NA_WIN_C)
    tap = np.clip(kc - qc, -(NA_WIN_C - 1), NA_WIN_C - 1) + NA_WIN_C - 1
    pick = jnp.asarray(tap[..., None] == np.arange(nc), F32)
    toep = jnp.einsum('dhab,qkb->dhaqk', rpb.astype(F32), pick, precision=lax.Precision.HIGHEST)
    toep = jnp.where(valid, toep, NEG_BIG)
    return jnp.concatenate([toep[:, :, :-1], toep[:, :, 1:]], axis=-1)


def _attn_latent(zb, kl, vl, cache_k, cache_v, layer, toep, att_prev, nb, seq, tok0):
    rows_total = seq // GRID_W
    kr = min(NA_WIN_R, rows_total)
    nq = NA_ROWS_PER_STEP * GRID_W
    steps = rows_total // NA_ROWS_PER_STEP
    q0 = tok0 // nq
    past = cache_k.shape[2]
    ck = cache_k.reshape(cache_k.shape[0], cache_k.shape[1], past, NA_HEADS * NA_DH)
    cv = cache_v.reshape(ck.shape)
    return pl.pallas_call(
        functools.partial(_na_kernel, rows_total, kr),
        grid=(nb, steps),
        in_specs=[
            pl.BlockSpec((None, nq, CHUNK_W), lambda b, r: (3, q0 + b * steps + r, 0)),
            pl.BlockSpec((seq, CHUNK_W), lambda b, r: (b, 0)),
            pl.BlockSpec((seq, CHUNK_W), lambda b, r: (b, 0)),
            pl.BlockSpec((None, None, past, CHUNK_W), lambda b, r: (b, layer, 0, 0)),
            pl.BlockSpec((None, None, past, CHUNK_W), lambda b, r: (b, layer, 0, 0)),
            pl.BlockSpec((None,) + toep.shape[1:], lambda b, r: (layer, 0, 0, 0, 0), pipeline_mode=pl.Buffered(1)),
            pl.BlockSpec(memory_space=pl.ANY),
        ],
        out_specs=pl.BlockSpec((nq, CHUNK_W), lambda b, r: (q0 + b * steps + r, 0)),
        out_shape=jax.ShapeDtypeStruct(att_prev.shape, BF16),
        input_output_aliases={6: 0},
        compiler_params=_cparams(("arbitrary", "arbitrary")), name="attention_latent",
    )(zb, kl, vl, ck, cv, toep, att_prev)


CONV_PAD = 16
CONV_ROWS = 64


def _conv_kernel(seq, aliased, *refs):
    if aliased:
        a_ref, gt_ref, w_ref, cb_ref, lg_ref, lb_ref, _alias, o_ref, z_s, sh_s = refs
    else:
        a_ref, gt_ref, w_ref, cb_ref, lg_ref, lb_ref, o_ref, z_s, sh_s = refs
    z_s[0:CONV_PAD, :] = jnp.zeros((CONV_PAD, CONV_CH), F32)
    z_s[CONV_PAD + seq:2 * CONV_PAD + seq, :] = jnp.zeros((CONV_PAD, CONV_CH), F32)
    z_s[CONV_PAD:CONV_PAD + seq, :] = a_ref[...].astype(F32) * _sigmoid(gt_ref[...].astype(F32))
    ext = seq + 2 * CONV_PAD - SUBLANES
    for b in range(1, SUBLANES):
        for r0 in range(0, ext, CONV_ROWS):
            n = min(CONV_ROWS, ext - r0)
            sh_s[b - 1, r0:r0 + n, :] = z_s[r0 + b:r0 + b + n, :]
    off = CONV_PAD - CONV_K // 2
    grp = (CONV_ROWS // SUBLANES, SUBLANES, CONV_CH)
    for t0 in range(0, seq, CONV_ROWS):
        acc = jnp.zeros(grp, F32)
        for k in range(CONV_K):
            a, b = divmod(off + k, SUBLANES)
            r0 = t0 + a * SUBLANES
            tap = z_s[r0:r0 + CONV_ROWS, :] if b == 0 else sh_s[b - 1, r0:r0 + CONV_ROWS, :]
            acc = acc + tap.reshape(grp) * w_ref[k][None]
        y = _layer_norm(acc.reshape(CONV_ROWS, CONV_CH) + cb_ref[...], lg_ref[...], lb_ref[...])
        o_ref[t0:t0 + CONV_ROWS, :] = _silu(y).astype(BF16)


def _conv_branch(zb, cv_w, cv_b, ln_g, ln_b, *, seq, nb, tok0, t_total, prev=None):
    b0 = tok0 // seq
    in_specs = [
        pl.BlockSpec((None, seq, CHUNK_W), lambda b: (4, b0 + b, 0)),
        pl.BlockSpec((None, seq, CHUNK_W), lambda b: (5, b0 + b, 0)),
        pl.BlockSpec((CONV_K, SUBLANES, CONV_CH), lambda b: (0, 0, 0)),
        pl.BlockSpec((1, CONV_CH), lambda b: (0, 0)),
        pl.BlockSpec((1, CONV_CH), lambda b: (0, 0)),
        pl.BlockSpec((1, CONV_CH), lambda b: (0, 0)),
    ]
    args = [zb, zb, jnp.broadcast_to(cv_w[:, None, :], (CONV_K, SUBLANES, CONV_CH)), cv_b, ln_g, ln_b]
    aliases = {}
    if prev is not None:
        in_specs.append(pl.BlockSpec(memory_space=pl.ANY))
        args.append(prev)
        aliases = {6: 0}
    return pl.pallas_call(
        functools.partial(_conv_kernel, seq, prev is not None),
        grid=(nb,), in_specs=in_specs,
        out_specs=pl.BlockSpec((seq, CONV_CH), lambda b: (b0 + b, 0)),
        out_shape=jax.ShapeDtypeStruct((t_total, CONV_CH), BF16),
        scratch_shapes=[pltpu.VMEM((seq + 2 * CONV_PAD, CONV_CH), F32),
                        pltpu.VMEM((SUBLANES - 1, seq + 2 * CONV_PAD - SUBLANES, CONV_CH), F32)],
        input_output_aliases=aliases,
        compiler_params=_cparams(("arbitrary",)), name="conv_module_%d" % seq,
    )(*args)


def _merge_kernel(alpha, n_ctx_tiles, with_router, *refs):
    if with_router:
        (ya_ref, yb_ref, yc_ref, mg_ref, xc_ref, xl_ref, g1_ref, sc2_ref, sh2_ref, wa_ref, wb_ref, wc_ref, wo_ref,
         lg_ref, lb_ref, wr_ref, x1_ref, h2_ref, gates_ref) = refs
    else:
        (ya_ref, yb_ref, yc_ref, mg_ref, xc_ref, xl_ref, g1_ref, sc2_ref, sh2_ref, wa_ref, wb_ref, wc_ref, wo_ref,
         lg_ref, lb_ref, x1_ref, h2_ref) = refs
    x = jnp.where(pl.program_id(0) < n_ctx_tiles, xc_ref[...], xl_ref[...])
    halves = []
    for n in range(2):
        m = None
        for j, (y_ref, w_ref) in enumerate(((ya_ref, wa_ref), (yb_ref, wb_ref), (yc_ref, wc_ref))):
            proj = _dot(y_ref[...], w_ref[:, n * CHUNK_W:(n + 1) * CHUNK_W])
            term = _sigmoid(mg_ref[2 * j + n].astype(F32)) * proj
            m = term if m is None else m + term
        halves.append(m.astype(BF16))
    mix = _dot(halves[0], wo_ref[0:CHUNK_W, :]) + _dot(halves[1], wo_ref[CHUNK_W:2 * CHUNK_W, :])
    x1 = _layer_norm(alpha * x + g1_ref[...] * mix, lg_ref[...], lb_ref[...])
    x1_ref[...] = x1
    h2 = x1 * (1.0 + sc2_ref[...]) + sh2_ref[...]
    h2_ref[...] = h2.astype(BF16)
    if with_router:
        w_hi, w_lo = _split(wr_ref[...])
        h_hi, h_lo = _split(h2)
        lg = _dot_nt(w_hi, h_hi) + _dot_nt(w_hi, h_lo) + _dot_nt(w_lo, h_hi)
        eid = lax.broadcasted_iota(jnp.int32, lg.shape, 0)
        m1 = jnp.max(lg, axis=0, keepdims=True)
        i1 = jnp.min(jnp.where(lg == m1, eid, N_EXPERTS), axis=0, keepdims=True)
        lg2 = jnp.where(eid == i1, -jnp.inf, lg)
        m2 = jnp.max(lg2, axis=0, keepdims=True)
        i2 = jnp.min(jnp.where(lg2 == m2, eid, N_EXPERTS), axis=0, keepdims=True)
        e2 = jnp.exp(m2 - m1)
        w1 = 1.0 / (1.0 + e2)
        gates_ref[...] = jnp.where(eid == i1, w1, 0.0) + jnp.where(eid == i2, e2 * w1, 0.0)


def _merge(ya, yb, yc, zb, x_pair, mod_l, wa, wb, wc, wo, ln_g, ln_b, alpha, tm, row_of_tile, w_router=None):
    xc, xl = x_pair
    d = xc.shape[1]
    t = xc.shape[0] + xl.shape[0]
    n_ctx_tiles = xc.shape[0] // tm
    full = lambda shape: pl.BlockSpec(shape, lambda i: tuple(0 for _ in shape), pipeline_mode=pl.Buffered(1))
    in_specs = [
        pl.BlockSpec((tm, CHUNK_W), lambda i: (i, 0)),
        pl.BlockSpec((tm, CHUNK_W), lambda i: (i, 0)),
        pl.BlockSpec((tm, CHUNK_W), lambda i: (i, 0)),
        pl.BlockSpec((6, tm, CHUNK_W), lambda i: (1, i, 0)),
    ] + _pair_specs(tm, d, n_ctx_tiles) + [
        _mod_spec(2, row_of_tile), _mod_spec(4, row_of_tile), _mod_spec(3, row_of_tile),
        full(wa.shape), full(wb.shape), full(wc.shape), full(wo.shape), full((1, d)), full((1, d)),
    ]
    args = [ya, yb, yc, zb, xc, xl, mod_l, mod_l, mod_l, wa, wb, wc, wo, ln_g, ln_b]
    out_specs = [pl.BlockSpec((tm, d), lambda i: (i, 0)), pl.BlockSpec((tm, d), lambda i: (i, 0))]
    out_shape = [jax.ShapeDtypeStruct((t, d), F32), jax.ShapeDtypeStruct((t, d), BF16)]
    if w_router is not None:
        in_specs.append(full(w_router.shape))
        args.append(w_router)
        out_specs.append(pl.BlockSpec((N_EXPERTS, tm), lambda i: (0, i)))
        out_shape.append(jax.ShapeDtypeStruct((N_EXPERTS, t), F32))
    return pl.pallas_call(
        functools.partial(_merge_kernel, alpha, n_ctx_tiles, w_router is not None),
        grid=(t // tm,), in_specs=in_specs, out_specs=out_specs, out_shape=out_shape,
        compiler_params=_cparams(("arbitrary",)), name="merge",
    )(*args)


def _ffn_kernel(alpha, n_ctx_tiles, n_split, h_ref, x1_ref, g2_ref, wg_ref, wu_ref, wd_ref, lg_ref, lb_ref,
                oc_ref, ol_ref):
    h = h_ref[...]
    tf = wg_ref.shape[1] // n_split
    y = None
    for f in range(n_split):
        cols = slice(f * tf, (f + 1) * tf)
        act = _silu(_dot(h, wg_ref[:, cols])) * _dot(h, wu_ref[:, cols])
        part = _dot(act.astype(BF16), wd_ref[cols, :])
        y = part if y is None else y + part
    out = _layer_norm(alpha * x1_ref[...] + g2_ref[...] * y, lg_ref[...], lb_ref[...])
    is_ctx = pl.program_id(0) < n_ctx_tiles

    @pl.when(is_ctx)
    def _():
        oc_ref[...] = out

    @pl.when(jnp.logical_not(is_ctx))
    def _():
        ol_ref[...] = out


def _dense_mixer(h2, x1, mod_l, wg, wu, wd, ln_g, ln_b, alpha, tm, n_split, row_of_tile, t_ctx):
    t, d = x1.shape
    n_ctx_tiles = t_ctx // tm
    const = lambda shape: pl.BlockSpec(shape, lambda i: tuple(0 for _ in shape), pipeline_mode=pl.Buffered(1))
    return pl.pallas_call(
        functools.partial(_ffn_kernel, alpha, n_ctx_tiles, n_split),
        grid=(t // tm,),
        in_specs=[
            pl.BlockSpec((tm, d), lambda i: (i, 0)),
            pl.BlockSpec((tm, d), lambda i: (i, 0)),
            _mod_spec(5, row_of_tile),
            const(wg.shape), const(wu.shape), const(wd.shape), const((1, d)), const((1, d)),
        ],
        out_specs=_pair_specs(tm, d, n_ctx_tiles),
        out_shape=[jax.ShapeDtypeStruct((t_ctx, d), F32), jax.ShapeDtypeStruct((t - t_ctx, d), F32)],
        compiler_params=_cparams(("arbitrary",)), name="dense_mixer",
    )(h2, x1, mod_l, wg, wu, wd, ln_g, ln_b)


MOE_TILE = 256
MOE_TF = 1792
GATHER_TILES = 2


def _route_kernel(g_ref, rank_ref, gate_ref, before_ref, run_s):
    w = pl.program_id(0)

    @pl.when(w == 0)
    def _():
        run_s[...] = jnp.zeros_like(run_s)

    g = g_ref[...]
    sel = g > 0.0
    row = lax.broadcasted_iota(jnp.int32, (MOE_TILE, MOE_TILE), 0)
    col = lax.broadcasted_iota(jnp.int32, (MOE_TILE, MOE_TILE), 1)
    earlier = (row < col).astype(BF16)
    ones = jnp.where(sel, 1.0, 0.0)
    rank = _dot(ones.astype(BF16), earlier) + run_s[:, 0:1]
    rank_ref[...] = jnp.where(sel, rank, -1.0)
    gate_ref[...] = g
    before_ref[...] = run_s[...]
    run_s[...] += jnp.sum(ones, axis=1, keepdims=True)


def _route(gates_t):
    t = gates_t.shape[1]
    nw = t // MOE_TILE
    return pl.pallas_call(
        _route_kernel,
        grid=(nw,),
        in_specs=[pl.BlockSpec((N_EXPERTS, MOE_TILE), lambda w: (0, w))],
        out_specs=[
            pl.BlockSpec((None, N_EXPERTS, MOE_TILE), lambda w: (w, 0, 0)),
            pl.BlockSpec((None, N_EXPERTS, MOE_TILE), lambda w: (w, 0, 0)),
            pl.BlockSpec((None, N_EXPERTS, LANES), lambda w: (w, 0, 0)),
        ],
        out_shape=[
            jax.ShapeDtypeStruct((nw, N_EXPERTS, MOE_TILE), F32),
            jax.ShapeDtypeStruct((nw, N_EXPERTS, MOE_TILE), F32),
            jax.ShapeDtypeStruct((nw, N_EXPERTS, LANES), F32),
        ],
        scratch_shapes=[pltpu.VMEM((N_EXPERTS, LANES), F32)],
        compiler_params=_cparams(("arbitrary",)), name="moe_route",
    )(gates_t)


def _tile_onehot(rank_ref, w, e, base):
    rank_row = rank_ref[w, pl.ds(e, 1), :]
    rows = lax.broadcasted_iota(jnp.int32, (MOE_TILE, MOE_TILE), 0).astype(F32) + base.astype(F32)
    return rank_row == rows


F8 = jnp.float8_e4m3fn
F8_TARGET = 224.0
F8_TINY = 1e-30


def _f8_quantize(x):
    m = jnp.max(jnp.max(jnp.abs(x), axis=1, keepdims=True), axis=0, keepdims=True)
    m = jnp.maximum(m, F8_TINY)
    return (x * (F8_TARGET / m)).astype(F8), m * (1.0 / F8_TARGET)


def _gather_kernel(te_ref, base_ref, wlo_ref, whi_ref, rank_ref, x_ref, o_ref, inv_ref, acc_s):
    acc_s[...] = jnp.zeros_like(acc_s)
    for u in range(GATHER_TILES):
        j = pl.program_id(0) * GATHER_TILES + u
        e, base = te_ref[j], base_ref[j]
        rows = slice(u * MOE_TILE, (u + 1) * MOE_TILE)

        def window(w, carry):
            p = jnp.where(_tile_onehot(rank_ref, w, e, base), 1.0, 0.0).astype(BF16)
            x_w = x_ref[pl.ds(pl.multiple_of(w * MOE_TILE, MOE_TILE), MOE_TILE), :]
            acc_s[rows, :] += _dot(p, x_w).astype(BF16)
            return carry

        lax.fori_loop(wlo_ref[j], whi_ref[j] + 1, window, 0)
        o_ref[rows, :], inv = _f8_quantize(acc_s[rows, :].astype(F32))
        inv_ref[u] = jnp.broadcast_to(inv, (1, LANES))


def _moe_gather(meta, rank_t, h2, n_tiles):
    t, d = h2.shape
    nw = t // MOE_TILE
    return pl.pallas_call(
        _gather_kernel,
        grid_spec=pltpu.PrefetchScalarGridSpec(
            num_scalar_prefetch=4, grid=(n_tiles // GATHER_TILES,),
            in_specs=[
                pl.BlockSpec((nw, N_EXPERTS, MOE_TILE), lambda j, *_: (0, 0, 0)),
                pl.BlockSpec((t, d), lambda j, *_: (0, 0), pipeline_mode=pl.Buffered(1)),
            ],
            out_specs=[
                pl.BlockSpec((GATHER_TILES * MOE_TILE, d), lambda j, *_: (j, 0)),
                pl.BlockSpec((GATHER_TILES, 1, LANES), lambda j, *_: (j, 0, 0)),
            ],
            scratch_shapes=[pltpu.VMEM((GATHER_TILES * MOE_TILE, d), BF16)],
        ),
        out_shape=[jax.ShapeDtypeStruct((n_tiles * MOE_TILE, d), F8),
                   jax.ShapeDtypeStruct((n_tiles, 1, LANES), F32)],
        compiler_params=_cparams(("arbitrary",)), name="moe_gather",
    )(*meta, rank_t, h2)


def _moe_up_kernel(te_ref, chg_ref, nused_ref, x_ref, invx_ref, wg_ref, wu_ref, o_ref, wg_s, wu_s, inv_s):
    j = pl.program_id(1)

    @pl.when(chg_ref[j] == 1)
    def _():
        for row, (w_ref, w_s) in enumerate(((wg_ref, wg_s), (wu_ref, wu_s))):
            w_s[...], inv = _f8_quantize(w_ref[...])
            inv_s[row:row + 1, :] = jnp.broadcast_to(inv, (1, LANES))

    @pl.when(j < nused_ref[0])
    def _():
        x8, inv_x = x_ref[...], invx_ref[0:1, 0:1]
        a = _dot(x8, wg_s[...]) * (inv_x * inv_s[0:1, 0:1])
        u = _dot(x8, wu_s[...]) * (inv_x * inv_s[1:2, 0:1])
        o_ref[...] = (_silu(a) * u).astype(BF16)

    @pl.when(j >= nused_ref[0])
    def _():
        o_ref[...] = jnp.zeros_like(o_ref)


def _moe_up(te, chg, nused, xs, inv_xs, wg, wu):
    rows, d = xs.shape
    n_tiles = rows // MOE_TILE
    ff = wg.shape[2]
    return pl.pallas_call(
        _moe_up_kernel,
        grid_spec=pltpu.PrefetchScalarGridSpec(
            num_scalar_prefetch=3, grid=(ff // MOE_TF, n_tiles),
            in_specs=[
                pl.BlockSpec((MOE_TILE, d), lambda f, j, *_: (j, 0)),
                pl.BlockSpec((None, 1, LANES), lambda f, j, *_: (j, 0, 0)),
                pl.BlockSpec((None, d, MOE_TF), lambda f, j, te, *_: (te[j], 0, f)),
                pl.BlockSpec((None, d, MOE_TF), lambda f, j, te, *_: (te[j], 0, f)),
            ],
            out_specs=pl.BlockSpec((MOE_TILE, MOE_TF), lambda f, j, *_: (j, f)),
            scratch_shapes=[pltpu.VMEM((d, MOE_TF), F8), pltpu.VMEM((d, MOE_TF), F8),
                            pltpu.VMEM((SUBLANES, LANES), F32)],
        ),
        out_shape=jax.ShapeDtypeStruct((rows, ff), BF16),
        compiler_params=_cparams(("arbitrary", "arbitrary")), name="moe_up",
    )(te, chg, nused, xs, inv_xs, wg, wu)


def _moe_down_kernel(te_ref, chg_ref, nused_ref, a_ref, wd_ref, o_ref, wd_s, inv_s):
    j = pl.program_id(0)

    @pl.when(chg_ref[j] == 1)
    def _():
        wd_s[...], inv = _f8_quantize(wd_ref[...])
        inv_s[0:1, :] = jnp.broadcast_to(inv, (1, LANES))

    @pl.when(j < nused_ref[0])
    def _():
        a8, inv_a = _f8_quantize(a_ref[...].astype(F32))
        o_ref[...] = (_dot(a8, wd_s[...]) * (inv_a * inv_s[0:1, 0:1])).astype(BF16)

    @pl.when(j >= nused_ref[0])
    def _():
        o_ref[...] = jnp.zeros_like(o_ref)


def _moe_down(te, chg, nused, act, wd):
    rows, ff = act.shape
    d = wd.shape[2]
    return pl.pallas_call(
        _moe_down_kernel,
        grid_spec=pltpu.PrefetchScalarGridSpec(
            num_scalar_prefetch=3, grid=(rows // MOE_TILE,),
            in_specs=[
                pl.BlockSpec((MOE_TILE, ff), lambda j, *_: (j, 0)),
                pl.BlockSpec((None, ff, d), lambda j, te, *_: (te[j], 0, 0)),
            ],
            out_specs=pl.BlockSpec((MOE_TILE, d), lambda j, *_: (j, 0)),
            scratch_shapes=[pltpu.VMEM((ff, d), F8), pltpu.VMEM((SUBLANES, LANES), F32)],
        ),
        out_shape=jax.ShapeDtypeStruct((rows, d), BF16),
        compiler_params=_cparams(("arbitrary",)), name="moe_down",
    )(te, chg, nused, act, wd)


def _combine_kernel(alpha, nw_ctx, blk_a_ref, blk_b_ref, rel_ref, two_ref, rank_ref, gate_ref, *refs):
    ya_refs, yb_refs = refs[:N_EXPERTS], refs[N_EXPERTS:2 * N_EXPERTS]
    x1_ref, g2_ref, lg_ref, lb_ref, oc_ref, ol_ref, acc_s = refs[2 * N_EXPERTS:]
    w = pl.program_id(0)
    row_id = lax.broadcasted_iota(jnp.int32, (MOE_TILE, MOE_TILE), 0).astype(F32)

    def weights(e, rel):
        hit = rank_ref[e:e + 1, :] == row_id + rel.astype(F32)
        return jnp.where(hit, gate_ref[e:e + 1, :], 0.0).T.astype(BF16)

    acc = None
    for e in range(N_EXPERTS):
        part = _dot(weights(e, rel_ref[w * N_EXPERTS + e]), ya_refs[e][...])
        acc = part if acc is None else acc + part
    acc_s[...] = acc
    for e in range(N_EXPERTS):
        @pl.when(two_ref[w * N_EXPERTS + e] == 1)
        def _():
            acc_s[...] += _dot(weights(e, rel_ref[w * N_EXPERTS + e] + MOE_TILE), yb_refs[e][...])

    out = _layer_norm(alpha * x1_ref[...] + g2_ref[...] * acc_s[...], lg_ref[...], lb_ref[...])

    @pl.when(w < nw_ctx)
    def _():
        oc_ref[...] = out

    @pl.when(w >= nw_ctx)
    def _():
        ol_ref[...] = out


def _moe_combine(bands, rank_t, gate_t, y, x1, mod_l, ln_g, ln_b, alpha, row_of_tile, t_ctx):
    t, d = x1.shape
    nw = t // MOE_TILE
    nw_ctx = t_ctx // MOE_TILE

    def y_spec(which, e):
        return pl.BlockSpec((MOE_TILE, d), lambda w, *pre: (pre[which][w * N_EXPERTS + e], 0))

    win_spec = pl.BlockSpec((None, N_EXPERTS, MOE_TILE), lambda w, *_: (w, 0, 0))
    return pl.pallas_call(
        functools.partial(_combine_kernel, alpha, nw_ctx),
        grid_spec=pltpu.PrefetchScalarGridSpec(
            num_scalar_prefetch=4, grid=(nw,),
            in_specs=[win_spec, win_spec]
            + [y_spec(0, e) for e in range(N_EXPERTS)] + [y_spec(1, e) for e in range(N_EXPERTS)]
            + [
                pl.BlockSpec((MOE_TILE, d), lambda w, *_: (w, 0)),
                pl.BlockSpec((None, None, 1, d), lambda w, *_: (row_of_tile(w), 5, 0, 0)),
                pl.BlockSpec((1, d), lambda w, *_: (0, 0)),
                pl.BlockSpec((1, d), lambda w, *_: (0, 0)),
            ],
            out_specs=[
                pl.BlockSpec((MOE_TILE, d), lambda w, *_: (jnp.minimum(w, nw_ctx - 1), 0)),
                pl.BlockSpec((MOE_TILE, d), lambda w, *_: (jnp.maximum(w - nw_ctx, 0), 0)),
            ],
            scratch_shapes=[pltpu.VMEM((MOE_TILE, d), F32)],
        ),
        out_shape=[jax.ShapeDtypeStruct((t_ctx, d), F32), jax.ShapeDtypeStruct((t - t_ctx, d), F32)],
        compiler_params=_cparams(("arbitrary",)), name="moe_combine",
    )(*bands, rank_t, gate_t, *([y] * (2 * N_EXPERTS)), x1, mod_l, ln_g, ln_b)


def _moe_layer(gates, h2, x1, mod_l, wg, wu, wd, ln_g, ln_b, alpha, row_of_tile, t_ctx):
    t = h2.shape[0]
    n_tiles = -(-(2 * t + N_EXPERTS * (MOE_TILE - 1)) // MOE_TILE)
    n_tiles += -n_tiles % GATHER_TILES
    rank_t, gate_t, before = _route(gates)
    before = before[:, :, 0].astype(jnp.int32)
    counts = jnp.sum(gates > 0.0, axis=1).astype(jnp.int32)
    tiles_e = (counts + MOE_TILE - 1) // MOE_TILE
    tile_end = jnp.cumsum(tiles_e)
    n_used = tile_end[-1]
    j = jnp.arange(n_tiles, dtype=jnp.int32)
    te = jnp.minimum(jnp.sum(j[:, None] >= tile_end[None, :], axis=1), N_EXPERTS - 1).astype(jnp.int32)
    base = (j - jnp.take(tile_end - tiles_e, te)) * MOE_TILE
    last = jnp.minimum(base + MOE_TILE - 1, jnp.take(counts, te) - 1)
    before_t = jnp.take(before, te, axis=1)
    used = j < n_used
    wlo = jnp.where(used, jnp.sum(before_t <= base[None, :], axis=0) - 1, 1).astype(jnp.int32)
    whi = jnp.where(used, jnp.sum(before_t <= last[None, :], axis=0) - 1, 0).astype(jnp.int32)
    chg = jnp.concatenate([jnp.ones((1,), jnp.int32), (te[1:] != te[:-1]).astype(jnp.int32)])
    nused = n_used.reshape(1).astype(jnp.int32)
    meta = (te, base.astype(jnp.int32), wlo, whi)
    xs, inv_xs = _moe_gather(meta, rank_t, h2, n_tiles)
    act = _moe_up(te, chg, nused, xs, inv_xs, wg, wu)
    y = _moe_down(te, chg, nused, act, wd)
    start_row = (tile_end - tiles_e) * MOE_TILE
    n_we = jnp.concatenate([before[1:], counts[None, :]], axis=0) - before
    first = start_row[None, :] + before
    blk_a = jnp.minimum(first // MOE_TILE, n_tiles - 1)
    two = ((n_we > 0) & ((first + n_we - 1) // MOE_TILE > blk_a)).astype(jnp.int32)
    rel = blk_a * MOE_TILE - start_row[None, :]
    bands = tuple(a.reshape(-1).astype(jnp.int32) for a in (blk_a, blk_a + two, rel, two))
    return _moe_combine(bands, rank_t, gate_t, y, x1, mod_l, ln_g, ln_b, alpha, row_of_tile, t_ctx)


def _rope_tables(seq):
    half = GLA_DK // 2
    n = half // 2
    t = np.arange(seq)
    inv = ROPE_BASE ** (-np.arange(n, dtype=np.float64) / n)
    cos = np.zeros((seq, GLA_DK), np.float64)
    sin = np.zeros((seq, GLA_DK), np.float64)
    for a, pos in enumerate((t // GRID_W, t % GRID_W)):
        ang = pos[:, None].astype(np.float64) * inv[None, :]
        base = a * half
        cos[:, base:base + n] = np.cos(ang)
        cos[:, base + n:base + half] = np.cos(ang)
        sin[:, base:base + n] = -np.sin(ang)
        sin[:, base + n:base + half] = np.sin(ang)
    tile = lambda m: jnp.asarray(np.tile(m, (1, GLA_HEADS)), F32)
    return tile(cos), tile(sin)


W_IN_GD = 1536
W_IN_KV = (2080, 2592)
W_IN_CHUNKS = (0, 512, 1024, 1568, 3104, 3616, 4128, 4640, 5152, 5664, 6176, 6688)


def _transpose_w_in(w):
    return jnp.swapaxes(w, 1, 2).astype(BF16)


def _gate_up_blocks(w_gup):
    qkw = w_gup.shape[-1]
    fwd = jnp.pad(w_gup[:, 0], ((0, 0), (0, 0), (0, qkw)))
    bwd = jnp.pad(w_gup[:, 1], ((0, 0), (0, 0), (qkw, 0)))
    return jnp.pad(jnp.concatenate([fwd, bwd], axis=1), ((0, 0), (0, LANES - 2 * GLA_RANK), (0, 0)))


def kernel(x_prompt, x_sample, cache_na_k, cache_na_v, state_gla, c, c_ctx, w_mod, b_mod, w_in, gla_w_gup, gla_b_g, gla_norm, w_br_gla, na_rpb, w_br_na, cv_w, cv_b, cv_ln_g, cv_ln_b, w_br_cv, w_out, ln_g, ln_b, ffd_w_gate, ffd_w_up, ffd_w_down, moe_w_router, moe_w_gate, moe_w_up, moe_w_down):
    nb_c, l_c, d = x_prompt.shape
    nb_l, l_l, _ = x_sample.shape
    depth = w_mod.shape[0]
    t_ctx, t_lat = nb_c * l_c, nb_l * l_l
    t_all = t_ctx + t_lat
    alpha = (2 * depth) ** 0.25
    tm = 512
    row_of_tile = _row_of_tile_fn(t_ctx, l_l, tm)

    x_pair = (x_prompt.reshape(t_ctx, d), x_sample.reshape(t_lat, d))
    cond8 = jnp.concatenate([c_ctx[None, :], c, jnp.zeros((8 - 1 - nb_l, d), F32)], axis=0)
    mod = _modulation(cond8, w_mod, b_mod).reshape(depth, 8, 6, 1, d)
    rope = _rope_tables(l_l)
    rows_lat = l_l // GRID_W
    kr = min(NA_WIN_R, rows_lat)
    s0_all = state_gla.reshape(nb_l, depth, 2, 2, LANES, GLA_DV)
    w_t = _transpose_w_in(w_in)
    wg_blks = _gate_up_blocks(gla_w_gup)
    toep = _na_bias_tables(na_rpb)
    w_br = [w.astype(BF16) for w in (w_br_gla, w_br_na, w_br_cv, w_out)]

    caches = None
    states = None
    for l in range(depth):
        mod_l = mod[l]
        gd, kc, vc, kl, vl, zb = _input_projection(x_pair, mod_l, w_t, l, depth, l_c, tm, row_of_tile, caches)
        caches = (kc, vc)

        wg_blk = wg_blks[l]
        bg = gla_b_g[l].reshape(1, CHUNK_W)
        gn = gla_norm[l].reshape(1, CHUNK_W)
        ya, states = _gla_branch(gd, zb, wg_blk, bg, gn, latent=False, seq=l_c, nb=nb_c, tok0=0, t_total=t_all,
                                 state_slot=(l, depth, states))
        ya = _gla_branch(gd, zb, wg_blk, bg, gn, latent=True, seq=l_l, nb=nb_l, tok0=t_ctx, t_total=t_all,
                         rope=rope, s0=s0_all[:, l], ya_prev=ya)

        yb = _attn_context(zb, kc, vc, l, nb_c, l_c, t_all)
        yb = _attn_latent(zb, kl, vl, cache_na_k, cache_na_v, l, toep, yb, nb_l, l_l, t_ctx)

        conv_args = (zb, cv_w[l], cv_b[l][None, :], cv_ln_g[l][None, :], cv_ln_b[l][None, :])
        yc = _conv_branch(*conv_args, seq=l_c, nb=nb_c, tok0=0, t_total=t_all)
        yc = _conv_branch(*conv_args, seq=l_l, nb=nb_l, tok0=t_ctx, t_total=t_all, prev=yc)

        moe = l % 2 == 1
        w_router = None
        if moe:
            w_router = moe_w_router[l // 2].T
        merged = _merge(ya, yb, yc, zb, x_pair, mod_l, w_br[0][l], w_br[1][l], w_br[2][l], w_br[3][l],
                        ln_g[l, 0][None, :], ln_b[l, 0][None, :], alpha, tm, row_of_tile, w_router)
        ln2 = (ln_g[l, 1][None, :], ln_b[l, 1][None, :])
        if moe:
            x1, h2, gates = merged
            x_pair = _moe_layer(gates, h2, x1, mod_l, moe_w_gate[l // 2], moe_w_up[l // 2], moe_w_down[l // 2],
                                *ln2, alpha, _row_of_tile_fn(t_ctx, l_l, MOE_TILE), t_ctx)
        else:
            x1, h2 = merged
            i = l // 2
            x_pair = _dense_mixer(h2, x1, mod_l, ffd_w_gate[i].astype(BF16), ffd_w_up[i].astype(BF16),
                                  ffd_w_down[i].astype(BF16), *ln2, alpha, tm, 1, row_of_tile, t_ctx)

    y_prompt = x_pair[0].reshape(nb_c, l_c, d)
    y_sample = x_pair[1].reshape(nb_l, l_l, d)
    cache_shape = (nb_c, depth, l_c, NA_HEADS, NA_DH)
    return (y_prompt, y_sample, caches[0].reshape(cache_shape), caches[1].reshape(cache_shape),
            states.reshape(nb_c, depth, 2, GLA_HEADS, GLA_DK, GLA_DV))


def _row_of_tile_fn(t_ctx, l_lat, tm):
    n_ctx = t_ctx // tm

    def row_of_tile(i):
        return jnp.where(i < n_ctx, 0, 1 + (i - n_ctx) // (l_lat // tm))

    return row_of_tile
```

```python
import functools

import numpy as np
import jax
import jax.numpy as jnp
from jax import lax
from jax.experimental import pallas as pl
from jax.experimental.pallas import tpu as pltpu

F32 = jnp.float32
BF16 = jnp.bfloat16

GRID_W = 64
GLA_HEADS = 4
GLA_DK = 64
GLA_DV = 128
GLA_RANK = 16
GLA_TAU = 16.0
GLA_CHUNK = 64
GLA_GROUP = 256
GLA_CTX_SEQS = 2
NA_HEADS = 8
NA_DH = 64
NA_WIN_R = 8
NA_WIN_C = 16
CONV_CH = 512
CONV_K = 31
N_EXPERTS = 8
ROPE_BASE = 10000.0
LN_EPS = 1e-5

D_MODEL = 1024
LANES = 128
SUBLANES = 8
CHUNK_W = 512
N_BF16_CHUNKS = 12
NEG_BIG = -1e30
VMEM_LIMIT = 56 * 1024 * 1024


def _cparams(sem, vmem=VMEM_LIMIT):
    return pltpu.CompilerParams(dimension_semantics=sem, vmem_limit_bytes=vmem)


def _dot(a, b):
    return jnp.dot(a, b, preferred_element_type=F32)


def _split(x):
    hi = x.astype(BF16)
    return hi, (x - hi.astype(F32)).astype(BF16)


def _dot_split(a, b):
    a_hi, a_lo = _split(a)
    b_hi, b_lo = _split(b)
    return _dot(a_hi, b_hi) + _dot(a_lo, b_hi) + _dot(a_hi, b_lo)


def _dot_split_rhs(a, b):
    b_hi, b_lo = _split(b)
    return _dot(a, b_hi) + _dot(a, b_lo)


def _dot_nt(a, b):
    return lax.dot_general(a, b, (((1,), (1,)), ((), ())), preferred_element_type=F32)


def _sigmoid(x):
    return 0.5 * jnp.tanh(0.5 * x) + 0.5


def _silu(x):
    return x * _sigmoid(x)


def _layer_norm(x, g, b):
    mu = jnp.mean(x, axis=-1, keepdims=True)
    xc = x - mu
    var = jnp.mean(xc * xc, axis=-1, keepdims=True)
    return xc * lax.rsqrt(var + LN_EPS) * g + b


def _mod_kernel(c_ref, w_ref, b_ref, o_ref):
    c = c_ref[...]
    s = _silu(c).astype(BF16)
    o_ref[...] = _dot(s, w_ref[...].astype(BF16)) + b_ref[...]


def _modulation(cond8, w_mod, b_mod):
    depth, d, n = w_mod.shape
    tn = 1024
    return pl.pallas_call(
        _mod_kernel,
        grid=(depth, n // tn),
        in_specs=[
            pl.BlockSpec((8, d), lambda l, j: (0, 0)),
            pl.BlockSpec((None, d, tn), lambda l, j: (l, 0, j)),
            pl.BlockSpec((None, 1, tn), lambda l, j: (l, 0, j)),
        ],
        out_specs=pl.BlockSpec((None, 8, tn), lambda l, j: (l, 0, j)),
        out_shape=jax.ShapeDtypeStruct((depth, 8, n), F32),
        compiler_params=_cparams(("arbitrary", "arbitrary")),
        name="modulation",
    )(cond8, w_mod, b_mod.reshape(depth, 1, n))


def _inproj_kernel(n_ctx_tiles, aliased, *refs):
    if aliased:
        (xc_ref, xl_ref, sc_ref, sh_ref, w_ref, _kc_in, _vc_in,
         gd_ref, kc_ref, vc_ref, kl_ref, vl_ref, zb_ref, h_ref) = refs
    else:
        (xc_ref, xl_ref, sc_ref, sh_ref, w_ref,
         gd_ref, kc_ref, vc_ref, kl_ref, vl_ref, zb_ref, h_ref) = refs
    is_ctx = pl.program_id(0) < n_ctx_tiles
    x = jnp.where(is_ctx, xc_ref[...], xl_ref[...])
    h_ref[...] = (x * (1.0 + sc_ref[...]) + sh_ref[...]).astype(BF16)

    def project(start, width):
        return _dot_nt(h_ref[...], w_ref[start:start + width, :])

    gd_ref[...] = project(W_IN_GD, LANES)
    k = project(W_IN_KV[0], CHUNK_W)
    v = project(W_IN_KV[1], CHUNK_W)
    seqs, l_c, _ = kc_ref.shape

    @pl.when(is_ctx)
    def _():
        for s in range(seqs):
            kc_ref[s] = k[s * l_c:(s + 1) * l_c]
            vc_ref[s] = v[s * l_c:(s + 1) * l_c]

    @pl.when(jnp.logical_not(is_ctx))
    def _():
        kl_ref[...] = k
        vl_ref[...] = v

    for j, start in enumerate(W_IN_CHUNKS):
        zb_ref[j] = project(start, CHUNK_W).astype(BF16)


def _mod_spec(piece, row_of_tile):
    return pl.BlockSpec((None, None, 1, D_MODEL), lambda i: (row_of_tile(i), piece, 0, 0))


def _pair_specs(tm, d, n_ctx_tiles):
    return [pl.BlockSpec((tm, d), lambda i, *_: (jnp.minimum(i, n_ctx_tiles - 1), 0)),
            pl.BlockSpec((tm, d), lambda i, *_: (jnp.maximum(i - n_ctx_tiles, 0), 0))]


def _input_projection(x_pair, mod_l, w_t, layer, depth, l_c, tm, row_of_tile, caches=None):
    xc, xl = x_pair
    d = xc.shape[1]
    t_ctx, t_lat = xc.shape[0], xl.shape[0]
    t = t_ctx + t_lat
    n_ctx_tiles = t_ctx // tm
    seqs = tm // l_c
    cache_spec = pl.BlockSpec((seqs, None, l_c, CHUNK_W),
                              lambda i: (jnp.minimum(i, n_ctx_tiles - 1), layer, 0, 0))
    lat_spec = pl.BlockSpec((tm, CHUNK_W), lambda i: (jnp.maximum(i - n_ctx_tiles, 0), 0))
    in_specs = _pair_specs(tm, d, n_ctx_tiles) + [
        _mod_spec(1, row_of_tile), _mod_spec(0, row_of_tile),
        pl.BlockSpec((None,) + w_t.shape[1:], lambda i: (layer, 0, 0), pipeline_mode=pl.Buffered(1)),
    ]
    args = [xc, xl, mod_l, mod_l, w_t]
    aliases = {}
    if caches is not None:
        in_specs += [pl.BlockSpec(memory_space=pl.ANY), pl.BlockSpec(memory_space=pl.ANY)]
        args += list(caches)
        aliases = {5: 1, 6: 2}
    cache_shape = jax.ShapeDtypeStruct((t_ctx // l_c, depth, l_c, CHUNK_W), F32)
    return pl.pallas_call(
        functools.partial(_inproj_kernel, n_ctx_tiles, caches is not None),
        grid=(t // tm,),
        in_specs=in_specs,
        out_specs=[
            pl.BlockSpec((tm, LANES), lambda i: (i, 0)),
            cache_spec, cache_spec, lat_spec, lat_spec,
            pl.BlockSpec((N_BF16_CHUNKS, tm, CHUNK_W), lambda i: (0, i, 0)),
        ],
        out_shape=[
            jax.ShapeDtypeStruct((t, LANES), F32),
            cache_shape, cache_shape,
            jax.ShapeDtypeStruct((t_lat, CHUNK_W), F32), jax.ShapeDtypeStruct((t_lat, CHUNK_W), F32),
            jax.ShapeDtypeStruct((N_BF16_CHUNKS, t, CHUNK_W), BF16),
        ],
        scratch_shapes=[pltpu.VMEM((tm, d), BF16)],
        input_output_aliases=aliases,
        compiler_params=_cparams(("arbitrary",)),
        name="input_projection",
    )(*args)


def _gla_kernel(latent, seq, n_seq, state_alias, *refs):
    if latent:
        (qk_ref, v_ref, ra_ref, gd_ref, wg_ref, bg_ref, gn_ref, cos_ref, sin_ref, s0_ref, _alias,
         ya_ref, q_s, k_s, g_s, acc_s, st_s) = refs
    elif state_alias:
        (qk_ref, v_ref, ra_ref, gd_ref, wg_ref, bg_ref, gn_ref, _alias,
         ya_ref, sfin_ref, q_s, k_s, g_s, acc_s, st_s) = refs
    else:
        (qk_ref, v_ref, ra_ref, gd_ref, wg_ref, bg_ref, gn_ref,
         ya_ref, sfin_ref, q_s, k_s, g_s, acc_s, st_s) = refs
    c = GLA_CHUNK
    n_chunks = n_seq * seq // c
    qkw = GLA_HEADS * GLA_DK

    q = qk_ref[:, 0:qkw].astype(F32) * (GLA_DK ** -0.5)
    k = qk_ref[:, qkw:2 * qkw].astype(F32)
    if latent:
        lane = lax.broadcasted_iota(jnp.int32, (1, qkw), 1)
        first_half = (lane % 32) < 16

        def rope(x):
            swapped = jnp.where(first_half, pltpu.roll(x, qkw - 16, 1), pltpu.roll(x, 16, 1))
            return x * cos_ref[...] + swapped * sin_ref[...]

        q, k = rope(q), rope(k)
    q_s[...] = q
    k_s[...] = k

    glog = _dot_split(gd_ref[...], wg_ref[...]) + bg_ref[...]
    g_s[...] = (jnp.minimum(glog, 0.0) - jnp.log(1.0 + jnp.exp(-jnp.abs(glog)))) * (1.0 / GLA_TAU)
    acc_s[...] = jnp.zeros_like(acc_s)
    if latent:
        st_s[0] = s0_ref[...]
    else:
        st_s[...] = jnp.zeros_like(st_s)

    grp = GLA_GROUP
    cpg = grp // c
    n_groups = seq // grp
    grow = lax.broadcasted_iota(jnp.int32, (grp, grp), 0)
    gcol = lax.broadcasted_iota(jnp.int32, (grp, grp), 1)
    same_chunk = (grow // c) == (gcol // c)
    keep = (same_chunk & (gcol <= grow), same_chunk & (gcol >= grow))
    tri = (keep[0].astype(BF16), keep[1].astype(BF16))
    lane128 = lax.broadcasted_iota(jnp.int32, (1, LANES), 1)
    head_mask = (lane128 < GLA_DK, lane128 >= GLA_DK)
    urow = lax.broadcasted_iota(jnp.int32, (cpg * LANES, GLA_DV), 0)
    srow = lax.broadcasted_iota(jnp.int32, (LANES, GLA_DV), 0)
    chunk_of_col = lax.broadcasted_iota(jnp.int32, (LANES, grp), 1) // c
    chunk_of_row = lax.broadcasted_iota(jnp.int32, (grp, LANES), 0) // c
    pad_rows = jnp.zeros((SUBLANES - cpg, qkw), F32)

    def group_step(i, carry):
        for sq, d in ((sq, d) for sq in range(n_seq) for d in range(2)):
            gi = i if d == 0 else n_groups - 1 - i
            rows = pl.ds(pl.multiple_of(sq * seq + gi * grp, grp), grp)
            cum = _dot_split_rhs(tri[d], g_s[rows, d * qkw:(d + 1) * qkw])
            edge = c - 1 if d == 0 else 0
            lasts = [cum[ci * c + edge:ci * c + edge + 1, :] for ci in range(cpg)]
            last = jnp.concatenate([jnp.broadcast_to(r, (c, qkw)) for r in lasts], axis=0)
            ref = 0.5 * last
            q = q_s[rows, :]
            k = k_s[rows, :]
            qe = q * jnp.exp(cum - ref)
            ke = k * jnp.exp(ref - cum)
            qe2 = q * jnp.exp(cum)
            ke2 = k * jnp.exp(last - cum)
            last8 = jnp.concatenate(lasts + [pad_rows], axis=0)
            for p in range(2):
                sl = slice(p * LANES, (p + 1) * LANES)
                ke_p = ke[:, sl].astype(BF16)
                ke2_t = ke2[:, sl].T
                ke2_blk = jnp.concatenate(
                    [jnp.where(chunk_of_col == ci, ke2_t, 0.0) for ci in range(cpg)], axis=0).astype(BF16)
                dec = jnp.exp(last8[:, sl]).T
                psl = slice(2 * p * GLA_DV, (2 * p + 2) * GLA_DV)
                v_pair = v_ref[rows, psl]
                u = _dot(ke2_blk, v_pair)
                upd = jnp.where(urow % LANES < GLA_DK, u[:, :GLA_DV], u[:, GLA_DV:])
                s = st_s[sq, d, p]
                s_in = [None] * cpg
                for ci in (range(cpg) if d == 0 else range(cpg - 1, -1, -1)):
                    s_in[ci] = jnp.concatenate(
                        [jnp.where(srow < GLA_DK, s, 0.0), jnp.where(srow >= GLA_DK, s, 0.0)], axis=1).astype(BF16)
                    s = dec[:, ci:ci + 1] * s + upd[ci * LANES:(ci + 1) * LANES]
                st_s[sq, d, p] = s
                s_stack = jnp.concatenate(s_in, axis=0)
                a2_blk = jnp.concatenate(
                    [jnp.where(chunk_of_row == ci, qe2[:, sl], 0.0) for ci in range(cpg)], axis=1).astype(BF16)
                intra = []
                for hh in range(2):
                    a = jnp.where(head_mask[hh], qe[:, sl], 0.0).astype(BF16)
                    att = jnp.where(keep[d], _dot_nt(a, ke_p), 0.0).astype(BF16)
                    intra.append(_dot(att, v_pair[:, hh * GLA_DV:(hh + 1) * GLA_DV]))
                acc_s[rows, psl] += jnp.concatenate(intra, axis=1) + _dot(a2_blk, s_stack)
        return carry

    lax.fori_loop(0, n_groups, group_step, 0)

    def epilogue(i, carry):
        rows = pl.ds(pl.multiple_of(i * c, c), c)
        for h in range(GLA_HEADS):
            vsl = slice(h * GLA_DV, (h + 1) * GLA_DV)
            o = acc_s[rows, vsl]
            o = o * lax.rsqrt(jnp.mean(o * o, axis=-1, keepdims=True) + LN_EPS) * gn_ref[:, vsl]
            ya_ref[rows, vsl] = (o * _silu(ra_ref[rows, vsl].astype(F32))).astype(BF16)
        return carry

    lax.fori_loop(0, n_chunks, epilogue, 0)
    if not latent:
        sfin_ref[...] = st_s[...]


def _gla_branch(gd, zb, wg_blk, bg, gn, *, latent, seq, nb, tok0, t_total, rope=None, s0=None, ya_prev=None,
                state_slot=None):
    n_seq = 1 if latent else GLA_CTX_SEQS
    rows = n_seq * seq
    b0 = tok0 // rows
    qkw = GLA_HEADS * GLA_DK
    in_specs = [
        pl.BlockSpec((None, rows, CHUNK_W), lambda b: (0, b0 + b, 0)),
        pl.BlockSpec((None, rows, CHUNK_W), lambda b: (1, b0 + b, 0)),
        pl.BlockSpec((None, rows, CHUNK_W), lambda b: (2, b0 + b, 0)),
        pl.BlockSpec((rows, LANES), lambda b: (b0 + b, 0)),
        pl.BlockSpec((LANES, CHUNK_W), lambda b: (0, 0)),
        pl.BlockSpec((1, CHUNK_W), lambda b: (0, 0)),
        pl.BlockSpec((1, CHUNK_W), lambda b: (0, 0)),
    ]
    args = [zb, zb, zb, gd, wg_blk, bg, gn]
    scratch = [
        pltpu.VMEM((rows, qkw), F32), pltpu.VMEM((rows, qkw), F32), pltpu.VMEM((rows, CHUNK_W), F32),
        pltpu.VMEM((rows, CHUNK_W), F32), pltpu.VMEM((n_seq, 2, 2, LANES, GLA_DV), F32),
    ]
    ya_shape = jax.ShapeDtypeStruct((t_total, CHUNK_W), BF16)
    ya_spec = pl.BlockSpec((rows, CHUNK_W), lambda b: (b0 + b, 0))
    if latent:
        cos_t, sin_t = rope
        in_specs += [
            pl.BlockSpec((seq, qkw), lambda b: (0, 0)),
            pl.BlockSpec((seq, qkw), lambda b: (0, 0)),
            pl.BlockSpec((None, 2, 2, LANES, GLA_DV), lambda b: (b, 0, 0, 0, 0)),
            pl.BlockSpec(memory_space=pl.ANY),
        ]
        args += [cos_t, sin_t, s0, ya_prev]
        return pl.pallas_call(
            functools.partial(_gla_kernel, True, seq, n_seq, False),
            grid=(nb,), in_specs=in_specs, out_specs=ya_spec, out_shape=ya_shape,
            scratch_shapes=scratch, input_output_aliases={len(args) - 1: 0},
            compiler_params=_cparams(("arbitrary",)), name="gla_latent",
        )(*args)
    layer, depth, states = state_slot
    aliases = {}
    if states is not None:
        in_specs.append(pl.BlockSpec(memory_space=pl.ANY))
        args.append(states)
        aliases = {len(args) - 1: 1}
    return pl.pallas_call(
        functools.partial(_gla_kernel, False, seq, n_seq, states is not None),
        grid=(nb // n_seq,), in_specs=in_specs,
        out_specs=[ya_spec, pl.BlockSpec((n_seq, None, 2, 2, LANES, GLA_DV), lambda b: (b, layer, 0, 0, 0, 0))],
        out_shape=[ya_shape, jax.ShapeDtypeStruct((nb, depth, 2, 2, LANES, GLA_DV), F32)],
        scratch_shapes=scratch, input_output_aliases=aliases,
        compiler_params=_cparams(("arbitrary",)), name="gla_context",
    )(*args)


ATTN_CTX_SEQS = 2


def _attn_ctx_kernel(q_ref, k_ref, v_ref, o_ref):
    lane = lax.broadcasted_iota(jnp.int32, (1, LANES), 1)
    masks = (lane < NA_DH, lane >= NA_DH)
    scale = NA_DH ** -0.5
    n_seq, seq, _ = k_ref.shape
    for b in range(n_seq):
        rows = slice(b * seq, (b + 1) * seq)
        for p in range(NA_HEADS // 2):
            sl = slice(p * LANES, (p + 1) * LANES)
            q2 = _stack_heads(q_ref[rows, sl], masks)
            kp = k_ref[b, :, sl].astype(BF16)
            vp = v_ref[b, :, sl].astype(BF16)
            s = _dot_nt(q2, kp) * scale
            e = jnp.exp(s - jnp.max(s, axis=-1, keepdims=True))
            o2 = _dot(e.astype(BF16), vp) / jnp.sum(e, axis=-1, keepdims=True)
            o_ref[rows, sl] = jnp.where(masks[0], o2[0:seq], o2[seq:2 * seq]).astype(BF16)


def _stack_heads(qp, masks):
    zero = jnp.zeros_like(qp)
    return jnp.concatenate([jnp.where(masks[0], qp, zero), jnp.where(masks[1], qp, zero)], axis=0)


def _attn_context(zb, kc, vc, layer, nb, seq, t_total):
    per = ATTN_CTX_SEQS
    return pl.pallas_call(
        _attn_ctx_kernel,
        grid=(nb // per,),
        in_specs=[
            pl.BlockSpec((None, per * seq, CHUNK_W), lambda b: (3, b, 0)),
            pl.BlockSpec((per, None, seq, CHUNK_W), lambda b: (b, layer, 0, 0)),
            pl.BlockSpec((per, None, seq, CHUNK_W), lambda b: (b, layer, 0, 0)),
        ],
        out_specs=pl.BlockSpec((per * seq, CHUNK_W), lambda b: (b, 0)),
        out_shape=jax.ShapeDtypeStruct((t_total, CHUNK_W), BF16),
        compiler_params=_cparams(("arbitrary",)), name="attention_context",
    )(zb, kc, vc)


NA_ROWS_PER_STEP = 2


def _na_kernel(rows_total, kr, q_ref, kl_ref, vl_ref, kc_ref, vc_ref, toep_ref, _alias, o_ref):
    lane = lax.broadcasted_iota(jnp.int32, (1, LANES), 1)
    masks = (lane < NA_DH, lane >= NA_DH)
    scale = NA_DH ** -0.5
    nq = GRID_W
    for rr in range(NA_ROWS_PER_STEP):
        r = pl.program_id(1) * NA_ROWS_PER_STEP + rr
        qrows = slice(rr * nq, (rr + 1) * nq)
        rs = jnp.clip(r - kr // 2, 0, rows_total - kr)
        krows = pl.ds(pl.multiple_of(rs * GRID_W, GRID_W), kr * GRID_W)
        dr0 = rs - r + NA_WIN_R - 1
        for p in range(NA_HEADS // 2):
            sl = slice(p * LANES, (p + 1) * LANES)
            q2 = _stack_heads(q_ref[qrows, sl], masks)
            klp = kl_ref[krows, sl].astype(BF16)
            vlp = vl_ref[krows, sl].astype(BF16)
            kcp = kc_ref[:, sl].astype(BF16)
            vcp = vc_ref[:, sl].astype(BF16)
            bias = jnp.concatenate(
                [jnp.concatenate([toep_ref[2 * p + hh, dr0 + 2 * m] for m in range(kr // 2)], axis=1)
                 for hh in range(2)], axis=0)
            s_loc = _dot_nt(q2, klp) * scale + bias
            s_ctx = _dot_nt(q2, kcp) * scale
            m = jnp.maximum(jnp.max(s_loc, axis=-1, keepdims=True), jnp.max(s_ctx, axis=-1, keepdims=True))
            e_loc = jnp.exp(s_loc - m)
            e_ctx = jnp.exp(s_ctx - m)
            den = jnp.sum(e_loc, axis=-1, keepdims=True) + jnp.sum(e_ctx, axis=-1, keepdims=True)
            o2 = (_dot(e_loc.astype(BF16), vlp) + _dot(e_ctx.astype(BF16), vcp)) / den
            o_ref[qrows, sl] = jnp.where(masks[0], o2[0:nq], o2[nq:2 * nq]).astype(BF16)


def _na_bias_tables(rpb):
    nc = rpb.shape[-1]
    qc = np.arange(GRID_W)[:, None]
    kc = np.arange(GRID_W)[None, :]
    cs = np.clip(qc - NA_WIN_C // 2, 0, GRID_W - NA_WIN_C)
    valid = (kc >= cs) & (kc < cs + NA_WIN_C)
    tap = np.clip(kc - qc, -(NA_WIN_C - 1), NA_WIN_C - 1) + NA_WIN_C - 1
    pick = jnp.asarray(tap[..., None] == np.arange(nc), F32)
    toep = jnp.einsum('dhab,qkb->dhaqk', rpb.astype(F32), pick, precision=lax.Precision.HIGHEST)
    toep = jnp.where(valid, toep, NEG_BIG)
    return jnp.concatenate([toep[:, :, :-1], toep[:, :, 1:]], axis=-1)


def _attn_latent(zb, kl, vl, cache_k, cache_v, layer, toep, att_prev, nb, seq, tok0):
    rows_total = seq // GRID_W
    kr = min(NA_WIN_R, rows_total)
    nq = NA_ROWS_PER_STEP * GRID_W
    steps = rows_total // NA_ROWS_PER_STEP
    q0 = tok0 // nq
    past = cache_k.shape[2]
    ck = cache_k.reshape(cache_k.shape[0], cache_k.shape[1], past, NA_HEADS * NA_DH)
    cv = cache_v.reshape(ck.shape)
    return pl.pallas_call(
        functools.partial(_na_kernel, rows_total, kr),
        grid=(nb, steps),
        in_specs=[
            pl.BlockSpec((None, nq, CHUNK_W), lambda b, r: (3, q0 + b * steps + r, 0)),
            pl.BlockSpec((seq, CHUNK_W), lambda b, r: (b, 0)),
            pl.BlockSpec((seq, CHUNK_W), lambda b, r: (b, 0)),
            pl.BlockSpec((None, None, past, CHUNK_W), lambda b, r: (b, layer, 0, 0)),
            pl.BlockSpec((None, None, past, CHUNK_W), lambda b, r: (b, layer, 0, 0)),
            pl.BlockSpec((None,) + toep.shape[1:], lambda b, r: (layer, 0, 0, 0, 0), pipeline_mode=pl.Buffered(1)),
            pl.BlockSpec(memory_space=pl.ANY),
        ],
        out_specs=pl.BlockSpec((nq, CHUNK_W), lambda b, r: (q0 + b * steps + r, 0)),
        out_shape=jax.ShapeDtypeStruct(att_prev.shape, BF16),
        input_output_aliases={6: 0},
        compiler_params=_cparams(("arbitrary", "arbitrary")), name="attention_latent",
    )(zb, kl, vl, ck, cv, toep, att_prev)


CONV_PAD = 16
CONV_ROWS = 64


def _conv_kernel(seq, aliased, *refs):
    if aliased:
        a_ref, gt_ref, w_ref, cb_ref, lg_ref, lb_ref, _alias, o_ref, z_s, sh_s = refs
    else:
        a_ref, gt_ref, w_ref, cb_ref, lg_ref, lb_ref, o_ref, z_s, sh_s = refs
    z_s[0:CONV_PAD, :] = jnp.zeros((CONV_PAD, CONV_CH), F32)
    z_s[CONV_PAD + seq:2 * CONV_PAD + seq, :] = jnp.zeros((CONV_PAD, CONV_CH), F32)
    z_s[CONV_PAD:CONV_PAD + seq, :] = a_ref[...].astype(F32) * _sigmoid(gt_ref[...].astype(F32))
    ext = seq + 2 * CONV_PAD - SUBLANES
    for b in range(1, SUBLANES):
        for r0 in range(0, ext, CONV_ROWS):
            n = min(CONV_ROWS, ext - r0)
            sh_s[b - 1, r0:r0 + n, :] = z_s[r0 + b:r0 + b + n, :]
    off = CONV_PAD - CONV_K // 2
    grp = (CONV_ROWS // SUBLANES, SUBLANES, CONV_CH)
    for t0 in range(0, seq, CONV_ROWS):
        acc = jnp.zeros(grp, F32)
        for k in range(CONV_K):
            a, b = divmod(off + k, SUBLANES)
            r0 = t0 + a * SUBLANES
            tap = z_s[r0:r0 + CONV_ROWS, :] if b == 0 else sh_s[b - 1, r0:r0 + CONV_ROWS, :]
            acc = acc + tap.reshape(grp) * w_ref[k][None]
        y = _layer_norm(acc.reshape(CONV_ROWS, CONV_CH) + cb_ref[...], lg_ref[...], lb_ref[...])
        o_ref[t0:t0 + CONV_ROWS, :] = _silu(y).astype(BF16)


def _conv_branch(zb, cv_w, cv_b, ln_g, ln_b, *, seq, nb, tok0, t_total, prev=None):
    b0 = tok0 // seq
    in_specs = [
        pl.BlockSpec((None, seq, CHUNK_W), lambda b: (4, b0 + b, 0)),
        pl.BlockSpec((None, seq, CHUNK_W), lambda b: (5, b0 + b, 0)),
        pl.BlockSpec((CONV_K, SUBLANES, CONV_CH), lambda b: (0, 0, 0)),
        pl.BlockSpec((1, CONV_CH), lambda b: (0, 0)),
        pl.BlockSpec((1, CONV_CH), lambda b: (0, 0)),
        pl.BlockSpec((1, CONV_CH), lambda b: (0, 0)),
    ]
    args = [zb, zb, jnp.broadcast_to(cv_w[:, None, :], (CONV_K, SUBLANES, CONV_CH)), cv_b, ln_g, ln_b]
    aliases = {}
    if prev is not None:
        in_specs.append(pl.BlockSpec(memory_space=pl.ANY))
        args.append(prev)
        aliases = {6: 0}
    return pl.pallas_call(
        functools.partial(_conv_kernel, seq, prev is not None),
        grid=(nb,), in_specs=in_specs,
        out_specs=pl.BlockSpec((seq, CONV_CH), lambda b: (b0 + b, 0)),
        out_shape=jax.ShapeDtypeStruct((t_total, CONV_CH), BF16),
        scratch_shapes=[pltpu.VMEM((seq + 2 * CONV_PAD, CONV_CH), F32),
                        pltpu.VMEM((SUBLANES - 1, seq + 2 * CONV_PAD - SUBLANES, CONV_CH), F32)],
        input_output_aliases=aliases,
        compiler_params=_cparams(("arbitrary",)), name="conv_module_%d" % seq,
    )(*args)


def _merge_kernel(alpha, n_ctx_tiles, with_router, *refs):
    if with_router:
        (ya_ref, yb_ref, yc_ref, mg_ref, xc_ref, xl_ref, g1_ref, sc2_ref, sh2_ref, wa_ref, wb_ref, wc_ref, wo_ref,
         lg_ref, lb_ref, wr_ref, x1_ref, h2_ref, gates_ref) = refs
    else:
        (ya_ref, yb_ref, yc_ref, mg_ref, xc_ref, xl_ref, g1_ref, sc2_ref, sh2_ref, wa_ref, wb_ref, wc_ref, wo_ref,
         lg_ref, lb_ref, x1_ref, h2_ref) = refs
    x = jnp.where(pl.program_id(0) < n_ctx_tiles, xc_ref[...], xl_ref[...])
    halves = []
    for n in range(2):
        m = None
        for j, (y_ref, w_ref) in enumerate(((ya_ref, wa_ref), (yb_ref, wb_ref), (yc_ref, wc_ref))):
            proj = _dot(y_ref[...], w_ref[:, n * CHUNK_W:(n + 1) * CHUNK_W])
            term = _sigmoid(mg_ref[2 * j + n].astype(F32)) * proj
            m = term if m is None else m + term
        halves.append(m.astype(BF16))
    mix = _dot(halves[0], wo_ref[0:CHUNK_W, :]) + _dot(halves[1], wo_ref[CHUNK_W:2 * CHUNK_W, :])
    x1 = _layer_norm(alpha * x + g1_ref[...] * mix, lg_ref[...], lb_ref[...])
    x1_ref[...] = x1
    h2 = x1 * (1.0 + sc2_ref[...]) + sh2_ref[...]
    h2_ref[...] = h2.astype(BF16)
    if with_router:
        w_hi, w_lo = _split(wr_ref[...])
        h_hi, h_lo = _split(h2)
        lg = _dot_nt(w_hi, h_hi) + _dot_nt(w_hi, h_lo) + _dot_nt(w_lo, h_hi)
        eid = lax.broadcasted_iota(jnp.int32, lg.shape, 0)
        m1 = jnp.max(lg, axis=0, keepdims=True)
        i1 = jnp.min(jnp.where(lg == m1, eid, N_EXPERTS), axis=0, keepdims=True)
        lg2 = jnp.where(eid == i1, -jnp.inf, lg)
        m2 = jnp.max(lg2, axis=0, keepdims=True)
        i2 = jnp.min(jnp.where(lg2 == m2, eid, N_EXPERTS), axis=0, keepdims=True)
        e2 = jnp.exp(m2 - m1)
        w1 = 1.0 / (1.0 + e2)
        gates_ref[...] = jnp.where(eid == i1, w1, 0.0) + jnp.where(eid == i2, e2 * w1, 0.0)


def _merge(ya, yb, yc, zb, x_pair, mod_l, wa, wb, wc, wo, ln_g, ln_b, alpha, tm, row_of_tile, w_router=None):
    xc, xl = x_pair
    d = xc.shape[1]
    t = xc.shape[0] + xl.shape[0]
    n_ctx_tiles = xc.shape[0] // tm
    full = lambda shape: pl.BlockSpec(shape, lambda i: tuple(0 for _ in shape), pipeline_mode=pl.Buffered(1))
    in_specs = [
        pl.BlockSpec((tm, CHUNK_W), lambda i: (i, 0)),
        pl.BlockSpec((tm, CHUNK_W), lambda i: (i, 0)),
        pl.BlockSpec((tm, CHUNK_W), lambda i: (i, 0)),
        pl.BlockSpec((6, tm, CHUNK_W), lambda i: (1, i, 0)),
    ] + _pair_specs(tm, d, n_ctx_tiles) + [
        _mod_spec(2, row_of_tile), _mod_spec(4, row_of_tile), _mod_spec(3, row_of_tile),
        full(wa.shape), full(wb.shape), full(wc.shape), full(wo.shape), full((1, d)), full((1, d)),
    ]
    args = [ya, yb, yc, zb, xc, xl, mod_l, mod_l, mod_l, wa, wb, wc, wo, ln_g, ln_b]
    out_specs = [pl.BlockSpec((tm, d), lambda i: (i, 0)), pl.BlockSpec((tm, d), lambda i: (i, 0))]
    out_shape = [jax.ShapeDtypeStruct((t, d), F32), jax.ShapeDtypeStruct((t, d), BF16)]
    if w_router is not None:
        in_specs.append(full(w_router.shape))
        args.append(w_router)
        out_specs.append(pl.BlockSpec((N_EXPERTS, tm), lambda i: (0, i)))
        out_shape.append(jax.ShapeDtypeStruct((N_EXPERTS, t), F32))
    return pl.pallas_call(
        functools.partial(_merge_kernel, alpha, n_ctx_tiles, w_router is not None),
        grid=(t // tm,), in_specs=in_specs, out_specs=out_specs, out_shape=out_shape,
        compiler_params=_cparams(("arbitrary",)), name="merge",
    )(*args)


def _ffn_kernel(alpha, n_ctx_tiles, n_split, h_ref, x1_ref, g2_ref, wg_ref, wu_ref, wd_ref, lg_ref, lb_ref,
                oc_ref, ol_ref):
    h = h_ref[...]
    tf = wg_ref.shape[1] // n_split
    y = None
    for f in range(n_split):
        cols = slice(f * tf, (f + 1) * tf)
        act = _silu(_dot(h, wg_ref[:, cols])) * _dot(h, wu_ref[:, cols])
        part = _dot(act.astype(BF16), wd_ref[cols, :])
        y = part if y is None else y + part
    out = _layer_norm(alpha * x1_ref[...] + g2_ref[...] * y, lg_ref[...], lb_ref[...])
    is_ctx = pl.program_id(0) < n_ctx_tiles

    @pl.when(is_ctx)
    def _():
        oc_ref[...] = out

    @pl.when(jnp.logical_not(is_ctx))
    def _():
        ol_ref[...] = out


def _dense_mixer(h2, x1, mod_l, wg, wu, wd, ln_g, ln_b, alpha, tm, n_split, row_of_tile, t_ctx):
    t, d = x1.shape
    n_ctx_tiles = t_ctx // tm
    const = lambda shape: pl.BlockSpec(shape, lambda i: tuple(0 for _ in shape), pipeline_mode=pl.Buffered(1))
    return pl.pallas_call(
        functools.partial(_ffn_kernel, alpha, n_ctx_tiles, n_split),
        grid=(t // tm,),
        in_specs=[
            pl.BlockSpec((tm, d), lambda i: (i, 0)),
            pl.BlockSpec((tm, d), lambda i: (i, 0)),
            _mod_spec(5, row_of_tile),
            const(wg.shape), const(wu.shape), const(wd.shape), const((1, d)), const((1, d)),
        ],
        out_specs=_pair_specs(tm, d, n_ctx_tiles),
        out_shape=[jax.ShapeDtypeStruct((t_ctx, d), F32), jax.ShapeDtypeStruct((t - t_ctx, d), F32)],
        compiler_params=_cparams(("arbitrary",)), name="dense_mixer",
    )(h2, x1, mod_l, wg, wu, wd, ln_g, ln_b)


MOE_TILE = 256
EXPERT_TILE = 512
MOE_TF = 1792
GATHER_TILES = 2


def _route_kernel(g_ref, rank_ref, gate_ref, before_ref, run_s):
    w = pl.program_id(0)

    @pl.when(w == 0)
    def _():
        run_s[...] = jnp.zeros_like(run_s)

    g = g_ref[...]
    sel = g > 0.0
    row = lax.broadcasted_iota(jnp.int32, (MOE_TILE, MOE_TILE), 0)
    col = lax.broadcasted_iota(jnp.int32, (MOE_TILE, MOE_TILE), 1)
    earlier = (row < col).astype(BF16)
    ones = jnp.where(sel, 1.0, 0.0)
    rank = _dot(ones.astype(BF16), earlier) + run_s[:, 0:1]
    rank_ref[...] = jnp.where(sel, rank, -1.0)
    gate_ref[...] = g
    before_ref[...] = run_s[...]
    run_s[...] += jnp.sum(ones, axis=1, keepdims=True)


def _route(gates_t):
    t = gates_t.shape[1]
    nw = t // MOE_TILE
    return pl.pallas_call(
        _route_kernel,
        grid=(nw,),
        in_specs=[pl.BlockSpec((N_EXPERTS, MOE_TILE), lambda w: (0, w))],
        out_specs=[
            pl.BlockSpec((None, N_EXPERTS, MOE_TILE), lambda w: (w, 0, 0)),
            pl.BlockSpec((None, N_EXPERTS, MOE_TILE), lambda w: (w, 0, 0)),
            pl.BlockSpec((None, N_EXPERTS, LANES), lambda w: (w, 0, 0)),
        ],
        out_shape=[
            jax.ShapeDtypeStruct((nw, N_EXPERTS, MOE_TILE), F32),
            jax.ShapeDtypeStruct((nw, N_EXPERTS, MOE_TILE), F32),
            jax.ShapeDtypeStruct((nw, N_EXPERTS, LANES), F32),
        ],
        scratch_shapes=[pltpu.VMEM((N_EXPERTS, LANES), F32)],
        compiler_params=_cparams(("arbitrary",)), name="moe_route",
    )(gates_t)


def _tile_onehot(rank_ref, w, e, base):
    rank_row = rank_ref[w, pl.ds(e, 1), :]
    rows = lax.broadcasted_iota(jnp.int32, (MOE_TILE, MOE_TILE), 0).astype(F32) + base.astype(F32)
    return rank_row == rows


def _gather_kernel(te_ref, base_ref, wlo_ref, whi_ref, rank_ref, x_ref, o_ref):
    o_ref[...] = jnp.zeros_like(o_ref)
    for u in range(GATHER_TILES):
        j = pl.program_id(0) * GATHER_TILES + u
        e, base = te_ref[j], base_ref[j]
        rows = slice(u * MOE_TILE, (u + 1) * MOE_TILE)

        def window(w, carry):
            p = jnp.where(_tile_onehot(rank_ref, w, e, base), 1.0, 0.0).astype(BF16)
            x_w = x_ref[pl.ds(pl.multiple_of(w * MOE_TILE, MOE_TILE), MOE_TILE), :]
            o_ref[rows, :] += _dot(p, x_w).astype(BF16)
            return carry

        lax.fori_loop(wlo_ref[j], whi_ref[j] + 1, window, 0)


def _moe_gather(meta, rank_t, h2, n_tiles):
    t, d = h2.shape
    nw = t // MOE_TILE
    return pl.pallas_call(
        _gather_kernel,
        grid_spec=pltpu.PrefetchScalarGridSpec(
            num_scalar_prefetch=4, grid=(n_tiles // GATHER_TILES,),
            in_specs=[
                pl.BlockSpec((nw, N_EXPERTS, MOE_TILE), lambda j, *_: (0, 0, 0)),
                pl.BlockSpec((t, d), lambda j, *_: (0, 0), pipeline_mode=pl.Buffered(1)),
            ],
            out_specs=pl.BlockSpec((GATHER_TILES * MOE_TILE, d), lambda j, *_: (j, 0)),
        ),
        out_shape=jax.ShapeDtypeStruct((n_tiles * MOE_TILE, d), BF16),
        compiler_params=_cparams(("arbitrary",)), name="moe_gather",
    )(*meta, rank_t, h2)


F8 = jnp.float8_e4m3fn
F8_TARGET = 224.0
F8_TINY = 1e-30


def _f8_quantize(x):
    m = jnp.max(jnp.max(jnp.abs(x), axis=1, keepdims=True), axis=0, keepdims=True)
    m = jnp.maximum(m, F8_TINY)
    return (x * (F8_TARGET / m)).astype(F8), m * (1.0 / F8_TARGET)


def _moe_up_kernel(te_ref, chg_ref, nused_ref, x_ref, wg_ref, wu_ref, o_ref, wg_s, wu_s, inv_s):
    j = pl.program_id(1)

    @pl.when(chg_ref[j] == 1)
    def _():
        for row, (w_ref, w_s) in enumerate(((wg_ref, wg_s), (wu_ref, wu_s))):
            w_s[...], inv = _f8_quantize(w_ref[...])
            inv_s[row:row + 1, :] = jnp.broadcast_to(inv, (1, LANES))

    @pl.when(j < nused_ref[0])
    def _():
        x8, inv_x = _f8_quantize(x_ref[...].astype(F32))
        a = _dot(x8, wg_s[...]) * (inv_x * inv_s[0:1, 0:1])
        u = _dot(x8, wu_s[...]) * (inv_x * inv_s[1:2, 0:1])
        o_ref[...] = (_silu(a) * u).astype(BF16)

    @pl.when(j >= nused_ref[0])
    def _():
        o_ref[...] = jnp.zeros_like(o_ref)


def _moe_up(te, chg, nused, xs, wg, wu):
    rows, d = xs.shape
    n_tiles = rows // EXPERT_TILE
    ff = wg.shape[2]
    return pl.pallas_call(
        _moe_up_kernel,
        grid_spec=pltpu.PrefetchScalarGridSpec(
            num_scalar_prefetch=3, grid=(ff // MOE_TF, n_tiles),
            in_specs=[
                pl.BlockSpec((EXPERT_TILE, d), lambda f, j, *_: (j, 0)),
                pl.BlockSpec((None, d, MOE_TF), lambda f, j, te, *_: (te[j], 0, f)),
                pl.BlockSpec((None, d, MOE_TF), lambda f, j, te, *_: (te[j], 0, f)),
            ],
            out_specs=pl.BlockSpec((EXPERT_TILE, MOE_TF), lambda f, j, *_: (j, f)),
            scratch_shapes=[pltpu.VMEM((d, MOE_TF), F8), pltpu.VMEM((d, MOE_TF), F8),
                            pltpu.VMEM((SUBLANES, LANES), F32)],
        ),
        out_shape=jax.ShapeDtypeStruct((rows, ff), BF16),
        compiler_params=_cparams(("arbitrary", "arbitrary")), name="moe_up",
    )(te, chg, nused, xs, wg, wu)


def _moe_down_kernel(te_ref, chg_ref, nused_ref, a_ref, wd_ref, o_ref, wd_s, inv_s):
    j = pl.program_id(0)

    @pl.when(chg_ref[j] == 1)
    def _():
        wd_s[...], inv = _f8_quantize(wd_ref[...])
        inv_s[0:1, :] = jnp.broadcast_to(inv, (1, LANES))

    @pl.when(j < nused_ref[0])
    def _():
        a8, inv_a = _f8_quantize(a_ref[...].astype(F32))
        o_ref[...] = (_dot(a8, wd_s[...]) * (inv_a * inv_s[0:1, 0:1])).astype(BF16)

    @pl.when(j >= nused_ref[0])
    def _():
        o_ref[...] = jnp.zeros_like(o_ref)


def _moe_down(te, chg, nused, act, wd):
    rows, ff = act.shape
    d = wd.shape[2]
    return pl.pallas_call(
        _moe_down_kernel,
        grid_spec=pltpu.PrefetchScalarGridSpec(
            num_scalar_prefetch=3, grid=(rows // EXPERT_TILE,),
            in_specs=[
                pl.BlockSpec((EXPERT_TILE, ff), lambda j, *_: (j, 0)),
                pl.BlockSpec((None, ff, d), lambda j, te, *_: (te[j], 0, 0)),
            ],
            out_specs=pl.BlockSpec((EXPERT_TILE, d), lambda j, *_: (j, 0)),
            scratch_shapes=[pltpu.VMEM((ff, d), F8), pltpu.VMEM((SUBLANES, LANES), F32)],
        ),
        out_shape=jax.ShapeDtypeStruct((rows, d), BF16),
        compiler_params=_cparams(("arbitrary",)), name="moe_down",
    )(te, chg, nused, act, wd)


def _combine_kernel(alpha, nw_ctx, blk_a_ref, blk_b_ref, rel_ref, two_ref, rank_ref, gate_ref, *refs):
    ya_refs, yb_refs = refs[:N_EXPERTS], refs[N_EXPERTS:2 * N_EXPERTS]
    x1_ref, g2_ref, lg_ref, lb_ref, oc_ref, ol_ref, acc_s = refs[2 * N_EXPERTS:]
    w = pl.program_id(0)
    row_id = lax.broadcasted_iota(jnp.int32, (MOE_TILE, MOE_TILE), 0).astype(F32)

    def weights(e, rel):
        hit = rank_ref[e:e + 1, :] == row_id + rel.astype(F32)
        return jnp.where(hit, gate_ref[e:e + 1, :], 0.0).T.astype(BF16)

    acc = None
    for e in range(N_EXPERTS):
        part = _dot(weights(e, rel_ref[w * N_EXPERTS + e]), ya_refs[e][...])
        acc = part if acc is None else acc + part
    acc_s[...] = acc
    for e in range(N_EXPERTS):
        @pl.when(two_ref[w * N_EXPERTS + e] == 1)
        def _():
            acc_s[...] += _dot(weights(e, rel_ref[w * N_EXPERTS + e] + MOE_TILE), yb_refs[e][...])

    out = _layer_norm(alpha * x1_ref[...] + g2_ref[...] * acc_s[...], lg_ref[...], lb_ref[...])

    @pl.when(w < nw_ctx)
    def _():
        oc_ref[...] = out

    @pl.when(w >= nw_ctx)
    def _():
        ol_ref[...] = out


def _moe_combine(bands, rank_t, gate_t, y, x1, mod_l, ln_g, ln_b, alpha, row_of_tile, t_ctx):
    t, d = x1.shape
    nw = t // MOE_TILE
    nw_ctx = t_ctx // MOE_TILE

    def y_spec(which, e):
        return pl.BlockSpec((MOE_TILE, d), lambda w, *pre: (pre[which][w * N_EXPERTS + e], 0))

    win_spec = pl.BlockSpec((None, N_EXPERTS, MOE_TILE), lambda w, *_: (w, 0, 0))
    return pl.pallas_call(
        functools.partial(_combine_kernel, alpha, nw_ctx),
        grid_spec=pltpu.PrefetchScalarGridSpec(
            num_scalar_prefetch=4, grid=(nw,),
            in_specs=[win_spec, win_spec]
            + [y_spec(0, e) for e in range(N_EXPERTS)] + [y_spec(1, e) for e in range(N_EXPERTS)]
            + [
                pl.BlockSpec((MOE_TILE, d), lambda w, *_: (w, 0)),
                pl.BlockSpec((None, None, 1, d), lambda w, *_: (row_of_tile(w), 5, 0, 0)),
                pl.BlockSpec((1, d), lambda w, *_: (0, 0)),
                pl.BlockSpec((1, d), lambda w, *_: (0, 0)),
            ],
            out_specs=[
                pl.BlockSpec((MOE_TILE, d), lambda w, *_: (jnp.minimum(w, nw_ctx - 1), 0)),
                pl.BlockSpec((MOE_TILE, d), lambda w, *_: (jnp.maximum(w - nw_ctx, 0), 0)),
            ],
            scratch_shapes=[pltpu.VMEM((MOE_TILE, d), F32)],
        ),
        out_shape=[jax.ShapeDtypeStruct((t_ctx, d), F32), jax.ShapeDtypeStruct((t - t_ctx, d), F32)],
        compiler_params=_cparams(("arbitrary",)), name="moe_combine",
    )(*bands, rank_t, gate_t, *([y] * (2 * N_EXPERTS)), x1, mod_l, ln_g, ln_b)


def _moe_layer(gates, h2, x1, mod_l, wg, wu, wd, ln_g, ln_b, alpha, row_of_tile, t_ctx):
    t = h2.shape[0]
    ratio = EXPERT_TILE // MOE_TILE
    n_big = -(-(2 * t + N_EXPERTS * (EXPERT_TILE - 1)) // EXPERT_TILE)
    n_tiles = n_big * ratio
    rank_t, gate_t, before = _route(gates)
    before = before[:, :, 0].astype(jnp.int32)
    counts = jnp.sum(gates > 0.0, axis=1).astype(jnp.int32)
    big_e = (counts + EXPERT_TILE - 1) // EXPERT_TILE
    big_end = jnp.cumsum(big_e)
    jb = jnp.arange(n_big, dtype=jnp.int32)
    te_big = jnp.minimum(jnp.sum(jb[:, None] >= big_end[None, :], axis=1), N_EXPERTS - 1).astype(jnp.int32)
    chg = jnp.concatenate([jnp.ones((1,), jnp.int32), (te_big[1:] != te_big[:-1]).astype(jnp.int32)])
    nused = big_end[-1].reshape(1).astype(jnp.int32)
    tiles_e = big_e * ratio
    tile_end = big_end * ratio
    n_used = tile_end[-1]
    j = jnp.arange(n_tiles, dtype=jnp.int32)
    te = jnp.minimum(jnp.sum(j[:, None] >= tile_end[None, :], axis=1), N_EXPERTS - 1).astype(jnp.int32)
    base = (j - jnp.take(tile_end - tiles_e, te)) * MOE_TILE
    last = jnp.minimum(base + MOE_TILE - 1, jnp.take(counts, te) - 1)
    before_t = jnp.take(before, te, axis=1)
    used = j < n_used
    wlo = jnp.where(used, jnp.sum(before_t <= base[None, :], axis=0) - 1, 1).astype(jnp.int32)
    whi = jnp.where(used, jnp.sum(before_t <= last[None, :], axis=0) - 1, 0).astype(jnp.int32)
    meta = (te, base.astype(jnp.int32), wlo, whi)
    xs = _moe_gather(meta, rank_t, h2, n_tiles)
    act = _moe_up(te_big, chg, nused, xs, wg, wu)
    y = _moe_down(te_big, chg, nused, act, wd)
    start_row = (tile_end - tiles_e) * MOE_TILE
    n_we = jnp.concatenate([before[1:], counts[None, :]], axis=0) - before
    first = start_row[None, :] + before
    blk_a = jnp.minimum(first // MOE_TILE, n_tiles - 1)
    two = ((n_we > 0) & ((first + n_we - 1) // MOE_TILE > blk_a)).astype(jnp.int32)
    rel = blk_a * MOE_TILE - start_row[None, :]
    bands = tuple(a.reshape(-1).astype(jnp.int32) for a in (blk_a, blk_a + two, rel, two))
    return _moe_combine(bands, rank_t, gate_t, y, x1, mod_l, ln_g, ln_b, alpha, row_of_tile, t_ctx)


def _rope_tables(seq):
    half = GLA_DK // 2
    n = half // 2
    t = np.arange(seq)
    inv = ROPE_BASE ** (-np.arange(n, dtype=np.float64) / n)
    cos = np.zeros((seq, GLA_DK), np.float64)
    sin = np.zeros((seq, GLA_DK), np.float64)
    for a, pos in enumerate((t // GRID_W, t % GRID_W)):
        ang = pos[:, None].astype(np.float64) * inv[None, :]
        base = a * half
        cos[:, base:base + n] = np.cos(ang)
        cos[:, base + n:base + half] = np.cos(ang)
        sin[:, base:base + n] = -np.sin(ang)
        sin[:, base + n:base + half] = np.sin(ang)
    tile = lambda m: jnp.asarray(np.tile(m, (1, GLA_HEADS)), F32)
    return tile(cos), tile(sin)


W_IN_GD = 1536
W_IN_KV = (2080, 2592)
W_IN_CHUNKS = (0, 512, 1024, 1568, 3104, 3616, 4128, 4640, 5152, 5664, 6176, 6688)


def _transpose_w_in(w):
    return jnp.swapaxes(w, 1, 2).astype(BF16)


def _gate_up_blocks(w_gup):
    qkw = w_gup.shape[-1]
    fwd = jnp.pad(w_gup[:, 0], ((0, 0), (0, 0), (0, qkw)))
    bwd = jnp.pad(w_gup[:, 1], ((0, 0), (0, 0), (qkw, 0)))
    return jnp.pad(jnp.concatenate([fwd, bwd], axis=1), ((0, 0), (0, LANES - 2 * GLA_RANK), (0, 0)))


def kernel(x_prompt, x_sample, cache_na_k, cache_na_v, state_gla, c, c_ctx, w_mod, b_mod, w_in, gla_w_gup, gla_b_g, gla_norm, w_br_gla, na_rpb, w_br_na, cv_w, cv_b, cv_ln_g, cv_ln_b, w_br_cv, w_out, ln_g, ln_b, ffd_w_gate, ffd_w_up, ffd_w_down, moe_w_router, moe_w_gate, moe_w_up, moe_w_down):
    nb_c, l_c, d = x_prompt.shape
    nb_l, l_l, _ = x_sample.shape
    depth = w_mod.shape[0]
    t_ctx, t_lat = nb_c * l_c, nb_l * l_l
    t_all = t_ctx + t_lat
    alpha = (2 * depth) ** 0.25
    tm = 512
    row_of_tile = _row_of_tile_fn(t_ctx, l_l, tm)

    x_pair = (x_prompt.reshape(t_ctx, d), x_sample.reshape(t_lat, d))
    cond8 = jnp.concatenate([c_ctx[None, :], c, jnp.zeros((8 - 1 - nb_l, d), F32)], axis=0)
    mod = _modulation(cond8, w_mod, b_mod).reshape(depth, 8, 6, 1, d)
    rope = _rope_tables(l_l)
    rows_lat = l_l // GRID_W
    kr = min(NA_WIN_R, rows_lat)
    s0_all = state_gla.reshape(nb_l, depth, 2, 2, LANES, GLA_DV)
    w_t = _transpose_w_in(w_in)
    wg_blks = _gate_up_blocks(gla_w_gup)
    toep = _na_bias_tables(na_rpb)
    w_br = [w.astype(BF16) for w in (w_br_gla, w_br_na, w_br_cv, w_out)]

    caches = None
    states = None
    for l in range(depth):
        mod_l = mod[l]
        gd, kc, vc, kl, vl, zb = _input_projection(x_pair, mod_l, w_t, l, depth, l_c, tm, row_of_tile, caches)
        caches = (kc, vc)

        wg_blk = wg_blks[l]
        bg = gla_b_g[l].reshape(1, CHUNK_W)
        gn = gla_norm[l].reshape(1, CHUNK_W)
        ya, states = _gla_branch(gd, zb, wg_blk, bg, gn, latent=False, seq=l_c, nb=nb_c, tok0=0, t_total=t_all,
                                 state_slot=(l, depth, states))
        ya = _gla_branch(gd, zb, wg_blk, bg, gn, latent=True, seq=l_l, nb=nb_l, tok0=t_ctx, t_total=t_all,
                         rope=rope, s0=s0_all[:, l], ya_prev=ya)

        yb = _attn_context(zb, kc, vc, l, nb_c, l_c, t_all)
        yb = _attn_latent(zb, kl, vl, cache_na_k, cache_na_v, l, toep, yb, nb_l, l_l, t_ctx)

        conv_args = (zb, cv_w[l], cv_b[l][None, :], cv_ln_g[l][None, :], cv_ln_b[l][None, :])
        yc = _conv_branch(*conv_args, seq=l_c, nb=nb_c, tok0=0, t_total=t_all)
        yc = _conv_branch(*conv_args, seq=l_l, nb=nb_l, tok0=t_ctx, t_total=t_all, prev=yc)

        moe = l % 2 == 1
        w_router = None
        if moe:
            w_router = moe_w_router[l // 2].T
        merged = _merge(ya, yb, yc, zb, x_pair, mod_l, w_br[0][l], w_br[1][l], w_br[2][l], w_br[3][l],
                        ln_g[l, 0][None, :], ln_b[l, 0][None, :], alpha, tm, row_of_tile, w_router)
        ln2 = (ln_g[l, 1][None, :], ln_b[l, 1][None, :])
        if moe:
            x1, h2, gates = merged
            x_pair = _moe_layer(gates, h2, x1, mod_l, moe_w_gate[l // 2], moe_w_up[l // 2], moe_w_down[l // 2],
                                *ln2, alpha, _row_of_tile_fn(t_ctx, l_l, MOE_TILE), t_ctx)
        else:
            x1, h2 = merged
            i = l // 2
            x_pair = _dense_mixer(h2, x1, mod_l, ffd_w_gate[i].astype(BF16), ffd_w_up[i].astype(BF16),
                                  ffd_w_down[i].astype(BF16), *ln2, alpha, tm, 1, row_of_tile, t_ctx)

    y_prompt = x_pair[0].reshape(nb_c, l_c, d)
    y_sample = x_pair[1].reshape(nb_l, l_l, d)
    cache_shape = (nb_c, depth, l_c, NA_HEADS, NA_DH)
    return (y_prompt, y_sample, caches[0].reshape(cache_shape), caches[1].reshape(cache_shape),
            states.reshape(nb_c, depth, 2, GLA_HEADS, GLA_DK, GLA_DV))


def _row_of_tile_fn(t_ctx, l_lat, tm):
    n_ctx = t_ctx // tm

    def row_of_tile(i):
        return jnp.where(i < n_ctx, 0, 1 + (i - n_ctx) // (l_lat // tm))

    return row_of_tile
```

```python
import functools

import numpy as np
import jax
import jax.numpy as jnp
from jax import lax
from jax.experimental import pallas as pl
from jax.experimental.pallas import tpu as pltpu

F32 = jnp.float32
BF16 = jnp.bfloat16

GRID_W = 64
GLA_HEADS = 4
GLA_DK = 64
GLA_DV = 128
GLA_RANK = 16
GLA_TAU = 16.0
GLA_CHUNK = 64
GLA_GROUP = 256
GLA_CTX_SEQS = 2
NA_HEADS = 8
NA_DH = 64
NA_WIN_R = 8
NA_WIN_C = 16
CONV_CH = 512
CONV_K = 31
N_EXPERTS = 8
ROPE_BASE = 10000.0
LN_EPS = 1e-5

D_MODEL = 1024
LANES = 128
SUBLANES = 8
CHUNK_W = 512
N_BF16_CHUNKS = 12
NEG_BIG = -1e30
VMEM_LIMIT = 56 * 1024 * 1024


def _cparams(sem, vmem=VMEM_LIMIT):
    return pltpu.CompilerParams(dimension_semantics=sem, vmem_limit_bytes=vmem)


def _dot(a, b):
    return jnp.dot(a, b, preferred_element_type=F32)


def _split(x):
    hi = x.astype(BF16)
    return hi, (x - hi.astype(F32)).astype(BF16)


def _dot_split(a, b):
    a_hi, a_lo = _split(a)
    b_hi, b_lo = _split(b)
    return _dot(a_hi, b_hi) + _dot(a_lo, b_hi) + _dot(a_hi, b_lo)


def _dot_split_rhs(a, b):
    b_hi, b_lo = _split(b)
    return _dot(a, b_hi) + _dot(a, b_lo)


def _dot_nt(a, b):
    return lax.dot_general(a, b, (((1,), (1,)), ((), ())), preferred_element_type=F32)


def _sigmoid(x):
    return 0.5 * jnp.tanh(0.5 * x) + 0.5


def _silu(x):
    return x * _sigmoid(x)


def _layer_norm(x, g, b):
    mu = jnp.mean(x, axis=-1, keepdims=True)
    xc = x - mu
    var = jnp.mean(xc * xc, axis=-1, keepdims=True)
    return xc * lax.rsqrt(var + LN_EPS) * g + b


def _mod_kernel(c_ref, w_ref, b_ref, o_ref):
    c = c_ref[...]
    s = _silu(c).astype(BF16)
    o_ref[...] = _dot(s, w_ref[...].astype(BF16)) + b_ref[...]


def _modulation(cond8, w_mod, b_mod):
    depth, d, n = w_mod.shape
    tn = 1024
    return pl.pallas_call(
        _mod_kernel,
        grid=(depth, n // tn),
        in_specs=[
            pl.BlockSpec((8, d), lambda l, j: (0, 0)),
            pl.BlockSpec((None, d, tn), lambda l, j: (l, 0, j)),
            pl.BlockSpec((None, 1, tn), lambda l, j: (l, 0, j)),
        ],
        out_specs=pl.BlockSpec((None, 8, tn), lambda l, j: (l, 0, j)),
        out_shape=jax.ShapeDtypeStruct((depth, 8, n), F32),
        compiler_params=_cparams(("arbitrary", "arbitrary")),
        name="modulation",
    )(cond8, w_mod, b_mod.reshape(depth, 1, n))


def _inproj_kernel(n_ctx_tiles, aliased, *refs):
    if aliased:
        (xc_ref, xl_ref, sc_ref, sh_ref, w_ref, _kc_in, _vc_in,
         gd_ref, kc_ref, vc_ref, kl_ref, vl_ref, zb_ref, h_ref) = refs
    else:
        (xc_ref, xl_ref, sc_ref, sh_ref, w_ref,
         gd_ref, kc_ref, vc_ref, kl_ref, vl_ref, zb_ref, h_ref) = refs
    is_ctx = pl.program_id(0) < n_ctx_tiles
    x = jnp.where(is_ctx, xc_ref[...], xl_ref[...])
    h_ref[...] = (x * (1.0 + sc_ref[...]) + sh_ref[...]).astype(BF16)

    def project(start, width):
        return _dot_nt(h_ref[...], w_ref[start:start + width, :])

    gd_ref[...] = project(W_IN_GD, LANES)
    k = project(W_IN_KV[0], CHUNK_W)
    v = project(W_IN_KV[1], CHUNK_W)
    seqs, l_c, _ = kc_ref.shape

    @pl.when(is_ctx)
    def _():
        for s in range(seqs):
            kc_ref[s] = k[s * l_c:(s + 1) * l_c]
            vc_ref[s] = v[s * l_c:(s + 1) * l_c]

    @pl.when(jnp.logical_not(is_ctx))
    def _():
        kl_ref[...] = k
        vl_ref[...] = v

    for j, start in enumerate(W_IN_CHUNKS):
        zb_ref[j] = project(start, CHUNK_W).astype(BF16)


def _mod_spec(piece, row_of_tile):
    return pl.BlockSpec((None, None, 1, D_MODEL), lambda i: (row_of_tile(i), piece, 0, 0))


def _pair_specs(tm, d, n_ctx_tiles):
    return [pl.BlockSpec((tm, d), lambda i, *_: (jnp.minimum(i, n_ctx_tiles - 1), 0)),
            pl.BlockSpec((tm, d), lambda i, *_: (jnp.maximum(i - n_ctx_tiles, 0), 0))]


def _input_projection(x_pair, mod_l, w_t, layer, depth, l_c, tm, row_of_tile, caches=None):
    xc, xl = x_pair
    d = xc.shape[1]
    t_ctx, t_lat = xc.shape[0], xl.shape[0]
    t = t_ctx + t_lat
    n_ctx_tiles = t_ctx // tm
    seqs = tm // l_c
    cache_spec = pl.BlockSpec((seqs, None, l_c, CHUNK_W),
                              lambda i: (jnp.minimum(i, n_ctx_tiles - 1), layer, 0, 0))
    lat_spec = pl.BlockSpec((tm, CHUNK_W), lambda i: (jnp.maximum(i - n_ctx_tiles, 0), 0))
    in_specs = _pair_specs(tm, d, n_ctx_tiles) + [
        _mod_spec(1, row_of_tile), _mod_spec(0, row_of_tile),
        pl.BlockSpec((None,) + w_t.shape[1:], lambda i: (layer, 0, 0), pipeline_mode=pl.Buffered(1)),
    ]
    args = [xc, xl, mod_l, mod_l, w_t]
    aliases = {}
    if caches is not None:
        in_specs += [pl.BlockSpec(memory_space=pl.ANY), pl.BlockSpec(memory_space=pl.ANY)]
        args += list(caches)
        aliases = {5: 1, 6: 2}
    cache_shape = jax.ShapeDtypeStruct((t_ctx // l_c, depth, l_c, CHUNK_W), F32)
    return pl.pallas_call(
        functools.partial(_inproj_kernel, n_ctx_tiles, caches is not None),
        grid=(t // tm,),
        in_specs=in_specs,
        out_specs=[
            pl.BlockSpec((tm, LANES), lambda i: (i, 0)),
            cache_spec, cache_spec, lat_spec, lat_spec,
            pl.BlockSpec((N_BF16_CHUNKS, tm, CHUNK_W), lambda i: (0, i, 0)),
        ],
        out_shape=[
            jax.ShapeDtypeStruct((t, LANES), F32),
            cache_shape, cache_shape,
            jax.ShapeDtypeStruct((t_lat, CHUNK_W), F32), jax.ShapeDtypeStruct((t_lat, CHUNK_W), F32),
            jax.ShapeDtypeStruct((N_BF16_CHUNKS, t, CHUNK_W), BF16),
        ],
        scratch_shapes=[pltpu.VMEM((tm, d), BF16)],
        input_output_aliases=aliases,
        compiler_params=_cparams(("arbitrary",)),
        name="input_projection",
    )(*args)


def _gla_kernel(latent, seq, n_seq, state_alias, *refs):
    if latent:
        (qk_ref, v_ref, ra_ref, gd_ref, wg_ref, bg_ref, gn_ref, cos_ref, sin_ref, s0_ref, _alias,
         ya_ref, q_s, k_s, g_s, acc_s, st_s) = refs
    elif state_alias:
        (qk_ref, v_ref, ra_ref, gd_ref, wg_ref, bg_ref, gn_ref, _alias,
         ya_ref, sfin_ref, q_s, k_s, g_s, acc_s, st_s) = refs
    else:
        (qk_ref, v_ref, ra_ref, gd_ref, wg_ref, bg_ref, gn_ref,
         ya_ref, sfin_ref, q_s, k_s, g_s, acc_s, st_s) = refs
    c = GLA_CHUNK
    n_chunks = n_seq * seq // c
    qkw = GLA_HEADS * GLA_DK

    q = qk_ref[:, 0:qkw].astype(F32) * (GLA_DK ** -0.5)
    k = qk_ref[:, qkw:2 * qkw].astype(F32)
    if latent:
        lane = lax.broadcasted_iota(jnp.int32, (1, qkw), 1)
        first_half = (lane % 32) < 16

        def rope(x):
            swapped = jnp.where(first_half, pltpu.roll(x, qkw - 16, 1), pltpu.roll(x, 16, 1))
            return x * cos_ref[...] + swapped * sin_ref[...]

        q, k = rope(q), rope(k)
    q_s[...] = q
    k_s[...] = k

    glog = _dot_split(gd_ref[...], wg_ref[...]) + bg_ref[...]
    g_s[...] = (jnp.minimum(glog, 0.0) - jnp.log(1.0 + jnp.exp(-jnp.abs(glog)))) * (1.0 / GLA_TAU)
    acc_s[...] = jnp.zeros_like(acc_s)
    if latent:
        st_s[0] = s0_ref[...]
    else:
        st_s[...] = jnp.zeros_like(st_s)

    grp = GLA_GROUP
    cpg = grp // c
    n_groups = seq // grp
    grow = lax.broadcasted_iota(jnp.int32, (grp, grp), 0)
    gcol = lax.broadcasted_iota(jnp.int32, (grp, grp), 1)
    same_chunk = (grow // c) == (gcol // c)
    keep = (same_chunk & (gcol <= grow), same_chunk & (gcol >= grow))
    tri = (keep[0].astype(BF16), keep[1].astype(BF16))
    lane128 = lax.broadcasted_iota(jnp.int32, (1, LANES), 1)
    head_mask = (lane128 < GLA_DK, lane128 >= GLA_DK)
    urow = lax.broadcasted_iota(jnp.int32, (cpg * LANES, GLA_DV), 0)
    srow = lax.broadcasted_iota(jnp.int32, (LANES, GLA_DV), 0)
    chunk_of_col = lax.broadcasted_iota(jnp.int32, (LANES, grp), 1) // c
    chunk_of_row = lax.broadcasted_iota(jnp.int32, (grp, LANES), 0) // c
    pad_rows = jnp.zeros((SUBLANES - cpg, qkw), F32)

    def group_step(i, carry):
        for sq, d in ((sq, d) for sq in range(n_seq) for d in range(2)):
            gi = i if d == 0 else n_groups - 1 - i
            rows = pl.ds(pl.multiple_of(sq * seq + gi * grp, grp), grp)
            cum = _dot_split_rhs(tri[d], g_s[rows, d * qkw:(d + 1) * qkw])
            edge = c - 1 if d == 0 else 0
            lasts = [cum[ci * c + edge:ci * c + edge + 1, :] for ci in range(cpg)]
            last = jnp.concatenate([jnp.broadcast_to(r, (c, qkw)) for r in lasts], axis=0)
            ref = 0.5 * last
            q = q_s[rows, :]
            k = k_s[rows, :]
            qe = q * jnp.exp(cum - ref)
            ke = k * jnp.exp(ref - cum)
            qe2 = q * jnp.exp(cum)
            ke2 = k * jnp.exp(last - cum)
            last8 = jnp.concatenate(lasts + [pad_rows], axis=0)
            for p in range(2):
                sl = slice(p * LANES, (p + 1) * LANES)
                ke_p = ke[:, sl].astype(BF16)
                ke2_t = ke2[:, sl].T
                ke2_blk = jnp.concatenate(
                    [jnp.where(chunk_of_col == ci, ke2_t, 0.0) for ci in range(cpg)], axis=0).astype(BF16)
                dec = jnp.exp(last8[:, sl]).T
                psl = slice(2 * p * GLA_DV, (2 * p + 2) * GLA_DV)
                v_pair = v_ref[rows, psl]
                u = _dot(ke2_blk, v_pair)
                upd = jnp.where(urow % LANES < GLA_DK, u[:, :GLA_DV], u[:, GLA_DV:])
                s = st_s[sq, d, p]
                s_in = [None] * cpg
                for ci in (range(cpg) if d == 0 else range(cpg - 1, -1, -1)):
                    s_in[ci] = jnp.concatenate(
                        [jnp.where(srow < GLA_DK, s, 0.0), jnp.where(srow >= GLA_DK, s, 0.0)], axis=1).astype(BF16)
                    s = dec[:, ci:ci + 1] * s + upd[ci * LANES:(ci + 1) * LANES]
                st_s[sq, d, p] = s
                s_stack = jnp.concatenate(s_in, axis=0)
                a2_blk = jnp.concatenate(
                    [jnp.where(chunk_of_row == ci, qe2[:, sl], 0.0) for ci in range(cpg)], axis=1).astype(BF16)
                intra = []
                for hh in range(2):
                    a = jnp.where(head_mask[hh], qe[:, sl], 0.0).astype(BF16)
                    att = jnp.where(keep[d], _dot_nt(a, ke_p), 0.0).astype(BF16)
                    intra.append(_dot(att, v_pair[:, hh * GLA_DV:(hh + 1) * GLA_DV]))
                acc_s[rows, psl] += jnp.concatenate(intra, axis=1) + _dot(a2_blk, s_stack)
        return carry

    lax.fori_loop(0, n_groups, group_step, 0)

    def epilogue(i, carry):
        rows = pl.ds(pl.multiple_of(i * c, c), c)
        for h in range(GLA_HEADS):
            vsl = slice(h * GLA_DV, (h + 1) * GLA_DV)
            o = acc_s[rows, vsl]
            o = o * lax.rsqrt(jnp.mean(o * o, axis=-1, keepdims=True) + LN_EPS) * gn_ref[:, vsl]
            ya_ref[rows, vsl] = (o * _silu(ra_ref[rows, vsl].astype(F32))).astype(BF16)
        return carry

    lax.fori_loop(0, n_chunks, epilogue, 0)
    if not latent:
        sfin_ref[...] = st_s[...]


def _gla_branch(gd, zb, wg_blk, bg, gn, *, latent, seq, nb, tok0, t_total, rope=None, s0=None, ya_prev=None,
                state_slot=None):
    n_seq = 1 if latent else GLA_CTX_SEQS
    rows = n_seq * seq
    b0 = tok0 // rows
    qkw = GLA_HEADS * GLA_DK
    in_specs = [
        pl.BlockSpec((None, rows, CHUNK_W), lambda b: (0, b0 + b, 0)),
        pl.BlockSpec((None, rows, CHUNK_W), lambda b: (1, b0 + b, 0)),
        pl.BlockSpec((None, rows, CHUNK_W), lambda b: (2, b0 + b, 0)),
        pl.BlockSpec((rows, LANES), lambda b: (b0 + b, 0)),
        pl.BlockSpec((LANES, CHUNK_W), lambda b: (0, 0)),
        pl.BlockSpec((1, CHUNK_W), lambda b: (0, 0)),
        pl.BlockSpec((1, CHUNK_W), lambda b: (0, 0)),
    ]
    args = [zb, zb, zb, gd, wg_blk, bg, gn]
    scratch = [
        pltpu.VMEM((rows, qkw), F32), pltpu.VMEM((rows, qkw), F32), pltpu.VMEM((rows, CHUNK_W), F32),
        pltpu.VMEM((rows, CHUNK_W), F32), pltpu.VMEM((n_seq, 2, 2, LANES, GLA_DV), F32),
    ]
    ya_shape = jax.ShapeDtypeStruct((t_total, CHUNK_W), BF16)
    ya_spec = pl.BlockSpec((rows, CHUNK_W), lambda b: (b0 + b, 0))
    if latent:
        cos_t, sin_t = rope
        in_specs += [
            pl.BlockSpec((seq, qkw), lambda b: (0, 0)),
            pl.BlockSpec((seq, qkw), lambda b: (0, 0)),
            pl.BlockSpec((None, 2, 2, LANES, GLA_DV), lambda b: (b, 0, 0, 0, 0)),
            pl.BlockSpec(memory_space=pl.ANY),
        ]
        args += [cos_t, sin_t, s0, ya_prev]
        return pl.pallas_call(
            functools.partial(_gla_kernel, True, seq, n_seq, False),
            grid=(nb,), in_specs=in_specs, out_specs=ya_spec, out_shape=ya_shape,
            scratch_shapes=scratch, input_output_aliases={len(args) - 1: 0},
            compiler_params=_cparams(("arbitrary",)), name="gla_latent",
        )(*args)
    layer, depth, states = state_slot
    aliases = {}
    if states is not None:
        in_specs.append(pl.BlockSpec(memory_space=pl.ANY))
        args.append(states)
        aliases = {len(args) - 1: 1}
    return pl.pallas_call(
        functools.partial(_gla_kernel, False, seq, n_seq, states is not None),
        grid=(nb // n_seq,), in_specs=in_specs,
        out_specs=[ya_spec, pl.BlockSpec((n_seq, None, 2, 2, LANES, GLA_DV), lambda b: (b, layer, 0, 0, 0, 0))],
        out_shape=[ya_shape, jax.ShapeDtypeStruct((nb, depth, 2, 2, LANES, GLA_DV), F32)],
        scratch_shapes=scratch, input_output_aliases=aliases,
        compiler_params=_cparams(("arbitrary",)), name="gla_context",
    )(*args)


ATTN_CTX_SEQS = 4


def _attn_ctx_kernel(q_ref, k_ref, v_ref, o_ref):
    lane = lax.broadcasted_iota(jnp.int32, (1, LANES), 1)
    masks = (lane < NA_DH, lane >= NA_DH)
    scale = NA_DH ** -0.5
    n_seq, seq, _ = k_ref.shape
    for b in range(n_seq):
        rows = slice(b * seq, (b + 1) * seq)
        for p in range(NA_HEADS // 2):
            sl = slice(p * LANES, (p + 1) * LANES)
            q2 = _stack_heads(q_ref[rows, sl], masks)
            kp = k_ref[b, :, sl].astype(BF16)
            vp = v_ref[b, :, sl].astype(BF16)
            s = _dot_nt(q2, kp) * scale
            e = jnp.exp(s - jnp.max(s, axis=-1, keepdims=True))
            o2 = _dot(e.astype(BF16), vp) / jnp.sum(e, axis=-1, keepdims=True)
            o_ref[rows, sl] = jnp.where(masks[0], o2[0:seq], o2[seq:2 * seq]).astype(BF16)


def _stack_heads(qp, masks):
    zero = jnp.zeros_like(qp)
    return jnp.concatenate([jnp.where(masks[0], qp, zero), jnp.where(masks[1], qp, zero)], axis=0)


def _attn_context(zb, kc, vc, layer, nb, seq, t_total):
    per = ATTN_CTX_SEQS
    return pl.pallas_call(
        _attn_ctx_kernel,
        grid=(nb // per,),
        in_specs=[
            pl.BlockSpec((None, per * seq, CHUNK_W), lambda b: (3, b, 0)),
            pl.BlockSpec((per, None, seq, CHUNK_W), lambda b: (b, layer, 0, 0)),
            pl.BlockSpec((per, None, seq, CHUNK_W), lambda b: (b, layer, 0, 0)),
        ],
        out_specs=pl.BlockSpec((per * seq, CHUNK_W), lambda b: (b, 0)),
        out_shape=jax.ShapeDtypeStruct((t_total, CHUNK_W), BF16),
        compiler_params=_cparams(("arbitrary",)), name="attention_context",
    )(zb, kc, vc)


NA_ROWS_PER_STEP = 4


def _na_kernel(rows_total, kr, q_ref, kl_ref, vl_ref, kc_ref, vc_ref, toep_ref, _alias, o_ref):
    lane = lax.broadcasted_iota(jnp.int32, (1, LANES), 1)
    masks = (lane < NA_DH, lane >= NA_DH)
    scale = NA_DH ** -0.5
    nq = GRID_W
    for rr in range(NA_ROWS_PER_STEP):
        r = pl.program_id(1) * NA_ROWS_PER_STEP + rr
        qrows = slice(rr * nq, (rr + 1) * nq)
        rs = jnp.clip(r - kr // 2, 0, rows_total - kr)
        krows = pl.ds(pl.multiple_of(rs * GRID_W, GRID_W), kr * GRID_W)
        dr0 = rs - r + NA_WIN_R - 1
        for p in range(NA_HEADS // 2):
            sl = slice(p * LANES, (p + 1) * LANES)
            q2 = _stack_heads(q_ref[qrows, sl], masks)
            klp = kl_ref[krows, sl].astype(BF16)
            vlp = vl_ref[krows, sl].astype(BF16)
            kcp = kc_ref[:, sl].astype(BF16)
            vcp = vc_ref[:, sl].astype(BF16)
            bias = jnp.concatenate(
                [jnp.concatenate([toep_ref[2 * p + hh, dr0 + 2 * m] for m in range(kr // 2)], axis=1)
                 for hh in range(2)], axis=0)
            s_loc = _dot_nt(q2, klp) * scale + bias
            s_ctx = _dot_nt(q2, kcp) * scale
            m = jnp.maximum(jnp.max(s_loc, axis=-1, keepdims=True), jnp.max(s_ctx, axis=-1, keepdims=True))
            e_loc = jnp.exp(s_loc - m)
            e_ctx = jnp.exp(s_ctx - m)
            den = jnp.sum(e_loc, axis=-1, keepdims=True) + jnp.sum(e_ctx, axis=-1, keepdims=True)
            o2 = (_dot(e_loc.astype(BF16), vlp) + _dot(e_ctx.astype(BF16), vcp)) / den
            o_ref[qrows, sl] = jnp.where(masks[0], o2[0:nq], o2[nq:2 * nq]).astype(BF16)


def _na_bias_tables(rpb):
    nc = rpb.shape[-1]
    qc = np.arange(GRID_W)[:, None]
    kc = np.arange(GRID_W)[None, :]
    cs = np.clip(qc - NA_WIN_C // 2, 0, GRID_W - NA_WIN_C)
    valid = (kc >= cs) & (kc < cs + NA_WIN_C)
    tap = np.clip(kc - qc, -(NA_WIN_C - 1), NA_WIN_C - 1) + NA_WIN_C - 1
    pick = jnp.asarray(tap[..., None] == np.arange(nc), F32)
    toep = jnp.einsum('dhab,qkb->dhaqk', rpb.astype(F32), pick, precision=lax.Precision.HIGHEST)
    toep = jnp.where(valid, toep, NEG_BIG)
    return jnp.concatenate([toep[:, :, :-1], toep[:, :, 1:]], axis=-1)


def _attn_latent(zb, kl, vl, cache_k, cache_v, layer, toep, att_prev, nb, seq, tok0):
    rows_total = seq // GRID_W
    kr = min(NA_WIN_R, rows_total)
    nq = NA_ROWS_PER_STEP * GRID_W
    steps = rows_total // NA_ROWS_PER_STEP
    q0 = tok0 // nq
    past = cache_k.shape[2]
    ck = cache_k.reshape(cache_k.shape[0], cache_k.shape[1], past, NA_HEADS * NA_DH)
    cv = cache_v.reshape(ck.shape)
    return pl.pallas_call(
        functools.partial(_na_kernel, rows_total, kr),
        grid=(nb, steps),
        in_specs=[
            pl.BlockSpec((None, nq, CHUNK_W), lambda b, r: (3, q0 + b * steps + r, 0)),
            pl.BlockSpec((seq, CHUNK_W), lambda b, r: (b, 0)),
            pl.BlockSpec((seq, CHUNK_W), lambda b, r: (b, 0)),
            pl.BlockSpec((None, None, past, CHUNK_W), lambda b, r: (b, layer, 0, 0)),
            pl.BlockSpec((None, None, past, CHUNK_W), lambda b, r: (b, layer, 0, 0)),
            pl.BlockSpec((None,) + toep.shape[1:], lambda b, r: (layer, 0, 0, 0, 0), pipeline_mode=pl.Buffered(1)),
            pl.BlockSpec(memory_space=pl.ANY),
        ],
        out_specs=pl.BlockSpec((nq, CHUNK_W), lambda b, r: (q0 + b * steps + r, 0)),
        out_shape=jax.ShapeDtypeStruct(att_prev.shape, BF16),
        input_output_aliases={6: 0},
        compiler_params=_cparams(("arbitrary", "arbitrary")), name="attention_latent",
    )(zb, kl, vl, ck, cv, toep, att_prev)


CONV_PAD = 16
CONV_ROWS = 64


def _conv_kernel(seq, aliased, *refs):
    if aliased:
        a_ref, gt_ref, w_ref, cb_ref, lg_ref, lb_ref, _alias, o_ref, z_s, sh_s = refs
    else:
        a_ref, gt_ref, w_ref, cb_ref, lg_ref, lb_ref, o_ref, z_s, sh_s = refs
    z_s[0:CONV_PAD, :] = jnp.zeros((CONV_PAD, CONV_CH), F32)
    z_s[CONV_PAD + seq:2 * CONV_PAD + seq, :] = jnp.zeros((CONV_PAD, CONV_CH), F32)
    z_s[CONV_PAD:CONV_PAD + seq, :] = a_ref[...].astype(F32) * _sigmoid(gt_ref[...].astype(F32))
    ext = seq + 2 * CONV_PAD - SUBLANES
    for b in range(1, SUBLANES):
        for r0 in range(0, ext, CONV_ROWS):
            n = min(CONV_ROWS, ext - r0)
            sh_s[b - 1, r0:r0 + n, :] = z_s[r0 + b:r0 + b + n, :]
    off = CONV_PAD - CONV_K // 2
    grp = (CONV_ROWS // SUBLANES, SUBLANES, CONV_CH)
    for t0 in range(0, seq, CONV_ROWS):
        acc = jnp.zeros(grp, F32)
        for k in range(CONV_K):
            a, b = divmod(off + k, SUBLANES)
            r0 = t0 + a * SUBLANES
            tap = z_s[r0:r0 + CONV_ROWS, :] if b == 0 else sh_s[b - 1, r0:r0 + CONV_ROWS, :]
            acc = acc + tap.reshape(grp) * w_ref[k][None]
        y = _layer_norm(acc.reshape(CONV_ROWS, CONV_CH) + cb_ref[...], lg_ref[...], lb_ref[...])
        o_ref[t0:t0 + CONV_ROWS, :] = _silu(y).astype(BF16)


def _conv_branch(zb, cv_w, cv_b, ln_g, ln_b, *, seq, nb, tok0, t_total, prev=None):
    b0 = tok0 // seq
    in_specs = [
        pl.BlockSpec((None, seq, CHUNK_W), lambda b: (4, b0 + b, 0)),
        pl.BlockSpec((None, seq, CHUNK_W), lambda b: (5, b0 + b, 0)),
        pl.BlockSpec((CONV_K, SUBLANES, CONV_CH), lambda b: (0, 0, 0)),
        pl.BlockSpec((1, CONV_CH), lambda b: (0, 0)),
        pl.BlockSpec((1, CONV_CH), lambda b: (0, 0)),
        pl.BlockSpec((1, CONV_CH), lambda b: (0, 0)),
    ]
    args = [zb, zb, jnp.broadcast_to(cv_w[:, None, :], (CONV_K, SUBLANES, CONV_CH)), cv_b, ln_g, ln_b]
    aliases = {}
    if prev is not None:
        in_specs.append(pl.BlockSpec(memory_space=pl.ANY))
        args.append(prev)
        aliases = {6: 0}
    return pl.pallas_call(
        functools.partial(_conv_kernel, seq, prev is not None),
        grid=(nb,), in_specs=in_specs,
        out_specs=pl.BlockSpec((seq, CONV_CH), lambda b: (b0 + b, 0)),
        out_shape=jax.ShapeDtypeStruct((t_total, CONV_CH), BF16),
        scratch_shapes=[pltpu.VMEM((seq + 2 * CONV_PAD, CONV_CH), F32),
                        pltpu.VMEM((SUBLANES - 1, seq + 2 * CONV_PAD - SUBLANES, CONV_CH), F32)],
        input_output_aliases=aliases,
        compiler_params=_cparams(("arbitrary",)), name="conv_module_%d" % seq,
    )(*args)


def _merge_kernel(alpha, n_ctx_tiles, with_router, *refs):
    if with_router:
        (ya_ref, yb_ref, yc_ref, mg_ref, xc_ref, xl_ref, g1_ref, sc2_ref, sh2_ref, wa_ref, wb_ref, wc_ref, wo_ref,
         lg_ref, lb_ref, wr_ref, x1_ref, h2_ref, gates_ref) = refs
    else:
        (ya_ref, yb_ref, yc_ref, mg_ref, xc_ref, xl_ref, g1_ref, sc2_ref, sh2_ref, wa_ref, wb_ref, wc_ref, wo_ref,
         lg_ref, lb_ref, x1_ref, h2_ref) = refs
    x = jnp.where(pl.program_id(0) < n_ctx_tiles, xc_ref[...], xl_ref[...])
    halves = []
    for n in range(2):
        m = None
        for j, (y_ref, w_ref) in enumerate(((ya_ref, wa_ref), (yb_ref, wb_ref), (yc_ref, wc_ref))):
            proj = _dot(y_ref[...], w_ref[:, n * CHUNK_W:(n + 1) * CHUNK_W])
            term = _sigmoid(mg_ref[2 * j + n].astype(F32)) * proj
            m = term if m is None else m + term
        halves.append(m.astype(BF16))
    mix = _dot(halves[0], wo_ref[0:CHUNK_W, :]) + _dot(halves[1], wo_ref[CHUNK_W:2 * CHUNK_W, :])
    x1 = _layer_norm(alpha * x + g1_ref[...] * mix, lg_ref[...], lb_ref[...])
    x1_ref[...] = x1
    h2 = x1 * (1.0 + sc2_ref[...]) + sh2_ref[...]
    h2_ref[...] = h2.astype(BF16)
    if with_router:
        w_hi, w_lo = _split(wr_ref[...])
        h_hi, h_lo = _split(h2)
        lg = _dot_nt(w_hi, h_hi) + _dot_nt(w_hi, h_lo) + _dot_nt(w_lo, h_hi)
        eid = lax.broadcasted_iota(jnp.int32, lg.shape, 0)
        m1 = jnp.max(lg, axis=0, keepdims=True)
        i1 = jnp.min(jnp.where(lg == m1, eid, N_EXPERTS), axis=0, keepdims=True)
        lg2 = jnp.where(eid == i1, -jnp.inf, lg)
        m2 = jnp.max(lg2, axis=0, keepdims=True)
        i2 = jnp.min(jnp.where(lg2 == m2, eid, N_EXPERTS), axis=0, keepdims=True)
        e2 = jnp.exp(m2 - m1)
        w1 = 1.0 / (1.0 + e2)
        gates_ref[...] = jnp.where(eid == i1, w1, 0.0) + jnp.where(eid == i2, e2 * w1, 0.0)


def _merge(ya, yb, yc, zb, x_pair, mod_l, wa, wb, wc, wo, ln_g, ln_b, alpha, tm, row_of_tile, w_router=None):
    xc, xl = x_pair
    d = xc.shape[1]
    t = xc.shape[0] + xl.shape[0]
    n_ctx_tiles = xc.shape[0] // tm
    full = lambda shape: pl.BlockSpec(shape, lambda i: tuple(0 for _ in shape), pipeline_mode=pl.Buffered(1))
    in_specs = [
        pl.BlockSpec((tm, CHUNK_W), lambda i: (i, 0)),
        pl.BlockSpec((tm, CHUNK_W), lambda i: (i, 0)),
        pl.BlockSpec((tm, CHUNK_W), lambda i: (i, 0)),
        pl.BlockSpec((6, tm, CHUNK_W), lambda i: (1, i, 0)),
    ] + _pair_specs(tm, d, n_ctx_tiles) + [
        _mod_spec(2, row_of_tile), _mod_spec(4, row_of_tile), _mod_spec(3, row_of_tile),
        full(wa.shape), full(wb.shape), full(wc.shape), full(wo.shape), full((1, d)), full((1, d)),
    ]
    args = [ya, yb, yc, zb, xc, xl, mod_l, mod_l, mod_l, wa, wb, wc, wo, ln_g, ln_b]
    out_specs = [pl.BlockSpec((tm, d), lambda i: (i, 0)), pl.BlockSpec((tm, d), lambda i: (i, 0))]
    out_shape = [jax.ShapeDtypeStruct((t, d), F32), jax.ShapeDtypeStruct((t, d), BF16)]
    if w_router is not None:
        in_specs.append(full(w_router.shape))
        args.append(w_router)
        out_specs.append(pl.BlockSpec((N_EXPERTS, tm), lambda i: (0, i)))
        out_shape.append(jax.ShapeDtypeStruct((N_EXPERTS, t), F32))
    return pl.pallas_call(
        functools.partial(_merge_kernel, alpha, n_ctx_tiles, w_router is not None),
        grid=(t // tm,), in_specs=in_specs, out_specs=out_specs, out_shape=out_shape,
        compiler_params=_cparams(("arbitrary",)), name="merge",
    )(*args)


def _ffn_kernel(alpha, n_ctx_tiles, n_split, h_ref, x1_ref, g2_ref, wg_ref, wu_ref, wd_ref, lg_ref, lb_ref,
                oc_ref, ol_ref):
    h = h_ref[...]
    tf = wg_ref.shape[1] // n_split
    y = None
    for f in range(n_split):
        cols = slice(f * tf, (f + 1) * tf)
        act = _silu(_dot(h, wg_ref[:, cols])) * _dot(h, wu_ref[:, cols])
        part = _dot(act.astype(BF16), wd_ref[cols, :])
        y = part if y is None else y + part
    out = _layer_norm(alpha * x1_ref[...] + g2_ref[...] * y, lg_ref[...], lb_ref[...])
    is_ctx = pl.program_id(0) < n_ctx_tiles

    @pl.when(is_ctx)
    def _():
        oc_ref[...] = out

    @pl.when(jnp.logical_not(is_ctx))
    def _():
        ol_ref[...] = out


def _dense_mixer(h2, x1, mod_l, wg, wu, wd, ln_g, ln_b, alpha, tm, n_split, row_of_tile, t_ctx):
    t, d = x1.shape
    n_ctx_tiles = t_ctx // tm
    const = lambda shape: pl.BlockSpec(shape, lambda i: tuple(0 for _ in shape), pipeline_mode=pl.Buffered(1))
    return pl.pallas_call(
        functools.partial(_ffn_kernel, alpha, n_ctx_tiles, n_split),
        grid=(t // tm,),
        in_specs=[
            pl.BlockSpec((tm, d), lambda i: (i, 0)),
            pl.BlockSpec((tm, d), lambda i: (i, 0)),
            _mod_spec(5, row_of_tile),
            const(wg.shape), const(wu.shape), const(wd.shape), const((1, d)), const((1, d)),
        ],
        out_specs=_pair_specs(tm, d, n_ctx_tiles),
        out_shape=[jax.ShapeDtypeStruct((t_ctx, d), F32), jax.ShapeDtypeStruct((t - t_ctx, d), F32)],
        compiler_params=_cparams(("arbitrary",)), name="dense_mixer",
    )(h2, x1, mod_l, wg, wu, wd, ln_g, ln_b)


MOE_TILE = 256
EXPERT_TILE = 512
MOE_TF = 1792
GATHER_TILES = 2


def _route_kernel(g_ref, rank_ref, gate_ref, before_ref, run_s):
    w = pl.program_id(0)

    @pl.when(w == 0)
    def _():
        run_s[...] = jnp.zeros_like(run_s)

    g = g_ref[...]
    sel = g > 0.0
    row = lax.broadcasted_iota(jnp.int32, (MOE_TILE, MOE_TILE), 0)
    col = lax.broadcasted_iota(jnp.int32, (MOE_TILE, MOE_TILE), 1)
    earlier = (row < col).astype(BF16)
    ones = jnp.where(sel, 1.0, 0.0)
    rank = _dot(ones.astype(BF16), earlier) + run_s[:, 0:1]
    rank_ref[...] = jnp.where(sel, rank, -1.0)
    gate_ref[...] = g
    before_ref[...] = run_s[...]
    run_s[...] += jnp.sum(ones, axis=1, keepdims=True)


def _route(gates_t):
    t = gates_t.shape[1]
    nw = t // MOE_TILE
    return pl.pallas_call(
        _route_kernel,
        grid=(nw,),
        in_specs=[pl.BlockSpec((N_EXPERTS, MOE_TILE), lambda w: (0, w))],
        out_specs=[
            pl.BlockSpec((None, N_EXPERTS, MOE_TILE), lambda w: (w, 0, 0)),
            pl.BlockSpec((None, N_EXPERTS, MOE_TILE), lambda w: (w, 0, 0)),
            pl.BlockSpec((None, N_EXPERTS, LANES), lambda w: (w, 0, 0)),
        ],
        out_shape=[
            jax.ShapeDtypeStruct((nw, N_EXPERTS, MOE_TILE), F32),
            jax.ShapeDtypeStruct((nw, N_EXPERTS, MOE_TILE), F32),
            jax.ShapeDtypeStruct((nw, N_EXPERTS, LANES), F32),
        ],
        scratch_shapes=[pltpu.VMEM((N_EXPERTS, LANES), F32)],
        compiler_params=_cparams(("arbitrary",)), name="moe_route",
    )(gates_t)


def _tile_onehot(rank_ref, w, e, base):
    rank_row = rank_ref[w, pl.ds(e, 1), :]
    rows = lax.broadcasted_iota(jnp.int32, (MOE_TILE, MOE_TILE), 0).astype(F32) + base.astype(F32)
    return rank_row == rows


def _gather_kernel(te_ref, base_ref, wlo_ref, whi_ref, rank_ref, x_ref, o_ref):
    o_ref[...] = jnp.zeros_like(o_ref)
    for u in range(GATHER_TILES):
        j = pl.program_id(0) * GATHER_TILES + u
        e, base = te_ref[j], base_ref[j]
        rows = slice(u * MOE_TILE, (u + 1) * MOE_TILE)

        def window(w, carry):
            p = jnp.where(_tile_onehot(rank_ref, w, e, base), 1.0, 0.0).astype(BF16)
            x_w = x_ref[pl.ds(pl.multiple_of(w * MOE_TILE, MOE_TILE), MOE_TILE), :]
            o_ref[rows, :] += _dot(p, x_w).astype(BF16)
            return carry

        lax.fori_loop(wlo_ref[j], whi_ref[j] + 1, window, 0)


def _moe_gather(meta, rank_t, h2, n_tiles):
    t, d = h2.shape
    nw = t // MOE_TILE
    return pl.pallas_call(
        _gather_kernel,
        grid_spec=pltpu.PrefetchScalarGridSpec(
            num_scalar_prefetch=4, grid=(n_tiles // GATHER_TILES,),
            in_specs=[
                pl.BlockSpec((nw, N_EXPERTS, MOE_TILE), lambda j, *_: (0, 0, 0)),
                pl.BlockSpec((t, d), lambda j, *_: (0, 0), pipeline_mode=pl.Buffered(1)),
            ],
            out_specs=pl.BlockSpec((GATHER_TILES * MOE_TILE, d), lambda j, *_: (j, 0)),
        ),
        out_shape=jax.ShapeDtypeStruct((n_tiles * MOE_TILE, d), BF16),
        compiler_params=_cparams(("arbitrary",)), name="moe_gather",
    )(*meta, rank_t, h2)


F8 = jnp.float8_e4m3fn
F8_TARGET = 224.0
F8_TINY = 1e-30


def _f8_quantize(x):
    m = jnp.max(jnp.max(jnp.abs(x), axis=1, keepdims=True), axis=0, keepdims=True)
    m = jnp.maximum(m, F8_TINY)
    return (x * (F8_TARGET / m)).astype(F8), m * (1.0 / F8_TARGET)


def _moe_up_kernel(te_ref, chg_ref, nused_ref, x_ref, wg_ref, wu_ref, o_ref, wg_s, wu_s, inv_s):
    j = pl.program_id(1)

    @pl.when(chg_ref[j] == 1)
    def _():
        for row, (w_ref, w_s) in enumerate(((wg_ref, wg_s), (wu_ref, wu_s))):
            w_s[...], inv = _f8_quantize(w_ref[...])
            inv_s[row:row + 1, :] = jnp.broadcast_to(inv, (1, LANES))

    @pl.when(j < nused_ref[0])
    def _():
        x8, inv_x = _f8_quantize(x_ref[...].astype(F32))
        a = _dot(x8, wg_s[...]) * (inv_x * inv_s[0:1, 0:1])
        u = _dot(x8, wu_s[...]) * (inv_x * inv_s[1:2, 0:1])
        o_ref[...] = (_silu(a) * u).astype(BF16)

    @pl.when(j >= nused_ref[0])
    def _():
        o_ref[...] = jnp.zeros_like(o_ref)


def _moe_up(te, chg, nused, xs, wg, wu):
    rows, d = xs.shape
    n_tiles = rows // EXPERT_TILE
    ff = wg.shape[2]
    return pl.pallas_call(
        _moe_up_kernel,
        grid_spec=pltpu.PrefetchScalarGridSpec(
            num_scalar_prefetch=3, grid=(ff // MOE_TF, n_tiles),
            in_specs=[
                pl.BlockSpec((EXPERT_TILE, d), lambda f, j, *_: (j, 0)),
                pl.BlockSpec((None, d, MOE_TF), lambda f, j, te, *_: (te[j], 0, f)),
                pl.BlockSpec((None, d, MOE_TF), lambda f, j, te, *_: (te[j], 0, f)),
            ],
            out_specs=pl.BlockSpec((EXPERT_TILE, MOE_TF), lambda f, j, *_: (j, f)),
            scratch_shapes=[pltpu.VMEM((d, MOE_TF), F8), pltpu.VMEM((d, MOE_TF), F8),
                            pltpu.VMEM((SUBLANES, LANES), F32)],
        ),
        out_shape=jax.ShapeDtypeStruct((rows, ff), BF16),
        compiler_params=_cparams(("arbitrary", "arbitrary")), name="moe_up",
    )(te, chg, nused, xs, wg, wu)


def _moe_down_kernel(te_ref, chg_ref, nused_ref, a_ref, wd_ref, o_ref, wd_s, inv_s):
    j = pl.program_id(0)

    @pl.when(chg_ref[j] == 1)
    def _():
        wd_s[...], inv = _f8_quantize(wd_ref[...])
        inv_s[0:1, :] = jnp.broadcast_to(inv, (1, LANES))

    @pl.when(j < nused_ref[0])
    def _():
        a8, inv_a = _f8_quantize(a_ref[...].astype(F32))
        o_ref[...] = (_dot(a8, wd_s[...]) * (inv_a * inv_s[0:1, 0:1])).astype(BF16)

    @pl.when(j >= nused_ref[0])
    def _():
        o_ref[...] = jnp.zeros_like(o_ref)


def _moe_down(te, chg, nused, act, wd):
    rows, ff = act.shape
    d = wd.shape[2]
    return pl.pallas_call(
        _moe_down_kernel,
        grid_spec=pltpu.PrefetchScalarGridSpec(
            num_scalar_prefetch=3, grid=(rows // EXPERT_TILE,),
            in_specs=[
                pl.BlockSpec((EXPERT_TILE, ff), lambda j, *_: (j, 0)),
                pl.BlockSpec((None, ff, d), lambda j, te, *_: (te[j], 0, 0)),
            ],
            out_specs=pl.BlockSpec((EXPERT_TILE, d), lambda j, *_: (j, 0)),
            scratch_shapes=[pltpu.VMEM((ff, d), F8), pltpu.VMEM((SUBLANES, LANES), F32)],
        ),
        out_shape=jax.ShapeDtypeStruct((rows, d), BF16),
        compiler_params=_cparams(("arbitrary",)), name="moe_down",
    )(te, chg, nused, act, wd)


def _combine_kernel(alpha, nw_ctx, blk_a_ref, blk_b_ref, rel_ref, two_ref, rank_ref, gate_ref, *refs):
    ya_refs, yb_refs = refs[:N_EXPERTS], refs[N_EXPERTS:2 * N_EXPERTS]
    x1_ref, g2_ref, lg_ref, lb_ref, oc_ref, ol_ref, acc_s = refs[2 * N_EXPERTS:]
    w = pl.program_id(0)
    row_id = lax.broadcasted_iota(jnp.int32, (MOE_TILE, MOE_TILE), 0).astype(F32)

    def weights(e, rel):
        hit = rank_ref[e:e + 1, :] == row_id + rel.astype(F32)
        return jnp.where(hit, gate_ref[e:e + 1, :], 0.0).T.astype(BF16)

    acc = None
    for e in range(N_EXPERTS):
        part = _dot(weights(e, rel_ref[w * N_EXPERTS + e]), ya_refs[e][...])
        acc = part if acc is None else acc + part
    acc_s[...] = acc
    for e in range(N_EXPERTS):
        @pl.when(two_ref[w * N_EXPERTS + e] == 1)
        def _():
            acc_s[...] += _dot(weights(e, rel_ref[w * N_EXPERTS + e] + MOE_TILE), yb_refs[e][...])

    out = _layer_norm(alpha * x1_ref[...] + g2_ref[...] * acc_s[...], lg_ref[...], lb_ref[...])

    @pl.when(w < nw_ctx)
    def _():
        oc_ref[...] = out

    @pl.when(w >= nw_ctx)
    def _():
        ol_ref[...] = out


def _moe_combine(bands, rank_t, gate_t, y, x1, mod_l, ln_g, ln_b, alpha, row_of_tile, t_ctx):
    t, d = x1.shape
    nw = t // MOE_TILE
    nw_ctx = t_ctx // MOE_TILE

    def y_spec(which, e):
        return pl.BlockSpec((MOE_TILE, d), lambda w, *pre: (pre[which][w * N_EXPERTS + e], 0))

    win_spec = pl.BlockSpec((None, N_EXPERTS, MOE_TILE), lambda w, *_: (w, 0, 0))
    return pl.pallas_call(
        functools.partial(_combine_kernel, alpha, nw_ctx),
        grid_spec=pltpu.PrefetchScalarGridSpec(
            num_scalar_prefetch=4, grid=(nw,),
            in_specs=[win_spec, win_spec]
            + [y_spec(0, e) for e in range(N_EXPERTS)] + [y_spec(1, e) for e in range(N_EXPERTS)]
            + [
                pl.BlockSpec((MOE_TILE, d), lambda w, *_: (w, 0)),
                pl.BlockSpec((None, None, 1, d), lambda w, *_: (row_of_tile(w), 5, 0, 0)),
                pl.BlockSpec((1, d), lambda w, *_: (0, 0)),
                pl.BlockSpec((1, d), lambda w, *_: (0, 0)),
            ],
            out_specs=[
                pl.BlockSpec((MOE_TILE, d), lambda w, *_: (jnp.minimum(w, nw_ctx - 1), 0)),
                pl.BlockSpec((MOE_TILE, d), lambda w, *_: (jnp.maximum(w - nw_ctx, 0), 0)),
            ],
            scratch_shapes=[pltpu.VMEM((MOE_TILE, d), F32)],
        ),
        out_shape=[jax.ShapeDtypeStruct((t_ctx, d), F32), jax.ShapeDtypeStruct((t - t_ctx, d), F32)],
        compiler_params=_cparams(("arbitrary",)), name="moe_combine",
    )(*bands, rank_t, gate_t, *([y] * (2 * N_EXPERTS)), x1, mod_l, ln_g, ln_b)


def _moe_layer(gates, h2, x1, mod_l, wg, wu, wd, ln_g, ln_b, alpha, row_of_tile, t_ctx):
    t = h2.shape[0]
    ratio = EXPERT_TILE // MOE_TILE
    n_big = -(-(2 * t + N_EXPERTS * (EXPERT_TILE - 1)) // EXPERT_TILE)
    n_tiles = n_big * ratio
    rank_t, gate_t, before = _route(gates)
    before = before[:, :, 0].astype(jnp.int32)
    counts = jnp.sum(gates > 0.0, axis=1).astype(jnp.int32)
    big_e = (counts + EXPERT_TILE - 1) // EXPERT_TILE
    big_end = jnp.cumsum(big_e)
    jb = jnp.arange(n_big, dtype=jnp.int32)
    te_big = jnp.minimum(jnp.sum(jb[:, None] >= big_end[None, :], axis=1), N_EXPERTS - 1).astype(jnp.int32)
    chg = jnp.concatenate([jnp.ones((1,), jnp.int32), (te_big[1:] != te_big[:-1]).astype(jnp.int32)])
    nused = big_end[-1].reshape(1).astype(jnp.int32)
    tiles_e = big_e * ratio
    tile_end = big_end * ratio
    n_used = tile_end[-1]
    j = jnp.arange(n_tiles, dtype=jnp.int32)
    te = jnp.minimum(jnp.sum(j[:, None] >= tile_end[None, :], axis=1), N_EXPERTS - 1).astype(jnp.int32)
    base = (j - jnp.take(tile_end - tiles_e, te)) * MOE_TILE
    last = jnp.minimum(base + MOE_TILE - 1, jnp.take(counts, te) - 1)
    before_t = jnp.take(before, te, axis=1)
    used = j < n_used
    wlo = jnp.where(used, jnp.sum(before_t <= base[None, :], axis=0) - 1, 1).astype(jnp.int32)
    whi = jnp.where(used, jnp.sum(before_t <= last[None, :], axis=0) - 1, 0).astype(jnp.int32)
    meta = (te, base.astype(jnp.int32), wlo, whi)
    xs = _moe_gather(meta, rank_t, h2, n_tiles)
    act = _moe_up(te_big, chg, nused, xs, wg, wu)
    y = _moe_down(te_big, chg, nused, act, wd)
    start_row = (tile_end - tiles_e) * MOE_TILE
    n_we = jnp.concatenate([before[1:], counts[None, :]], axis=0) - before
    first = start_row[None, :] + before
    blk_a = jnp.minimum(first // MOE_TILE, n_tiles - 1)
    two = ((n_we > 0) & ((first + n_we - 1) // MOE_TILE > blk_a)).astype(jnp.int32)
    rel = blk_a * MOE_TILE - start_row[None, :]
    bands = tuple(a.reshape(-1).astype(jnp.int32) for a in (blk_a, blk_a + two, rel, two))
    return _moe_combine(bands, rank_t, gate_t, y, x1, mod_l, ln_g, ln_b, alpha, row_of_tile, t_ctx)


def _rope_tables(seq):
    half = GLA_DK // 2
    n = half // 2
    t = np.arange(seq)
    inv = ROPE_BASE ** (-np.arange(n, dtype=np.float64) / n)
    cos = np.zeros((seq, GLA_DK), np.float64)
    sin = np.zeros((seq, GLA_DK), np.float64)
    for a, pos in enumerate((t // GRID_W, t % GRID_W)):
        ang = pos[:, None].astype(np.float64) * inv[None, :]
        base = a * half
        cos[:, base:base + n] = np.cos(ang)
        cos[:, base + n:base + half] = np.cos(ang)
        sin[:, base:base + n] = -np.sin(ang)
        sin[:, base + n:base + half] = np.sin(ang)
    tile = lambda m: jnp.asarray(np.tile(m, (1, GLA_HEADS)), F32)
    return tile(cos), tile(sin)


W_IN_GD = 1536
W_IN_KV = (2080, 2592)
W_IN_CHUNKS = (0, 512, 1024, 1568, 3104, 3616, 4128, 4640, 5152, 5664, 6176, 6688)


def _transpose_w_in(w):
    return jnp.swapaxes(w, 1, 2).astype(BF16)


def _gate_up_blocks(w_gup):
    qkw = w_gup.shape[-1]
    fwd = jnp.pad(w_gup[:, 0], ((0, 0), (0, 0), (0, qkw)))
    bwd = jnp.pad(w_gup[:, 1], ((0, 0), (0, 0), (qkw, 0)))
    return jnp.pad(jnp.concatenate([fwd, bwd], axis=1), ((0, 0), (0, LANES - 2 * GLA_RANK), (0, 0)))


def kernel(x_prompt, x_sample, cache_na_k, cache_na_v, state_gla, c, c_ctx, w_mod, b_mod, w_in, gla_w_gup, gla_b_g, gla_norm, w_br_gla, na_rpb, w_br_na, cv_w, cv_b, cv_ln_g, cv_ln_b, w_br_cv, w_out, ln_g, ln_b, ffd_w_gate, ffd_w_up, ffd_w_down, moe_w_router, moe_w_gate, moe_w_up, moe_w_down):
    nb_c, l_c, d = x_prompt.shape
    nb_l, l_l, _ = x_sample.shape
    depth = w_mod.shape[0]
    t_ctx, t_lat = nb_c * l_c, nb_l * l_l
    t_all = t_ctx + t_lat
    alpha = (2 * depth) ** 0.25
    tm = 512
    row_of_tile = _row_of_tile_fn(t_ctx, l_l, tm)

    x_pair = (x_prompt.reshape(t_ctx, d), x_sample.reshape(t_lat, d))
    cond8 = jnp.concatenate([c_ctx[None, :], c, jnp.zeros((8 - 1 - nb_l, d), F32)], axis=0)
    mod = _modulation(cond8, w_mod, b_mod).reshape(depth, 8, 6, 1, d)
    rope = _rope_tables(l_l)
    rows_lat = l_l // GRID_W
    kr = min(NA_WIN_R, rows_lat)
    s0_all = state_gla.reshape(nb_l, depth, 2, 2, LANES, GLA_DV)
    w_t = _transpose_w_in(w_in)
    wg_blks = _gate_up_blocks(gla_w_gup)
    toep = _na_bias_tables(na_rpb)
    w_br = [w.astype(BF16) for w in (w_br_gla, w_br_na, w_br_cv, w_out)]

    caches = None
    states = None
    for l in range(depth):
        mod_l = mod[l]
        gd, kc, vc, kl, vl, zb = _input_projection(x_pair, mod_l, w_t, l, depth, l_c, tm, row_of_tile, caches)
        caches = (kc, vc)

        wg_blk = wg_blks[l]
        bg = gla_b_g[l].reshape(1, CHUNK_W)
        gn = gla_norm[l].reshape(1, CHUNK_W)
        ya, states = _gla_branch(gd, zb, wg_blk, bg, gn, latent=False, seq=l_c, nb=nb_c, tok0=0, t_total=t_all,
                                 state_slot=(l, depth, states))
        ya = _gla_branch(gd, zb, wg_blk, bg, gn, latent=True, seq=l_l, nb=nb_l, tok0=t_ctx, t_total=t_all,
                         rope=rope, s0=s0_all[:, l], ya_prev=ya)

        yb = _attn_context(zb, kc, vc, l, nb_c, l_c, t_all)
        yb = _attn_latent(zb, kl, vl, cache_na_k, cache_na_v, l, toep, yb, nb_l, l_l, t_ctx)

        conv_args = (zb, cv_w[l], cv_b[l][None, :], cv_ln_g[l][None, :], cv_ln_b[l][None, :])
        yc = _conv_branch(*conv_args, seq=l_c, nb=nb_c, tok0=0, t_total=t_all)
        yc = _conv_branch(*conv_args, seq=l_l, nb=nb_l, tok0=t_ctx, t_total=t_all, prev=yc)

        moe = l % 2 == 1
        w_router = None
        if moe:
            w_router = moe_w_router[l // 2].T
        merged = _merge(ya, yb, yc, zb, x_pair, mod_l, w_br[0][l], w_br[1][l], w_br[2][l], w_br[3][l],
                        ln_g[l, 0][None, :], ln_b[l, 0][None, :], alpha, tm, row_of_tile, w_router)
        ln2 = (ln_g[l, 1][None, :], ln_b[l, 1][None, :])
        if moe:
            x1, h2, gates = merged
            x_pair = _moe_layer(gates, h2, x1, mod_l, moe_w_gate[l // 2], moe_w_up[l // 2], moe_w_down[l // 2],
                                *ln2, alpha, _row_of_tile_fn(t_ctx, l_l, MOE_TILE), t_ctx)
        else:
            x1, h2 = merged
            i = l // 2
            x_pair = _dense_mixer(h2, x1, mod_l, ffd_w_gate[i].astype(BF16), ffd_w_up[i].astype(BF16),
                                  ffd_w_down[i].astype(BF16), *ln2, alpha, tm, 1, row_of_tile, t_ctx)

    y_prompt = x_pair[0].reshape(nb_c, l_c, d)
    y_sample = x_pair[1].reshape(nb_l, l_l, d)
    cache_shape = (nb_c, depth, l_c, NA_HEADS, NA_DH)
    return (y_prompt, y_sample, caches[0].reshape(cache_shape), caches[1].reshape(cache_shape),
            states.reshape(nb_c, depth, 2, GLA_HEADS, GLA_DK, GLA_DV))


def _row_of_tile_fn(t_ctx, l_lat, tm):
    n_ctx = t_ctx // tm

    def row_of_tile(i):
        return jnp.where(i < n_ctx, 0, 1 + (i - n_ctx) // (l_lat // tm))

    return row_of_tile
```

```python
import functools

import numpy as np
import jax
import jax.numpy as jnp
from jax import lax
from jax.experimental import pallas as pl
from jax.experimental.pallas import tpu as pltpu

F32 = jnp.float32
BF16 = jnp.bfloat16

GRID_W = 64
GLA_HEADS = 4
GLA_DK = 64
GLA_DV = 128
GLA_RANK = 16
GLA_TAU = 16.0
GLA_CHUNK = 64
GLA_GROUP = 256
GLA_CTX_SEQS = 2
NA_HEADS = 8
NA_DH = 64
NA_WIN_R = 8
NA_WIN_C = 16
CONV_CH = 512
CONV_K = 31
N_EXPERTS = 8
ROPE_BASE = 10000.0
LN_EPS = 1e-5

D_MODEL = 1024
LANES = 128
SUBLANES = 8
CHUNK_W = 512
N_BF16_CHUNKS = 12
NEG_BIG = -1e30
VMEM_LIMIT = 56 * 1024 * 1024


def _cparams(sem, vmem=VMEM_LIMIT):
    return pltpu.CompilerParams(dimension_semantics=sem, vmem_limit_bytes=vmem)


def _dot(a, b):
    return jnp.dot(a, b, preferred_element_type=F32)


def _split(x):
    hi = x.astype(BF16)
    return hi, (x - hi.astype(F32)).astype(BF16)


def _dot_split(a, b):
    a_hi, a_lo = _split(a)
    b_hi, b_lo = _split(b)
    return _dot(a_hi, b_hi) + _dot(a_lo, b_hi) + _dot(a_hi, b_lo)


def _dot_split_rhs(a, b):
    b_hi, b_lo = _split(b)
    return _dot(a, b_hi) + _dot(a, b_lo)


def _dot_nt(a, b):
    return lax.dot_general(a, b, (((1,), (1,)), ((), ())), preferred_element_type=F32)


def _sigmoid(x):
    return 0.5 * jnp.tanh(0.5 * x) + 0.5


def _silu(x):
    return x * _sigmoid(x)


def _layer_norm(x, g, b):
    mu = jnp.mean(x, axis=-1, keepdims=True)
    xc = x - mu
    var = jnp.mean(xc * xc, axis=-1, keepdims=True)
    return xc * lax.rsqrt(var + LN_EPS) * g + b


def _mod_kernel(c_ref, w_ref, b_ref, o_ref):
    c = c_ref[...]
    s = _silu(c).astype(BF16)
    o_ref[...] = _dot(s, w_ref[...].astype(BF16)) + b_ref[...]


def _modulation(cond8, w_mod, b_mod):
    depth, d, n = w_mod.shape
    tn = 1024
    return pl.pallas_call(
        _mod_kernel,
        grid=(depth, n // tn),
        in_specs=[
            pl.BlockSpec((8, d), lambda l, j: (0, 0)),
            pl.BlockSpec((None, d, tn), lambda l, j: (l, 0, j)),
            pl.BlockSpec((None, 1, tn), lambda l, j: (l, 0, j)),
        ],
        out_specs=pl.BlockSpec((None, 8, tn), lambda l, j: (l, 0, j)),
        out_shape=jax.ShapeDtypeStruct((depth, 8, n), F32),
        compiler_params=_cparams(("arbitrary", "arbitrary")),
        name="modulation",
    )(cond8, w_mod, b_mod.reshape(depth, 1, n))


def _inproj_kernel(n_ctx_tiles, aliased, *refs):
    if aliased:
        (xc_ref, xl_ref, sc_ref, sh_ref, w_ref, _kc_in, _vc_in,
         gd_ref, kc_ref, vc_ref, kl_ref, vl_ref, zb_ref, h_ref) = refs
    else:
        (xc_ref, xl_ref, sc_ref, sh_ref, w_ref,
         gd_ref, kc_ref, vc_ref, kl_ref, vl_ref, zb_ref, h_ref) = refs
    is_ctx = pl.program_id(0) < n_ctx_tiles
    x = jnp.where(is_ctx, xc_ref[...], xl_ref[...])
    h_ref[...] = (x * (1.0 + sc_ref[...]) + sh_ref[...]).astype(BF16)

    def project(start, width):
        return _dot_nt(h_ref[...], w_ref[start:start + width, :])

    gd_ref[...] = project(W_IN_GD, LANES)
    k = project(W_IN_KV[0], CHUNK_W)
    v = project(W_IN_KV[1], CHUNK_W)
    seqs, l_c, _ = kc_ref.shape

    @pl.when(is_ctx)
    def _():
        for s in range(seqs):
            kc_ref[s] = k[s * l_c:(s + 1) * l_c]
            vc_ref[s] = v[s * l_c:(s + 1) * l_c]

    @pl.when(jnp.logical_not(is_ctx))
    def _():
        kl_ref[...] = k
        vl_ref[...] = v

    for j, start in enumerate(W_IN_CHUNKS):
        zb_ref[j] = project(start, CHUNK_W).astype(BF16)


def _mod_spec(piece, row_of_tile):
    return pl.BlockSpec((None, None, 1, D_MODEL), lambda i: (row_of_tile(i), piece, 0, 0))


def _pair_specs(tm, d, n_ctx_tiles):
    return [pl.BlockSpec((tm, d), lambda i, *_: (jnp.minimum(i, n_ctx_tiles - 1), 0)),
            pl.BlockSpec((tm, d), lambda i, *_: (jnp.maximum(i - n_ctx_tiles, 0), 0))]


def _input_projection(x_pair, mod_l, w_t, layer, depth, l_c, tm, row_of_tile, caches=None):
    xc, xl = x_pair
    d = xc.shape[1]
    t_ctx, t_lat = xc.shape[0], xl.shape[0]
    t = t_ctx + t_lat
    n_ctx_tiles = t_ctx // tm
    seqs = tm // l_c
    cache_spec = pl.BlockSpec((seqs, None, l_c, CHUNK_W),
                              lambda i: (jnp.minimum(i, n_ctx_tiles - 1), layer, 0, 0))
    lat_spec = pl.BlockSpec((tm, CHUNK_W), lambda i: (jnp.maximum(i - n_ctx_tiles, 0), 0))
    in_specs = _pair_specs(tm, d, n_ctx_tiles) + [
        _mod_spec(1, row_of_tile), _mod_spec(0, row_of_tile),
        pl.BlockSpec((None,) + w_t.shape[1:], lambda i: (layer, 0, 0), pipeline_mode=pl.Buffered(1)),
    ]
    args = [xc, xl, mod_l, mod_l, w_t]
    aliases = {}
    if caches is not None:
        in_specs += [pl.BlockSpec(memory_space=pl.ANY), pl.BlockSpec(memory_space=pl.ANY)]
        args += list(caches)
        aliases = {5: 1, 6: 2}
    cache_shape = jax.ShapeDtypeStruct((t_ctx // l_c, depth, l_c, CHUNK_W), F32)
    return pl.pallas_call(
        functools.partial(_inproj_kernel, n_ctx_tiles, caches is not None),
        grid=(t // tm,),
        in_specs=in_specs,
        out_specs=[
            pl.BlockSpec((tm, LANES), lambda i: (i, 0)),
            cache_spec, cache_spec, lat_spec, lat_spec,
            pl.BlockSpec((N_BF16_CHUNKS, tm, CHUNK_W), lambda i: (0, i, 0)),
        ],
        out_shape=[
            jax.ShapeDtypeStruct((t, LANES), F32),
            cache_shape, cache_shape,
            jax.ShapeDtypeStruct((t_lat, CHUNK_W), F32), jax.ShapeDtypeStruct((t_lat, CHUNK_W), F32),
            jax.ShapeDtypeStruct((N_BF16_CHUNKS, t, CHUNK_W), BF16),
        ],
        scratch_shapes=[pltpu.VMEM((tm, d), BF16)],
        input_output_aliases=aliases,
        compiler_params=_cparams(("arbitrary",)),
        name="input_projection",
    )(*args)


def _gla_kernel(latent, seq, n_seq, state_alias, *refs):
    if latent:
        (qk_ref, v_ref, ra_ref, gd_ref, wg_ref, bg_ref, gn_ref, cos_ref, sin_ref, s0_ref, _alias,
         ya_ref, q_s, k_s, g_s, acc_s, st_s) = refs
    elif state_alias:
        (qk_ref, v_ref, ra_ref, gd_ref, wg_ref, bg_ref, gn_ref, _alias,
         ya_ref, sfin_ref, q_s, k_s, g_s, acc_s, st_s) = refs
    else:
        (qk_ref, v_ref, ra_ref, gd_ref, wg_ref, bg_ref, gn_ref,
         ya_ref, sfin_ref, q_s, k_s, g_s, acc_s, st_s) = refs
    c = GLA_CHUNK
    n_chunks = n_seq * seq // c
    qkw = GLA_HEADS * GLA_DK

    q = qk_ref[:, 0:qkw].astype(F32) * (GLA_DK ** -0.5)
    k = qk_ref[:, qkw:2 * qkw].astype(F32)
    if latent:
        lane = lax.broadcasted_iota(jnp.int32, (1, qkw), 1)
        first_half = (lane % 32) < 16

        def rope(x):
            swapped = jnp.where(first_half, pltpu.roll(x, qkw - 16, 1), pltpu.roll(x, 16, 1))
            return x * cos_ref[...] + swapped * sin_ref[...]

        q, k = rope(q), rope(k)
    q_s[...] = q
    k_s[...] = k

    glog = _dot_split(gd_ref[...], wg_ref[...]) + bg_ref[...]
    g_s[...] = (jnp.minimum(glog, 0.0) - jnp.log(1.0 + jnp.exp(-jnp.abs(glog)))) * (1.0 / GLA_TAU)
    acc_s[...] = jnp.zeros_like(acc_s)
    if latent:
        st_s[0] = s0_ref[...]
    else:
        st_s[...] = jnp.zeros_like(st_s)

    grp = GLA_GROUP
    cpg = grp // c
    n_groups = seq // grp
    grow = lax.broadcasted_iota(jnp.int32, (grp, grp), 0)
    gcol = lax.broadcasted_iota(jnp.int32, (grp, grp), 1)
    same_chunk = (grow // c) == (gcol // c)
    keep = (same_chunk & (gcol <= grow), same_chunk & (gcol >= grow))
    tri = (keep[0].astype(BF16), keep[1].astype(BF16))
    lane128 = lax.broadcasted_iota(jnp.int32, (1, LANES), 1)
    head_mask = (lane128 < GLA_DK, lane128 >= GLA_DK)
    urow = lax.broadcasted_iota(jnp.int32, (cpg * LANES, GLA_DV), 0)
    srow = lax.broadcasted_iota(jnp.int32, (LANES, GLA_DV), 0)
    chunk_of_col = lax.broadcasted_iota(jnp.int32, (LANES, grp), 1) // c
    chunk_of_row = lax.broadcasted_iota(jnp.int32, (grp, LANES), 0) // c
    pad_rows = jnp.zeros((SUBLANES - cpg, qkw), F32)

    def group_step(i, carry):
        for sq, d in ((sq, d) for sq in range(n_seq) for d in range(2)):
            gi = i if d == 0 else n_groups - 1 - i
            rows = pl.ds(pl.multiple_of(sq * seq + gi * grp, grp), grp)
            cum = _dot_split_rhs(tri[d], g_s[rows, d * qkw:(d + 1) * qkw])
            edge = c - 1 if d == 0 else 0
            lasts = [cum[ci * c + edge:ci * c + edge + 1, :] for ci in range(cpg)]
            last = jnp.concatenate([jnp.broadcast_to(r, (c, qkw)) for r in lasts], axis=0)
            ref = 0.5 * last
            q = q_s[rows, :]
            k = k_s[rows, :]
            qe = q * jnp.exp(cum - ref)
            ke = k * jnp.exp(ref - cum)
            qe2 = q * jnp.exp(cum)
            ke2 = k * jnp.exp(last - cum)
            last8 = jnp.concatenate(lasts + [pad_rows], axis=0)
            for p in range(2):
                sl = slice(p * LANES, (p + 1) * LANES)
                ke_p = ke[:, sl].astype(BF16)
                ke2_t = ke2[:, sl].T
                ke2_blk = jnp.concatenate(
                    [jnp.where(chunk_of_col == ci, ke2_t, 0.0) for ci in range(cpg)], axis=0).astype(BF16)
                dec = jnp.exp(last8[:, sl]).T
                psl = slice(2 * p * GLA_DV, (2 * p + 2) * GLA_DV)
                v_pair = v_ref[rows, psl]
                u = _dot(ke2_blk, v_pair)
                upd = jnp.where(urow % LANES < GLA_DK, u[:, :GLA_DV], u[:, GLA_DV:])
                s = st_s[sq, d, p]
                s_in = [None] * cpg
                for ci in (range(cpg) if d == 0 else range(cpg - 1, -1, -1)):
                    s_in[ci] = jnp.concatenate(
                        [jnp.where(srow < GLA_DK, s, 0.0), jnp.where(srow >= GLA_DK, s, 0.0)], axis=1).astype(BF16)
                    s = dec[:, ci:ci + 1] * s + upd[ci * LANES:(ci + 1) * LANES]
                st_s[sq, d, p] = s
                s_stack = jnp.concatenate(s_in, axis=0)
                a2_blk = jnp.concatenate(
                    [jnp.where(chunk_of_row == ci, qe2[:, sl], 0.0) for ci in range(cpg)], axis=1).astype(BF16)
                intra = []
                for hh in range(2):
                    a = jnp.where(head_mask[hh], qe[:, sl], 0.0).astype(BF16)
                    att = jnp.where(keep[d], _dot_nt(a, ke_p), 0.0).astype(BF16)
                    intra.append(_dot(att, v_pair[:, hh * GLA_DV:(hh + 1) * GLA_DV]))
                acc_s[rows, psl] += jnp.concatenate(intra, axis=1) + _dot(a2_blk, s_stack)
        return carry

    lax.fori_loop(0, n_groups, group_step, 0)

    def epilogue(i, carry):
        rows = pl.ds(pl.multiple_of(i * c, c), c)
        for h in range(GLA_HEADS):
            vsl = slice(h * GLA_DV, (h + 1) * GLA_DV)
            o = acc_s[rows, vsl]
            o = o * lax.rsqrt(jnp.mean(o * o, axis=-1, keepdims=True) + LN_EPS) * gn_ref[:, vsl]
            ya_ref[rows, vsl] = (o * _silu(ra_ref[rows, vsl].astype(F32))).astype(BF16)
        return carry

    lax.fori_loop(0, n_chunks, epilogue, 0)
    if not latent:
        sfin_ref[...] = st_s[...]


def _gla_branch(gd, zb, wg_blk, bg, gn, *, latent, seq, nb, tok0, t_total, rope=None, s0=None, ya_prev=None,
                state_slot=None):
    n_seq = 1 if latent else GLA_CTX_SEQS
    rows = n_seq * seq
    b0 = tok0 // rows
    qkw = GLA_HEADS * GLA_DK
    in_specs = [
        pl.BlockSpec((None, rows, CHUNK_W), lambda b: (0, b0 + b, 0)),
        pl.BlockSpec((None, rows, CHUNK_W), lambda b: (1, b0 + b, 0)),
        pl.BlockSpec((None, rows, CHUNK_W), lambda b: (2, b0 + b, 0)),
        pl.BlockSpec((rows, LANES), lambda b: (b0 + b, 0)),
        pl.BlockSpec((LANES, CHUNK_W), lambda b: (0, 0)),
        pl.BlockSpec((1, CHUNK_W), lambda b: (0, 0)),
        pl.BlockSpec((1, CHUNK_W), lambda b: (0, 0)),
    ]
    args = [zb, zb, zb, gd, wg_blk, bg, gn]
    scratch = [
        pltpu.VMEM((rows, qkw), F32), pltpu.VMEM((rows, qkw), F32), pltpu.VMEM((rows, CHUNK_W), F32),
        pltpu.VMEM((rows, CHUNK_W), F32), pltpu.VMEM((n_seq, 2, 2, LANES, GLA_DV), F32),
    ]
    ya_shape = jax.ShapeDtypeStruct((t_total, CHUNK_W), BF16)
    ya_spec = pl.BlockSpec((rows, CHUNK_W), lambda b: (b0 + b, 0))
    if latent:
        cos_t, sin_t = rope
        in_specs += [
            pl.BlockSpec((seq, qkw), lambda b: (0, 0)),
            pl.BlockSpec((seq, qkw), lambda b: (0, 0)),
            pl.BlockSpec((None, 2, 2, LANES, GLA_DV), lambda b: (b, 0, 0, 0, 0)),
            pl.BlockSpec(memory_space=pl.ANY),
        ]
        args += [cos_t, sin_t, s0, ya_prev]
        return pl.pallas_call(
            functools.partial(_gla_kernel, True, seq, n_seq, False),
            grid=(nb,), in_specs=in_specs, out_specs=ya_spec, out_shape=ya_shape,
            scratch_shapes=scratch, input_output_aliases={len(args) - 1: 0},
            compiler_params=_cparams(("arbitrary",)), name="gla_latent",
        )(*args)
    layer, depth, states = state_slot
    aliases = {}
    if states is not None:
        in_specs.append(pl.BlockSpec(memory_space=pl.ANY))
        args.append(states)
        aliases = {len(args) - 1: 1}
    return pl.pallas_call(
        functools.partial(_gla_kernel, False, seq, n_seq, states is not None),
        grid=(nb // n_seq,), in_specs=in_specs,
        out_specs=[ya_spec, pl.BlockSpec((n_seq, None, 2, 2, LANES, GLA_DV), lambda b: (b, layer, 0, 0, 0, 0))],
        out_shape=[ya_shape, jax.ShapeDtypeStruct((nb, depth, 2, 2, LANES, GLA_DV), F32)],
        scratch_shapes=scratch, input_output_aliases=aliases,
        compiler_params=_cparams(("arbitrary",)), name="gla_context",
    )(*args)


ATTN_CTX_SEQS = 4


def _attn_ctx_kernel(q_ref, k_ref, v_ref, o_ref):
    lane = lax.broadcasted_iota(jnp.int32, (1, LANES), 1)
    masks = (lane < NA_DH, lane >= NA_DH)
    scale = NA_DH ** -0.5
    n_seq, seq, _ = k_ref.shape
    for b in range(n_seq):
        rows = slice(b * seq, (b + 1) * seq)
        for p in range(NA_HEADS // 2):
            sl = slice(p * LANES, (p + 1) * LANES)
            q2 = _stack_heads(q_ref[rows, sl], masks)
            kp = k_ref[b, :, sl].astype(BF16)
            vp = v_ref[b, :, sl].astype(BF16)
            s = _dot_nt(q2, kp) * scale
            e = jnp.exp(s - jnp.max(s, axis=-1, keepdims=True))
            o2 = _dot(e.astype(BF16), vp) / jnp.sum(e, axis=-1, keepdims=True)
            o_ref[rows, sl] = jnp.where(masks[0], o2[0:seq], o2[seq:2 * seq]).astype(BF16)


def _stack_heads(qp, masks):
    zero = jnp.zeros_like(qp)
    return jnp.concatenate([jnp.where(masks[0], qp, zero), jnp.where(masks[1], qp, zero)], axis=0)


def _attn_context(zb, kc, vc, layer, nb, seq, t_total):
    per = ATTN_CTX_SEQS
    return pl.pallas_call(
        _attn_ctx_kernel,
        grid=(nb // per,),
        in_specs=[
            pl.BlockSpec((None, per * seq, CHUNK_W), lambda b: (3, b, 0)),
            pl.BlockSpec((per, None, seq, CHUNK_W), lambda b: (b, layer, 0, 0)),
            pl.BlockSpec((per, None, seq, CHUNK_W), lambda b: (b, layer, 0, 0)),
        ],
        out_specs=pl.BlockSpec((per * seq, CHUNK_W), lambda b: (b, 0)),
        out_shape=jax.ShapeDtypeStruct((t_total, CHUNK_W), BF16),
        compiler_params=_cparams(("arbitrary",)), name="attention_context",
    )(zb, kc, vc)


NA_ROWS_PER_STEP = 4


def _na_kernel(rows_total, kr, q_ref, kl_ref, vl_ref, kc_ref, vc_ref, toep_ref, _alias, o_ref):
    lane = lax.broadcasted_iota(jnp.int32, (1, LANES), 1)
    masks = (lane < NA_DH, lane >= NA_DH)
    scale = NA_DH ** -0.5
    nq = GRID_W
    for rr in range(NA_ROWS_PER_STEP):
        r = pl.program_id(1) * NA_ROWS_PER_STEP + rr
        qrows = slice(rr * nq, (rr + 1) * nq)
        rs = jnp.clip(r - kr // 2, 0, rows_total - kr)
        krows = pl.ds(pl.multiple_of(rs * GRID_W, GRID_W), kr * GRID_W)
        dr0 = rs - r + NA_WIN_R - 1
        for p in range(NA_HEADS // 2):
            sl = slice(p * LANES, (p + 1) * LANES)
            q2 = _stack_heads(q_ref[qrows, sl], masks)
            klp = kl_ref[krows, sl].astype(BF16)
            vlp = vl_ref[krows, sl].astype(BF16)
            kcp = kc_ref[:, sl].astype(BF16)
            vcp = vc_ref[:, sl].astype(BF16)
            bias = jnp.concatenate(
                [jnp.concatenate([toep_ref[2 * p + hh, dr0 + 2 * m] for m in range(kr // 2)], axis=1)
                 for hh in range(2)], axis=0)
            s_loc = _dot_nt(q2, klp) * scale + bias
            s_ctx = _dot_nt(q2, kcp) * scale
            m = jnp.maximum(jnp.max(s_loc, axis=-1, keepdims=True), jnp.max(s_ctx, axis=-1, keepdims=True))
            e_loc = jnp.exp(s_loc - m)
            e_ctx = jnp.exp(s_ctx - m)
            den = jnp.sum(e_loc, axis=-1, keepdims=True) + jnp.sum(e_ctx, axis=-1, keepdims=True)
            o2 = (_dot(e_loc.astype(BF16), vlp) + _dot(e_ctx.astype(BF16), vcp)) / den
            o_ref[qrows, sl] = jnp.where(masks[0], o2[0:nq], o2[nq:2 * nq]).astype(BF16)


def _na_bias_tables(rpb):
    nc = rpb.shape[-1]
    qc = np.arange(GRID_W)[:, None]
    kc = np.arange(GRID_W)[None, :]
    cs = np.clip(qc - NA_WIN_C // 2, 0, GRID_W - NA_WIN_C)
    valid = (kc >= cs) & (kc < cs + NA_WIN_C)
    tap = np.clip(kc - qc, -(NA_WIN_C - 1), NA_WIN_C - 1) + NA_WIN_C - 1
    pick = jnp.asarray(tap[..., None] == np.arange(nc), F32)
    toep = jnp.einsum('dhab,qkb->dhaqk', rpb.astype(F32), pick, precision=lax.Precision.HIGHEST)
    toep = jnp.where(valid, toep, NEG_BIG)
    return jnp.concatenate([toep[:, :, :-1], toep[:, :, 1:]], axis=-1)


def _attn_latent(zb, kl, vl, cache_k, cache_v, layer, toep, att_prev, nb, seq, tok0):
    rows_total = seq // GRID_W
    kr = min(NA_WIN_R, rows_total)
    nq = NA_ROWS_PER_STEP * GRID_W
    steps = rows_total // NA_ROWS_PER_STEP
    q0 = tok0 // nq
    past = cache_k.shape[2]
    ck = cache_k.reshape(cache_k.shape[0], cache_k.shape[1], past, NA_HEADS * NA_DH)
    cv = cache_v.reshape(ck.shape)
    return pl.pallas_call(
        functools.partial(_na_kernel, rows_total, kr),
        grid=(nb, steps),
        in_specs=[
            pl.BlockSpec((None, nq, CHUNK_W), lambda b, r: (3, q0 + b * steps + r, 0)),
            pl.BlockSpec((seq, CHUNK_W), lambda b, r: (b, 0)),
            pl.BlockSpec((seq, CHUNK_W), lambda b, r: (b, 0)),
            pl.BlockSpec((None, None, past, CHUNK_W), lambda b, r: (b, layer, 0, 0)),
            pl.BlockSpec((None, None, past, CHUNK_W), lambda b, r: (b, layer, 0, 0)),
            pl.BlockSpec((None,) + toep.shape[1:], lambda b, r: (layer, 0, 0, 0, 0), pipeline_mode=pl.Buffered(1)),
            pl.BlockSpec(memory_space=pl.ANY),
        ],
        out_specs=pl.BlockSpec((nq, CHUNK_W), lambda b, r: (q0 + b * steps + r, 0)),
        out_shape=jax.ShapeDtypeStruct(att_prev.shape, BF16),
        input_output_aliases={6: 0},
        compiler_params=_cparams(("arbitrary", "arbitrary")), name="attention_latent",
    )(zb, kl, vl, ck, cv, toep, att_prev)


CONV_PAD = 16
CONV_ROWS = 64
CONV_SEQS = 2
CONV_SEQS_MAX_LEN = 256


def _conv_kernel(seq, n_seq, aliased, *refs):
    if aliased:
        a_ref, gt_ref, w_ref, cb_ref, lg_ref, lb_ref, _alias, o_ref, z_s, sh_s = refs
    else:
        a_ref, gt_ref, w_ref, cb_ref, lg_ref, lb_ref, o_ref, z_s, sh_s = refs
    blk = seq + 2 * CONV_PAD
    ext = blk - SUBLANES
    off = CONV_PAD - CONV_K // 2
    grp = (CONV_ROWS // SUBLANES, SUBLANES, CONV_CH)
    for s in range(n_seq):
        zo, so, io = s * blk, s * ext, s * seq
        z_s[zo:zo + CONV_PAD, :] = jnp.zeros((CONV_PAD, CONV_CH), F32)
        z_s[zo + CONV_PAD + seq:zo + blk, :] = jnp.zeros((CONV_PAD, CONV_CH), F32)
        z_s[zo + CONV_PAD:zo + CONV_PAD + seq, :] = (
            a_ref[io:io + seq, :].astype(F32) * _sigmoid(gt_ref[io:io + seq, :].astype(F32)))
        for b in range(1, SUBLANES):
            for r0 in range(0, ext, CONV_ROWS):
                n = min(CONV_ROWS, ext - r0)
                sh_s[b - 1, so + r0:so + r0 + n, :] = z_s[zo + r0 + b:zo + r0 + b + n, :]
        for t0 in range(0, seq, CONV_ROWS):
            acc = jnp.zeros(grp, F32)
            for k in range(CONV_K):
                a, b = divmod(off + k, SUBLANES)
                r0 = t0 + a * SUBLANES
                tap = (z_s[zo + r0:zo + r0 + CONV_ROWS, :] if b == 0
                       else sh_s[b - 1, so + r0:so + r0 + CONV_ROWS, :])
                acc = acc + tap.reshape(grp) * w_ref[k][None]
            y = _layer_norm(acc.reshape(CONV_ROWS, CONV_CH) + cb_ref[...], lg_ref[...], lb_ref[...])
            o_ref[io + t0:io + t0 + CONV_ROWS, :] = _silu(y).astype(BF16)


def _conv_branch(zb, cv_w, cv_b, ln_g, ln_b, *, seq, nb, tok0, t_total, prev=None):
    n_seq = CONV_SEQS if nb % CONV_SEQS == 0 and seq <= CONV_SEQS_MAX_LEN else 1
    rows = n_seq * seq
    b0 = tok0 // rows
    in_specs = [
        pl.BlockSpec((None, rows, CHUNK_W), lambda b: (4, b0 + b, 0)),
        pl.BlockSpec((None, rows, CHUNK_W), lambda b: (5, b0 + b, 0)),
        pl.BlockSpec((CONV_K, SUBLANES, CONV_CH), lambda b: (0, 0, 0)),
        pl.BlockSpec((1, CONV_CH), lambda b: (0, 0)),
        pl.BlockSpec((1, CONV_CH), lambda b: (0, 0)),
        pl.BlockSpec((1, CONV_CH), lambda b: (0, 0)),
    ]
    args = [zb, zb, jnp.broadcast_to(cv_w[:, None, :], (CONV_K, SUBLANES, CONV_CH)), cv_b, ln_g, ln_b]
    aliases = {}
    if prev is not None:
        in_specs.append(pl.BlockSpec(memory_space=pl.ANY))
        args.append(prev)
        aliases = {6: 0}
    return pl.pallas_call(
        functools.partial(_conv_kernel, seq, n_seq, prev is not None),
        grid=(nb // n_seq,), in_specs=in_specs,
        out_specs=pl.BlockSpec((rows, CONV_CH), lambda b: (b0 + b, 0)),
        out_shape=jax.ShapeDtypeStruct((t_total, CONV_CH), BF16),
        scratch_shapes=[pltpu.VMEM((n_seq * (seq + 2 * CONV_PAD), CONV_CH), F32),
                        pltpu.VMEM((SUBLANES - 1, n_seq * (seq + 2 * CONV_PAD - SUBLANES), CONV_CH), F32)],
        input_output_aliases=aliases,
        compiler_params=_cparams(("arbitrary",)), name="conv_module_%d" % seq,
    )(*args)


def _merge_kernel(alpha, n_ctx_tiles, with_router, *refs):
    if with_router:
        (ya_ref, yb_ref, yc_ref, mg_ref, xc_ref, xl_ref, g1_ref, sc2_ref, sh2_ref, wa_ref, wb_ref, wc_ref, wo_ref,
         lg_ref, lb_ref, wr_ref, x1_ref, h2_ref, gates_ref) = refs
    else:
        (ya_ref, yb_ref, yc_ref, mg_ref, xc_ref, xl_ref, g1_ref, sc2_ref, sh2_ref, wa_ref, wb_ref, wc_ref, wo_ref,
         lg_ref, lb_ref, x1_ref, h2_ref) = refs
    x = jnp.where(pl.program_id(0) < n_ctx_tiles, xc_ref[...], xl_ref[...])
    halves = []
    for n in range(2):
        m = None
        for j, (y_ref, w_ref) in enumerate(((ya_ref, wa_ref), (yb_ref, wb_ref), (yc_ref, wc_ref))):
            proj = _dot(y_ref[...], w_ref[:, n * CHUNK_W:(n + 1) * CHUNK_W])
            term = _sigmoid(mg_ref[2 * j + n].astype(F32)) * proj
            m = term if m is None else m + term
        halves.append(m.astype(BF16))
    mix = _dot(halves[0], wo_ref[0:CHUNK_W, :]) + _dot(halves[1], wo_ref[CHUNK_W:2 * CHUNK_W, :])
    x1 = _layer_norm(alpha * x + g1_ref[...] * mix, lg_ref[...], lb_ref[...])
    x1_ref[...] = x1
    h2 = x1 * (1.0 + sc2_ref[...]) + sh2_ref[...]
    h2_ref[...] = h2.astype(BF16)
    if with_router:
        w_hi, w_lo = _split(wr_ref[...])
        h_hi, h_lo = _split(h2)
        lg = _dot_nt(w_hi, h_hi) + _dot_nt(w_hi, h_lo) + _dot_nt(w_lo, h_hi)
        eid = lax.broadcasted_iota(jnp.int32, lg.shape, 0)
        m1 = jnp.max(lg, axis=0, keepdims=True)
        i1 = jnp.min(jnp.where(lg == m1, eid, N_EXPERTS), axis=0, keepdims=True)
        lg2 = jnp.where(eid == i1, -jnp.inf, lg)
        m2 = jnp.max(lg2, axis=0, keepdims=True)
        i2 = jnp.min(jnp.where(lg2 == m2, eid, N_EXPERTS), axis=0, keepdims=True)
        e2 = jnp.exp(m2 - m1)
        w1 = 1.0 / (1.0 + e2)
        gates_ref[...] = jnp.where(eid == i1, w1, 0.0) + jnp.where(eid == i2, e2 * w1, 0.0)


def _merge(ya, yb, yc, zb, x_pair, mod_l, wa, wb, wc, wo, ln_g, ln_b, alpha, tm, row_of_tile, w_router=None):
    xc, xl = x_pair
    d = xc.shape[1]
    t = xc.shape[0] + xl.shape[0]
    n_ctx_tiles = xc.shape[0] // tm
    full = lambda shape: pl.BlockSpec(shape, lambda i: tuple(0 for _ in shape), pipeline_mode=pl.Buffered(1))
    in_specs = [
        pl.BlockSpec((tm, CHUNK_W), lambda i: (i, 0)),
        pl.BlockSpec((tm, CHUNK_W), lambda i: (i, 0)),
        pl.BlockSpec((tm, CHUNK_W), lambda i: (i, 0)),
        pl.BlockSpec((6, tm, CHUNK_W), lambda i: (1, i, 0)),
    ] + _pair_specs(tm, d, n_ctx_tiles) + [
        _mod_spec(2, row_of_tile), _mod_spec(4, row_of_tile), _mod_spec(3, row_of_tile),
        full(wa.shape), full(wb.shape), full(wc.shape), full(wo.shape), full((1, d)), full((1, d)),
    ]
    args = [ya, yb, yc, zb, xc, xl, mod_l, mod_l, mod_l, wa, wb, wc, wo, ln_g, ln_b]
    out_specs = [pl.BlockSpec((tm, d), lambda i: (i, 0)), pl.BlockSpec((tm, d), lambda i: (i, 0))]
    out_shape = [jax.ShapeDtypeStruct((t, d), F32), jax.ShapeDtypeStruct((t, d), BF16)]
    if w_router is not None:
        in_specs.append(full(w_router.shape))
        args.append(w_router)
        out_specs.append(pl.BlockSpec((N_EXPERTS, tm), lambda i: (0, i)))
        out_shape.append(jax.ShapeDtypeStruct((N_EXPERTS, t), F32))
    return pl.pallas_call(
        functools.partial(_merge_kernel, alpha, n_ctx_tiles, w_router is not None),
        grid=(t // tm,), in_specs=in_specs, out_specs=out_specs, out_shape=out_shape,
        compiler_params=_cparams(("arbitrary",)), name="merge",
    )(*args)


def _ffn_kernel(alpha, n_ctx_tiles, n_split, h_ref, x1_ref, g2_ref, wg_ref, wu_ref, wd_ref, lg_ref, lb_ref,
                oc_ref, ol_ref):
    h = h_ref[...]
    tf = wg_ref.shape[1] // n_split
    y = None
    for f in range(n_split):
        cols = slice(f * tf, (f + 1) * tf)
        act = _silu(_dot(h, wg_ref[:, cols])) * _dot(h, wu_ref[:, cols])
        part = _dot(act.astype(BF16), wd_ref[cols, :])
        y = part if y is None else y + part
    out = _layer_norm(alpha * x1_ref[...] + g2_ref[...] * y, lg_ref[...], lb_ref[...])
    is_ctx = pl.program_id(0) < n_ctx_tiles

    @pl.when(is_ctx)
    def _():
        oc_ref[...] = out

    @pl.when(jnp.logical_not(is_ctx))
    def _():
        ol_ref[...] = out


def _dense_mixer(h2, x1, mod_l, wg, wu, wd, ln_g, ln_b, alpha, tm, n_split, row_of_tile, t_ctx):
    t, d = x1.shape
    n_ctx_tiles = t_ctx // tm
    const = lambda shape: pl.BlockSpec(shape, lambda i: tuple(0 for _ in shape), pipeline_mode=pl.Buffered(1))
    return pl.pallas_call(
        functools.partial(_ffn_kernel, alpha, n_ctx_tiles, n_split),
        grid=(t // tm,),
        in_specs=[
            pl.BlockSpec((tm, d), lambda i: (i, 0)),
            pl.BlockSpec((tm, d), lambda i: (i, 0)),
            _mod_spec(5, row_of_tile),
            const(wg.shape), const(wu.shape), const(wd.shape), const((1, d)), const((1, d)),
        ],
        out_specs=_pair_specs(tm, d, n_ctx_tiles),
        out_shape=[jax.ShapeDtypeStruct((t_ctx, d), F32), jax.ShapeDtypeStruct((t - t_ctx, d), F32)],
        compiler_params=_cparams(("arbitrary",)), name="dense_mixer",
    )(h2, x1, mod_l, wg, wu, wd, ln_g, ln_b)


MOE_TILE = 256
EXPERT_TILE = 512
MOE_TF = 1792
GATHER_TILES = 2


def _route_kernel(g_ref, rank_ref, gate_ref, before_ref, run_s):
    w = pl.program_id(0)

    @pl.when(w == 0)
    def _():
        run_s[...] = jnp.zeros_like(run_s)

    g = g_ref[...]
    sel = g > 0.0
    row = lax.broadcasted_iota(jnp.int32, (MOE_TILE, MOE_TILE), 0)
    col = lax.broadcasted_iota(jnp.int32, (MOE_TILE, MOE_TILE), 1)
    earlier = (row < col).astype(BF16)
    ones = jnp.where(sel, 1.0, 0.0)
    rank = _dot(ones.astype(BF16), earlier) + run_s[:, 0:1]
    rank_ref[...] = jnp.where(sel, rank, -1.0)
    gate_ref[...] = g
    before_ref[...] = run_s[...]
    run_s[...] += jnp.sum(ones, axis=1, keepdims=True)


def _route(gates_t):
    t = gates_t.shape[1]
    nw = t // MOE_TILE
    return pl.pallas_call(
        _route_kernel,
        grid=(nw,),
        in_specs=[pl.BlockSpec((N_EXPERTS, MOE_TILE), lambda w: (0, w))],
        out_specs=[
            pl.BlockSpec((None, N_EXPERTS, MOE_TILE), lambda w: (w, 0, 0)),
            pl.BlockSpec((None, N_EXPERTS, MOE_TILE), lambda w: (w, 0, 0)),
            pl.BlockSpec((None, N_EXPERTS, LANES), lambda w: (w, 0, 0)),
        ],
        out_shape=[
            jax.ShapeDtypeStruct((nw, N_EXPERTS, MOE_TILE), F32),
            jax.ShapeDtypeStruct((nw, N_EXPERTS, MOE_TILE), F32),
            jax.ShapeDtypeStruct((nw, N_EXPERTS, LANES), F32),
        ],
        scratch_shapes=[pltpu.VMEM((N_EXPERTS, LANES), F32)],
        compiler_params=_cparams(("arbitrary",)), name="moe_route",
    )(gates_t)


def _tile_onehot(rank_ref, w, e, base):
    rank_row = rank_ref[w, pl.ds(e, 1), :]
    rows = lax.broadcasted_iota(jnp.int32, (MOE_TILE, MOE_TILE), 0).astype(F32) + base.astype(F32)
    return rank_row == rows


def _gather_kernel(te_ref, base_ref, wlo_ref, whi_ref, rank_ref, x_ref, o_ref):
    o_ref[...] = jnp.zeros_like(o_ref)
    for u in range(GATHER_TILES):
        j = pl.program_id(0) * GATHER_TILES + u
        e, base = te_ref[j], base_ref[j]
        rows = slice(u * MOE_TILE, (u + 1) * MOE_TILE)

        def window(w, carry):
            p = jnp.where(_tile_onehot(rank_ref, w, e, base), 1.0, 0.0).astype(BF16)
            x_w = x_ref[pl.ds(pl.multiple_of(w * MOE_TILE, MOE_TILE), MOE_TILE), :]
            o_ref[rows, :] += _dot(p, x_w).astype(BF16)
            return carry

        lax.fori_loop(wlo_ref[j], whi_ref[j] + 1, window, 0)


def _moe_gather(meta, rank_t, h2, n_tiles):
    t, d = h2.shape
    nw = t // MOE_TILE
    return pl.pallas_call(
        _gather_kernel,
        grid_spec=pltpu.PrefetchScalarGridSpec(
            num_scalar_prefetch=4, grid=(n_tiles // GATHER_TILES,),
            in_specs=[
                pl.BlockSpec((nw, N_EXPERTS, MOE_TILE), lambda j, *_: (0, 0, 0)),
                pl.BlockSpec((t, d), lambda j, *_: (0, 0), pipeline_mode=pl.Buffered(1)),
            ],
            out_specs=pl.BlockSpec((GATHER_TILES * MOE_TILE, d), lambda j, *_: (j, 0)),
        ),
        out_shape=jax.ShapeDtypeStruct((n_tiles * MOE_TILE, d), BF16),
        compiler_params=_cparams(("arbitrary",)), name="moe_gather",
    )(*meta, rank_t, h2)


F8 = jnp.float8_e4m3fn
F8_TARGET = 224.0
F8_TINY = 1e-30


def _f8_quantize(x):
    m = jnp.max(jnp.max(jnp.abs(x), axis=1, keepdims=True), axis=0, keepdims=True)
    m = jnp.maximum(m, F8_TINY)
    return (x * (F8_TARGET / m)).astype(F8), m * (1.0 / F8_TARGET)


def _moe_up_kernel(te_ref, chg_ref, nused_ref, x_ref, wg_ref, wu_ref, o_ref, wg_s, wu_s, inv_s):
    j = pl.program_id(1)

    @pl.when(chg_ref[j] == 1)
    def _():
        for row, (w_ref, w_s) in enumerate(((wg_ref, wg_s), (wu_ref, wu_s))):
            w_s[...], inv = _f8_quantize(w_ref[...])
            inv_s[row:row + 1, :] = jnp.broadcast_to(inv, (1, LANES))

    @pl.when(j < nused_ref[0])
    def _():
        x8, inv_x = _f8_quantize(x_ref[...].astype(F32))
        a = _dot(x8, wg_s[...]) * (inv_x * inv_s[0:1, 0:1])
        u = _dot(x8, wu_s[...]) * (inv_x * inv_s[1:2, 0:1])
        o_ref[...] = (_silu(a) * u).astype(BF16)

    @pl.when(j >= nused_ref[0])
    def _():
        o_ref[...] = jnp.zeros_like(o_ref)


def _moe_up(te, chg, nused, xs, wg, wu):
    rows, d = xs.shape
    n_tiles = rows // EXPERT_TILE
    ff = wg.shape[2]
    return pl.pallas_call(
        _moe_up_kernel,
        grid_spec=pltpu.PrefetchScalarGridSpec(
            num_scalar_prefetch=3, grid=(ff // MOE_TF, n_tiles),
            in_specs=[
                pl.BlockSpec((EXPERT_TILE, d), lambda f, j, *_: (j, 0)),
                pl.BlockSpec((None, d, MOE_TF), lambda f, j, te, *_: (te[j], 0, f)),
                pl.BlockSpec((None, d, MOE_TF), lambda f, j, te, *_: (te[j], 0, f)),
            ],
            out_specs=pl.BlockSpec((EXPERT_TILE, MOE_TF), lambda f, j, *_: (j, f)),
            scratch_shapes=[pltpu.VMEM((d, MOE_TF), F8), pltpu.VMEM((d, MOE_TF), F8),
                            pltpu.VMEM((SUBLANES, LANES), F32)],
        ),
        out_shape=jax.ShapeDtypeStruct((rows, ff), BF16),
        compiler_params=_cparams(("arbitrary", "arbitrary")), name="moe_up",
    )(te, chg, nused, xs, wg, wu)


def _moe_down_kernel(te_ref, chg_ref, nused_ref, a_ref, wd_ref, o_ref, wd_s, inv_s):
    j = pl.program_id(0)

    @pl.when(chg_ref[j] == 1)
    def _():
        wd_s[...], inv = _f8_quantize(wd_ref[...])
        inv_s[0:1, :] = jnp.broadcast_to(inv, (1, LANES))

    @pl.when(j < nused_ref[0])
    def _():
        a8, inv_a = _f8_quantize(a_ref[...].astype(F32))
        o_ref[...] = (_dot(a8, wd_s[...]) * (inv_a * inv_s[0:1, 0:1])).astype(BF16)

    @pl.when(j >= nused_ref[0])
    def _():
        o_ref[...] = jnp.zeros_like(o_ref)


def _moe_down(te, chg, nused, act, wd):
    rows, ff = act.shape
    d = wd.shape[2]
    return pl.pallas_call(
        _moe_down_kernel,
        grid_spec=pltpu.PrefetchScalarGridSpec(
            num_scalar_prefetch=3, grid=(rows // EXPERT_TILE,),
            in_specs=[
                pl.BlockSpec((EXPERT_TILE, ff), lambda j, *_: (j, 0)),
                pl.BlockSpec((None, ff, d), lambda j, te, *_: (te[j], 0, 0)),
            ],
            out_specs=pl.BlockSpec((EXPERT_TILE, d), lambda j, *_: (j, 0)),
            scratch_shapes=[pltpu.VMEM((ff, d), F8), pltpu.VMEM((SUBLANES, LANES), F32)],
        ),
        out_shape=jax.ShapeDtypeStruct((rows, d), BF16),
        compiler_params=_cparams(("arbitrary",)), name="moe_down",
    )(te, chg, nused, act, wd)


def _combine_kernel(alpha, nw_ctx, blk_a_ref, blk_b_ref, rel_ref, two_ref, rank_ref, gate_ref, *refs):
    ya_refs, yb_refs = refs[:N_EXPERTS], refs[N_EXPERTS:2 * N_EXPERTS]
    x1_ref, g2_ref, lg_ref, lb_ref, oc_ref, ol_ref, acc_s = refs[2 * N_EXPERTS:]
    w = pl.program_id(0)
    row_id = lax.broadcasted_iota(jnp.int32, (MOE_TILE, MOE_TILE), 0).astype(F32)

    def weights(e, rel):
        hit = rank_ref[e:e + 1, :] == row_id + rel.astype(F32)
        return jnp.where(hit, gate_ref[e:e + 1, :], 0.0).T.astype(BF16)

    acc = None
    for e in range(N_EXPERTS):
        part = _dot(weights(e, rel_ref[w * N_EXPERTS + e]), ya_refs[e][...])
        acc = part if acc is None else acc + part
    acc_s[...] = acc
    for e in range(N_EXPERTS):
        @pl.when(two_ref[w * N_EXPERTS + e] == 1)
        def _():
            acc_s[...] += _dot(weights(e, rel_ref[w * N_EXPERTS + e] + MOE_TILE), yb_refs[e][...])

    out = _layer_norm(alpha * x1_ref[...] + g2_ref[...] * acc_s[...], lg_ref[...], lb_ref[...])

    @pl.when(w < nw_ctx)
    def _():
        oc_ref[...] = out

    @pl.when(w >= nw_ctx)
    def _():
        ol_ref[...] = out


def _moe_combine(bands, rank_t, gate_t, y, x1, mod_l, ln_g, ln_b, alpha, row_of_tile, t_ctx):
    t, d = x1.shape
    nw = t // MOE_TILE
    nw_ctx = t_ctx // MOE_TILE

    def y_spec(which, e):
        return pl.BlockSpec((MOE_TILE, d), lambda w, *pre: (pre[which][w * N_EXPERTS + e], 0))

    win_spec = pl.BlockSpec((None, N_EXPERTS, MOE_TILE), lambda w, *_: (w, 0, 0))
    return pl.pallas_call(
        functools.partial(_combine_kernel, alpha, nw_ctx),
        grid_spec=pltpu.PrefetchScalarGridSpec(
            num_scalar_prefetch=4, grid=(nw,),
            in_specs=[win_spec, win_spec]
            + [y_spec(0, e) for e in range(N_EXPERTS)] + [y_spec(1, e) for e in range(N_EXPERTS)]
            + [
                pl.BlockSpec((MOE_TILE, d), lambda w, *_: (w, 0)),
                pl.BlockSpec((None, None, 1, d), lambda w, *_: (row_of_tile(w), 5, 0, 0)),
                pl.BlockSpec((1, d), lambda w, *_: (0, 0)),
                pl.BlockSpec((1, d), lambda w, *_: (0, 0)),
            ],
            out_specs=[
                pl.BlockSpec((MOE_TILE, d), lambda w, *_: (jnp.minimum(w, nw_ctx - 1), 0)),
                pl.BlockSpec((MOE_TILE, d), lambda w, *_: (jnp.maximum(w - nw_ctx, 0), 0)),
            ],
            scratch_shapes=[pltpu.VMEM((MOE_TILE, d), F32)],
        ),
        out_shape=[jax.ShapeDtypeStruct((t_ctx, d), F32), jax.ShapeDtypeStruct((t - t_ctx, d), F32)],
        compiler_params=_cparams(("arbitrary",)), name="moe_combine",
    )(*bands, rank_t, gate_t, *([y] * (2 * N_EXPERTS)), x1, mod_l, ln_g, ln_b)


def _moe_layer(gates, h2, x1, mod_l, wg, wu, wd, ln_g, ln_b, alpha, row_of_tile, t_ctx):
    t = h2.shape[0]
    ratio = EXPERT_TILE // MOE_TILE
    n_big = -(-(2 * t + N_EXPERTS * (EXPERT_TILE - 1)) // EXPERT_TILE)
    n_tiles = n_big * ratio
    rank_t, gate_t, before = _route(gates)
    before = before[:, :, 0].astype(jnp.int32)
    counts = jnp.sum(gates > 0.0, axis=1).astype(jnp.int32)
    big_e = (counts + EXPERT_TILE - 1) // EXPERT_TILE
    big_end = jnp.cumsum(big_e)
    jb = jnp.arange(n_big, dtype=jnp.int32)
    te_big = jnp.minimum(jnp.sum(jb[:, None] >= big_end[None, :], axis=1), N_EXPERTS - 1).astype(jnp.int32)
    chg = jnp.concatenate([jnp.ones((1,), jnp.int32), (te_big[1:] != te_big[:-1]).astype(jnp.int32)])
    nused = big_end[-1].reshape(1).astype(jnp.int32)
    tiles_e = big_e * ratio
    tile_end = big_end * ratio
    n_used = tile_end[-1]
    j = jnp.arange(n_tiles, dtype=jnp.int32)
    te = jnp.minimum(jnp.sum(j[:, None] >= tile_end[None, :], axis=1), N_EXPERTS - 1).astype(jnp.int32)
    base = (j - jnp.take(tile_end - tiles_e, te)) * MOE_TILE
    last = jnp.minimum(base + MOE_TILE - 1, jnp.take(counts, te) - 1)
    before_t = jnp.take(before, te, axis=1)
    used = j < n_used
    wlo = jnp.where(used, jnp.sum(before_t <= base[None, :], axis=0) - 1, 1).astype(jnp.int32)
    whi = jnp.where(used, jnp.sum(before_t <= last[None, :], axis=0) - 1, 0).astype(jnp.int32)
    meta = (te, base.astype(jnp.int32), wlo, whi)
    xs = _moe_gather(meta, rank_t, h2, n_tiles)
    act = _moe_up(te_big, chg, nused, xs, wg, wu)
    y = _moe_down(te_big, chg, nused, act, wd)
    start_row = (tile_end - tiles_e) * MOE_TILE
    n_we = jnp.concatenate([before[1:], counts[None, :]], axis=0) - before
    first = start_row[None, :] + before
    blk_a = jnp.minimum(first // MOE_TILE, n_tiles - 1)
    two = ((n_we > 0) & ((first + n_we - 1) // MOE_TILE > blk_a)).astype(jnp.int32)
    rel = blk_a * MOE_TILE - start_row[None, :]
    bands = tuple(a.reshape(-1).astype(jnp.int32) for a in (blk_a, blk_a + two, rel, two))
    return _moe_combine(bands, rank_t, gate_t, y, x1, mod_l, ln_g, ln_b, alpha, row_of_tile, t_ctx)


def _rope_tables(seq):
    half = GLA_DK // 2
    n = half // 2
    t = np.arange(seq)
    inv = ROPE_BASE ** (-np.arange(n, dtype=np.float64) / n)
    cos = np.zeros((seq, GLA_DK), np.float64)
    sin = np.zeros((seq, GLA_DK), np.float64)
    for a, pos in enumerate((t // GRID_W, t % GRID_W)):
        ang = pos[:, None].astype(np.float64) * inv[None, :]
        base = a * half
        cos[:, base:base + n] = np.cos(ang)
        cos[:, base + n:base + half] = np.cos(ang)
        sin[:, base:base + n] = -np.sin(ang)
        sin[:, base + n:base + half] = np.sin(ang)
    tile = lambda m: jnp.asarray(np.tile(m, (1, GLA_HEADS)), F32)
    return tile(cos), tile(sin)


W_IN_GD = 1536
W_IN_KV = (2080, 2592)
W_IN_CHUNKS = (0, 512, 1024, 1568, 3104, 3616, 4128, 4640, 5152, 5664, 6176, 6688)


def _transpose_w_in(w):
    return jnp.swapaxes(w, 1, 2).astype(BF16)


def _gate_up_blocks(w_gup):
    qkw = w_gup.shape[-1]
    fwd = jnp.pad(w_gup[:, 0], ((0, 0), (0, 0), (0, qkw)))
    bwd = jnp.pad(w_gup[:, 1], ((0, 0), (0, 0), (qkw, 0)))
    return jnp.pad(jnp.concatenate([fwd, bwd], axis=1), ((0, 0), (0, LANES - 2 * GLA_RANK), (0, 0)))


def kernel(x_prompt, x_sample, cache_na_k, cache_na_v, state_gla, c, c_ctx, w_mod, b_mod, w_in, gla_w_gup, gla_b_g, gla_norm, w_br_gla, na_rpb, w_br_na, cv_w, cv_b, cv_ln_g, cv_ln_b, w_br_cv, w_out, ln_g, ln_b, ffd_w_gate, ffd_w_up, ffd_w_down, moe_w_router, moe_w_gate, moe_w_up, moe_w_down):
    nb_c, l_c, d = x_prompt.shape
    nb_l, l_l, _ = x_sample.shape
    depth = w_mod.shape[0]
    t_ctx, t_lat = nb_c * l_c, nb_l * l_l
    t_all = t_ctx + t_lat
    alpha = (2 * depth) ** 0.25
    tm = 512
    row_of_tile = _row_of_tile_fn(t_ctx, l_l, tm)

    x_pair = (x_prompt.reshape(t_ctx, d), x_sample.reshape(t_lat, d))
    cond8 = jnp.concatenate([c_ctx[None, :], c, jnp.zeros((8 - 1 - nb_l, d), F32)], axis=0)
    mod = _modulation(cond8, w_mod, b_mod).reshape(depth, 8, 6, 1, d)
    rope = _rope_tables(l_l)
    rows_lat = l_l // GRID_W
    kr = min(NA_WIN_R, rows_lat)
    s0_all = state_gla.reshape(nb_l, depth, 2, 2, LANES, GLA_DV)
    w_t = _transpose_w_in(w_in)
    wg_blks = _gate_up_blocks(gla_w_gup)
    toep = _na_bias_tables(na_rpb)
    w_br = [w.astype(BF16) for w in (w_br_gla, w_br_na, w_br_cv, w_out)]

    caches = None
    states = None
    for l in range(depth):
        mod_l = mod[l]
        gd, kc, vc, kl, vl, zb = _input_projection(x_pair, mod_l, w_t, l, depth, l_c, tm, row_of_tile, caches)
        caches = (kc, vc)

        wg_blk = wg_blks[l]
        bg = gla_b_g[l].reshape(1, CHUNK_W)
        gn = gla_norm[l].reshape(1, CHUNK_W)
        ya, states = _gla_branch(gd, zb, wg_blk, bg, gn, latent=False, seq=l_c, nb=nb_c, tok0=0, t_total=t_all,
                                 state_slot=(l, depth, states))
        ya = _gla_branch(gd, zb, wg_blk, bg, gn, latent=True, seq=l_l, nb=nb_l, tok0=t_ctx, t_total=t_all,
                         rope=rope, s0=s0_all[:, l], ya_prev=ya)

        yb = _attn_context(zb, kc, vc, l, nb_c, l_c, t_all)
        yb = _attn_latent(zb, kl, vl, cache_na_k, cache_na_v, l, toep, yb, nb_l, l_l, t_ctx)

        conv_args = (zb, cv_w[l], cv_b[l][None, :], cv_ln_g[l][None, :], cv_ln_b[l][None, :])
        yc = _conv_branch(*conv_args, seq=l_c, nb=nb_c, tok0=0, t_total=t_all)
        yc = _conv_branch(*conv_args, seq=l_l, nb=nb_l, tok0=t_ctx, t_total=t_all, prev=yc)

        moe = l % 2 == 1
        w_router = None
        if moe:
            w_router = moe_w_router[l // 2].T
        merged = _merge(ya, yb, yc, zb, x_pair, mod_l, w_br[0][l], w_br[1][l], w_br[2][l], w_br[3][l],
                        ln_g[l, 0][None, :], ln_b[l, 0][None, :], alpha, tm, row_of_tile, w_router)
        ln2 = (ln_g[l, 1][None, :], ln_b[l, 1][None, :])
        if moe:
            x1, h2, gates = merged
            x_pair = _moe_layer(gates, h2, x1, mod_l, moe_w_gate[l // 2], moe_w_up[l // 2], moe_w_down[l // 2],
                                *ln2, alpha, _row_of_tile_fn(t_ctx, l_l, MOE_TILE), t_ctx)
        else:
            x1, h2 = merged
            i = l // 2
            x_pair = _dense_mixer(h2, x1, mod_l, ffd_w_gate[i].astype(BF16), ffd_w_up[i].astype(BF16),
                                  ffd_w_down[i].astype(BF16), *ln2, alpha, tm, 1, row_of_tile, t_ctx)

    y_prompt = x_pair[0].reshape(nb_c, l_c, d)
    y_sample = x_pair[1].reshape(nb_l, l_l, d)
    cache_shape = (nb_c, depth, l_c, NA_HEADS, NA_DH)
    return (y_prompt, y_sample, caches[0].reshape(cache_shape), caches[1].reshape(cache_shape),
            states.reshape(nb_c, depth, 2, GLA_HEADS, GLA_DK, GLA_DV))


def _row_of_tile_fn(t_ctx, l_lat, tm):
    n_ctx = t_ctx // tm

    def row_of_tile(i):
        return jnp.where(i < n_ctx, 0, 1 + (i - n_ctx) // (l_lat // tm))

    return row_of_tile
```
